```python
import math
import jax, jax.numpy as jnp
from jax import lax
import numpy as np

D_MODEL = 1024
BATCH = 2
SEQ = 16384
DEPTH = 2

N_MIXERS = 2
N_ATTN_LAYERS = (DEPTH + N_MIXERS - 1) // N_MIXERS
N_SSM_LAYERS = DEPTH // N_MIXERS

ATTN_HEAD_DIM = 64
ATTN_HEADS = D_MODEL // ATTN_HEAD_DIM
Q_BLOCK = 128
SSM_EXPAND = 2
SSM_D_INNER = SSM_EXPAND * D_MODEL
SSM_HEAD_DIM = 64
SSM_HEADS = SSM_D_INNER // SSM_HEAD_DIM
SSM_GROUPS = 8
SSM_STATE = 128
SSM_CONV = 4
SSM_CHUNK = 128
SSM_XBC = SSM_D_INNER + 2 * SSM_GROUPS * SSM_STATE
FFN_DIM = 2816
FFN_CONV = 3
PLE_DIM = 256

LN_EPS = 1e-5
RMS_EPS = 1e-5
DEEPNORM_ALPHA = (2 * DEPTH) ** 0.25
DEEPNORM_BETA = (8 * DEPTH) ** -0.25

kernel_name = "fox_ssd_interleaved_deepnorm_trunk"


def layer_norm(x, g, b):
    xf = x.astype(jnp.float32)
    mu = jnp.mean(xf, axis=-1, keepdims=True)
    var = jnp.mean(jnp.square(xf - mu), axis=-1, keepdims=True)
    return ((xf - mu) * lax.rsqrt(var + LN_EPS) * g + b).astype(x.dtype)


def causal_dwconv(u, w, b):
    K = w.shape[0]
    S = u.shape[1]
    up = jnp.pad(u, ((0, 0), (K - 1, 0), (0, 0)))
    out = b
    for k in range(K):
        out = out + up[:, k:k + S] * w[k]
    return out


def fox_mixer(x, w_in, b_f, w_out):
    Bsz, S, _ = x.shape
    H, Dh = ATTN_HEADS, ATTN_HEAD_DIM
    proj = x @ w_in
    q, k, v, f_logit = jnp.split(proj, [D_MODEL, 2 * D_MODEL, 3 * D_MODEL], axis=-1)
    q = q.reshape(Bsz, S, H, Dh).transpose(0, 2, 1, 3)
    k = k.reshape(Bsz, S, H, Dh).transpose(0, 2, 1, 3)
    v = v.reshape(Bsz, S, H, Dh).transpose(0, 2, 1, 3)
    log_f = jax.nn.log_sigmoid(f_logit.astype(jnp.float32) + b_f)
    c = jnp.cumsum(log_f, axis=1).transpose(0, 2, 1)
    nb = S // Q_BLOCK
    qb = q.reshape(Bsz, H, nb, Q_BLOCK, Dh).transpose(2, 0, 1, 3, 4)
    cqb = c.reshape(Bsz, H, nb, Q_BLOCK).transpose(2, 0, 1, 3)
    kpos = jnp.arange(S)
    scale = 1.0 / math.sqrt(Dh)

    def block(args):
        qi, cqi, bi = args
        s = jnp.einsum('bhqd,bhkd->bhqk', qi, k).astype(jnp.float32) * scale
        s = s + cqi[..., None] - c[:, :, None, :]
        qpos = bi * Q_BLOCK + jnp.arange(Q_BLOCK)
        s = jnp.where(kpos[None, :] <= qpos[:, None], s, -jnp.inf)
        pr = jax.nn.softmax(s, axis=-1)
        return jnp.einsum('bhqk,bhkd->bhqd', pr.astype(v.dtype), v)

    o = lax.map(block, (qb, cqb, jnp.arange(nb)))
    o = o.transpose(1, 0, 3, 2, 4).reshape(Bsz, S, D_MODEL)
    return o @ w_out


def ssd_scan(xh, dt, A, Bm, Cm):
    Bsz, S = xh.shape[0], xh.shape[1]
    G, R, P, N, Q = SSM_GROUPS, SSM_HEADS // SSM_GROUPS, SSM_HEAD_DIM, SSM_STATE, SSM_CHUNK
    nc = S // Q

    def to_chunks(t):
        return jnp.moveaxis(t.reshape((Bsz, nc, Q) + t.shape[2:]), 1, 0)

    xc = to_chunks(xh.reshape(Bsz, S, G, R, P))
    dtc = to_chunks(dt.reshape(Bsz, S, G, R))
    Bc = to_chunks(Bm)
    Cc = to_chunks(Cm)
    A_g = A.reshape(G, R)
    tri = jnp.tril(jnp.ones((Q, Q), dtype=bool))[None, :, :, None, None]

    def step(state, inp):
        x_, dt_, B_, C_ = inp
        acum = jnp.cumsum(dt_ * A_g, axis=1)
        seg = acum[:, :, None] - acum[:, None, :]
        L = jnp.exp(jnp.where(tri, seg, -jnp.inf))
        CB = jnp.einsum('btgn,bsgn->btsg', C_, B_)
        y_intra = jnp.einsum('btsg,btsgr,bsgr,bsgrp->btgrp', CB, L, dt_, x_)
        y_inter = jnp.einsum('btgn,bgrpn,btgr->btgrp', C_, state, jnp.exp(acum))
        w_end = jnp.exp(acum[:, -1:] - acum) * dt_
        new_state = state * jnp.exp(acum[:, -1])[..., None, None] + \
            jnp.einsum('bsgn,bsgr,bsgrp->bgrpn', B_, w_end, x_)
        return new_state, (y_intra + y_inter).astype(jnp.float32)

    state0 = jnp.zeros((Bsz, G, R, P, N), jnp.float32)
    _, y = lax.scan(step, state0, (xc, dtc, Bc, Cc))
    return jnp.moveaxis(y, 0, 1).reshape(Bsz, S, SSM_HEADS, P)


def ssd_mixer(x, w_in, conv_w, conv_b, dt_bias, A_log, D_skip, norm_w, w_out):
    Bsz, S, _ = x.shape
    GN = SSM_GROUPS * SSM_STATE
    proj = x @ w_in
    z, xBC, dt_raw = jnp.split(proj, [SSM_D_INNER, SSM_D_INNER + SSM_XBC], axis=-1)
    xBC = jax.nn.silu(causal_dwconv(xBC, conv_w, conv_b))
    xs, Bm, Cm = jnp.split(xBC, [SSM_D_INNER, SSM_D_INNER + GN], axis=-1)
    dt = jax.nn.softplus(dt_raw.astype(jnp.float32) + dt_bias)
    A = -jnp.exp(A_log.astype(jnp.float32))
    xh = xs.reshape(Bsz, S, SSM_HEADS, SSM_HEAD_DIM)
    y = ssd_scan(xh, dt, A,
                 Bm.reshape(Bsz, S, SSM_GROUPS, SSM_STATE),
                 Cm.reshape(Bsz, S, SSM_GROUPS, SSM_STATE))
    y = y + D_skip[:, None] * xh
    y = y.reshape(Bsz, S, SSM_D_INNER) * jax.nn.silu(z.astype(jnp.float32))
    yg = y.reshape(Bsz, S, SSM_GROUPS, SSM_D_INNER // SSM_GROUPS)
    yg = yg * lax.rsqrt(jnp.mean(jnp.square(yg), axis=-1, keepdims=True) + RMS_EPS)
    y = yg.reshape(Bsz, S, SSM_D_INNER) * norm_w
    return y.astype(x.dtype) @ w_out


def conv_ffn(x, w_up, conv_w, conv_b, w_down):
    u, g = jnp.split(x @ w_up, [FFN_DIM], axis=-1)
    g = causal_dwconv(g, conv_w, conv_b)
    return (jax.nn.gelu(g, approximate=False) * u) @ w_down


def setup_inputs(seed: int = 0) -> dict:
    key = jax.random.key(seed)
    ks = iter(jax.random.split(key, 40))

    def nrm(shape, scale):
        return jax.random.normal(next(ks), shape, jnp.float32) * scale

    D, H = D_MODEL, ATTN_HEADS
    NA, NB = N_ATTN_LAYERS, N_SSM_LAYERS
    beta = DEEPNORM_BETA
    x = nrm((BATCH, SEQ, D), 1.0)
    p = nrm((DEPTH, BATCH, SEQ, PLE_DIM), 1.0)
    attn_w_in = nrm((NA, D, 3 * D + H), D ** -0.5)
    attn_w_in = attn_w_in.at[:, :, 2 * D:3 * D].multiply(beta)
    attn_b_f = jax.random.uniform(next(ks), (NA, H), jnp.float32, 1.0, 6.0)
    attn_w_out = nrm((NA, D, D), beta * D ** -0.5)
    ssm_in_dim = 2 * SSM_D_INNER + 2 * SSM_GROUPS * SSM_STATE + SSM_HEADS
    ssm_w_in = nrm((NB, D, ssm_in_dim), D ** -0.5)
    ssm_conv_w = nrm((NB, SSM_CONV, SSM_XBC), SSM_CONV ** -0.5)
    ssm_conv_b = nrm((NB, SSM_XBC), 0.02)
    dt0 = jnp.exp(jax.random.uniform(next(ks), (NB, SSM_HEADS), jnp.float32,
                                     math.log(1e-3), math.log(1e-1)))
    ssm_dt_bias = dt0 + jnp.log(-jnp.expm1(-dt0))
    ssm_A_log = jnp.log(jax.random.uniform(next(ks), (NB, SSM_HEADS), jnp.float32, 1.0, 16.0))
    ssm_D = 1.0 + nrm((NB, SSM_HEADS), 0.1)
    ssm_norm_w = 1.0 + nrm((NB, SSM_D_INNER), 0.05)
    ssm_w_out = nrm((NB, SSM_D_INNER, D), beta * SSM_D_INNER ** -0.5)
    ln_mix_g = 1.0 + nrm((DEPTH, D), 0.05)
    ln_mix_b = nrm((DEPTH, D), 0.02)
    ffn_w_up = nrm((DEPTH, D, 2 * FFN_DIM), D ** -0.5)
    ffn_conv_w = nrm((DEPTH, FFN_CONV, FFN_DIM), FFN_CONV ** -0.5)
    ffn_conv_b = nrm((DEPTH, FFN_DIM), 0.02)
    ffn_w_down = nrm((DEPTH, FFN_DIM, D), beta * FFN_DIM ** -0.5)
    ln_ffn_g = 1.0 + nrm((DEPTH, D), 0.05)
    ln_ffn_b = nrm((DEPTH, D), 0.02)
    ple_w_proj = nrm((DEPTH, PLE_DIM, D), beta * PLE_DIM ** -0.5)
    ple_w_gate = nrm((DEPTH, D, D), D ** -0.5)
    ple_b_gate = nrm((DEPTH, D), 0.02)
    return {"x": x, "p": p,
            "attn_w_in": attn_w_in, "attn_b_f": attn_b_f, "attn_w_out": attn_w_out,
            "ssm_w_in": ssm_w_in, "ssm_conv_w": ssm_conv_w, "ssm_conv_b": ssm_conv_b,
            "ssm_dt_bias": ssm_dt_bias, "ssm_A_log": ssm_A_log, "ssm_D": ssm_D,
            "ssm_norm_w": ssm_norm_w, "ssm_w_out": ssm_w_out,
            "ln_mix_g": ln_mix_g, "ln_mix_b": ln_mix_b,
            "ffn_w_up": ffn_w_up, "ffn_conv_w": ffn_conv_w, "ffn_conv_b": ffn_conv_b,
            "ffn_w_down": ffn_w_down, "ln_ffn_g": ln_ffn_g, "ln_ffn_b": ln_ffn_b,
            "ple_w_proj": ple_w_proj, "ple_w_gate": ple_w_gate, "ple_b_gate": ple_b_gate}


def reference(x, p, attn_w_in, attn_b_f, attn_w_out, ssm_w_in, ssm_conv_w, ssm_conv_b,
              ssm_dt_bias, ssm_A_log, ssm_D, ssm_norm_w, ssm_w_out, ln_mix_g, ln_mix_b,
              ffn_w_up, ffn_conv_w, ffn_conv_b, ffn_w_down, ln_ffn_g, ln_ffn_b,
              ple_w_proj, ple_w_gate, ple_b_gate):
    for i in range(DEPTH):
        j = i // N_MIXERS
        if i % N_MIXERS == 0:
            mix = fox_mixer(x, attn_w_in[j], attn_b_f[j], attn_w_out[j])
        else:
            mix = ssd_mixer(x, ssm_w_in[j], ssm_conv_w[j], ssm_conv_b[j], ssm_dt_bias[j],
                            ssm_A_log[j], ssm_D[j], ssm_norm_w[j], ssm_w_out[j])
        x = layer_norm(DEEPNORM_ALPHA * x + mix, ln_mix_g[i], ln_mix_b[i])
        ffn = conv_ffn(x, ffn_w_up[i], ffn_conv_w[i], ffn_conv_b[i], ffn_w_down[i])
        x = layer_norm(DEEPNORM_ALPHA * x + ffn, ln_ffn_g[i], ln_ffn_b[i])
        gate = jax.nn.sigmoid(x @ ple_w_gate[i] + ple_b_gate[i])
        x = x + gate * (p[i] @ ple_w_proj[i])
    return x
```

```python
import functools
import math

import jax
import jax.numpy as jnp
from jax import lax
from jax.experimental import pallas as pl
from jax.experimental.pallas import tpu as pltpu

F32 = jnp.float32
BF16 = jnp.bfloat16

ATTN_HEAD_DIM = 64
SSM_HEAD_DIM = 64
SSM_GROUPS = 8
SSM_STATE = 128
LN_EPS = 1e-5
RMS_EPS = 1e-5

LANES = 128
SUBLANES = 8
VMEM_LIMIT_BYTES = 56 * 1024 * 1024

ROW_TILE = 512
ATTN_BLOCK = 512
SSD_CHUNK = 256
CUMSUM_CHUNK = 256
MASK_VALUE = -1e30


def _params(*sem):
    return pltpu.CompilerParams(dimension_semantics=sem, vmem_limit_bytes=VMEM_LIMIT_BYTES)


def _resident(shape):
    nd = len(shape)
    return pl.BlockSpec(shape, lambda *_: (0,) * nd, pipeline_mode=pl.Buffered(1))


def _split3(v):
    hi = v.astype(BF16)
    r1 = v - hi.astype(F32)
    mid = r1.astype(BF16)
    lo = (r1 - mid.astype(F32)).astype(BF16)
    return hi, mid, lo


def _softplus(v):
    return jnp.maximum(v, 0.0) + jnp.log1p(jnp.exp(-jnp.abs(v)))


def _sigmoid(v):
    return 1.0 / (1.0 + jnp.exp(-v))


def _layer_norm(h, g, b):
    mu = jnp.mean(h, axis=-1, keepdims=True)
    d = h - mu
    var = jnp.mean(d * d, axis=-1, keepdims=True)
    return d * lax.rsqrt(var + LN_EPS) * g + b


def _fox_proj_kernel(x_ref, w_ref, wft_ref, bf_ref, ut_ref, q_ref, k_ref, v_ref, c_ref,
                     carry_ref, *, steps_per_batch):
    tm, d = x_ref.shape

    @pl.when(pl.program_id(0) % steps_per_batch == 0)
    def _():
        carry_ref[...] = jnp.zeros_like(carry_ref)

    xb = x_ref[...].astype(BF16)
    for idx, o_ref in enumerate((q_ref, k_ref, v_ref)):
        o_ref[...] = jnp.dot(xb, w_ref[:, idx * d:(idx + 1) * d],
                             preferred_element_type=F32).astype(BF16)

    fl = lax.dot_general(wft_ref[...], xb, (((1,), (1,)), ((), ())),
                         preferred_element_type=F32) + bf_ref[...]
    logf = -_softplus(-fl)
    nh = logf.shape[0]
    parts = jnp.concatenate(_split3(logf), axis=0)
    carry = carry_ref[...]
    for j in range(tm // CUMSUM_CHUNK):
        sl = slice(j * CUMSUM_CHUNK, (j + 1) * CUMSUM_CHUNK)
        pj = jnp.dot(parts[:, sl], ut_ref[...], preferred_element_type=F32)
        cj = (pj[0:nh] + pj[nh:2 * nh] + pj[2 * nh:3 * nh]) + carry
        c_ref[:, sl] = cj
        carry = cj[:, CUMSUM_CHUNK - 1:CUMSUM_CHUNK]
    carry_ref[...] = carry


def _fox_proj(xf, w_qkv, wft, bf_col, utri, seq):
    m, d = xf.shape
    nh = wft.shape[0]
    tm = min(ROW_TILE, seq)
    grid = (m // tm,)
    row = lambda i: (i, 0)
    out_shape = [jax.ShapeDtypeStruct((m, d), BF16)] * 3 + [jax.ShapeDtypeStruct((nh, m), F32)]
    return pl.pallas_call(
        functools.partial(_fox_proj_kernel, steps_per_batch=seq // tm),
        grid=grid,
        in_specs=[pl.BlockSpec((tm, d), row), _resident(w_qkv.shape), _resident(wft.shape),
                  _resident(bf_col.shape), _resident(utri.shape)],
        out_specs=[pl.BlockSpec((tm, d), row)] * 3 + [pl.BlockSpec((nh, tm), lambda i: (0, i))],
        out_shape=out_shape,
        scratch_shapes=[pltpu.VMEM((nh, 1), F32)],
        compiler_params=_params("arbitrary"),
        name="fox_proj",
    )(xf, w_qkv, wft, bf_col, utri)


def _fox_attn_kernel(q_ref, k_ref, v_ref, ccol_ref, crow_ref, o_ref, m_scr, l_scr, acc_scr, *, blk):
    hp = pl.program_id(1)
    qi = pl.program_id(2)
    q2 = q_ref[0]
    lane = lax.broadcasted_iota(jnp.int32, (1, LANES), 1)
    head_lane = lax.broadcasted_iota(jnp.int32, (1, ccol_ref.shape[2]), 1)
    ccol_all = ccol_ref[0]
    row_id = lax.broadcasted_iota(jnp.int32, (blk, blk), 0)
    col_id = lax.broadcasted_iota(jnp.int32, (blk, blk), 1)
    outs = []
    for hh in range(2):
        in_head = (lane // ATTN_HEAD_DIM) == hh
        qh = jnp.where(in_head, q2, jnp.zeros_like(q2))
        ccol = jnp.sum(jnp.where(head_lane == 2 * hp + hh, ccol_all, 0.0), axis=1, keepdims=True)
        m_scr[...] = jnp.full_like(m_scr, MASK_VALUE)
        l_scr[...] = jnp.zeros_like(l_scr)
        acc_scr[...] = jnp.zeros_like(acc_scr)

        def step(j, masked, hh=hh, qh=qh, ccol=ccol):
            start = pl.multiple_of(j * blk, blk)
            kj = k_ref[0, pl.ds(start, blk), :]
            vj = v_ref[0, pl.ds(start, blk), :]
            crow = crow_ref[0, 0, j, hh:hh + 1, :]
            s = lax.dot_general(qh, kj, (((1,), (1,)), ((), ())), preferred_element_type=F32)
            s = s + (ccol - crow)
            if masked:
                s = jnp.where(col_id <= row_id, s, MASK_VALUE)
            m_old = m_scr[...]
            m_new = jnp.maximum(m_old, jnp.max(s, axis=1, keepdims=True))
            p = jnp.exp(s - m_new)
            alpha = jnp.exp(m_old - m_new)
            l_scr[...] = alpha * l_scr[...] + jnp.sum(p, axis=1, keepdims=True)
            acc_scr[...] = alpha * acc_scr[...] + jnp.dot(p.astype(BF16), vj,
                                                          preferred_element_type=F32)
            m_scr[...] = m_new

        def body(j, carry):
            step(j, False)
            return carry

        lax.fori_loop(0, qi, body, 0)
        step(qi, True)
        outs.append(acc_scr[...] * (1.0 / l_scr[...]))
    o_ref[0] = jnp.where(lane < ATTN_HEAD_DIM, outs[0], outs[1]).astype(BF16)


def _fox_attn(q, k, v, ccol, crow, blk):
    b, s, d = q.shape
    nhp = d // LANES
    nh = ccol.shape[-1]
    nblk = s // blk
    grid = (b, nhp, nblk)
    return pl.pallas_call(
        functools.partial(_fox_attn_kernel, blk=blk),
        grid=grid,
        in_specs=[
            pl.BlockSpec((1, blk, LANES), lambda bi, hp, qi: (bi, qi, hp)),
            pl.BlockSpec((1, s, LANES), lambda bi, hp, qi: (bi, 0, hp)),
            pl.BlockSpec((1, s, LANES), lambda bi, hp, qi: (bi, 0, hp)),
            pl.BlockSpec((1, blk, nh), lambda bi, hp, qi: (bi, qi, 0)),
            pl.BlockSpec((1, 1, nblk, 2, blk), lambda bi, hp, qi: (bi, hp, 0, 0, 0)),
        ],
        out_specs=pl.BlockSpec((1, blk, LANES), lambda bi, hp, qi: (bi, qi, hp)),
        out_shape=jax.ShapeDtypeStruct((b, s, d), BF16),
        scratch_shapes=[pltpu.VMEM((blk, 1), F32), pltpu.VMEM((blk, 1), F32),
                        pltpu.VMEM((blk, LANES), F32)],
        compiler_params=_params("arbitrary", "arbitrary", "arbitrary"),
        name="fox_attn",
    )(q, k, v, ccol, crow)


def _outproj_ln_kernel(y_ref, w_ref, x_ref, g_ref, b_ref, o_ref, *, alpha):
    mix = jnp.dot(y_ref[...], w_ref[...], preferred_element_type=F32)
    o_ref[...] = _layer_norm(alpha * x_ref[...] + mix, g_ref[...], b_ref[...])


def _outproj_ln(y, w, xf, g, b, alpha, seq):
    m, d = xf.shape
    kdim = y.shape[1]
    tm = min(ROW_TILE, seq)
    row = lambda i: (i, 0)
    return pl.pallas_call(
        functools.partial(_outproj_ln_kernel, alpha=alpha),
        grid=(m // tm,),
        in_specs=[pl.BlockSpec((tm, kdim), row), _resident(w.shape), pl.BlockSpec((tm, d), row),
                  _resident(g.shape), _resident(b.shape)],
        out_specs=pl.BlockSpec((tm, d), row),
        out_shape=jax.ShapeDtypeStruct((m, d), F32),
        compiler_params=_params("arbitrary"),
        name="outproj_ln",
    )(y, w, xf, g, b)


def _ffn_kernel(x_ref, wu_ref, wg_ref, cw_ref, cb_ref, wd_ref, g_ref, b_ref, p_ref, wproj_ref,
                wgate_ref, bgate_ref, o_ref, gbuf_ref, carry_ref, *, chunks, steps_per_batch, alpha):
    tm = x_ref.shape[0]
    halo = SUBLANES

    @pl.when(pl.program_id(0) % steps_per_batch == 0)
    def _():
        carry_ref[...] = jnp.zeros_like(carry_ref)

    x = x_ref[...]
    xb = x.astype(BF16)
    acc = None
    for c0, cw in chunks:
        cs = slice(c0, c0 + cw)
        u = jnp.dot(xb, wu_ref[:, cs], preferred_element_type=F32)
        g = jnp.dot(xb, wg_ref[:, cs], preferred_element_type=F32)
        gbuf_ref[0:halo, 0:cw] = carry_ref[:, cs]
        gbuf_ref[halo:halo + tm, 0:cw] = g
        carry_ref[:, cs] = g[tm - halo:tm, :]
        conv = (cb_ref[:, cs] + cw_ref[2:3, cs] * g
                + cw_ref[1:2, cs] * gbuf_ref[halo - 1:halo - 1 + tm, 0:cw]
                + cw_ref[0:1, cs] * gbuf_ref[halo - 2:halo - 2 + tm, 0:cw])
        gelu = 0.5 * conv * (1.0 + lax.erf(conv * (1.0 / math.sqrt(2.0))))
        part = jnp.dot((gelu * u).astype(BF16), wd_ref[cs, :], preferred_element_type=F32)
        acc = part if acc is None else acc + part
    x2 = _layer_norm(alpha * x + acc, g_ref[...], b_ref[...])
    gate = _sigmoid(jnp.dot(x2.astype(BF16), wgate_ref[...], preferred_element_type=F32)
                    + bgate_ref[...])
    emb = jnp.dot(p_ref[...].astype(BF16), wproj_ref[...], preferred_element_type=F32)
    o_ref[...] = x2 + gate * emb


def _ffn_chunks(f):
    tile = 2 * LANES
    if f <= 6 * tile or f % tile:
        return ((0, f),)
    first = (f // tile + 1) // 2 * tile
    return ((0, first), (first, f - first))


def _ffn(xf, wu, wg, cw, cb, wd, g, b, pf, wproj, wgate, bgate, alpha, seq):
    m, d = xf.shape
    f = wu.shape[1]
    pdim = pf.shape[1]
    tm = min(ROW_TILE, seq)
    chunks = _ffn_chunks(f)
    wmax = max(c[1] for c in chunks)
    row = lambda i: (i, 0)
    return pl.pallas_call(
        functools.partial(_ffn_kernel, chunks=chunks, steps_per_batch=seq // tm, alpha=alpha),
        grid=(m // tm,),
        in_specs=[pl.BlockSpec((tm, d), row), _resident(wu.shape), _resident(wg.shape),
                  _resident(cw.shape), _resident(cb.shape), _resident(wd.shape),
                  _resident(g.shape), _resident(b.shape), pl.BlockSpec((tm, pdim), row),
                  _resident(wproj.shape), _resident(wgate.shape), _resident(bgate.shape)],
        out_specs=pl.BlockSpec((tm, d), row),
        out_shape=jax.ShapeDtypeStruct((m, d), F32),
        scratch_shapes=[pltpu.VMEM((tm + SUBLANES, wmax), F32), pltpu.VMEM((SUBLANES, f), F32)],
        compiler_params=_params("arbitrary"),
        name="conv_ffn",
    )(xf, wu, wg, cw, cb, wd, g, b, pf, wproj, wgate, bgate)


def _ssd_proj_kernel(x_ref, wz_ref, wxbc_ref, wdt_ref, cw_ref, cb_ref, dtb_ref,
                     z_ref, xs_ref, b_ref, c_ref, dt_ref, buf_ref, carry_ref, *,
                     steps_per_batch, col_tile):
    tm = x_ref.shape[0]
    halo = SUBLANES
    kconv = cw_ref.shape[0]

    @pl.when(pl.program_id(0) % steps_per_batch == 0)
    def _():
        carry_ref[...] = jnp.zeros_like(carry_ref)

    xb = x_ref[...].astype(BF16)
    for c0 in range(0, wz_ref.shape[1], col_tile):
        cs = slice(c0, c0 + col_tile)
        z_ref[:, cs] = jnp.dot(xb, wz_ref[:, cs], preferred_element_type=F32).astype(BF16)

    d_inner = xs_ref.shape[1]
    gn = b_ref.shape[1]
    for c0 in range(0, wxbc_ref.shape[1], col_tile):
        cs = slice(c0, c0 + col_tile)
        r = jnp.dot(xb, wxbc_ref[:, cs], preferred_element_type=F32)
        buf_ref[0:halo, :] = carry_ref[:, cs]
        buf_ref[halo:halo + tm, :] = r
        carry_ref[:, cs] = r[tm - halo:tm, :]
        conv = cb_ref[:, cs] + cw_ref[kconv - 1:kconv, cs] * r
        for k in range(kconv - 1):
            off = halo - (kconv - 1) + k
            conv = conv + cw_ref[k:k + 1, cs] * buf_ref[off:off + tm, :]
        act = (conv * _sigmoid(conv)).astype(BF16)
        if c0 < d_inner:
            xs_ref[:, cs] = act
        elif c0 < d_inner + gn:
            b_ref[:, c0 - d_inner:c0 - d_inner + col_tile] = act
        else:
            c_ref[:, c0 - d_inner - gn:c0 - d_inner - gn + col_tile] = act
    dt_ref[...] = _softplus(jnp.dot(xb, wdt_ref[...], preferred_element_type=F32) + dtb_ref[...])


def _ssd_proj(xf, wz, wxbc, wdt, cw, cb, dtb, seq, gn):
    m, d = xf.shape
    d_inner = wz.shape[1]
    tm = min(ROW_TILE, seq)
    col_tile = min(1024, gn)
    row = lambda i: (i, 0)
    out_shape = [jax.ShapeDtypeStruct((m, d_inner), BF16), jax.ShapeDtypeStruct((m, d_inner), BF16),
                 jax.ShapeDtypeStruct((m, gn), BF16), jax.ShapeDtypeStruct((m, gn), BF16),
                 jax.ShapeDtypeStruct((m, LANES), F32)]
    return pl.pallas_call(
        functools.partial(_ssd_proj_kernel, steps_per_batch=seq // tm, col_tile=col_tile),
        grid=(m // tm,),
        in_specs=[pl.BlockSpec((tm, d), row), _resident(wz.shape), _resident(wxbc.shape),
                  _resident(wdt.shape), _resident(cw.shape), _resident(cb.shape),
                  _resident(dtb.shape)],
        out_specs=[pl.BlockSpec((tm, d_inner), row), pl.BlockSpec((tm, d_inner), row),
                   pl.BlockSpec((tm, gn), row), pl.BlockSpec((tm, gn), row),
                   pl.BlockSpec((tm, LANES), row)],
        out_shape=out_shape,
        scratch_shapes=[pltpu.VMEM((tm + SUBLANES, col_tile), F32),
                        pltpu.VMEM((SUBLANES, wxbc.shape[1]), F32)],
        compiler_params=_params("arbitrary"),
        name="ssd_proj",
    )(xf, wz, wxbc, wdt, cw, cb, dtb)


def _expand_heads(cols, lane_head):
    out = cols[0]
    for r in range(1, len(cols)):
        out = jnp.where(lane_head >= r, cols[r], out)
    return out


def _ssd_scan_kernel(xs_ref, b_ref, bt_ref, c_ref, z_ref, dt_ref, dtt_ref, arow_ref, acol_ref,
                     dskip_ref, nw_ref, ltri_ref, utri_ref, y_ref, state_ref, *, heads_per_group):
    q = xs_ref.shape[1]
    n = SSM_STATE
    hp_ = SSM_HEAD_DIM
    gw = heads_per_group * hp_
    ngroups = xs_ref.shape[2] // gw

    @pl.when(pl.program_id(1) == 0)
    def _():
        state_ref[...] = jnp.zeros_like(state_ref)

    dt_col = dt_ref[0]
    dt_row = dtt_ref[0]
    nheads = dt_row.shape[0]
    a_col = dt_col * arow_ref[...]
    pc = jnp.dot(ltri_ref[...], jnp.concatenate(_split3(a_col), axis=1),
                 preferred_element_type=F32)
    acum_col = pc[:, 0:LANES] + pc[:, LANES:2 * LANES] + pc[:, 2 * LANES:3 * LANES]
    a_row = dt_row * acol_ref[...]
    pr = jnp.dot(jnp.concatenate(_split3(a_row), axis=0), utri_ref[...],
                 preferred_element_type=F32)
    acum_row = pr[0:nheads] + pr[nheads:2 * nheads] + pr[2 * nheads:3 * nheads]
    alast = acum_col[q - 1:q, :]
    e_col = jnp.exp(acum_col)
    w_col = jnp.exp(alast - acum_col) * dt_col
    dec_row = jnp.exp(alast)

    tri = (lax.broadcasted_iota(jnp.int32, (q, q), 0) >= lax.broadcasted_iota(jnp.int32, (q, q), 1))
    lane_head = lax.broadcasted_iota(jnp.int32, (1, gw), 1) // hp_

    for g in range(ngroups):
        gs = slice(g * gw, (g + 1) * gw)
        ns = slice(g * n, (g + 1) * n)
        cg = c_ref[0, :, ns]
        btg = bt_ref[0, ns, :]
        xg = xs_ref[0, :, gs].astype(F32)
        heads = range(g * heads_per_group, (g + 1) * heads_per_group)
        cb = jnp.dot(cg, btg, preferred_element_type=F32)
        state = state_ref[g]
        y_inter = jnp.dot(cg, state.astype(BF16), preferred_element_type=F32)
        dt_x = _expand_heads([dt_col[:, h:h + 1] for h in heads], lane_head)
        e_x = _expand_heads([e_col[:, h:h + 1] for h in heads], lane_head)
        w_x = _expand_heads([w_col[:, h:h + 1] for h in heads], lane_head)
        d_x = _expand_heads([dec_row[:, h:h + 1] for h in heads], lane_head)
        xdt = xg * dt_x
        y_g = y_inter * e_x
        for r, h in enumerate(heads):
            seg = acum_col[:, h:h + 1] - acum_row[h:h + 1, :]
            mh = (cb * jnp.exp(jnp.where(tri, seg, -jnp.inf))).astype(BF16)
            xh = jnp.where(lane_head == r, xdt, 0.0).astype(BF16)
            y_g = y_g + jnp.dot(mh, xh, preferred_element_type=F32)
        state_ref[g] = state * d_x + jnp.dot(btg, (xg * w_x).astype(BF16),
                                             preferred_element_type=F32)
        zg = z_ref[0, :, gs].astype(F32)
        yv = (y_g + dskip_ref[:, gs] * xg) * (zg * _sigmoid(zg))
        ms = jnp.mean(yv * yv, axis=-1, keepdims=True)
        y_ref[0, :, gs] = (yv * lax.rsqrt(ms + RMS_EPS) * nw_ref[:, gs]).astype(BF16)


def _ssd_scan(xs, bm, bmt, cm, z, dt, dtt, arow, acol, dskip, nw, ltri, utri, q):
    b, s, d_inner = xs.shape
    gn = bm.shape[2]
    nheads = dtt.shape[1]
    ngroups = gn // SSM_STATE
    heads_per_group = nheads // ngroups
    gw = heads_per_group * SSM_HEAD_DIM
    blk = lambda w: pl.BlockSpec((1, q, w), lambda bi, ci: (bi, ci, 0))
    return pl.pallas_call(
        functools.partial(_ssd_scan_kernel, heads_per_group=heads_per_group),
        grid=(b, s // q),
        in_specs=[blk(d_inner), blk(gn), pl.BlockSpec((1, gn, q), lambda bi, ci: (bi, 0, ci)),
                  blk(gn), blk(d_inner), blk(LANES),
                  pl.BlockSpec((1, nheads, q), lambda bi, ci: (bi, 0, ci)),
                  _resident(arow.shape), _resident(acol.shape), _resident(dskip.shape),
                  _resident(nw.shape), _resident(ltri.shape), _resident(utri.shape)],
        out_specs=blk(d_inner),
        out_shape=jax.ShapeDtypeStruct((b, s, d_inner), BF16),
        scratch_shapes=[pltpu.VMEM((ngroups, SSM_STATE, gw), F32)],
        compiler_params=_params("arbitrary", "arbitrary"),
        name="ssd_scan",
    )(xs, bm, bmt, cm, z, dt, dtt, arow, acol, dskip, nw, ltri, utri)


def _pad_cols(a, width):
    return jnp.pad(a, ((0, 0), (0, width - a.shape[1])))


def _fox_layer(xf, bsz, seq, w_in, b_f, w_out, g, b, alpha):
    m, d = xf.shape
    nh = d // ATTN_HEAD_DIM
    scale = 1.0 / math.sqrt(ATTN_HEAD_DIM)
    w_qkv = jnp.concatenate([w_in[:, :d] * scale, w_in[:, d:3 * d]], axis=1).astype(BF16)
    wft = w_in[:, 3 * d:].T.astype(BF16)
    utri = jnp.triu(jnp.ones((CUMSUM_CHUNK, CUMSUM_CHUNK), F32)).astype(BF16)
    q, k, v, ct = _fox_proj(xf, w_qkv, wft, b_f.reshape(nh, 1), utri, seq)
    blk = min(ATTN_BLOCK, seq)
    nblk = seq // blk
    ccol = ct.T.reshape(bsz, seq, nh)
    crow = ct.reshape(nh // 2, 2, bsz, nblk, blk).transpose(2, 0, 3, 1, 4)
    o = _fox_attn(q.reshape(bsz, seq, d), k.reshape(bsz, seq, d), v.reshape(bsz, seq, d),
                  ccol, crow, blk)
    return _outproj_ln(o.reshape(m, d), w_out.astype(BF16), xf, g.reshape(1, d), b.reshape(1, d),
                       alpha, seq)


def _ssd_layer(xf, bsz, seq, w_in, conv_w, conv_b, dt_bias, a_log, d_skip, norm_w, w_out, g, b,
               alpha):
    m, d = xf.shape
    nheads = dt_bias.shape[0]
    d_inner = nheads * SSM_HEAD_DIM
    gn = SSM_GROUPS * SSM_STATE
    wz = w_in[:, :d_inner].astype(BF16)
    wxbc = w_in[:, d_inner:2 * d_inner + 2 * gn].astype(BF16)
    wdt = _pad_cols(w_in[:, 2 * d_inner + 2 * gn:], LANES).astype(BF16)
    dtb = _pad_cols(dt_bias.reshape(1, nheads), LANES)
    z, xs, bm, cm, dt = _ssd_proj(xf, wz, wxbc, wdt, conv_w, conv_b.reshape(1, -1), dtb, seq, gn)
    a = -jnp.exp(a_log.astype(F32))
    arow = _pad_cols(a.reshape(1, nheads), LANES)
    acol = a.reshape(nheads, 1)
    dskip = jnp.repeat(d_skip, SSM_HEAD_DIM).reshape(1, d_inner)
    q = min(SSD_CHUNK, seq)
    ltri = jnp.tril(jnp.ones((q, q), F32)).astype(BF16)
    utri = jnp.triu(jnp.ones((q, q), F32)).astype(BF16)
    dt3 = dt.reshape(bsz, seq, LANES)
    dtt = dt3[:, :, :nheads].transpose(0, 2, 1)
    bm3 = bm.reshape(bsz, seq, gn)
    y = _ssd_scan(xs.reshape(bsz, seq, d_inner), bm3, bm3.transpose(0, 2, 1),
                  cm.reshape(bsz, seq, gn), z.reshape(bsz, seq, d_inner), dt3, dtt, arow, acol,
                  dskip, norm_w.reshape(1, d_inner), ltri, utri, q)
    return _outproj_ln(y.reshape(m, d_inner), w_out.astype(BF16), xf, g.reshape(1, d),
                       b.reshape(1, d), alpha, seq)


def kernel(x, p, attn_w_in, attn_b_f, attn_w_out, ssm_w_in, ssm_conv_w, ssm_conv_b, ssm_dt_bias, ssm_A_log, ssm_D, ssm_norm_w, ssm_w_out, ln_mix_g, ln_mix_b, ffn_w_up, ffn_conv_w, ffn_conv_b, ffn_w_down, ln_ffn_g, ln_ffn_b, ple_w_proj, ple_w_gate, ple_b_gate):
    bsz, seq, d = x.shape
    depth = p.shape[0]
    n_mixers = 2
    alpha = (2 * depth) ** 0.25
    xf = x.reshape(bsz * seq, d)
    for i in range(depth):
        j = i // n_mixers
        if i % n_mixers == 0:
            xf = _fox_layer(xf, bsz, seq, attn_w_in[j], attn_b_f[j], attn_w_out[j],
                            ln_mix_g[i], ln_mix_b[i], alpha)
        else:
            xf = _ssd_layer(xf, bsz, seq, ssm_w_in[j], ssm_conv_w[j], ssm_conv_b[j],
                            ssm_dt_bias[j], ssm_A_log[j], ssm_D[j], ssm_norm_w[j], ssm_w_out[j],
                            ln_mix_g[i], ln_mix_b[i], alpha)
        f = ffn_conv_w.shape[-1]
        xf = _ffn(xf, ffn_w_up[i][:, :f].astype(BF16), ffn_w_up[i][:, f:].astype(BF16),
                  ffn_conv_w[i], ffn_conv_b[i].reshape(1, f), ffn_w_down[i].astype(BF16),
                  ln_ffn_g[i].reshape(1, d), ln_ffn_b[i].reshape(1, d),
                  p[i].reshape(bsz * seq, -1), ple_w_proj[i].astype(BF16),
                  ple_w_gate[i].astype(BF16), ple_b_gate[i].reshape(1, d), alpha, seq)
    return xf.reshape(bsz, seq, d)
```

```python
import functools
import math

import jax
import jax.numpy as jnp
from jax import lax
from jax.experimental import pallas as pl
from jax.experimental.pallas import tpu as pltpu

F32 = jnp.float32
BF16 = jnp.bfloat16

ATTN_HEAD_DIM = 64
SSM_HEAD_DIM = 64
SSM_GROUPS = 8
SSM_STATE = 128
LN_EPS = 1e-5
RMS_EPS = 1e-5

LANES = 128
SUBLANES = 8
VMEM_LIMIT_BYTES = 56 * 1024 * 1024

ROW_TILE = 512
ATTN_BLOCK = 512
SSD_CHUNK = 256
CUMSUM_CHUNK = 256
MASK_VALUE = -1e30
LOG2E = 1.4426950408889634


def _params(*sem):
    return pltpu.CompilerParams(dimension_semantics=sem, vmem_limit_bytes=VMEM_LIMIT_BYTES)


def _resident(shape):
    nd = len(shape)
    return pl.BlockSpec(shape, lambda *_: (0,) * nd, pipeline_mode=pl.Buffered(1))


def _split3(v):
    hi = v.astype(BF16)
    r1 = v - hi.astype(F32)
    mid = r1.astype(BF16)
    lo = (r1 - mid.astype(F32)).astype(BF16)
    return hi, mid, lo


def _softplus(v):
    return jnp.maximum(v, 0.0) + jnp.log1p(jnp.exp(-jnp.abs(v)))


def _sigmoid(v):
    return 1.0 / (1.0 + jnp.exp(-v))


def _layer_norm(h, g, b):
    mu = jnp.mean(h, axis=-1, keepdims=True)
    d = h - mu
    var = jnp.mean(d * d, axis=-1, keepdims=True)
    return d * lax.rsqrt(var + LN_EPS) * g + b


_NT = (((1,), (1,)), ((), ()))


def _fox_proj_kernel(x_ref, wk_ref, wqt_ref, wvt_ref, wft_ref, bf_ref, ut_ref,
                     k_ref, qt_ref, vt_ref, c_ref, carry_ref, *, steps_per_batch):
    tm, d = x_ref.shape

    @pl.when(pl.program_id(0) % steps_per_batch == 0)
    def _():
        carry_ref[...] = jnp.zeros_like(carry_ref)

    xb = x_ref[...].astype(BF16)
    k_ref[...] = jnp.dot(xb, wk_ref[...], preferred_element_type=F32).astype(BF16)
    qt_ref[0] = lax.dot_general(wqt_ref[...], xb, _NT, preferred_element_type=F32).astype(BF16)
    vt_ref[0, 0] = lax.dot_general(wvt_ref[...], xb, _NT, preferred_element_type=F32).astype(BF16)

    fl = lax.dot_general(wft_ref[...], xb, _NT, preferred_element_type=F32) + bf_ref[...]
    logf = -_softplus(-fl) * LOG2E
    nh = logf.shape[0]
    parts = jnp.concatenate(_split3(logf), axis=0)
    carry = carry_ref[...]
    for j in range(tm // CUMSUM_CHUNK):
        sl = slice(j * CUMSUM_CHUNK, (j + 1) * CUMSUM_CHUNK)
        pj = jnp.dot(parts[:, sl], ut_ref[...], preferred_element_type=F32)
        cj = (pj[0:nh] + pj[nh:2 * nh] + pj[2 * nh:3 * nh]) + carry
        for piece, cpiece in enumerate(_split3(cj)):
            c_ref[piece, :, sl] = cpiece
        carry = cj[:, CUMSUM_CHUNK - 1:CUMSUM_CHUNK]
    carry_ref[...] = carry


def _fox_proj(xf, wk, wqt, wvt, wft, bf_col, utri, bsz, seq, tm):
    m, d = xf.shape
    nh = wft.shape[0]
    nblk = seq // tm
    row = lambda i: (i, 0)
    out_shape = [jax.ShapeDtypeStruct((m, d), BF16), jax.ShapeDtypeStruct((bsz, d, seq), BF16),
                 jax.ShapeDtypeStruct((bsz, nblk, d, tm), BF16),
                 jax.ShapeDtypeStruct((3, nh, m), BF16)]
    return pl.pallas_call(
        functools.partial(_fox_proj_kernel, steps_per_batch=nblk),
        grid=(m // tm,),
        in_specs=[pl.BlockSpec((tm, d), row), _resident(wk.shape), _resident(wqt.shape),
                  _resident(wvt.shape), _resident(wft.shape), _resident(bf_col.shape),
                  _resident(utri.shape)],
        out_specs=[pl.BlockSpec((tm, d), row),
                   pl.BlockSpec((1, d, tm), lambda i: (i // nblk, 0, i % nblk)),
                   pl.BlockSpec((1, 1, d, tm), lambda i: (i // nblk, i % nblk, 0, 0)),
                   pl.BlockSpec((3, nh, tm), lambda i: (0, 0, i))],
        out_shape=out_shape,
        scratch_shapes=[pltpu.VMEM((nh, 1), F32)],
        compiler_params=_params("arbitrary"),
        name="fox_proj",
    )(xf, wk, wqt, wvt, wft, bf_col, utri)


AUG_ROWS = 16
ONES_ROWS = 16


def _fox_attn_kernel(qt_ref, qta_ref, k_ref, ka_ref, vt_ref, o_ref, m_scr, acc_scr, *, blk):
    qi = pl.program_id(2)
    hd = ATTN_HEAD_DIM
    qt2 = qt_ref[0]
    zhead = jnp.zeros((hd, blk), BF16)
    zpad = jnp.zeros((LANES - AUG_ROWS, blk), BF16)
    qts = [jnp.concatenate([qt2[0:hd], zhead, qta_ref[0, 0], zpad], axis=0),
           jnp.concatenate([zhead, qt2[hd:2 * hd], qta_ref[0, 1], zpad], axis=0)]
    ones_rows = jnp.ones((ONES_ROWS, blk), BF16)
    key_id = lax.broadcasted_iota(jnp.int32, (blk, blk), 0)
    qry_id = lax.broadcasted_iota(jnp.int32, (blk, blk), 1)
    m_scr[...] = jnp.full_like(m_scr, MASK_VALUE)
    acc_scr[...] = jnp.zeros_like(acc_scr)

    def step(j, masked):
        start = pl.multiple_of(j * blk, blk)
        kk = jnp.concatenate([k_ref[0, pl.ds(start, blk), :], ka_ref[0, pl.ds(start, blk), :]],
                             axis=1)
        vt = jnp.concatenate([vt_ref[0, j], ones_rows], axis=0)
        for h in range(2):
            s = jnp.dot(kk, qts[h], preferred_element_type=F32)
            if masked:
                s = jnp.where(key_id <= qry_id, s, MASK_VALUE)
            m_old = m_scr[h]
            m_new = jnp.maximum(m_old, jnp.max(s, axis=0, keepdims=True))
            p = jnp.exp2(s - m_new).astype(BF16)
            alpha = jnp.exp2(m_old - m_new)
            acc_scr[h] = alpha * acc_scr[h] + jnp.dot(vt, p, preferred_element_type=F32)
            m_scr[h] = m_new

    def body(j, carry):
        step(j, False)
        return carry

    lax.fori_loop(0, qi, body, 0)
    step(qi, True)
    outs = []
    for h in range(2):
        acc = acc_scr[h]
        outs.append(acc[h * hd:(h + 1) * hd] * (1.0 / acc[2 * hd:2 * hd + 1]))
    o_ref[0] = jnp.concatenate(outs, axis=0).T.astype(BF16)


def _fox_attn(qt, qta, k, ka, vt, blk):
    b, d, s = qt.shape
    nhp = d // LANES
    nblk = s // blk
    return pl.pallas_call(
        functools.partial(_fox_attn_kernel, blk=blk),
        grid=(b, nhp, nblk),
        in_specs=[
            pl.BlockSpec((1, LANES, blk), lambda bi, hp, qi: (bi, hp, qi)),
            pl.BlockSpec((1, 2, AUG_ROWS, blk), lambda bi, hp, qi: (bi, hp, 0, qi)),
            pl.BlockSpec((1, s, LANES), lambda bi, hp, qi: (bi, 0, hp)),
            pl.BlockSpec((1, s, LANES), lambda bi, hp, qi: (bi, 0, hp)),
            pl.BlockSpec((1, nblk, LANES, blk), lambda bi, hp, qi: (bi, 0, hp, 0)),
        ],
        out_specs=pl.BlockSpec((1, blk, LANES), lambda bi, hp, qi: (bi, qi, hp)),
        out_shape=jax.ShapeDtypeStruct((b, s, d), BF16),
        scratch_shapes=[pltpu.VMEM((2, 1, blk), F32),
                        pltpu.VMEM((2, LANES + ONES_ROWS, blk), F32)],
        compiler_params=_params("arbitrary", "arbitrary", "arbitrary"),
        name="fox_attn",
    )(qt, qta, k, ka, vt)


def _outproj_ln_kernel(y_ref, w_ref, x_ref, g_ref, b_ref, o_ref, *, alpha):
    mix = jnp.dot(y_ref[...], w_ref[...], preferred_element_type=F32)
    o_ref[...] = _layer_norm(alpha * x_ref[...] + mix, g_ref[...], b_ref[...])


def _outproj_ln(y, w, xf, g, b, alpha, seq):
    m, d = xf.shape
    kdim = y.shape[1]
    tm = min(ROW_TILE, seq)
    row = lambda i: (i, 0)
    return pl.pallas_call(
        functools.partial(_outproj_ln_kernel, alpha=alpha),
        grid=(m // tm,),
        in_specs=[pl.BlockSpec((tm, kdim), row), _resident(w.shape), pl.BlockSpec((tm, d), row),
                  _resident(g.shape), _resident(b.shape)],
        out_specs=pl.BlockSpec((tm, d), row),
        out_shape=jax.ShapeDtypeStruct((m, d), F32),
        compiler_params=_params("arbitrary"),
        name="outproj_ln",
    )(y, w, xf, g, b)


def _ffn_kernel(x_ref, wu_ref, wg_ref, cw_ref, cb_ref, wd_ref, g_ref, b_ref, p_ref, wproj_ref,
                wgate_ref, bgate_ref, o_ref, gbuf_ref, carry_ref, *, chunks, steps_per_batch, alpha):
    tm = x_ref.shape[0]
    halo = SUBLANES

    @pl.when(pl.program_id(0) % steps_per_batch == 0)
    def _():
        carry_ref[...] = jnp.zeros_like(carry_ref)

    x = x_ref[...]
    xb = x.astype(BF16)
    acc = None
    for c0, cw in chunks:
        cs = slice(c0, c0 + cw)
        u = jnp.dot(xb, wu_ref[:, cs], preferred_element_type=F32)
        g = jnp.dot(xb, wg_ref[:, cs], preferred_element_type=F32)
        gbuf_ref[0:halo, 0:cw] = carry_ref[:, cs]
        gbuf_ref[halo:halo + tm, 0:cw] = g
        carry_ref[:, cs] = g[tm - halo:tm, :]
        conv = (cb_ref[:, cs] + cw_ref[2:3, cs] * g
                + cw_ref[1:2, cs] * gbuf_ref[halo - 1:halo - 1 + tm, 0:cw]
                + cw_ref[0:1, cs] * gbuf_ref[halo - 2:halo - 2 + tm, 0:cw])
        gelu = 0.5 * conv * (1.0 + lax.erf(conv * (1.0 / math.sqrt(2.0))))
        part = jnp.dot((gelu * u).astype(BF16), wd_ref[cs, :], preferred_element_type=F32)
        acc = part if acc is None else acc + part
    x2 = _layer_norm(alpha * x + acc, g_ref[...], b_ref[...])
    gate = _sigmoid(jnp.dot(x2.astype(BF16), wgate_ref[...], preferred_element_type=F32)
                    + bgate_ref[...])
    emb = jnp.dot(p_ref[...].astype(BF16), wproj_ref[...], preferred_element_type=F32)
    o_ref[...] = x2 + gate * emb


def _ffn_chunks(f):
    tile = 2 * LANES
    if f <= 6 * tile or f % tile:
        return ((0, f),)
    first = (f // tile + 1) // 2 * tile
    return ((0, first), (first, f - first))


def _ffn(xf, wu, wg, cw, cb, wd, g, b, pf, wproj, wgate, bgate, alpha, seq):
    m, d = xf.shape
    f = wu.shape[1]
    pdim = pf.shape[1]
    tm = min(ROW_TILE, seq)
    chunks = _ffn_chunks(f)
    wmax = max(c[1] for c in chunks)
    row = lambda i: (i, 0)
    return pl.pallas_call(
        functools.partial(_ffn_kernel, chunks=chunks, steps_per_batch=seq // tm, alpha=alpha),
        grid=(m // tm,),
        in_specs=[pl.BlockSpec((tm, d), row), _resident(wu.shape), _resident(wg.shape),
                  _resident(cw.shape), _resident(cb.shape), _resident(wd.shape),
                  _resident(g.shape), _resident(b.shape), pl.BlockSpec((tm, pdim), row),
                  _resident(wproj.shape), _resident(wgate.shape), _resident(bgate.shape)],
        out_specs=pl.BlockSpec((tm, d), row),
        out_shape=jax.ShapeDtypeStruct((m, d), F32),
        scratch_shapes=[pltpu.VMEM((tm + SUBLANES, wmax), F32), pltpu.VMEM((SUBLANES, f), F32)],
        compiler_params=_params("arbitrary"),
        name="conv_ffn",
    )(xf, wu, wg, cw, cb, wd, g, b, pf, wproj, wgate, bgate)


def _ssd_proj_kernel(x_ref, wz_ref, wxbc_ref, wdt_ref, cw_ref, cb_ref, dtb_ref,
                     z_ref, xs_ref, b_ref, c_ref, dt_ref, buf_ref, carry_ref, *,
                     steps_per_batch, col_tile):
    tm = x_ref.shape[0]
    halo = SUBLANES
    kconv = cw_ref.shape[0]

    @pl.when(pl.program_id(0) % steps_per_batch == 0)
    def _():
        carry_ref[...] = jnp.zeros_like(carry_ref)

    xb = x_ref[...].astype(BF16)
    for c0 in range(0, wz_ref.shape[1], col_tile):
        cs = slice(c0, c0 + col_tile)
        z_ref[:, cs] = jnp.dot(xb, wz_ref[:, cs], preferred_element_type=F32).astype(BF16)

    d_inner = xs_ref.shape[1]
    gn = b_ref.shape[1]
    for c0 in range(0, wxbc_ref.shape[1], col_tile):
        cs = slice(c0, c0 + col_tile)
        r = jnp.dot(xb, wxbc_ref[:, cs], preferred_element_type=F32)
        buf_ref[0:halo, :] = carry_ref[:, cs]
        buf_ref[halo:halo + tm, :] = r
        carry_ref[:, cs] = r[tm - halo:tm, :]
        conv = cb_ref[:, cs] + cw_ref[kconv - 1:kconv, cs] * r
        for k in range(kconv - 1):
            off = halo - (kconv - 1) + k
            conv = conv + cw_ref[k:k + 1, cs] * buf_ref[off:off + tm, :]
        act = (conv * _sigmoid(conv)).astype(BF16)
        if c0 < d_inner:
            xs_ref[:, cs] = act
        elif c0 < d_inner + gn:
            b_ref[:, c0 - d_inner:c0 - d_inner + col_tile] = act
        else:
            c_ref[:, c0 - d_inner - gn:c0 - d_inner - gn + col_tile] = act
    dt_ref[...] = _softplus(jnp.dot(xb, wdt_ref[...], preferred_element_type=F32) + dtb_ref[...])


def _ssd_proj(xf, wz, wxbc, wdt, cw, cb, dtb, seq, gn):
    m, d = xf.shape
    d_inner = wz.shape[1]
    tm = min(ROW_TILE, seq)
    col_tile = min(1024, gn)
    row = lambda i: (i, 0)
    out_shape = [jax.ShapeDtypeStruct((m, d_inner), BF16), jax.ShapeDtypeStruct((m, d_inner), BF16),
                 jax.ShapeDtypeStruct((m, gn), BF16), jax.ShapeDtypeStruct((m, gn), BF16),
                 jax.ShapeDtypeStruct((m, LANES), F32)]
    return pl.pallas_call(
        functools.partial(_ssd_proj_kernel, steps_per_batch=seq // tm, col_tile=col_tile),
        grid=(m // tm,),
        in_specs=[pl.BlockSpec((tm, d), row), _resident(wz.shape), _resident(wxbc.shape),
                  _resident(wdt.shape), _resident(cw.shape), _resident(cb.shape),
                  _resident(dtb.shape)],
        out_specs=[pl.BlockSpec((tm, d_inner), row), pl.BlockSpec((tm, d_inner), row),
                   pl.BlockSpec((tm, gn), row), pl.BlockSpec((tm, gn), row),
                   pl.BlockSpec((tm, LANES), row)],
        out_shape=out_shape,
        scratch_shapes=[pltpu.VMEM((tm + SUBLANES, col_tile), F32),
                        pltpu.VMEM((SUBLANES, wxbc.shape[1]), F32)],
        compiler_params=_params("arbitrary"),
        name="ssd_proj",
    )(xf, wz, wxbc, wdt, cw, cb, dtb)


def _expand_heads(cols, lane_head):
    out = cols[0]
    for r in range(1, len(cols)):
        out = jnp.where(lane_head >= r, cols[r], out)
    return out


def _ssd_scan_kernel(xs_ref, b_ref, bt_ref, c_ref, z_ref, dt_ref, dtt_ref, arow_ref, acol_ref,
                     dskip_ref, nw_ref, ltri_ref, utri_ref, y_ref, state_ref, *, heads_per_group):
    q = xs_ref.shape[1]
    n = SSM_STATE
    hp_ = SSM_HEAD_DIM
    gw = heads_per_group * hp_
    ngroups = xs_ref.shape[2] // gw

    @pl.when(pl.program_id(1) == 0)
    def _():
        state_ref[...] = jnp.zeros_like(state_ref)

    dt_col = dt_ref[0]
    dt_row = dtt_ref[0]
    nheads = dt_row.shape[0]
    a_col = dt_col * arow_ref[...]
    pc = jnp.dot(ltri_ref[...], jnp.concatenate(_split3(a_col), axis=1),
                 preferred_element_type=F32)
    acum_col = pc[:, 0:LANES] + pc[:, LANES:2 * LANES] + pc[:, 2 * LANES:3 * LANES]
    a_row = dt_row * acol_ref[...]
    pr = jnp.dot(jnp.concatenate(_split3(a_row), axis=0), utri_ref[...],
                 preferred_element_type=F32)
    acum_row = pr[0:nheads] + pr[nheads:2 * nheads] + pr[2 * nheads:3 * nheads]
    alast = acum_col[q - 1:q, :]
    e_col = jnp.exp(acum_col)
    w_col = jnp.exp(alast - acum_col) * dt_col
    dec_row = jnp.exp(alast)

    tri = (lax.broadcasted_iota(jnp.int32, (q, q), 0) >= lax.broadcasted_iota(jnp.int32, (q, q), 1))
    lane_head = lax.broadcasted_iota(jnp.int32, (1, gw), 1) // hp_

    for g in range(ngroups):
        gs = slice(g * gw, (g + 1) * gw)
        ns = slice(g * n, (g + 1) * n)
        cg = c_ref[0, :, ns]
        btg = bt_ref[0, ns, :]
        xg = xs_ref[0, :, gs].astype(F32)
        heads = range(g * heads_per_group, (g + 1) * heads_per_group)
        cb = jnp.dot(cg, btg, preferred_element_type=F32)
        state = state_ref[g]
        y_inter = jnp.dot(cg, state.astype(BF16), preferred_element_type=F32)
        dt_x = _expand_heads([dt_col[:, h:h + 1] for h in heads], lane_head)
        e_x = _expand_heads([e_col[:, h:h + 1] for h in heads], lane_head)
        w_x = _expand_heads([w_col[:, h:h + 1] for h in heads], lane_head)
        d_x = _expand_heads([dec_row[:, h:h + 1] for h in heads], lane_head)
        xdt = xg * dt_x
        y_g = y_inter * e_x
        for r, h in enumerate(heads):
            seg = acum_col[:, h:h + 1] - acum_row[h:h + 1, :]
            mh = (cb * jnp.exp(jnp.where(tri, seg, -jnp.inf))).astype(BF16)
            xh = jnp.where(lane_head == r, xdt, 0.0).astype(BF16)
            y_g = y_g + jnp.dot(mh, xh, preferred_element_type=F32)
        state_ref[g] = state * d_x + jnp.dot(btg, (xg * w_x).astype(BF16),
                                             preferred_element_type=F32)
        zg = z_ref[0, :, gs].astype(F32)
        yv = (y_g + dskip_ref[:, gs] * xg) * (zg * _sigmoid(zg))
        ms = jnp.mean(yv * yv, axis=-1, keepdims=True)
        y_ref[0, :, gs] = (yv * lax.rsqrt(ms + RMS_EPS) * nw_ref[:, gs]).astype(BF16)


def _ssd_scan(xs, bm, bmt, cm, z, dt, dtt, arow, acol, dskip, nw, ltri, utri, q):
    b, s, d_inner = xs.shape
    gn = bm.shape[2]
    nheads = dtt.shape[1]
    ngroups = gn // SSM_STATE
    heads_per_group = nheads // ngroups
    gw = heads_per_group * SSM_HEAD_DIM
    blk = lambda w: pl.BlockSpec((1, q, w), lambda bi, ci: (bi, ci, 0))
    return pl.pallas_call(
        functools.partial(_ssd_scan_kernel, heads_per_group=heads_per_group),
        grid=(b, s // q),
        in_specs=[blk(d_inner), blk(gn), pl.BlockSpec((1, gn, q), lambda bi, ci: (bi, 0, ci)),
                  blk(gn), blk(d_inner), blk(LANES),
                  pl.BlockSpec((1, nheads, q), lambda bi, ci: (bi, 0, ci)),
                  _resident(arow.shape), _resident(acol.shape), _resident(dskip.shape),
                  _resident(nw.shape), _resident(ltri.shape), _resident(utri.shape)],
        out_specs=blk(d_inner),
        out_shape=jax.ShapeDtypeStruct((b, s, d_inner), BF16),
        scratch_shapes=[pltpu.VMEM((ngroups, SSM_STATE, gw), F32)],
        compiler_params=_params("arbitrary", "arbitrary"),
        name="ssd_scan",
    )(xs, bm, bmt, cm, z, dt, dtt, arow, acol, dskip, nw, ltri, utri)


def _pad_cols(a, width):
    return jnp.pad(a, ((0, 0), (0, width - a.shape[1])))


def _bias_operands(cparts, bsz, seq):
    _, nh, m = cparts.shape
    pieces = cparts.transpose(1, 0, 2)
    ones = jnp.ones_like(pieces)
    own = jnp.concatenate([ones, pieces], axis=1)
    zero6 = jnp.zeros_like(own)
    even = jnp.concatenate([own, zero6], axis=1)
    odd = jnp.concatenate([zero6, own], axis=1)
    parity = (jnp.arange(nh) % 2 == 1)[:, None, None]
    qrows = jnp.pad(jnp.where(parity, odd, even), ((0, 0), (0, AUG_ROWS - 12), (0, 0)))
    qta = qrows.reshape(nh, AUG_ROWS, bsz, seq).transpose(2, 0, 1, 3)
    kown = jnp.concatenate([-pieces, ones], axis=1)
    kpair = kown.reshape(nh // 2, 12, m)
    klanes = jnp.pad(kpair, ((0, 0), (0, LANES - 12), (0, 0)))
    ka = klanes.transpose(2, 0, 1).reshape(bsz, seq, (nh // 2) * LANES)
    return qta, ka


def _fox_layer(xf, bsz, seq, w_in, b_f, w_out, g, b, alpha):
    m, d = xf.shape
    nh = d // ATTN_HEAD_DIM
    scale = LOG2E / math.sqrt(ATTN_HEAD_DIM)
    wqt = (w_in[:, :d] * scale).T.astype(BF16)
    wk = w_in[:, d:2 * d].astype(BF16)
    wvt = w_in[:, 2 * d:3 * d].T.astype(BF16)
    wft = w_in[:, 3 * d:].T.astype(BF16)
    utri = jnp.triu(jnp.ones((CUMSUM_CHUNK, CUMSUM_CHUNK), F32)).astype(BF16)
    blk = min(ATTN_BLOCK, seq)
    k, qt, vt, cparts = _fox_proj(xf, wk, wqt, wvt, wft, b_f.reshape(nh, 1), utri, bsz, seq, blk)
    qta, ka = _bias_operands(cparts, bsz, seq)
    o = _fox_attn(qt, qta, k.reshape(bsz, seq, d), ka, vt, blk)
    return _outproj_ln(o.reshape(m, d), w_out.astype(BF16), xf, g.reshape(1, d), b.reshape(1, d),
                       alpha, seq)


def _ssd_layer(xf, bsz, seq, w_in, conv_w, conv_b, dt_bias, a_log, d_skip, norm_w, w_out, g, b,
               alpha):
    m, d = xf.shape
    nheads = dt_bias.shape[0]
    d_inner = nheads * SSM_HEAD_DIM
    gn = SSM_GROUPS * SSM_STATE
    wz = w_in[:, :d_inner].astype(BF16)
    wxbc = w_in[:, d_inner:2 * d_inner + 2 * gn].astype(BF16)
    wdt = _pad_cols(w_in[:, 2 * d_inner + 2 * gn:], LANES).astype(BF16)
    dtb = _pad_cols(dt_bias.reshape(1, nheads), LANES)
    z, xs, bm, cm, dt = _ssd_proj(xf, wz, wxbc, wdt, conv_w, conv_b.reshape(1, -1), dtb, seq, gn)
    a = -jnp.exp(a_log.astype(F32))
    arow = _pad_cols(a.reshape(1, nheads), LANES)
    acol = a.reshape(nheads, 1)
    dskip = jnp.repeat(d_skip, SSM_HEAD_DIM).reshape(1, d_inner)
    q = min(SSD_CHUNK, seq)
    ltri = jnp.tril(jnp.ones((q, q), F32)).astype(BF16)
    utri = jnp.triu(jnp.ones((q, q), F32)).astype(BF16)
    dt3 = dt.reshape(bsz, seq, LANES)
    dtt = dt3[:, :, :nheads].transpose(0, 2, 1)
    bm3 = bm.reshape(bsz, seq, gn)
    y = _ssd_scan(xs.reshape(bsz, seq, d_inner), bm3, bm3.transpose(0, 2, 1),
                  cm.reshape(bsz, seq, gn), z.reshape(bsz, seq, d_inner), dt3, dtt, arow, acol,
                  dskip, norm_w.reshape(1, d_inner), ltri, utri, q)
    return _outproj_ln(y.reshape(m, d_inner), w_out.astype(BF16), xf, g.reshape(1, d),
                       b.reshape(1, d), alpha, seq)


def kernel(x, p, attn_w_in, attn_b_f, attn_w_out, ssm_w_in, ssm_conv_w, ssm_conv_b, ssm_dt_bias, ssm_A_log, ssm_D, ssm_norm_w, ssm_w_out, ln_mix_g, ln_mix_b, ffn_w_up, ffn_conv_w, ffn_conv_b, ffn_w_down, ln_ffn_g, ln_ffn_b, ple_w_proj, ple_w_gate, ple_b_gate):
    bsz, seq, d = x.shape
    depth = p.shape[0]
    n_mixers = 2
    alpha = (2 * depth) ** 0.25
    xf = x.reshape(bsz * seq, d)
    for i in range(depth):
        j = i // n_mixers
        if i % n_mixers == 0:
            xf = _fox_layer(xf, bsz, seq, attn_w_in[j], attn_b_f[j], attn_w_out[j],
                            ln_mix_g[i], ln_mix_b[i], alpha)
        else:
            xf = _ssd_layer(xf, bsz, seq, ssm_w_in[j], ssm_conv_w[j], ssm_conv_b[j],
                            ssm_dt_bias[j], ssm_A_log[j], ssm_D[j], ssm_norm_w[j], ssm_w_out[j],
                            ln_mix_g[i], ln_mix_b[i], alpha)
        f = ffn_conv_w.shape[-1]
        xf = _ffn(xf, ffn_w_up[i][:, :f].astype(BF16), ffn_w_up[i][:, f:].astype(BF16),
                  ffn_conv_w[i], ffn_conv_b[i].reshape(1, f), ffn_w_down[i].astype(BF16),
                  ln_ffn_g[i].reshape(1, d), ln_ffn_b[i].reshape(1, d),
                  p[i].reshape(bsz * seq, -1), ple_w_proj[i].astype(BF16),
                  ple_w_gate[i].astype(BF16), ple_b_gate[i].reshape(1, d), alpha, seq)
    return xf.reshape(bsz, seq, d)
```

```python
import functools
import math

import jax
import jax.numpy as jnp
from jax import lax
from jax.experimental import pallas as pl
from jax.experimental.pallas import tpu as pltpu

F32 = jnp.float32
BF16 = jnp.bfloat16

ATTN_HEAD_DIM = 64
SSM_HEAD_DIM = 64
SSM_GROUPS = 8
SSM_STATE = 128
LN_EPS = 1e-5
RMS_EPS = 1e-5

LANES = 128
SUBLANES = 8
VMEM_LIMIT_BYTES = 56 * 1024 * 1024

ROW_TILE = 512
ATTN_BLOCK = 512
SSD_CHUNK = 256
CUMSUM_CHUNK = 256
MASK_VALUE = -1e30
LOG2E = 1.4426950408889634


def _params(*sem):
    return pltpu.CompilerParams(dimension_semantics=sem, vmem_limit_bytes=VMEM_LIMIT_BYTES)


def _resident(shape):
    nd = len(shape)
    return pl.BlockSpec(shape, lambda *_: (0,) * nd, pipeline_mode=pl.Buffered(1))


def _split3(v):
    hi = v.astype(BF16)
    r1 = v - hi.astype(F32)
    mid = r1.astype(BF16)
    lo = (r1 - mid.astype(F32)).astype(BF16)
    return hi, mid, lo


def _softplus(v):
    return jnp.maximum(v, 0.0) + jnp.log1p(jnp.exp(-jnp.abs(v)))


def _sigmoid(v):
    return 1.0 / (1.0 + jnp.exp(-v))


def _layer_norm(h, g, b):
    mu = jnp.mean(h, axis=-1, keepdims=True)
    d = h - mu
    var = jnp.mean(d * d, axis=-1, keepdims=True)
    return d * lax.rsqrt(var + LN_EPS) * g + b


_NT = (((1,), (1,)), ((), ()))


def _fox_proj_kernel(x_ref, wk_ref, wqt_ref, wvt_ref, wft_ref, bf_ref, ut_ref,
                     k_ref, qt_ref, vt_ref, c_ref, carry_ref, *, steps_per_batch):
    tm, d = x_ref.shape

    @pl.when(pl.program_id(0) % steps_per_batch == 0)
    def _():
        carry_ref[...] = jnp.zeros_like(carry_ref)

    xb = x_ref[...].astype(BF16)
    k_ref[...] = jnp.dot(xb, wk_ref[...], preferred_element_type=F32).astype(BF16)
    qt_ref[0] = lax.dot_general(wqt_ref[...], xb, _NT, preferred_element_type=F32).astype(BF16)
    vt_ref[0, 0] = lax.dot_general(wvt_ref[...], xb, _NT, preferred_element_type=F32).astype(BF16)

    fl = lax.dot_general(wft_ref[...], xb, _NT, preferred_element_type=F32) + bf_ref[...]
    logf = -_softplus(-fl) * LOG2E
    nh = logf.shape[0]
    parts = jnp.concatenate(_split3(logf), axis=0)
    carry = carry_ref[...]
    for j in range(tm // CUMSUM_CHUNK):
        sl = slice(j * CUMSUM_CHUNK, (j + 1) * CUMSUM_CHUNK)
        pj = jnp.dot(parts[:, sl], ut_ref[...], preferred_element_type=F32)
        cj = (pj[0:nh] + pj[nh:2 * nh] + pj[2 * nh:3 * nh]) + carry
        for piece, cpiece in enumerate(_split3(cj)):
            c_ref[piece, :, sl] = cpiece
        carry = cj[:, CUMSUM_CHUNK - 1:CUMSUM_CHUNK]
    carry_ref[...] = carry


def _fox_proj(xf, wk, wqt, wvt, wft, bf_col, utri, bsz, seq, tm):
    m, d = xf.shape
    nh = wft.shape[0]
    nblk = seq // tm
    row = lambda i: (i, 0)
    out_shape = [jax.ShapeDtypeStruct((m, d), BF16), jax.ShapeDtypeStruct((bsz, d, seq), BF16),
                 jax.ShapeDtypeStruct((bsz, nblk, d, tm), BF16),
                 jax.ShapeDtypeStruct((3, nh, m), BF16)]
    return pl.pallas_call(
        functools.partial(_fox_proj_kernel, steps_per_batch=nblk),
        grid=(m // tm,),
        in_specs=[pl.BlockSpec((tm, d), row), _resident(wk.shape), _resident(wqt.shape),
                  _resident(wvt.shape), _resident(wft.shape), _resident(bf_col.shape),
                  _resident(utri.shape)],
        out_specs=[pl.BlockSpec((tm, d), row),
                   pl.BlockSpec((1, d, tm), lambda i: (i // nblk, 0, i % nblk)),
                   pl.BlockSpec((1, 1, d, tm), lambda i: (i // nblk, i % nblk, 0, 0)),
                   pl.BlockSpec((3, nh, tm), lambda i: (0, 0, i))],
        out_shape=out_shape,
        scratch_shapes=[pltpu.VMEM((nh, 1), F32)],
        compiler_params=_params("arbitrary"),
        name="fox_proj",
    )(xf, wk, wqt, wvt, wft, bf_col, utri)


AUG_ROWS = 16
ONES_ROWS = 16


def _fox_attn_kernel(qt_ref, qta_ref, k_ref, ka_ref, vt_ref, o_ref, m_scr, acc_scr,
                     sa_scr, sb_scr, ma_scr, mb_scr, *, blk):
    qi = pl.program_id(2)
    hd = ATTN_HEAD_DIM
    qt2 = qt_ref[0]
    zhead = jnp.zeros((hd, blk), BF16)
    zpad = jnp.zeros((LANES - AUG_ROWS, blk), BF16)
    qts = [jnp.concatenate([qt2[0:hd], zhead, qta_ref[0, 0], zpad], axis=0),
           jnp.concatenate([zhead, qt2[hd:2 * hd], qta_ref[0, 1], zpad], axis=0)]
    ones_rows = jnp.ones((ONES_ROWS, blk), BF16)
    key_id = lax.broadcasted_iota(jnp.int32, (blk, blk), 0)
    qry_id = lax.broadcasted_iota(jnp.int32, (blk, blk), 1)
    m_scr[...] = jnp.full_like(m_scr, MASK_VALUE)
    acc_scr[...] = jnp.zeros_like(acc_scr)

    def keys(j):
        start = pl.multiple_of(j * blk, blk)
        return jnp.concatenate([k_ref[0, pl.ds(start, blk), :], ka_ref[0, pl.ds(start, blk), :]],
                               axis=1)

    def values(j):
        return jnp.concatenate([vt_ref[0, j], ones_rows], axis=0)

    def produce(kk, h, s_buf, m_buf):
        s = jnp.dot(kk, qts[h], preferred_element_type=F32)
        s_buf[h] = s
        m_buf[h] = jnp.max(s, axis=0, keepdims=True)

    def consume(vt, h, s_buf, m_buf, masked):
        s = s_buf[h]
        if masked:
            s = jnp.where(key_id <= qry_id, s, MASK_VALUE)
            m_blk = jnp.max(s, axis=0, keepdims=True)
        else:
            m_blk = m_buf[h]
        m_old = m_scr[h]
        m_new = jnp.maximum(m_old, m_blk)
        p = jnp.exp2(s - m_new).astype(BF16)
        alpha = jnp.exp2(m_old - m_new)
        acc_scr[h] = alpha * acc_scr[h] + jnp.dot(vt, p, preferred_element_type=F32)
        m_scr[h] = m_new

    kk0 = keys(0)
    for h in range(2):
        produce(kk0, h, sa_scr, ma_scr)

    def body(t, carry):
        kk1, vt0 = keys(2 * t + 1), values(2 * t)
        for h in range(2):
            produce(kk1, h, sb_scr, mb_scr)
            consume(vt0, h, sa_scr, ma_scr, False)
        kk2, vt1 = keys(2 * t + 2), values(2 * t + 1)
        for h in range(2):
            produce(kk2, h, sa_scr, ma_scr)
            consume(vt1, h, sb_scr, mb_scr, False)
        return carry

    lax.fori_loop(0, qi // 2, body, 0)

    @pl.when(qi % 2 == 0)
    def _():
        vt = values(qi)
        for h in range(2):
            consume(vt, h, sa_scr, ma_scr, True)

    @pl.when(qi % 2 == 1)
    def _():
        kk, vt0, vt1 = keys(qi), values(qi - 1), values(qi)
        for h in range(2):
            produce(kk, h, sb_scr, mb_scr)
            consume(vt0, h, sa_scr, ma_scr, False)
        for h in range(2):
            consume(vt1, h, sb_scr, mb_scr, True)

    outs = []
    for h in range(2):
        acc = acc_scr[h]
        outs.append(acc[h * hd:(h + 1) * hd] * (1.0 / acc[2 * hd:2 * hd + 1]))
    o_ref[0] = jnp.concatenate(outs, axis=0).T.astype(BF16)


def _fox_attn(qt, qta, k, ka, vt, blk):
    b, d, s = qt.shape
    nhp = d // LANES
    nblk = s // blk
    return pl.pallas_call(
        functools.partial(_fox_attn_kernel, blk=blk),
        grid=(b, nhp, nblk),
        in_specs=[
            pl.BlockSpec((1, LANES, blk), lambda bi, hp, qi: (bi, hp, qi)),
            pl.BlockSpec((1, 2, AUG_ROWS, blk), lambda bi, hp, qi: (bi, hp, 0, qi)),
            pl.BlockSpec((1, s, LANES), lambda bi, hp, qi: (bi, 0, hp)),
            pl.BlockSpec((1, s, LANES), lambda bi, hp, qi: (bi, 0, hp)),
            pl.BlockSpec((1, nblk, LANES, blk), lambda bi, hp, qi: (bi, 0, hp, 0)),
        ],
        out_specs=pl.BlockSpec((1, blk, LANES), lambda bi, hp, qi: (bi, qi, hp)),
        out_shape=jax.ShapeDtypeStruct((b, s, d), BF16),
        scratch_shapes=[pltpu.VMEM((2, 1, blk), F32),
                        pltpu.VMEM((2, LANES + ONES_ROWS, blk), F32),
                        pltpu.VMEM((2, blk, blk), F32), pltpu.VMEM((2, blk, blk), F32),
                        pltpu.VMEM((2, 1, blk), F32), pltpu.VMEM((2, 1, blk), F32)],
        compiler_params=_params("arbitrary", "arbitrary", "arbitrary"),
        name="fox_attn",
    )(qt, qta, k, ka, vt)


def _outproj_ln_kernel(y_ref, w_ref, x_ref, g_ref, b_ref, o_ref, *, alpha):
    mix = jnp.dot(y_ref[...], w_ref[...], preferred_element_type=F32)
    o_ref[...] = _layer_norm(alpha * x_ref[...] + mix, g_ref[...], b_ref[...])


def _outproj_ln(y, w, xf, g, b, alpha, seq):
    m, d = xf.shape
    kdim = y.shape[1]
    tm = min(ROW_TILE, seq)
    row = lambda i: (i, 0)
    return pl.pallas_call(
        functools.partial(_outproj_ln_kernel, alpha=alpha),
        grid=(m // tm,),
        in_specs=[pl.BlockSpec((tm, kdim), row), _resident(w.shape), pl.BlockSpec((tm, d), row),
                  _resident(g.shape), _resident(b.shape)],
        out_specs=pl.BlockSpec((tm, d), row),
        out_shape=jax.ShapeDtypeStruct((m, d), F32),
        compiler_params=_params("arbitrary"),
        name="outproj_ln",
    )(y, w, xf, g, b)


def _ffn_kernel(x_ref, wu_ref, wg_ref, cw_ref, cb_ref, wd_ref, g_ref, b_ref, p_ref, wproj_ref,
                wgate_ref, bgate_ref, o_ref, gbuf_ref, carry_ref, *, chunks, steps_per_batch, alpha):
    tm = x_ref.shape[0]
    halo = SUBLANES

    @pl.when(pl.program_id(0) % steps_per_batch == 0)
    def _():
        carry_ref[...] = jnp.zeros_like(carry_ref)

    x = x_ref[...]
    xb = x.astype(BF16)
    acc = None
    for c0, cw in chunks:
        cs = slice(c0, c0 + cw)
        u = jnp.dot(xb, wu_ref[:, cs], preferred_element_type=F32)
        g = jnp.dot(xb, wg_ref[:, cs], preferred_element_type=F32)
        gbuf_ref[0:halo, 0:cw] = carry_ref[:, cs]
        gbuf_ref[halo:halo + tm, 0:cw] = g
        carry_ref[:, cs] = g[tm - halo:tm, :]
        conv = (cb_ref[:, cs] + cw_ref[2:3, cs] * g
                + cw_ref[1:2, cs] * gbuf_ref[halo - 1:halo - 1 + tm, 0:cw]
                + cw_ref[0:1, cs] * gbuf_ref[halo - 2:halo - 2 + tm, 0:cw])
        gelu = 0.5 * conv * (1.0 + lax.erf(conv * (1.0 / math.sqrt(2.0))))
        part = jnp.dot((gelu * u).astype(BF16), wd_ref[cs, :], preferred_element_type=F32)
        acc = part if acc is None else acc + part
    x2 = _layer_norm(alpha * x + acc, g_ref[...], b_ref[...])
    gate = _sigmoid(jnp.dot(x2.astype(BF16), wgate_ref[...], preferred_element_type=F32)
                    + bgate_ref[...])
    emb = jnp.dot(p_ref[...].astype(BF16), wproj_ref[...], preferred_element_type=F32)
    o_ref[...] = x2 + gate * emb


def _ffn_chunks(f):
    tile = 2 * LANES
    if f <= 6 * tile or f % tile:
        return ((0, f),)
    first = (f // tile + 1) // 2 * tile
    return ((0, first), (first, f - first))


def _ffn(xf, wu, wg, cw, cb, wd, g, b, pf, wproj, wgate, bgate, alpha, seq):
    m, d = xf.shape
    f = wu.shape[1]
    pdim = pf.shape[1]
    tm = min(ROW_TILE, seq)
    chunks = _ffn_chunks(f)
    wmax = max(c[1] for c in chunks)
    row = lambda i: (i, 0)
    return pl.pallas_call(
        functools.partial(_ffn_kernel, chunks=chunks, steps_per_batch=seq // tm, alpha=alpha),
        grid=(m // tm,),
        in_specs=[pl.BlockSpec((tm, d), row), _resident(wu.shape), _resident(wg.shape),
                  _resident(cw.shape), _resident(cb.shape), _resident(wd.shape),
                  _resident(g.shape), _resident(b.shape), pl.BlockSpec((tm, pdim), row),
                  _resident(wproj.shape), _resident(wgate.shape), _resident(bgate.shape)],
        out_specs=pl.BlockSpec((tm, d), row),
        out_shape=jax.ShapeDtypeStruct((m, d), F32),
        scratch_shapes=[pltpu.VMEM((tm + SUBLANES, wmax), F32), pltpu.VMEM((SUBLANES, f), F32)],
        compiler_params=_params("arbitrary"),
        name="conv_ffn",
    )(xf, wu, wg, cw, cb, wd, g, b, pf, wproj, wgate, bgate)


def _ssd_proj_kernel(x_ref, wz_ref, wxbc_ref, wdt_ref, cw_ref, cb_ref, dtb_ref,
                     z_ref, xs_ref, b_ref, c_ref, dt_ref, buf_ref, carry_ref, *,
                     steps_per_batch, col_tile):
    tm = x_ref.shape[0]
    halo = SUBLANES
    kconv = cw_ref.shape[0]

    @pl.when(pl.program_id(0) % steps_per_batch == 0)
    def _():
        carry_ref[...] = jnp.zeros_like(carry_ref)

    xb = x_ref[...].astype(BF16)
    for c0 in range(0, wz_ref.shape[1], col_tile):
        cs = slice(c0, c0 + col_tile)
        z_ref[:, cs] = jnp.dot(xb, wz_ref[:, cs], preferred_element_type=F32).astype(BF16)

    d_inner = xs_ref.shape[1]
    gn = b_ref.shape[1]
    for c0 in range(0, wxbc_ref.shape[1], col_tile):
        cs = slice(c0, c0 + col_tile)
        r = jnp.dot(xb, wxbc_ref[:, cs], preferred_element_type=F32)
        buf_ref[0:halo, :] = carry_ref[:, cs]
        buf_ref[halo:halo + tm, :] = r
        carry_ref[:, cs] = r[tm - halo:tm, :]
        conv = cb_ref[:, cs] + cw_ref[kconv - 1:kconv, cs] * r
        for k in range(kconv - 1):
            off = halo - (kconv - 1) + k
            conv = conv + cw_ref[k:k + 1, cs] * buf_ref[off:off + tm, :]
        act = (conv * _sigmoid(conv)).astype(BF16)
        if c0 < d_inner:
            xs_ref[:, cs] = act
        elif c0 < d_inner + gn:
            b_ref[:, c0 - d_inner:c0 - d_inner + col_tile] = act
        else:
            c_ref[:, c0 - d_inner - gn:c0 - d_inner - gn + col_tile] = act
    dt_ref[...] = _softplus(jnp.dot(xb, wdt_ref[...], preferred_element_type=F32) + dtb_ref[...])


def _ssd_proj(xf, wz, wxbc, wdt, cw, cb, dtb, seq, gn):
    m, d = xf.shape
    d_inner = wz.shape[1]
    tm = min(ROW_TILE, seq)
    col_tile = min(1024, gn)
    row = lambda i: (i, 0)
    out_shape = [jax.ShapeDtypeStruct((m, d_inner), BF16), jax.ShapeDtypeStruct((m, d_inner), BF16),
                 jax.ShapeDtypeStruct((m, gn), BF16), jax.ShapeDtypeStruct((m, gn), BF16),
                 jax.ShapeDtypeStruct((m, LANES), F32)]
    return pl.pallas_call(
        functools.partial(_ssd_proj_kernel, steps_per_batch=seq // tm, col_tile=col_tile),
        grid=(m // tm,),
        in_specs=[pl.BlockSpec((tm, d), row), _resident(wz.shape), _resident(wxbc.shape),
                  _resident(wdt.shape), _resident(cw.shape), _resident(cb.shape),
                  _resident(dtb.shape)],
        out_specs=[pl.BlockSpec((tm, d_inner), row), pl.BlockSpec((tm, d_inner), row),
                   pl.BlockSpec((tm, gn), row), pl.BlockSpec((tm, gn), row),
                   pl.BlockSpec((tm, LANES), row)],
        out_shape=out_shape,
        scratch_shapes=[pltpu.VMEM((tm + SUBLANES, col_tile), F32),
                        pltpu.VMEM((SUBLANES, wxbc.shape[1]), F32)],
        compiler_params=_params("arbitrary"),
        name="ssd_proj",
    )(xf, wz, wxbc, wdt, cw, cb, dtb)


def _expand_heads(cols, lane_head):
    out = cols[0]
    for r in range(1, len(cols)):
        out = jnp.where(lane_head >= r, cols[r], out)
    return out


def _ssd_scan_kernel(xs_ref, b_ref, bt_ref, c_ref, z_ref, dt_ref, dtt_ref, arow_ref, acol_ref,
                     dskip_ref, nw_ref, ltri_ref, utri_ref, y_ref, state_ref, *, heads_per_group):
    q = xs_ref.shape[1]
    n = SSM_STATE
    hp_ = SSM_HEAD_DIM
    gw = heads_per_group * hp_
    ngroups = xs_ref.shape[2] // gw

    @pl.when(pl.program_id(1) == 0)
    def _():
        state_ref[...] = jnp.zeros_like(state_ref)

    dt_col = dt_ref[0]
    dt_row = dtt_ref[0]
    nheads = dt_row.shape[0]
    a_col = dt_col * arow_ref[...]
    pc = jnp.dot(ltri_ref[...], jnp.concatenate(_split3(a_col), axis=1),
                 preferred_element_type=F32)
    acum_col = pc[:, 0:LANES] + pc[:, LANES:2 * LANES] + pc[:, 2 * LANES:3 * LANES]
    a_row = dt_row * acol_ref[...]
    pr = jnp.dot(jnp.concatenate(_split3(a_row), axis=0), utri_ref[...],
                 preferred_element_type=F32)
    acum_row = pr[0:nheads] + pr[nheads:2 * nheads] + pr[2 * nheads:3 * nheads]
    alast = acum_col[q - 1:q, :]
    e_col = jnp.exp(acum_col)
    w_col = jnp.exp(alast - acum_col) * dt_col
    dec_row = jnp.exp(alast)

    tri = (lax.broadcasted_iota(jnp.int32, (q, q), 0) >= lax.broadcasted_iota(jnp.int32, (q, q), 1))
    lane_head = lax.broadcasted_iota(jnp.int32, (1, gw), 1) // hp_

    for g in range(ngroups):
        gs = slice(g * gw, (g + 1) * gw)
        ns = slice(g * n, (g + 1) * n)
        cg = c_ref[0, :, ns]
        btg = bt_ref[0, ns, :]
        xg = xs_ref[0, :, gs].astype(F32)
        heads = range(g * heads_per_group, (g + 1) * heads_per_group)
        cb = jnp.dot(cg, btg, preferred_element_type=F32)
        state = state_ref[g]
        y_inter = jnp.dot(cg, state.astype(BF16), preferred_element_type=F32)
        dt_x = _expand_heads([dt_col[:, h:h + 1] for h in heads], lane_head)
        e_x = _expand_heads([e_col[:, h:h + 1] for h in heads], lane_head)
        w_x = _expand_heads([w_col[:, h:h + 1] for h in heads], lane_head)
        d_x = _expand_heads([dec_row[:, h:h + 1] for h in heads], lane_head)
        xdt = xg * dt_x
        y_g = y_inter * e_x
        for r, h in enumerate(heads):
            seg = acum_col[:, h:h + 1] - acum_row[h:h + 1, :]
            mh = (cb * jnp.exp(jnp.where(tri, seg, -jnp.inf))).astype(BF16)
            xh = jnp.where(lane_head == r, xdt, 0.0).astype(BF16)
            y_g = y_g + jnp.dot(mh, xh, preferred_element_type=F32)
        state_ref[g] = state * d_x + jnp.dot(btg, (xg * w_x).astype(BF16),
                                             preferred_element_type=F32)
        zg = z_ref[0, :, gs].astype(F32)
        yv = (y_g + dskip_ref[:, gs] * xg) * (zg * _sigmoid(zg))
        ms = jnp.mean(yv * yv, axis=-1, keepdims=True)
        y_ref[0, :, gs] = (yv * lax.rsqrt(ms + RMS_EPS) * nw_ref[:, gs]).astype(BF16)


def _ssd_scan(xs, bm, bmt, cm, z, dt, dtt, arow, acol, dskip, nw, ltri, utri, q):
    b, s, d_inner = xs.shape
    gn = bm.shape[2]
    nheads = dtt.shape[1]
    ngroups = gn // SSM_STATE
    heads_per_group = nheads // ngroups
    gw = heads_per_group * SSM_HEAD_DIM
    blk = lambda w: pl.BlockSpec((1, q, w), lambda bi, ci: (bi, ci, 0))
    return pl.pallas_call(
        functools.partial(_ssd_scan_kernel, heads_per_group=heads_per_group),
        grid=(b, s // q),
        in_specs=[blk(d_inner), blk(gn), pl.BlockSpec((1, gn, q), lambda bi, ci: (bi, 0, ci)),
                  blk(gn), blk(d_inner), blk(LANES),
                  pl.BlockSpec((1, nheads, q), lambda bi, ci: (bi, 0, ci)),
                  _resident(arow.shape), _resident(acol.shape), _resident(dskip.shape),
                  _resident(nw.shape), _resident(ltri.shape), _resident(utri.shape)],
        out_specs=blk(d_inner),
        out_shape=jax.ShapeDtypeStruct((b, s, d_inner), BF16),
        scratch_shapes=[pltpu.VMEM((ngroups, SSM_STATE, gw), F32)],
        compiler_params=_params("arbitrary", "arbitrary"),
        name="ssd_scan",
    )(xs, bm, bmt, cm, z, dt, dtt, arow, acol, dskip, nw, ltri, utri)


def _pad_cols(a, width):
    return jnp.pad(a, ((0, 0), (0, width - a.shape[1])))


def _bias_operands(cparts, bsz, seq):
    _, nh, m = cparts.shape
    pieces = cparts.transpose(1, 0, 2)
    ones = jnp.ones_like(pieces)
    own = jnp.concatenate([ones, pieces], axis=1)
    zero6 = jnp.zeros_like(own)
    even = jnp.concatenate([own, zero6], axis=1)
    odd = jnp.concatenate([zero6, own], axis=1)
    parity = (jnp.arange(nh) % 2 == 1)[:, None, None]
    qrows = jnp.pad(jnp.where(parity, odd, even), ((0, 0), (0, AUG_ROWS - 12), (0, 0)))
    qta = qrows.reshape(nh, AUG_ROWS, bsz, seq).transpose(2, 0, 1, 3)
    kown = jnp.concatenate([-pieces, ones], axis=1)
    kpair = kown.reshape(nh // 2, 12, m)
    klanes = jnp.pad(kpair, ((0, 0), (0, LANES - 12), (0, 0)))
    ka = klanes.transpose(2, 0, 1).reshape(bsz, seq, (nh // 2) * LANES)
    return qta, ka


def _fox_layer(xf, bsz, seq, w_in, b_f, w_out, g, b, alpha):
    m, d = xf.shape
    nh = d // ATTN_HEAD_DIM
    scale = LOG2E / math.sqrt(ATTN_HEAD_DIM)
    wqt = (w_in[:, :d] * scale).T.astype(BF16)
    wk = w_in[:, d:2 * d].astype(BF16)
    wvt = w_in[:, 2 * d:3 * d].T.astype(BF16)
    wft = w_in[:, 3 * d:].T.astype(BF16)
    utri = jnp.triu(jnp.ones((CUMSUM_CHUNK, CUMSUM_CHUNK), F32)).astype(BF16)
    blk = min(ATTN_BLOCK, seq)
    k, qt, vt, cparts = _fox_proj(xf, wk, wqt, wvt, wft, b_f.reshape(nh, 1), utri, bsz, seq, blk)
    qta, ka = _bias_operands(cparts, bsz, seq)
    o = _fox_attn(qt, qta, k.reshape(bsz, seq, d), ka, vt, blk)
    return _outproj_ln(o.reshape(m, d), w_out.astype(BF16), xf, g.reshape(1, d), b.reshape(1, d),
                       alpha, seq)


def _ssd_layer(xf, bsz, seq, w_in, conv_w, conv_b, dt_bias, a_log, d_skip, norm_w, w_out, g, b,
               alpha):
    m, d = xf.shape
    nheads = dt_bias.shape[0]
    d_inner = nheads * SSM_HEAD_DIM
    gn = SSM_GROUPS * SSM_STATE
    wz = w_in[:, :d_inner].astype(BF16)
    wxbc = w_in[:, d_inner:2 * d_inner + 2 * gn].astype(BF16)
    wdt = _pad_cols(w_in[:, 2 * d_inner + 2 * gn:], LANES).astype(BF16)
    dtb = _pad_cols(dt_bias.reshape(1, nheads), LANES)
    z, xs, bm, cm, dt = _ssd_proj(xf, wz, wxbc, wdt, conv_w, conv_b.reshape(1, -1), dtb, seq, gn)
    a = -jnp.exp(a_log.astype(F32))
    arow = _pad_cols(a.reshape(1, nheads), LANES)
    acol = a.reshape(nheads, 1)
    dskip = jnp.repeat(d_skip, SSM_HEAD_DIM).reshape(1, d_inner)
    q = min(SSD_CHUNK, seq)
    ltri = jnp.tril(jnp.ones((q, q), F32)).astype(BF16)
    utri = jnp.triu(jnp.ones((q, q), F32)).astype(BF16)
    dt3 = dt.reshape(bsz, seq, LANES)
    dtt = dt3[:, :, :nheads].transpose(0, 2, 1)
    bm3 = bm.reshape(bsz, seq, gn)
    y = _ssd_scan(xs.reshape(bsz, seq, d_inner), bm3, bm3.transpose(0, 2, 1),
                  cm.reshape(bsz, seq, gn), z.reshape(bsz, seq, d_inner), dt3, dtt, arow, acol,
                  dskip, norm_w.reshape(1, d_inner), ltri, utri, q)
    return _outproj_ln(y.reshape(m, d_inner), w_out.astype(BF16), xf, g.reshape(1, d),
                       b.reshape(1, d), alpha, seq)


def kernel(x, p, attn_w_in, attn_b_f, attn_w_out, ssm_w_in, ssm_conv_w, ssm_conv_b, ssm_dt_bias, ssm_A_log, ssm_D, ssm_norm_w, ssm_w_out, ln_mix_g, ln_mix_b, ffn_w_up, ffn_conv_w, ffn_conv_b, ffn_w_down, ln_ffn_g, ln_ffn_b, ple_w_proj, ple_w_gate, ple_b_gate):
    bsz, seq, d = x.shape
    depth = p.shape[0]
    n_mixers = 2
    alpha = (2 * depth) ** 0.25
    xf = x.reshape(bsz * seq, d)
    for i in range(depth):
        j = i // n_mixers
        if i % n_mixers == 0:
            xf = _fox_layer(xf, bsz, seq, attn_w_in[j], attn_b_f[j], attn_w_out[j],
                            ln_mix_g[i], ln_mix_b[i], alpha)
        else:
            xf = _ssd_layer(xf, bsz, seq, ssm_w_in[j], ssm_conv_w[j], ssm_conv_b[j],
                            ssm_dt_bias[j], ssm_A_log[j], ssm_D[j], ssm_norm_w[j], ssm_w_out[j],
                            ln_mix_g[i], ln_mix_b[i], alpha)
        f = ffn_conv_w.shape[-1]
        xf = _ffn(xf, ffn_w_up[i][:, :f].astype(BF16), ffn_w_up[i][:, f:].astype(BF16),
                  ffn_conv_w[i], ffn_conv_b[i].reshape(1, f), ffn_w_down[i].astype(BF16),
                  ln_ffn_g[i].reshape(1, d), ln_ffn_b[i].reshape(1, d),
                  p[i].reshape(bsz * seq, -1), ple_w_proj[i].astype(BF16),
                  ple_w_gate[i].astype(BF16), ple_b_gate[i].reshape(1, d), alpha, seq)
    return xf.reshape(bsz, seq, d)
```

```python
import functools
import math

import jax
import jax.numpy as jnp
from jax import lax
from jax.experimental import pallas as pl
from jax.experimental.pallas import tpu as pltpu

F32 = jnp.float32
BF16 = jnp.bfloat16

ATTN_HEAD_DIM = 64
SSM_HEAD_DIM = 64
SSM_GROUPS = 8
SSM_STATE = 128
LN_EPS = 1e-5
RMS_EPS = 1e-5

LANES = 128
SUBLANES = 8
VMEM_LIMIT_BYTES = 56 * 1024 * 1024

ROW_TILE = 512
ATTN_BLOCK = 512
SSD_CHUNK = 256
CUMSUM_CHUNK = 256
MASK_VALUE = -1e30
LOG2E = 1.4426950408889634


def _params(*sem):
    return pltpu.CompilerParams(dimension_semantics=sem, vmem_limit_bytes=VMEM_LIMIT_BYTES)


def _resident(shape):
    nd = len(shape)
    return pl.BlockSpec(shape, lambda *_: (0,) * nd, pipeline_mode=pl.Buffered(1))


def _split3(v):
    hi = v.astype(BF16)
    r1 = v - hi.astype(F32)
    mid = r1.astype(BF16)
    lo = (r1 - mid.astype(F32)).astype(BF16)
    return hi, mid, lo


def _softplus(v):
    return jnp.maximum(v, 0.0) + jnp.log1p(jnp.exp(-jnp.abs(v)))


def _sigmoid(v):
    return 1.0 / (1.0 + jnp.exp2(v * (-LOG2E)))


def _layer_norm(h, g, b):
    mu = jnp.mean(h, axis=-1, keepdims=True)
    d = h - mu
    var = jnp.mean(d * d, axis=-1, keepdims=True)
    return d * lax.rsqrt(var + LN_EPS) * g + b


_NT = (((1,), (1,)), ((), ()))


def _fox_proj_kernel(x_ref, wk_ref, wqt_ref, wvt_ref, wft_ref, bf_ref, ut_ref,
                     k_ref, qt_ref, vt_ref, c_ref, carry_ref, *, steps_per_batch):
    tm, d = x_ref.shape

    @pl.when(pl.program_id(0) % steps_per_batch == 0)
    def _():
        carry_ref[...] = jnp.zeros_like(carry_ref)

    xb = x_ref[...].astype(BF16)
    k_ref[...] = jnp.dot(xb, wk_ref[...], preferred_element_type=F32).astype(BF16)
    qt_ref[0] = lax.dot_general(wqt_ref[...], xb, _NT, preferred_element_type=F32).astype(BF16)
    vt_ref[0, 0] = lax.dot_general(wvt_ref[...], xb, _NT, preferred_element_type=F32).astype(BF16)

    fl = lax.dot_general(wft_ref[...], xb, _NT, preferred_element_type=F32) + bf_ref[...]
    logf = -_softplus(-fl) * LOG2E
    nh = logf.shape[0]
    parts = jnp.concatenate(_split3(logf), axis=0)
    carry = carry_ref[...]
    for j in range(tm // CUMSUM_CHUNK):
        sl = slice(j * CUMSUM_CHUNK, (j + 1) * CUMSUM_CHUNK)
        pj = jnp.dot(parts[:, sl], ut_ref[...], preferred_element_type=F32)
        cj = (pj[0:nh] + pj[nh:2 * nh] + pj[2 * nh:3 * nh]) + carry
        for piece, cpiece in enumerate(_split3(cj)):
            c_ref[piece, :, sl] = cpiece
        carry = cj[:, CUMSUM_CHUNK - 1:CUMSUM_CHUNK]
    carry_ref[...] = carry


def _fox_proj(xf, wk, wqt, wvt, wft, bf_col, utri, bsz, seq, tm):
    m, d = xf.shape
    nh = wft.shape[0]
    nblk = seq // tm
    row = lambda i: (i, 0)
    out_shape = [jax.ShapeDtypeStruct((m, d), BF16), jax.ShapeDtypeStruct((bsz, d, seq), BF16),
                 jax.ShapeDtypeStruct((bsz, nblk, d, tm), BF16),
                 jax.ShapeDtypeStruct((3, nh, m), BF16)]
    return pl.pallas_call(
        functools.partial(_fox_proj_kernel, steps_per_batch=nblk),
        grid=(m // tm,),
        in_specs=[pl.BlockSpec((tm, d), row), _resident(wk.shape), _resident(wqt.shape),
                  _resident(wvt.shape), _resident(wft.shape), _resident(bf_col.shape),
                  _resident(utri.shape)],
        out_specs=[pl.BlockSpec((tm, d), row),
                   pl.BlockSpec((1, d, tm), lambda i: (i // nblk, 0, i % nblk)),
                   pl.BlockSpec((1, 1, d, tm), lambda i: (i // nblk, i % nblk, 0, 0)),
                   pl.BlockSpec((3, nh, tm), lambda i: (0, 0, i))],
        out_shape=out_shape,
        scratch_shapes=[pltpu.VMEM((nh, 1), F32)],
        compiler_params=_params("arbitrary"),
        name="fox_proj",
    )(xf, wk, wqt, wvt, wft, bf_col, utri)


AUG_ROWS = 16
ONES_ROWS = 16


def _fox_attn_kernel(qt_ref, qta_ref, k_ref, ka_ref, vt_ref, o_ref, m_scr, acc_scr,
                     sa_scr, sb_scr, ma_scr, mb_scr, *, blk):
    qi = pl.program_id(2)
    hd = ATTN_HEAD_DIM
    qt2 = qt_ref[0]
    zhead = jnp.zeros((hd, blk), BF16)
    zpad = jnp.zeros((LANES - AUG_ROWS, blk), BF16)
    qts = [jnp.concatenate([qt2[0:hd], zhead, qta_ref[0, 0], zpad], axis=0),
           jnp.concatenate([zhead, qt2[hd:2 * hd], qta_ref[0, 1], zpad], axis=0)]
    ones_rows = jnp.ones((ONES_ROWS, blk), BF16)
    key_id = lax.broadcasted_iota(jnp.int32, (blk, blk), 0)
    qry_id = lax.broadcasted_iota(jnp.int32, (blk, blk), 1)
    m_scr[...] = jnp.full_like(m_scr, MASK_VALUE)
    acc_scr[...] = jnp.zeros_like(acc_scr)

    def keys(j):
        start = pl.multiple_of(j * blk, blk)
        return jnp.concatenate([k_ref[0, pl.ds(start, blk), :], ka_ref[0, pl.ds(start, blk), :]],
                               axis=1)

    def values(j):
        return jnp.concatenate([vt_ref[0, j], ones_rows], axis=0)

    def produce(kk, h, s_buf, m_buf):
        s = jnp.dot(kk, qts[h], preferred_element_type=F32)
        s_buf[h] = s
        m_buf[h] = jnp.max(s, axis=0, keepdims=True)

    def consume(vt, h, s_buf, m_buf, masked):
        s = s_buf[h]
        if masked:
            s = jnp.where(key_id <= qry_id, s, MASK_VALUE)
            m_blk = jnp.max(s, axis=0, keepdims=True)
        else:
            m_blk = m_buf[h]
        m_old = m_scr[h]
        m_new = jnp.maximum(m_old, m_blk)
        p = jnp.exp2(s - m_new).astype(BF16)
        alpha = jnp.exp2(m_old - m_new)
        acc_scr[h] = alpha * acc_scr[h] + jnp.dot(vt, p, preferred_element_type=F32)
        m_scr[h] = m_new

    kk0 = keys(0)
    for h in range(2):
        produce(kk0, h, sa_scr, ma_scr)

    def body(t, carry):
        kk1, vt0 = keys(2 * t + 1), values(2 * t)
        for h in range(2):
            produce(kk1, h, sb_scr, mb_scr)
            consume(vt0, h, sa_scr, ma_scr, False)
        kk2, vt1 = keys(2 * t + 2), values(2 * t + 1)
        for h in range(2):
            produce(kk2, h, sa_scr, ma_scr)
            consume(vt1, h, sb_scr, mb_scr, False)
        return carry

    lax.fori_loop(0, qi // 2, body, 0)

    @pl.when(qi % 2 == 0)
    def _():
        vt = values(qi)
        for h in range(2):
            consume(vt, h, sa_scr, ma_scr, True)

    @pl.when(qi % 2 == 1)
    def _():
        kk, vt0, vt1 = keys(qi), values(qi - 1), values(qi)
        for h in range(2):
            produce(kk, h, sb_scr, mb_scr)
            consume(vt0, h, sa_scr, ma_scr, False)
        for h in range(2):
            consume(vt1, h, sb_scr, mb_scr, True)

    outs = []
    for h in range(2):
        acc = acc_scr[h]
        outs.append(acc[h * hd:(h + 1) * hd] * (1.0 / acc[2 * hd:2 * hd + 1]))
    o_ref[0] = jnp.concatenate(outs, axis=0).T.astype(BF16)


def _fox_attn(qt, qta, k, ka, vt, blk):
    b, d, s = qt.shape
    nhp = d // LANES
    nblk = s // blk
    return pl.pallas_call(
        functools.partial(_fox_attn_kernel, blk=blk),
        grid=(b, nhp, nblk),
        in_specs=[
            pl.BlockSpec((1, LANES, blk), lambda bi, hp, qi: (bi, hp, qi)),
            pl.BlockSpec((1, 2, AUG_ROWS, blk), lambda bi, hp, qi: (bi, hp, 0, qi)),
            pl.BlockSpec((1, s, LANES), lambda bi, hp, qi: (bi, 0, hp)),
            pl.BlockSpec((1, s, LANES), lambda bi, hp, qi: (bi, 0, hp)),
            pl.BlockSpec((1, nblk, LANES, blk), lambda bi, hp, qi: (bi, 0, hp, 0)),
        ],
        out_specs=pl.BlockSpec((1, blk, LANES), lambda bi, hp, qi: (bi, qi, hp)),
        out_shape=jax.ShapeDtypeStruct((b, s, d), BF16),
        scratch_shapes=[pltpu.VMEM((2, 1, blk), F32),
                        pltpu.VMEM((2, LANES + ONES_ROWS, blk), F32),
                        pltpu.VMEM((2, blk, blk), F32), pltpu.VMEM((2, blk, blk), F32),
                        pltpu.VMEM((2, 1, blk), F32), pltpu.VMEM((2, 1, blk), F32)],
        compiler_params=_params("arbitrary", "arbitrary", "arbitrary"),
        name="fox_attn",
    )(qt, qta, k, ka, vt)


def _outproj_ln_kernel(y_ref, w_ref, x_ref, g_ref, b_ref, o_ref, *, alpha):
    mix = jnp.dot(y_ref[...], w_ref[...], preferred_element_type=F32)
    o_ref[...] = _layer_norm(alpha * x_ref[...] + mix, g_ref[...], b_ref[...])


def _outproj_ln(y, w, xf, g, b, alpha, seq):
    m, d = xf.shape
    kdim = y.shape[1]
    tm = min(ROW_TILE, seq)
    row = lambda i: (i, 0)
    return pl.pallas_call(
        functools.partial(_outproj_ln_kernel, alpha=alpha),
        grid=(m // tm,),
        in_specs=[pl.BlockSpec((tm, kdim), row), _resident(w.shape), pl.BlockSpec((tm, d), row),
                  _resident(g.shape), _resident(b.shape)],
        out_specs=pl.BlockSpec((tm, d), row),
        out_shape=jax.ShapeDtypeStruct((m, d), F32),
        compiler_params=_params("arbitrary"),
        name="outproj_ln",
    )(y, w, xf, g, b)


def _ffn_kernel(x_ref, wu_ref, wg_ref, cw_ref, cb_ref, wd_ref, g_ref, b_ref, p_ref, wproj_ref,
                wgate_ref, bgate_ref, o_ref, gbuf_ref, carry_ref, *, chunks, steps_per_batch, alpha):
    tm = x_ref.shape[0]
    halo = SUBLANES

    @pl.when(pl.program_id(0) % steps_per_batch == 0)
    def _():
        carry_ref[...] = jnp.zeros_like(carry_ref)

    x = x_ref[...]
    xb = x.astype(BF16)
    acc = None
    for c0, cw in chunks:
        cs = slice(c0, c0 + cw)
        u = jnp.dot(xb, wu_ref[:, cs], preferred_element_type=F32)
        g = jnp.dot(xb, wg_ref[:, cs], preferred_element_type=F32)
        gbuf_ref[0:halo, 0:cw] = carry_ref[:, cs]
        gbuf_ref[halo:halo + tm, 0:cw] = g
        carry_ref[:, cs] = g[tm - halo:tm, :]
        conv = (cb_ref[:, cs] + cw_ref[2:3, cs] * g
                + cw_ref[1:2, cs] * gbuf_ref[halo - 1:halo - 1 + tm, 0:cw]
                + cw_ref[0:1, cs] * gbuf_ref[halo - 2:halo - 2 + tm, 0:cw])
        gelu = 0.5 * conv * (1.0 + lax.erf(conv * (1.0 / math.sqrt(2.0))))
        part = jnp.dot((gelu * u).astype(BF16), wd_ref[cs, :], preferred_element_type=F32)
        acc = part if acc is None else acc + part
    x2 = _layer_norm(alpha * x + acc, g_ref[...], b_ref[...])
    gate = _sigmoid(jnp.dot(x2.astype(BF16), wgate_ref[...], preferred_element_type=F32)
                    + bgate_ref[...])
    emb = jnp.dot(p_ref[...].astype(BF16), wproj_ref[...], preferred_element_type=F32)
    o_ref[...] = x2 + gate * emb


def _ffn_chunks(f):
    tile = 2 * LANES
    if f <= 6 * tile or f % tile:
        return ((0, f),)
    first = (f // tile + 1) // 2 * tile
    return ((0, first), (first, f - first))


def _ffn(xf, wu, wg, cw, cb, wd, g, b, pf, wproj, wgate, bgate, alpha, seq):
    m, d = xf.shape
    f = wu.shape[1]
    pdim = pf.shape[1]
    tm = min(ROW_TILE, seq)
    chunks = _ffn_chunks(f)
    wmax = max(c[1] for c in chunks)
    row = lambda i: (i, 0)
    return pl.pallas_call(
        functools.partial(_ffn_kernel, chunks=chunks, steps_per_batch=seq // tm, alpha=alpha),
        grid=(m // tm,),
        in_specs=[pl.BlockSpec((tm, d), row), _resident(wu.shape), _resident(wg.shape),
                  _resident(cw.shape), _resident(cb.shape), _resident(wd.shape),
                  _resident(g.shape), _resident(b.shape), pl.BlockSpec((tm, pdim), row),
                  _resident(wproj.shape), _resident(wgate.shape), _resident(bgate.shape)],
        out_specs=pl.BlockSpec((tm, d), row),
        out_shape=jax.ShapeDtypeStruct((m, d), F32),
        scratch_shapes=[pltpu.VMEM((tm + SUBLANES, wmax), F32), pltpu.VMEM((SUBLANES, f), F32)],
        compiler_params=_params("arbitrary"),
        name="conv_ffn",
    )(xf, wu, wg, cw, cb, wd, g, b, pf, wproj, wgate, bgate)


def _ssd_proj_kernel(x_ref, wz_ref, wxbc_ref, wdt_ref, cw_ref, cb_ref, dtb_ref,
                     z_ref, xs_ref, b_ref, c_ref, dt_ref, buf0_ref, buf1_ref, carry_ref, *,
                     steps_per_batch, col_tile):
    tm = x_ref.shape[0]
    halo = SUBLANES
    kconv = cw_ref.shape[0]
    bufs = (buf0_ref, buf1_ref)

    @pl.when(pl.program_id(0) % steps_per_batch == 0)
    def _():
        carry_ref[...] = jnp.zeros_like(carry_ref)

    xb = x_ref[...].astype(BF16)
    d_inner = xs_ref.shape[1]
    gn = b_ref.shape[1]
    nconv = wxbc_ref.shape[1] // col_tile
    z_chunks = list(range(0, wz_ref.shape[1], col_tile))

    def project(i):
        cs = slice(i * col_tile, (i + 1) * col_tile)
        buf = bufs[i % 2]
        r = jnp.dot(xb, wxbc_ref[:, cs], preferred_element_type=F32)
        buf[0:halo, :] = carry_ref[:, cs]
        buf[halo:halo + tm, :] = r
        carry_ref[:, cs] = r[tm - halo:tm, :]

    def project_z():
        if z_chunks:
            zs = slice(z_chunks[0], z_chunks.pop(0) + col_tile)
            z_ref[:, zs] = jnp.dot(xb, wz_ref[:, zs], preferred_element_type=F32).astype(BF16)

    def conv_act(i):
        c0 = i * col_tile
        cs = slice(c0, c0 + col_tile)
        buf = bufs[i % 2]
        conv = cb_ref[:, cs] + cw_ref[kconv - 1:kconv, cs] * buf[halo:halo + tm, :]
        for k in range(kconv - 1):
            off = halo - (kconv - 1) + k
            conv = conv + cw_ref[k:k + 1, cs] * buf[off:off + tm, :]
        act = (conv * _sigmoid(conv)).astype(BF16)
        if c0 < d_inner:
            xs_ref[:, cs] = act
        elif c0 < d_inner + gn:
            b_ref[:, c0 - d_inner:c0 - d_inner + col_tile] = act
        else:
            c_ref[:, c0 - d_inner - gn:c0 - d_inner - gn + col_tile] = act

    project(0)
    for i in range(nconv):
        if i + 1 < nconv:
            project(i + 1)
        else:
            project_z()
        conv_act(i)
    while z_chunks:
        project_z()
    dt_ref[...] = _softplus(jnp.dot(xb, wdt_ref[...], preferred_element_type=F32) + dtb_ref[...])


def _ssd_proj(xf, wz, wxbc, wdt, cw, cb, dtb, seq, gn):
    m, d = xf.shape
    d_inner = wz.shape[1]
    tm = min(ROW_TILE, seq)
    col_tile = min(1024, gn)
    row = lambda i: (i, 0)
    out_shape = [jax.ShapeDtypeStruct((m, d_inner), BF16), jax.ShapeDtypeStruct((m, d_inner), BF16),
                 jax.ShapeDtypeStruct((m, gn), BF16), jax.ShapeDtypeStruct((m, gn), BF16),
                 jax.ShapeDtypeStruct((m, LANES), F32)]
    return pl.pallas_call(
        functools.partial(_ssd_proj_kernel, steps_per_batch=seq // tm, col_tile=col_tile),
        grid=(m // tm,),
        in_specs=[pl.BlockSpec((tm, d), row), _resident(wz.shape), _resident(wxbc.shape),
                  _resident(wdt.shape), _resident(cw.shape), _resident(cb.shape),
                  _resident(dtb.shape)],
        out_specs=[pl.BlockSpec((tm, d_inner), row), pl.BlockSpec((tm, d_inner), row),
                   pl.BlockSpec((tm, gn), row), pl.BlockSpec((tm, gn), row),
                   pl.BlockSpec((tm, LANES), row)],
        out_shape=out_shape,
        scratch_shapes=[pltpu.VMEM((tm + SUBLANES, col_tile), F32),
                        pltpu.VMEM((tm + SUBLANES, col_tile), F32),
                        pltpu.VMEM((SUBLANES, wxbc.shape[1]), F32)],
        compiler_params=_params("arbitrary"),
        name="ssd_proj",
    )(xf, wz, wxbc, wdt, cw, cb, dtb)


def _expand_heads(cols, lane_head):
    out = cols[0]
    for r in range(1, len(cols)):
        out = jnp.where(lane_head >= r, cols[r], out)
    return out


def _ssd_scan_kernel(xs_ref, bt_ref, c_ref, z_ref, dt_ref, dtt_ref, arow_ref, acol_ref,
                     dskip_ref, nw_ref, ltri_ref, utri_ref, y_ref, state_ref, *, heads_per_group):
    q = xs_ref.shape[1]
    n = SSM_STATE
    hp_ = SSM_HEAD_DIM
    gw = heads_per_group * hp_
    ngroups = xs_ref.shape[2] // gw

    @pl.when(pl.program_id(1) == 0)
    def _():
        state_ref[...] = jnp.zeros_like(state_ref)

    dt_col = dt_ref[0]
    dt_row = dtt_ref[0]
    nheads = dt_row.shape[0]
    a_col = dt_col * (arow_ref[...] * LOG2E)
    pc = jnp.dot(ltri_ref[...], jnp.concatenate(_split3(a_col), axis=1),
                 preferred_element_type=F32)
    acum_col = pc[:, 0:LANES] + pc[:, LANES:2 * LANES] + pc[:, 2 * LANES:3 * LANES]
    a_row = dt_row * (acol_ref[...] * LOG2E)
    pr = jnp.dot(jnp.concatenate(_split3(a_row), axis=0), utri_ref[...],
                 preferred_element_type=F32)
    acum_row = pr[0:nheads] + pr[nheads:2 * nheads] + pr[2 * nheads:3 * nheads]
    dec_row = jnp.exp2(acum_col[q - 1:q, :])
    w_row = jnp.exp2(acum_row[:, q - 1:q] - acum_row) * dt_row
    src_row = acum_row - jnp.log2(dt_row)

    tri = (lax.broadcasted_iota(jnp.int32, (q, q), 0) >= lax.broadcasted_iota(jnp.int32, (q, q), 1))
    lane_head = lax.broadcasted_iota(jnp.int32, (1, gw), 1) // hp_
    low_half = lax.broadcasted_iota(jnp.int32, (1, LANES), 1) < hp_

    for g in range(ngroups):
        gs = slice(g * gw, (g + 1) * gw)
        ns = slice(g * n, (g + 1) * n)
        cg = c_ref[0, :, ns]
        btg = bt_ref[0, ns, :]
        btg_f = btg.astype(F32)
        xg_b = xs_ref[0, :, gs]
        heads = range(g * heads_per_group, (g + 1) * heads_per_group)
        cb = jnp.dot(cg, btg, preferred_element_type=F32)
        state = state_ref[g]
        y_inter = jnp.dot(cg, state.astype(BF16), preferred_element_type=F32)
        e_cols, y_intra, st_new = [], None, None
        for r, h in enumerate(heads):
            acol_b = jnp.broadcast_to(acum_col[:, h:h + 1], (q, LANES))
            e_cols.append(jnp.exp2(acol_b))
            seg = jnp.concatenate([acol_b] * (q // LANES), axis=1) - src_row[h:h + 1, :]
            mh = (cb * jnp.exp2(jnp.where(tri, seg, -jnp.inf))).astype(BF16)
            xh = jnp.where(lane_head == r, xg_b, jnp.zeros_like(xg_b))
            part = jnp.dot(mh, xh, preferred_element_type=F32)
            y_intra = part if y_intra is None else y_intra + part
            btw = (btg_f * w_row[h:h + 1, :]).astype(BF16)
            part = jnp.dot(btw, xh, preferred_element_type=F32)
            st_new = part if st_new is None else st_new + part
        e_x = jnp.concatenate([jnp.where(low_half, e_cols[2 * i], e_cols[2 * i + 1])
                               for i in range(gw // LANES)], axis=1)
        y_g = y_inter * e_x + y_intra
        d_x = _expand_heads([dec_row[:, h:h + 1] for h in heads], lane_head)
        state_ref[g] = state * d_x + st_new
        xg = xg_b.astype(F32)
        zg = z_ref[0, :, gs].astype(F32)
        yv = (y_g + dskip_ref[:, gs] * xg) * (zg * _sigmoid(zg))
        ms = jnp.mean(yv * yv, axis=-1, keepdims=True)
        y_ref[0, :, gs] = (yv * lax.rsqrt(ms + RMS_EPS) * nw_ref[:, gs]).astype(BF16)


def _ssd_scan(xs, bmt, cm, z, dt, dtt, arow, acol, dskip, nw, ltri, utri, q):
    b, s, d_inner = xs.shape
    gn = cm.shape[2]
    nheads = dtt.shape[1]
    ngroups = gn // SSM_STATE
    heads_per_group = nheads // ngroups
    gw = heads_per_group * SSM_HEAD_DIM
    blk = lambda w: pl.BlockSpec((1, q, w), lambda bi, ci: (bi, ci, 0))
    return pl.pallas_call(
        functools.partial(_ssd_scan_kernel, heads_per_group=heads_per_group),
        grid=(b, s // q),
        in_specs=[blk(d_inner), pl.BlockSpec((1, gn, q), lambda bi, ci: (bi, 0, ci)),
                  blk(gn), blk(d_inner), blk(LANES),
                  pl.BlockSpec((1, nheads, q), lambda bi, ci: (bi, 0, ci)),
                  _resident(arow.shape), _resident(acol.shape), _resident(dskip.shape),
                  _resident(nw.shape), _resident(ltri.shape), _resident(utri.shape)],
        out_specs=blk(d_inner),
        out_shape=jax.ShapeDtypeStruct((b, s, d_inner), BF16),
        scratch_shapes=[pltpu.VMEM((ngroups, SSM_STATE, gw), F32)],
        compiler_params=_params("arbitrary", "arbitrary"),
        name="ssd_scan",
    )(xs, bmt, cm, z, dt, dtt, arow, acol, dskip, nw, ltri, utri)


def _pad_cols(a, width):
    return jnp.pad(a, ((0, 0), (0, width - a.shape[1])))


def _bias_operands(cparts, bsz, seq):
    _, nh, m = cparts.shape
    pieces = cparts.transpose(1, 0, 2)
    ones = jnp.ones_like(pieces)
    own = jnp.concatenate([ones, pieces], axis=1)
    zero6 = jnp.zeros_like(own)
    even = jnp.concatenate([own, zero6], axis=1)
    odd = jnp.concatenate([zero6, own], axis=1)
    parity = (jnp.arange(nh) % 2 == 1)[:, None, None]
    qrows = jnp.pad(jnp.where(parity, odd, even), ((0, 0), (0, AUG_ROWS - 12), (0, 0)))
    qta = qrows.reshape(nh, AUG_ROWS, bsz, seq).transpose(2, 0, 1, 3)
    kown = jnp.concatenate([-pieces, ones], axis=1)
    kpair = kown.reshape(nh // 2, 12, m)
    klanes = jnp.pad(kpair, ((0, 0), (0, LANES - 12), (0, 0)))
    ka = klanes.transpose(2, 0, 1).reshape(bsz, seq, (nh // 2) * LANES)
    return qta, ka


def _fox_layer(xf, bsz, seq, w_in, b_f, w_out, g, b, alpha):
    m, d = xf.shape
    nh = d // ATTN_HEAD_DIM
    scale = LOG2E / math.sqrt(ATTN_HEAD_DIM)
    wqt = (w_in[:, :d] * scale).T.astype(BF16)
    wk = w_in[:, d:2 * d].astype(BF16)
    wvt = w_in[:, 2 * d:3 * d].T.astype(BF16)
    wft = w_in[:, 3 * d:].T.astype(BF16)
    utri = jnp.triu(jnp.ones((CUMSUM_CHUNK, CUMSUM_CHUNK), F32)).astype(BF16)
    blk = min(ATTN_BLOCK, seq)
    k, qt, vt, cparts = _fox_proj(xf, wk, wqt, wvt, wft, b_f.reshape(nh, 1), utri, bsz, seq, blk)
    qta, ka = _bias_operands(cparts, bsz, seq)
    o = _fox_attn(qt, qta, k.reshape(bsz, seq, d), ka, vt, blk)
    return _outproj_ln(o.reshape(m, d), w_out.astype(BF16), xf, g.reshape(1, d), b.reshape(1, d),
                       alpha, seq)


def _ssd_layer(xf, bsz, seq, w_in, conv_w, conv_b, dt_bias, a_log, d_skip, norm_w, w_out, g, b,
               alpha):
    m, d = xf.shape
    nheads = dt_bias.shape[0]
    d_inner = nheads * SSM_HEAD_DIM
    gn = SSM_GROUPS * SSM_STATE
    wz = w_in[:, :d_inner].astype(BF16)
    wxbc = w_in[:, d_inner:2 * d_inner + 2 * gn].astype(BF16)
    wdt = _pad_cols(w_in[:, 2 * d_inner + 2 * gn:], LANES).astype(BF16)
    dtb = _pad_cols(dt_bias.reshape(1, nheads), LANES)
    z, xs, bm, cm, dt = _ssd_proj(xf, wz, wxbc, wdt, conv_w, conv_b.reshape(1, -1), dtb, seq, gn)
    a = -jnp.exp(a_log.astype(F32))
    arow = _pad_cols(a.reshape(1, nheads), LANES)
    acol = a.reshape(nheads, 1)
    dskip = jnp.repeat(d_skip, SSM_HEAD_DIM).reshape(1, d_inner)
    q = min(SSD_CHUNK, seq)
    ltri = jnp.tril(jnp.ones((q, q), F32)).astype(BF16)
    utri = jnp.triu(jnp.ones((q, q), F32)).astype(BF16)
    dt3 = dt.reshape(bsz, seq, LANES)
    dtt = dt3[:, :, :nheads].transpose(0, 2, 1)
    bm3 = bm.reshape(bsz, seq, gn)
    y = _ssd_scan(xs.reshape(bsz, seq, d_inner), bm3.transpose(0, 2, 1),
                  cm.reshape(bsz, seq, gn), z.reshape(bsz, seq, d_inner), dt3, dtt, arow, acol,
                  dskip, norm_w.reshape(1, d_inner), ltri, utri, q)
    return _outproj_ln(y.reshape(m, d_inner), w_out.astype(BF16), xf, g.reshape(1, d),
                       b.reshape(1, d), alpha, seq)


def kernel(x, p, attn_w_in, attn_b_f, attn_w_out, ssm_w_in, ssm_conv_w, ssm_conv_b, ssm_dt_bias, ssm_A_log, ssm_D, ssm_norm_w, ssm_w_out, ln_mix_g, ln_mix_b, ffn_w_up, ffn_conv_w, ffn_conv_b, ffn_w_down, ln_ffn_g, ln_ffn_b, ple_w_proj, ple_w_gate, ple_b_gate):
    bsz, seq, d = x.shape
    depth = p.shape[0]
    n_mixers = 2
    alpha = (2 * depth) ** 0.25
    xf = x.reshape(bsz * seq, d)
    for i in range(depth):
        j = i // n_mixers
        if i % n_mixers == 0:
            xf = _fox_layer(xf, bsz, seq, attn_w_in[j], attn_b_f[j], attn_w_out[j],
                            ln_mix_g[i], ln_mix_b[i], alpha)
        else:
            xf = _ssd_layer(xf, bsz, seq, ssm_w_in[j], ssm_conv_w[j], ssm_conv_b[j],
                            ssm_dt_bias[j], ssm_A_log[j], ssm_D[j], ssm_norm_w[j], ssm_w_out[j],
                            ln_mix_g[i], ln_mix_b[i], alpha)
        f = ffn_conv_w.shape[-1]
        xf = _ffn(xf, ffn_w_up[i][:, :f].astype(BF16), ffn_w_up[i][:, f:].astype(BF16),
                  ffn_conv_w[i], ffn_conv_b[i].reshape(1, f), ffn_w_down[i].astype(BF16),
                  ln_ffn_g[i].reshape(1, d), ln_ffn_b[i].reshape(1, d),
                  p[i].reshape(bsz * seq, -1), ple_w_proj[i].astype(BF16),
                  ple_w_gate[i].astype(BF16), ple_b_gate[i].reshape(1, d), alpha, seq)
    return xf.reshape(bsz, seq, d)
```

```python
import functools
import math

import jax
import jax.numpy as jnp
import numpy as np
from jax import lax
from jax.experimental import pallas as pl
from jax.experimental.pallas import tpu as pltpu

F32 = jnp.float32
BF16 = jnp.bfloat16

ATTN_HEAD_DIM = 64
SSM_HEAD_DIM = 64
SSM_GROUPS = 8
SSM_STATE = 128
LN_EPS = 1e-5
RMS_EPS = 1e-5

LANES = 128
SUBLANES = 8
VMEM_LIMIT_BYTES = 56 * 1024 * 1024

ROW_TILE = 512
ATTN_BLOCK = 512
SSD_CHUNK = 256
CUMSUM_CHUNK = 256
MASK_VALUE = -1e30
LOG2E = 1.4426950408889634
AUG_ROWS = 16
ONES_ROWS = 16


def _params(*sem):
    return pltpu.CompilerParams(dimension_semantics=sem, vmem_limit_bytes=VMEM_LIMIT_BYTES)


def _resident(shape):
    nd = len(shape)
    return pl.BlockSpec(shape, lambda *_: (0,) * nd, pipeline_mode=pl.Buffered(1))


def _split3(v):
    hi = v.astype(BF16)
    r1 = v - hi.astype(F32)
    mid = r1.astype(BF16)
    lo = (r1 - mid.astype(F32)).astype(BF16)
    return hi, mid, lo


def _softplus(v):
    return jnp.maximum(v, 0.0) + jnp.log1p(jnp.exp(-jnp.abs(v)))


def _sigmoid(v):
    return 1.0 / (1.0 + jnp.exp2(v * (-LOG2E)))


def _layer_norm(h, g, b):
    mu = jnp.mean(h, axis=-1, keepdims=True)
    d = h - mu
    var = jnp.mean(d * d, axis=-1, keepdims=True)
    return d * lax.rsqrt(var + LN_EPS) * g + b


_NT = (((1,), (1,)), ((), ()))


def _bias_placement(nh):
    place_q = np.zeros((nh * AUG_ROWS, LANES), np.float32)
    ones_q = np.zeros((nh * AUG_ROWS, 1), np.float32)
    place_k = np.zeros((LANES, (nh // 2) * LANES), np.float32)
    ones_k = np.zeros((1, (nh // 2) * LANES), np.float32)
    for h in range(nh):
        off = 6 * (h % 2)
        for p in range(3):
            ones_q[h * AUG_ROWS + off + p, 0] = 1.0
            place_q[h * AUG_ROWS + off + 3 + p, p * nh + h] = 1.0
            place_k[p * nh + h, (h // 2) * LANES + off + p] = -1.0
            ones_k[0, (h // 2) * LANES + off + 3 + p] = 1.0
    return (jnp.asarray(place_q, BF16), jnp.asarray(ones_q), jnp.asarray(place_k, BF16),
            jnp.asarray(ones_k))


def _fox_proj_kernel(x_ref, wk_ref, wqt_ref, wvt_ref, wft_ref, bf_ref, ut_ref, pq_ref, oq_ref,
                     pk_ref, ok_ref, k_ref, qt_ref, vt_ref, qta_ref, ka_ref, carry_ref, *,
                     steps_per_batch):
    tm, d = x_ref.shape

    @pl.when(pl.program_id(0) % steps_per_batch == 0)
    def _():
        carry_ref[...] = jnp.zeros_like(carry_ref)

    xb = x_ref[...].astype(BF16)
    k_ref[...] = jnp.dot(xb, wk_ref[...], preferred_element_type=F32).astype(BF16)
    qt_ref[0] = lax.dot_general(wqt_ref[...], xb, _NT, preferred_element_type=F32).astype(BF16)
    vt_ref[0, 0] = lax.dot_general(wvt_ref[...], xb, _NT, preferred_element_type=F32).astype(BF16)

    fl = lax.dot_general(wft_ref[...], xb, _NT, preferred_element_type=F32) + bf_ref[...]
    logf = -_softplus(-fl) * LOG2E
    nh = logf.shape[0]
    parts = jnp.concatenate(_split3(logf), axis=0)
    carry = carry_ref[...]
    zrows = jnp.zeros((LANES - 3 * nh, CUMSUM_CHUNK), F32)
    stacks = []
    for j in range(tm // CUMSUM_CHUNK):
        sl = slice(j * CUMSUM_CHUNK, (j + 1) * CUMSUM_CHUNK)
        pj = jnp.dot(parts[:, sl], ut_ref[...], preferred_element_type=F32)
        cj = (pj[0:nh] + pj[nh:2 * nh] + pj[2 * nh:3 * nh]) + carry
        stacks.append(jnp.concatenate([cp.astype(F32) for cp in _split3(cj)] + [zrows], axis=0))
        carry = cj[:, CUMSUM_CHUNK - 1:CUMSUM_CHUNK]
    carry_ref[...] = carry
    stack = jnp.concatenate(stacks, axis=1)
    qta_ref[0] = (jnp.dot(pq_ref[...], stack.astype(BF16), preferred_element_type=F32)
                  + oq_ref[...]).astype(BF16)
    ka_ref[...] = (jnp.dot(stack.T.astype(BF16), pk_ref[...], preferred_element_type=F32)
                   + ok_ref[...]).astype(BF16)


def _fox_proj(xf, wk, wqt, wvt, wft, bf_col, utri, bsz, seq, tm):
    m, d = xf.shape
    nh = wft.shape[0]
    nblk = seq // tm
    row = lambda i: (i, 0)
    place = _bias_placement(nh)
    out_shape = [jax.ShapeDtypeStruct((m, d), BF16), jax.ShapeDtypeStruct((bsz, d, seq), BF16),
                 jax.ShapeDtypeStruct((bsz, nblk, d, tm), BF16),
                 jax.ShapeDtypeStruct((bsz, nh * AUG_ROWS, seq), BF16),
                 jax.ShapeDtypeStruct((m, (nh // 2) * LANES), BF16)]
    return pl.pallas_call(
        functools.partial(_fox_proj_kernel, steps_per_batch=nblk),
        grid=(m // tm,),
        in_specs=[pl.BlockSpec((tm, d), row), _resident(wk.shape), _resident(wqt.shape),
                  _resident(wvt.shape), _resident(wft.shape), _resident(bf_col.shape),
                  _resident(utri.shape)] + [_resident(a.shape) for a in place],
        out_specs=[pl.BlockSpec((tm, d), row),
                   pl.BlockSpec((1, d, tm), lambda i: (i // nblk, 0, i % nblk)),
                   pl.BlockSpec((1, 1, d, tm), lambda i: (i // nblk, i % nblk, 0, 0)),
                   pl.BlockSpec((1, nh * AUG_ROWS, tm), lambda i: (i // nblk, 0, i % nblk)),
                   pl.BlockSpec((tm, (nh // 2) * LANES), row)],
        out_shape=out_shape,
        scratch_shapes=[pltpu.VMEM((nh, 1), F32)],
        compiler_params=_params("arbitrary"),
        name="fox_proj",
    )(xf, wk, wqt, wvt, wft, bf_col, utri, *place)


def _fox_attn_kernel(qt_ref, qta_ref, k_ref, ka_ref, vt_ref, o_ref, m_scr, acc_scr,
                     sa_scr, sb_scr, ma_scr, mb_scr, *, blk):
    qi = pl.program_id(2)
    hd = ATTN_HEAD_DIM
    qt2 = qt_ref[0]
    zhead = jnp.zeros((hd, blk), BF16)
    zpad = jnp.zeros((LANES - AUG_ROWS, blk), BF16)
    qts = [jnp.concatenate([qt2[0:hd], zhead, qta_ref[0, 0:AUG_ROWS], zpad], axis=0),
           jnp.concatenate([zhead, qt2[hd:2 * hd], qta_ref[0, AUG_ROWS:2 * AUG_ROWS], zpad], axis=0)]
    ones_rows = jnp.ones((ONES_ROWS, blk), BF16)
    key_id = lax.broadcasted_iota(jnp.int32, (blk, blk), 0)
    qry_id = lax.broadcasted_iota(jnp.int32, (blk, blk), 1)
    m_scr[...] = jnp.full_like(m_scr, MASK_VALUE)
    acc_scr[...] = jnp.zeros_like(acc_scr)

    def keys(j):
        start = pl.multiple_of(j * blk, blk)
        return jnp.concatenate([k_ref[0, pl.ds(start, blk), :], ka_ref[0, pl.ds(start, blk), :]],
                               axis=1)

    def values(j):
        return jnp.concatenate([vt_ref[0, j], ones_rows], axis=0)

    def produce(kk, h, s_buf, m_buf):
        s = jnp.dot(kk, qts[h], preferred_element_type=F32)
        s_buf[h] = s
        m_buf[h] = jnp.max(s, axis=0, keepdims=True)

    def consume(vt, h, s_buf, m_buf, masked):
        s = s_buf[h]
        if masked:
            s = jnp.where(key_id <= qry_id, s, MASK_VALUE)
            m_blk = jnp.max(s, axis=0, keepdims=True)
        else:
            m_blk = m_buf[h]
        m_old = m_scr[h]
        m_new = jnp.maximum(m_old, m_blk)
        p = jnp.exp2(s - m_new).astype(BF16)
        alpha = jnp.exp2(m_old - m_new)
        acc_scr[h] = alpha * acc_scr[h] + jnp.dot(vt, p, preferred_element_type=F32)
        m_scr[h] = m_new

    kk0 = keys(0)
    for h in range(2):
        produce(kk0, h, sa_scr, ma_scr)

    def body(t, carry):
        kk1, vt0 = keys(2 * t + 1), values(2 * t)
        for h in range(2):
            produce(kk1, h, sb_scr, mb_scr)
            consume(vt0, h, sa_scr, ma_scr, False)
        kk2, vt1 = keys(2 * t + 2), values(2 * t + 1)
        for h in range(2):
            produce(kk2, h, sa_scr, ma_scr)
            consume(vt1, h, sb_scr, mb_scr, False)
        return carry

    lax.fori_loop(0, qi // 2, body, 0)

    @pl.when(qi % 2 == 0)
    def _():
        vt = values(qi)
        for h in range(2):
            consume(vt, h, sa_scr, ma_scr, True)

    @pl.when(qi % 2 == 1)
    def _():
        kk, vt0, vt1 = keys(qi), values(qi - 1), values(qi)
        for h in range(2):
            produce(kk, h, sb_scr, mb_scr)
            consume(vt0, h, sa_scr, ma_scr, False)
        for h in range(2):
            consume(vt1, h, sb_scr, mb_scr, True)

    outs = []
    for h in range(2):
        acc = acc_scr[h]
        outs.append(acc[h * hd:(h + 1) * hd] * (1.0 / acc[2 * hd:2 * hd + 1]))
    o_ref[0] = jnp.concatenate(outs, axis=0).T.astype(BF16)


def _fox_attn(qt, qta, k, ka, vt, blk):
    b, d, s = qt.shape
    nhp = d // LANES
    nblk = s // blk
    return pl.pallas_call(
        functools.partial(_fox_attn_kernel, blk=blk),
        grid=(b, nhp, nblk),
        in_specs=[
            pl.BlockSpec((1, LANES, blk), lambda bi, hp, qi: (bi, hp, qi)),
            pl.BlockSpec((1, 2 * AUG_ROWS, blk), lambda bi, hp, qi: (bi, hp, qi)),
            pl.BlockSpec((1, s, LANES), lambda bi, hp, qi: (bi, 0, hp)),
            pl.BlockSpec((1, s, LANES), lambda bi, hp, qi: (bi, 0, hp)),
            pl.BlockSpec((1, nblk, LANES, blk), lambda bi, hp, qi: (bi, 0, hp, 0)),
        ],
        out_specs=pl.BlockSpec((1, blk, LANES), lambda bi, hp, qi: (bi, qi, hp)),
        out_shape=jax.ShapeDtypeStruct((b, s, d), BF16),
        scratch_shapes=[pltpu.VMEM((2, 1, blk), F32),
                        pltpu.VMEM((2, LANES + ONES_ROWS, blk), F32),
                        pltpu.VMEM((2, blk, blk), F32), pltpu.VMEM((2, blk, blk), F32),
                        pltpu.VMEM((2, 1, blk), F32), pltpu.VMEM((2, 1, blk), F32)],
        compiler_params=_params("arbitrary", "arbitrary", "arbitrary"),
        name="fox_attn",
    )(qt, qta, k, ka, vt)


def _outproj_ln_kernel(y_ref, w_ref, x_ref, g_ref, b_ref, o_ref, *, alpha):
    mix = jnp.dot(y_ref[...], w_ref[...], preferred_element_type=F32)
    o_ref[...] = _layer_norm(alpha * x_ref[...] + mix, g_ref[...], b_ref[...])


def _outproj_ln(y, w, xf, g, b, alpha, seq):
    m, d = xf.shape
    kdim = y.shape[1]
    tm = min(ROW_TILE, seq)
    row = lambda i: (i, 0)
    return pl.pallas_call(
        functools.partial(_outproj_ln_kernel, alpha=alpha),
        grid=(m // tm,),
        in_specs=[pl.BlockSpec((tm, kdim), row), _resident(w.shape), pl.BlockSpec((tm, d), row),
                  _resident(g.shape), _resident(b.shape)],
        out_specs=pl.BlockSpec((tm, d), row),
        out_shape=jax.ShapeDtypeStruct((m, d), F32),
        compiler_params=_params("arbitrary"),
        name="outproj_ln",
    )(y, w, xf, g, b)


def _ffn_kernel(x_ref, wu_ref, wg_ref, cw_ref, cb_ref, wd_ref, g_ref, b_ref, p_ref, wproj_ref,
                wgate_ref, bgate_ref, o_ref, gbuf_ref, carry_ref, *, chunks, steps_per_batch, alpha):
    tm = x_ref.shape[0]
    halo = SUBLANES

    @pl.when(pl.program_id(0) % steps_per_batch == 0)
    def _():
        carry_ref[...] = jnp.zeros_like(carry_ref)

    x = x_ref[...]
    xb = x.astype(BF16)
    acc = None
    for c0, cw in chunks:
        cs = slice(c0, c0 + cw)
        u = jnp.dot(xb, wu_ref[:, cs], preferred_element_type=F32)
        g = jnp.dot(xb, wg_ref[:, cs], preferred_element_type=F32)
        gbuf_ref[0:halo, 0:cw] = carry_ref[:, cs]
        gbuf_ref[halo:halo + tm, 0:cw] = g
        carry_ref[:, cs] = g[tm - halo:tm, :]
        conv = (cb_ref[:, cs] + cw_ref[2:3, cs] * g
                + cw_ref[1:2, cs] * gbuf_ref[halo - 1:halo - 1 + tm, 0:cw]
                + cw_ref[0:1, cs] * gbuf_ref[halo - 2:halo - 2 + tm, 0:cw])
        gelu = 0.5 * conv * (1.0 + lax.erf(conv * (1.0 / math.sqrt(2.0))))
        part = jnp.dot((gelu * u).astype(BF16), wd_ref[cs, :], preferred_element_type=F32)
        acc = part if acc is None else acc + part
    x2 = _layer_norm(alpha * x + acc, g_ref[...], b_ref[...])
    gate = _sigmoid(jnp.dot(x2.astype(BF16), wgate_ref[...], preferred_element_type=F32)
                    + bgate_ref[...])
    emb = jnp.dot(p_ref[...].astype(BF16), wproj_ref[...], preferred_element_type=F32)
    o_ref[...] = x2 + gate * emb


def _ffn_chunks(f):
    tile = 2 * LANES
    if f <= 6 * tile or f % tile:
        return ((0, f),)
    first = (f // tile + 1) // 2 * tile
    return ((0, first), (first, f - first))


def _ffn(xf, wu, wg, cw, cb, wd, g, b, pf, wproj, wgate, bgate, alpha, seq):
    m, d = xf.shape
    f = wu.shape[1]
    pdim = pf.shape[1]
    tm = min(ROW_TILE, seq)
    chunks = _ffn_chunks(f)
    wmax = max(c[1] for c in chunks)
    row = lambda i: (i, 0)
    return pl.pallas_call(
        functools.partial(_ffn_kernel, chunks=chunks, steps_per_batch=seq // tm, alpha=alpha),
        grid=(m // tm,),
        in_specs=[pl.BlockSpec((tm, d), row), _resident(wu.shape), _resident(wg.shape),
                  _resident(cw.shape), _resident(cb.shape), _resident(wd.shape),
                  _resident(g.shape), _resident(b.shape), pl.BlockSpec((tm, pdim), row),
                  _resident(wproj.shape), _resident(wgate.shape), _resident(bgate.shape)],
        out_specs=pl.BlockSpec((tm, d), row),
        out_shape=jax.ShapeDtypeStruct((m, d), F32),
        scratch_shapes=[pltpu.VMEM((tm + SUBLANES, wmax), F32), pltpu.VMEM((SUBLANES, f), F32)],
        compiler_params=_params("arbitrary"),
        name="conv_ffn",
    )(xf, wu, wg, cw, cb, wd, g, b, pf, wproj, wgate, bgate)


def _ssd_proj_kernel(x_ref, wz_ref, wxbc_ref, wdt_ref, cw_ref, cb_ref, dtb_ref,
                     z_ref, xs_ref, bt_ref, c_ref, dt_ref, dtt_ref, buf0_ref, buf1_ref, carry_ref,
                     *, steps_per_batch, col_tile):
    tm = x_ref.shape[0]
    halo = SUBLANES
    kconv = cw_ref.shape[0]
    bufs = (buf0_ref, buf1_ref)

    @pl.when(pl.program_id(0) % steps_per_batch == 0)
    def _():
        carry_ref[...] = jnp.zeros_like(carry_ref)

    xb = x_ref[...].astype(BF16)
    d_inner = xs_ref.shape[1]
    gn = c_ref.shape[1]
    nconv = wxbc_ref.shape[1] // col_tile
    z_chunks = list(range(0, wz_ref.shape[1], col_tile))

    def project(i):
        cs = slice(i * col_tile, (i + 1) * col_tile)
        buf = bufs[i % 2]
        r = jnp.dot(xb, wxbc_ref[:, cs], preferred_element_type=F32)
        buf[0:halo, :] = carry_ref[:, cs]
        buf[halo:halo + tm, :] = r
        carry_ref[:, cs] = r[tm - halo:tm, :]

    def project_z():
        if z_chunks:
            zs = slice(z_chunks[0], z_chunks.pop(0) + col_tile)
            z_ref[:, zs] = jnp.dot(xb, wz_ref[:, zs], preferred_element_type=F32).astype(BF16)

    def conv_act(i):
        c0 = i * col_tile
        cs = slice(c0, c0 + col_tile)
        buf = bufs[i % 2]
        conv = cb_ref[:, cs] + cw_ref[kconv - 1:kconv, cs] * buf[halo:halo + tm, :]
        for k in range(kconv - 1):
            off = halo - (kconv - 1) + k
            conv = conv + cw_ref[k:k + 1, cs] * buf[off:off + tm, :]
        act = conv * _sigmoid(conv)
        if c0 < d_inner:
            xs_ref[:, cs] = act.astype(BF16)
        elif c0 < d_inner + gn:
            bt_ref[0, c0 - d_inner:c0 - d_inner + col_tile, :] = act.T.astype(BF16)
        else:
            c_ref[:, c0 - d_inner - gn:c0 - d_inner - gn + col_tile] = act.astype(BF16)

    project(0)
    for i in range(nconv):
        if i + 1 < nconv:
            project(i + 1)
        else:
            project_z()
        conv_act(i)
    while z_chunks:
        project_z()
    dt = _softplus(jnp.dot(xb, wdt_ref[...], preferred_element_type=F32) + dtb_ref[...])
    dt_ref[...] = dt
    dtt_ref[0] = dt.T[0:dtt_ref.shape[1], :]


def _ssd_proj(xf, wz, wxbc, wdt, cw, cb, dtb, bsz, seq, gn, nheads):
    m, d = xf.shape
    d_inner = wz.shape[1]
    tm = min(ROW_TILE, seq)
    nblk = seq // tm
    col_tile = min(1024, gn)
    row = lambda i: (i, 0)
    tposed = lambda i: (i // nblk, 0, i % nblk)
    out_shape = [jax.ShapeDtypeStruct((m, d_inner), BF16), jax.ShapeDtypeStruct((m, d_inner), BF16),
                 jax.ShapeDtypeStruct((bsz, gn, seq), BF16), jax.ShapeDtypeStruct((m, gn), BF16),
                 jax.ShapeDtypeStruct((m, LANES), F32),
                 jax.ShapeDtypeStruct((bsz, nheads, seq), F32)]
    return pl.pallas_call(
        functools.partial(_ssd_proj_kernel, steps_per_batch=seq // tm, col_tile=col_tile),
        grid=(m // tm,),
        in_specs=[pl.BlockSpec((tm, d), row), _resident(wz.shape), _resident(wxbc.shape),
                  _resident(wdt.shape), _resident(cw.shape), _resident(cb.shape),
                  _resident(dtb.shape)],
        out_specs=[pl.BlockSpec((tm, d_inner), row), pl.BlockSpec((tm, d_inner), row),
                   pl.BlockSpec((1, gn, tm), tposed), pl.BlockSpec((tm, gn), row),
                   pl.BlockSpec((tm, LANES), row), pl.BlockSpec((1, nheads, tm), tposed)],
        out_shape=out_shape,
        scratch_shapes=[pltpu.VMEM((tm + SUBLANES, col_tile), F32),
                        pltpu.VMEM((tm + SUBLANES, col_tile), F32),
                        pltpu.VMEM((SUBLANES, wxbc.shape[1]), F32)],
        compiler_params=_params("arbitrary"),
        name="ssd_proj",
    )(xf, wz, wxbc, wdt, cw, cb, dtb)


def _expand_heads(cols, lane_head):
    out = cols[0]
    for r in range(1, len(cols)):
        out = jnp.where(lane_head >= r, cols[r], out)
    return out


def _ssd_scan_kernel(xs_ref, bt_ref, c_ref, z_ref, dt_ref, dtt_ref, arow_ref, acol_ref,
                     dskip_ref, nw_ref, ltri_ref, utri_ref, y_ref, state_ref, *, heads_per_group):
    q = xs_ref.shape[1]
    n = SSM_STATE
    hp_ = SSM_HEAD_DIM
    gw = heads_per_group * hp_
    ngroups = xs_ref.shape[2] // gw

    @pl.when(pl.program_id(1) == 0)
    def _():
        state_ref[...] = jnp.zeros_like(state_ref)

    dt_col = dt_ref[0]
    dt_row = dtt_ref[0]
    nheads = dt_row.shape[0]
    a_col = dt_col * (arow_ref[...] * LOG2E)
    pc = jnp.dot(ltri_ref[...], jnp.concatenate(_split3(a_col), axis=1),
                 preferred_element_type=F32)
    acum_col = pc[:, 0:LANES] + pc[:, LANES:2 * LANES] + pc[:, 2 * LANES:3 * LANES]
    a_row = dt_row * (acol_ref[...] * LOG2E)
    pr = jnp.dot(jnp.concatenate(_split3(a_row), axis=0), utri_ref[...],
                 preferred_element_type=F32)
    acum_row = pr[0:nheads] + pr[nheads:2 * nheads] + pr[2 * nheads:3 * nheads]
    dec_row = jnp.exp2(acum_col[q - 1:q, :])
    w_row = jnp.exp2(acum_row[:, q - 1:q] - acum_row) * dt_row
    src_row = acum_row - jnp.log2(dt_row)

    tri = (lax.broadcasted_iota(jnp.int32, (q, q), 0) >= lax.broadcasted_iota(jnp.int32, (q, q), 1))
    lane_head = lax.broadcasted_iota(jnp.int32, (1, gw), 1) // hp_
    low_half = lax.broadcasted_iota(jnp.int32, (1, LANES), 1) < hp_

    for g in range(ngroups):
        gs = slice(g * gw, (g + 1) * gw)
        ns = slice(g * n, (g + 1) * n)
        cg = c_ref[0, :, ns]
        btg = bt_ref[0, ns, :]
        btg_f = btg.astype(F32)
        xg_b = xs_ref[0, :, gs]
        heads = range(g * heads_per_group, (g + 1) * heads_per_group)
        cb = jnp.dot(cg, btg, preferred_element_type=F32)
        state = state_ref[g]
        y_inter = jnp.dot(cg, state.astype(BF16), preferred_element_type=F32)
        e_cols, y_intra, st_new = [], None, None
        for r, h in enumerate(heads):
            acol_b = jnp.broadcast_to(acum_col[:, h:h + 1], (q, LANES))
            e_cols.append(jnp.exp2(acol_b))
            seg = jnp.concatenate([acol_b] * (q // LANES), axis=1) - src_row[h:h + 1, :]
            mh = (cb * jnp.exp2(jnp.where(tri, seg, -jnp.inf))).astype(BF16)
            xh = jnp.where(lane_head == r, xg_b, jnp.zeros_like(xg_b))
            part = jnp.dot(mh, xh, preferred_element_type=F32)
            y_intra = part if y_intra is None else y_intra + part
            btw = (btg_f * w_row[h:h + 1, :]).astype(BF16)
            part = jnp.dot(btw, xh, preferred_element_type=F32)
            st_new = part if st_new is None else st_new + part
        e_x = jnp.concatenate([jnp.where(low_half, e_cols[2 * i], e_cols[2 * i + 1])
                               for i in range(gw // LANES)], axis=1)
        y_g = y_inter * e_x + y_intra
        d_x = _expand_heads([dec_row[:, h:h + 1] for h in heads], lane_head)
        state_ref[g] = state * d_x + st_new
        xg = xg_b.astype(F32)
        zg = z_ref[0, :, gs].astype(F32)
        yv = (y_g + dskip_ref[:, gs] * xg) * (zg * _sigmoid(zg))
        ms = jnp.mean(yv * yv, axis=-1, keepdims=True)
        y_ref[0, :, gs] = (yv * lax.rsqrt(ms + RMS_EPS) * nw_ref[:, gs]).astype(BF16)


def _ssd_scan(xs, bmt, cm, z, dt, dtt, arow, acol, dskip, nw, ltri, utri, q):
    b, s, d_inner = xs.shape
    gn = cm.shape[2]
    nheads = dtt.shape[1]
    ngroups = gn // SSM_STATE
    heads_per_group = nheads // ngroups
    gw = heads_per_group * SSM_HEAD_DIM
    blk = lambda w: pl.BlockSpec((1, q, w), lambda bi, ci: (bi, ci, 0))
    return pl.pallas_call(
        functools.partial(_ssd_scan_kernel, heads_per_group=heads_per_group),
        grid=(b, s // q),
        in_specs=[blk(d_inner), pl.BlockSpec((1, gn, q), lambda bi, ci: (bi, 0, ci)),
                  blk(gn), blk(d_inner), blk(LANES),
                  pl.BlockSpec((1, nheads, q), lambda bi, ci: (bi, 0, ci)),
                  _resident(arow.shape), _resident(acol.shape), _resident(dskip.shape),
                  _resident(nw.shape), _resident(ltri.shape), _resident(utri.shape)],
        out_specs=blk(d_inner),
        out_shape=jax.ShapeDtypeStruct((b, s, d_inner), BF16),
        scratch_shapes=[pltpu.VMEM((ngroups, SSM_STATE, gw), F32)],
        compiler_params=_params("arbitrary", "arbitrary"),
        name="ssd_scan",
    )(xs, bmt, cm, z, dt, dtt, arow, acol, dskip, nw, ltri, utri)


def _pad_cols(a, width):
    return jnp.pad(a, ((0, 0), (0, width - a.shape[1])))


def _fox_layer(xf, bsz, seq, w_in, b_f, w_out, g, b, alpha):
    m, d = xf.shape
    nh = d // ATTN_HEAD_DIM
    scale = LOG2E / math.sqrt(ATTN_HEAD_DIM)
    wqt = (w_in[:, :d] * scale).T.astype(BF16)
    wk = w_in[:, d:2 * d].astype(BF16)
    wvt = w_in[:, 2 * d:3 * d].T.astype(BF16)
    wft = w_in[:, 3 * d:].T.astype(BF16)
    utri = jnp.triu(jnp.ones((CUMSUM_CHUNK, CUMSUM_CHUNK), F32)).astype(BF16)
    blk = min(ATTN_BLOCK, seq)
    k, qt, vt, qta, ka = _fox_proj(xf, wk, wqt, wvt, wft, b_f.reshape(nh, 1), utri, bsz, seq, blk)
    o = _fox_attn(qt, qta, k.reshape(bsz, seq, d), ka.reshape(bsz, seq, -1), vt, blk)
    return _outproj_ln(o.reshape(m, d), w_out.astype(BF16), xf, g.reshape(1, d), b.reshape(1, d),
                       alpha, seq)


def _ssd_layer(xf, bsz, seq, w_in, conv_w, conv_b, dt_bias, a_log, d_skip, norm_w, w_out, g, b,
               alpha):
    m, d = xf.shape
    nheads = dt_bias.shape[0]
    d_inner = nheads * SSM_HEAD_DIM
    gn = SSM_GROUPS * SSM_STATE
    wz = w_in[:, :d_inner].astype(BF16)
    wxbc = w_in[:, d_inner:2 * d_inner + 2 * gn].astype(BF16)
    wdt = _pad_cols(w_in[:, 2 * d_inner + 2 * gn:], LANES).astype(BF16)
    dtb = _pad_cols(dt_bias.reshape(1, nheads), LANES)
    z, xs, bmt, cm, dt, dtt = _ssd_proj(xf, wz, wxbc, wdt, conv_w, conv_b.reshape(1, -1), dtb,
                                        bsz, seq, gn, nheads)
    a = -jnp.exp(a_log.astype(F32))
    arow = _pad_cols(a.reshape(1, nheads), LANES)
    acol = a.reshape(nheads, 1)
    dskip = jnp.repeat(d_skip, SSM_HEAD_DIM).reshape(1, d_inner)
    q = min(SSD_CHUNK, seq)
    ltri = jnp.tril(jnp.ones((q, q), F32)).astype(BF16)
    utri = jnp.triu(jnp.ones((q, q), F32)).astype(BF16)
    y = _ssd_scan(xs.reshape(bsz, seq, d_inner), bmt, cm.reshape(bsz, seq, gn),
                  z.reshape(bsz, seq, d_inner), dt.reshape(bsz, seq, LANES), dtt, arow, acol,
                  dskip, norm_w.reshape(1, d_inner), ltri, utri, q)
    return _outproj_ln(y.reshape(m, d_inner), w_out.astype(BF16), xf, g.reshape(1, d),
                       b.reshape(1, d), alpha, seq)


def kernel(x, p, attn_w_in, attn_b_f, attn_w_out, ssm_w_in, ssm_conv_w, ssm_conv_b, ssm_dt_bias, ssm_A_log, ssm_D, ssm_norm_w, ssm_w_out, ln_mix_g, ln_mix_b, ffn_w_up, ffn_conv_w, ffn_conv_b, ffn_w_down, ln_ffn_g, ln_ffn_b, ple_w_proj, ple_w_gate, ple_b_gate):
    bsz, seq, d = x.shape
    depth = p.shape[0]
    n_mixers = 2
    alpha = (2 * depth) ** 0.25
    xf = x.reshape(bsz * seq, d)
    for i in range(depth):
        j = i // n_mixers
        if i % n_mixers == 0:
            xf = _fox_layer(xf, bsz, seq, attn_w_in[j], attn_b_f[j], attn_w_out[j],
                            ln_mix_g[i], ln_mix_b[i], alpha)
        else:
            xf = _ssd_layer(xf, bsz, seq, ssm_w_in[j], ssm_conv_w[j], ssm_conv_b[j],
                            ssm_dt_bias[j], ssm_A_log[j], ssm_D[j], ssm_norm_w[j], ssm_w_out[j],
                            ln_mix_g[i], ln_mix_b[i], alpha)
        f = ffn_conv_w.shape[-1]
        xf = _ffn(xf, ffn_w_up[i][:, :f].astype(BF16), ffn_w_up[i][:, f:].astype(BF16),
                  ffn_conv_w[i], ffn_conv_b[i].reshape(1, f), ffn_w_down[i].astype(BF16),
                  ln_ffn_g[i].reshape(1, d), ln_ffn_b[i].reshape(1, d),
                  p[i].reshape(bsz * seq, -1), ple_w_proj[i].astype(BF16),
                  ple_w_gate[i].astype(BF16), ple_b_gate[i].reshape(1, d), alpha, seq)
    return xf.reshape(bsz, seq, d)
```

```python
import functools
import math

import jax
import jax.numpy as jnp
import numpy as np
from jax import lax
from jax.experimental import pallas as pl
from jax.experimental.pallas import tpu as pltpu

F32 = jnp.float32
BF16 = jnp.bfloat16

ATTN_HEAD_DIM = 64
SSM_HEAD_DIM = 64
SSM_GROUPS = 8
SSM_STATE = 128
LN_EPS = 1e-5
RMS_EPS = 1e-5

LANES = 128
SUBLANES = 8
VMEM_LIMIT_BYTES = 56 * 1024 * 1024

ROW_TILE = 512
ATTN_BLOCK = 512
ATTN_UNROLL = 4
SSD_CHUNK = 256
CUMSUM_CHUNK = 256
MASK_VALUE = -1e30
LOG2E = 1.4426950408889634
AUG_ROWS = 16
ONES_ROWS = 16


def _params(*sem):
    return pltpu.CompilerParams(dimension_semantics=sem, vmem_limit_bytes=VMEM_LIMIT_BYTES)


def _resident(shape):
    nd = len(shape)
    return pl.BlockSpec(shape, lambda *_: (0,) * nd, pipeline_mode=pl.Buffered(1))


def _split3(v):
    hi = v.astype(BF16)
    r1 = v - hi.astype(F32)
    mid = r1.astype(BF16)
    lo = (r1 - mid.astype(F32)).astype(BF16)
    return hi, mid, lo


def _softplus(v):
    return jnp.maximum(v, 0.0) + jnp.log1p(jnp.exp(-jnp.abs(v)))


def _sigmoid(v):
    return 1.0 / (1.0 + jnp.exp2(v * (-LOG2E)))


def _layer_norm(h, g, b):
    mu = jnp.mean(h, axis=-1, keepdims=True)
    d = h - mu
    var = jnp.mean(d * d, axis=-1, keepdims=True)
    return d * lax.rsqrt(var + LN_EPS) * g + b


_NT = (((1,), (1,)), ((), ()))


def _bias_placement(nh):
    place_q = np.zeros((nh * AUG_ROWS, LANES), np.float32)
    ones_q = np.zeros((nh * AUG_ROWS, 1), np.float32)
    place_k = np.zeros((LANES, (nh // 2) * LANES), np.float32)
    ones_k = np.zeros((1, (nh // 2) * LANES), np.float32)
    for h in range(nh):
        off = 6 * (h % 2)
        for p in range(3):
            ones_q[h * AUG_ROWS + off + p, 0] = 1.0
            place_q[h * AUG_ROWS + off + 3 + p, p * nh + h] = 1.0
            place_k[p * nh + h, (h // 2) * LANES + off + p] = -1.0
            ones_k[0, (h // 2) * LANES + off + 3 + p] = 1.0
    return (jnp.asarray(place_q, BF16), jnp.asarray(ones_q), jnp.asarray(place_k, BF16),
            jnp.asarray(ones_k))


def _fox_proj_kernel(x_ref, wk_ref, wqt_ref, wvt_ref, wft_ref, bf_ref, ut_ref, pq_ref, oq_ref,
                     pk_ref, ok_ref, k_ref, qt_ref, vt_ref, qta_ref, ka_ref, carry_ref, *,
                     steps_per_batch):
    tm, d = x_ref.shape

    @pl.when(pl.program_id(0) % steps_per_batch == 0)
    def _():
        carry_ref[...] = jnp.zeros_like(carry_ref)

    xb = x_ref[...].astype(BF16)
    k_ref[...] = jnp.dot(xb, wk_ref[...], preferred_element_type=F32).astype(BF16)
    qt_ref[0] = lax.dot_general(wqt_ref[...], xb, _NT, preferred_element_type=F32).astype(BF16)
    vt_ref[0, 0] = lax.dot_general(wvt_ref[...], xb, _NT, preferred_element_type=F32).astype(BF16)

    fl = lax.dot_general(wft_ref[...], xb, _NT, preferred_element_type=F32) + bf_ref[...]
    logf = -_softplus(-fl) * LOG2E
    nh = logf.shape[0]
    parts = jnp.concatenate(_split3(logf), axis=0)
    carry = carry_ref[...]
    zrows = jnp.zeros((LANES - 3 * nh, CUMSUM_CHUNK), F32)
    stacks = []
    for j in range(tm // CUMSUM_CHUNK):
        sl = slice(j * CUMSUM_CHUNK, (j + 1) * CUMSUM_CHUNK)
        pj = jnp.dot(parts[:, sl], ut_ref[...], preferred_element_type=F32)
        cj = (pj[0:nh] + pj[nh:2 * nh] + pj[2 * nh:3 * nh]) + carry
        stacks.append(jnp.concatenate([cp.astype(F32) for cp in _split3(cj)] + [zrows], axis=0))
        carry = cj[:, CUMSUM_CHUNK - 1:CUMSUM_CHUNK]
    carry_ref[...] = carry
    stack = jnp.concatenate(stacks, axis=1)
    qta_ref[0] = (jnp.dot(pq_ref[...], stack.astype(BF16), preferred_element_type=F32)
                  + oq_ref[...]).astype(BF16)
    ka_ref[...] = (jnp.dot(stack.T.astype(BF16), pk_ref[...], preferred_element_type=F32)
                   + ok_ref[...]).astype(BF16)


def _fox_proj(xf, wk, wqt, wvt, wft, bf_col, utri, bsz, seq, tm):
    m, d = xf.shape
    nh = wft.shape[0]
    nblk = seq // tm
    row = lambda i: (i, 0)
    place = _bias_placement(nh)
    out_shape = [jax.ShapeDtypeStruct((m, d), BF16), jax.ShapeDtypeStruct((bsz, d, seq), BF16),
                 jax.ShapeDtypeStruct((bsz, nblk, d, tm), BF16),
                 jax.ShapeDtypeStruct((bsz, nh * AUG_ROWS, seq), BF16),
                 jax.ShapeDtypeStruct((m, (nh // 2) * LANES), BF16)]
    return pl.pallas_call(
        functools.partial(_fox_proj_kernel, steps_per_batch=nblk),
        grid=(m // tm,),
        in_specs=[pl.BlockSpec((tm, d), row), _resident(wk.shape), _resident(wqt.shape),
                  _resident(wvt.shape), _resident(wft.shape), _resident(bf_col.shape),
                  _resident(utri.shape)] + [_resident(a.shape) for a in place],
        out_specs=[pl.BlockSpec((tm, d), row),
                   pl.BlockSpec((1, d, tm), lambda i: (i // nblk, 0, i % nblk)),
                   pl.BlockSpec((1, 1, d, tm), lambda i: (i // nblk, i % nblk, 0, 0)),
                   pl.BlockSpec((1, nh * AUG_ROWS, tm), lambda i: (i // nblk, 0, i % nblk)),
                   pl.BlockSpec((tm, (nh // 2) * LANES), row)],
        out_shape=out_shape,
        scratch_shapes=[pltpu.VMEM((nh, 1), F32)],
        compiler_params=_params("arbitrary"),
        name="fox_proj",
    )(xf, wk, wqt, wvt, wft, bf_col, utri, *place)


def _fox_attn_kernel(qt_ref, qta_ref, k_ref, ka_ref, vt_ref, o_ref, m_scr, acc_scr,
                     sa_scr, sb_scr, ma_scr, mb_scr, *, blk):
    qi = pl.program_id(2)
    hd = ATTN_HEAD_DIM
    qt2 = qt_ref[0]
    zhead = jnp.zeros((hd, blk), BF16)
    zpad = jnp.zeros((LANES - AUG_ROWS, blk), BF16)
    qts = [jnp.concatenate([qt2[0:hd], zhead, qta_ref[0, 0:AUG_ROWS], zpad], axis=0),
           jnp.concatenate([zhead, qt2[hd:2 * hd], qta_ref[0, AUG_ROWS:2 * AUG_ROWS], zpad], axis=0)]
    ones_rows = jnp.ones((ONES_ROWS, blk), BF16)
    key_id = lax.broadcasted_iota(jnp.int32, (blk, blk), 0)
    qry_id = lax.broadcasted_iota(jnp.int32, (blk, blk), 1)
    m_scr[...] = jnp.full_like(m_scr, MASK_VALUE)
    acc_scr[...] = jnp.zeros_like(acc_scr)

    def keys(j):
        start = pl.multiple_of(j * blk, blk)
        return jnp.concatenate([k_ref[0, pl.ds(start, blk), :], ka_ref[0, pl.ds(start, blk), :]],
                               axis=1)

    def values(j):
        return jnp.concatenate([vt_ref[0, j], ones_rows], axis=0)

    def produce(kk, h, s_buf, m_buf):
        s = jnp.dot(kk, qts[h], preferred_element_type=F32)
        s_buf[h] = s
        m_buf[h] = jnp.max(s, axis=0, keepdims=True)

    def consume(vt, h, s_buf, m_buf, masked):
        s = s_buf[h]
        if masked:
            s = jnp.where(key_id <= qry_id, s, MASK_VALUE)
            m_blk = jnp.max(s, axis=0, keepdims=True)
        else:
            m_blk = m_buf[h]
        m_old = m_scr[h]
        m_new = jnp.maximum(m_old, m_blk)
        p = jnp.exp2(s - m_new).astype(BF16)
        alpha = jnp.exp2(m_old - m_new)
        acc_scr[h] = alpha * acc_scr[h] + jnp.dot(vt, p, preferred_element_type=F32)
        m_scr[h] = m_new

    kk0 = keys(0)
    for h in range(2):
        produce(kk0, h, sa_scr, ma_scr)

    bufs = ((sa_scr, ma_scr), (sb_scr, mb_scr))

    def advance(j0, nsteps):
        for i in range(nsteps):
            kk, vt = keys(j0 + i + 1), values(j0 + i)
            for h in range(2):
                produce(kk, h, *bufs[(i + 1) % 2])
                consume(vt, h, *bufs[i % 2], False)

    def body(t, carry):
        advance(ATTN_UNROLL * t, ATTN_UNROLL)
        return carry

    lax.fori_loop(0, qi // ATTN_UNROLL, body, 0)

    for rem in range(ATTN_UNROLL):
        @pl.when(qi % ATTN_UNROLL == rem)
        def _(rem=rem):
            advance(qi - rem, rem)
            vt = values(qi)
            for h in range(2):
                consume(vt, h, *bufs[rem % 2], True)

    outs = []
    for h in range(2):
        acc = acc_scr[h]
        outs.append(acc[h * hd:(h + 1) * hd] * (1.0 / acc[2 * hd:2 * hd + 1]))
    o_ref[0] = jnp.concatenate(outs, axis=0).T.astype(BF16)


def _fox_attn(qt, qta, k, ka, vt, blk):
    b, d, s = qt.shape
    nhp = d // LANES
    nblk = s // blk
    return pl.pallas_call(
        functools.partial(_fox_attn_kernel, blk=blk),
        grid=(b, nhp, nblk),
        in_specs=[
            pl.BlockSpec((1, LANES, blk), lambda bi, hp, qi: (bi, hp, qi)),
            pl.BlockSpec((1, 2 * AUG_ROWS, blk), lambda bi, hp, qi: (bi, hp, qi)),
            pl.BlockSpec((1, s, LANES), lambda bi, hp, qi: (bi, 0, hp)),
            pl.BlockSpec((1, s, LANES), lambda bi, hp, qi: (bi, 0, hp)),
            pl.BlockSpec((1, nblk, LANES, blk), lambda bi, hp, qi: (bi, 0, hp, 0)),
        ],
        out_specs=pl.BlockSpec((1, blk, LANES), lambda bi, hp, qi: (bi, qi, hp)),
        out_shape=jax.ShapeDtypeStruct((b, s, d), BF16),
        scratch_shapes=[pltpu.VMEM((2, 1, blk), F32),
                        pltpu.VMEM((2, LANES + ONES_ROWS, blk), F32),
                        pltpu.VMEM((2, blk, blk), F32), pltpu.VMEM((2, blk, blk), F32),
                        pltpu.VMEM((2, 1, blk), F32), pltpu.VMEM((2, 1, blk), F32)],
        compiler_params=_params("arbitrary", "arbitrary", "arbitrary"),
        name="fox_attn",
    )(qt, qta, k, ka, vt)


def _outproj_ln_kernel(y_ref, w_ref, x_ref, g_ref, b_ref, o_ref, *, alpha):
    mix = jnp.dot(y_ref[...], w_ref[...], preferred_element_type=F32)
    o_ref[...] = _layer_norm(alpha * x_ref[...] + mix, g_ref[...], b_ref[...])


def _outproj_ln(y, w, xf, g, b, alpha, seq):
    m, d = xf.shape
    kdim = y.shape[1]
    tm = min(ROW_TILE, seq)
    row = lambda i: (i, 0)
    return pl.pallas_call(
        functools.partial(_outproj_ln_kernel, alpha=alpha),
        grid=(m // tm,),
        in_specs=[pl.BlockSpec((tm, kdim), row), _resident(w.shape), pl.BlockSpec((tm, d), row),
                  _resident(g.shape), _resident(b.shape)],
        out_specs=pl.BlockSpec((tm, d), row),
        out_shape=jax.ShapeDtypeStruct((m, d), F32),
        compiler_params=_params("arbitrary"),
        name="outproj_ln",
    )(y, w, xf, g, b)


def _ffn_kernel(x_ref, wu_ref, wg_ref, cw_ref, cb_ref, wd_ref, g_ref, b_ref, p_ref, wproj_ref,
                wgate_ref, bgate_ref, o_ref, gbuf_ref, carry_ref, *, chunks, steps_per_batch, alpha):
    tm = x_ref.shape[0]
    halo = SUBLANES

    @pl.when(pl.program_id(0) % steps_per_batch == 0)
    def _():
        carry_ref[...] = jnp.zeros_like(carry_ref)

    x = x_ref[...]
    xb = x.astype(BF16)
    acc = None
    for c0, cw in chunks:
        cs = slice(c0, c0 + cw)
        u = jnp.dot(xb, wu_ref[:, cs], preferred_element_type=F32)
        g = jnp.dot(xb, wg_ref[:, cs], preferred_element_type=F32)
        gbuf_ref[0:halo, 0:cw] = carry_ref[:, cs]
        gbuf_ref[halo:halo + tm, 0:cw] = g
        carry_ref[:, cs] = g[tm - halo:tm, :]
        conv = (cb_ref[:, cs] + cw_ref[2:3, cs] * g
                + cw_ref[1:2, cs] * gbuf_ref[halo - 1:halo - 1 + tm, 0:cw]
                + cw_ref[0:1, cs] * gbuf_ref[halo - 2:halo - 2 + tm, 0:cw])
        gelu = 0.5 * conv * (1.0 + lax.erf(conv * (1.0 / math.sqrt(2.0))))
        part = jnp.dot((gelu * u).astype(BF16), wd_ref[cs, :], preferred_element_type=F32)
        acc = part if acc is None else acc + part
    x2 = _layer_norm(alpha * x + acc, g_ref[...], b_ref[...])
    gate = _sigmoid(jnp.dot(x2.astype(BF16), wgate_ref[...], preferred_element_type=F32)
                    + bgate_ref[...])
    emb = jnp.dot(p_ref[...].astype(BF16), wproj_ref[...], preferred_element_type=F32)
    o_ref[...] = x2 + gate * emb


def _ffn_chunks(f):
    tile = 2 * LANES
    if f <= 6 * tile or f % tile:
        return ((0, f),)
    first = (f // tile + 1) // 2 * tile
    return ((0, first), (first, f - first))


def _ffn(xf, wu, wg, cw, cb, wd, g, b, pf, wproj, wgate, bgate, alpha, seq):
    m, d = xf.shape
    f = wu.shape[1]
    pdim = pf.shape[1]
    tm = min(ROW_TILE, seq)
    chunks = _ffn_chunks(f)
    wmax = max(c[1] for c in chunks)
    row = lambda i: (i, 0)
    return pl.pallas_call(
        functools.partial(_ffn_kernel, chunks=chunks, steps_per_batch=seq // tm, alpha=alpha),
        grid=(m // tm,),
        in_specs=[pl.BlockSpec((tm, d), row), _resident(wu.shape), _resident(wg.shape),
                  _resident(cw.shape), _resident(cb.shape), _resident(wd.shape),
                  _resident(g.shape), _resident(b.shape), pl.BlockSpec((tm, pdim), row),
                  _resident(wproj.shape), _resident(wgate.shape), _resident(bgate.shape)],
        out_specs=pl.BlockSpec((tm, d), row),
        out_shape=jax.ShapeDtypeStruct((m, d), F32),
        scratch_shapes=[pltpu.VMEM((tm + SUBLANES, wmax), F32), pltpu.VMEM((SUBLANES, f), F32)],
        compiler_params=_params("arbitrary"),
        name="conv_ffn",
    )(xf, wu, wg, cw, cb, wd, g, b, pf, wproj, wgate, bgate)


def _ssd_proj_kernel(x_ref, wz_ref, wxbc_ref, wdt_ref, cw_ref, cb_ref, dtb_ref,
                     z_ref, xs_ref, bt_ref, c_ref, dt_ref, dtt_ref, buf0_ref, buf1_ref, carry_ref,
                     *, steps_per_batch, col_tile):
    tm = x_ref.shape[0]
    halo = SUBLANES
    kconv = cw_ref.shape[0]
    bufs = (buf0_ref, buf1_ref)

    @pl.when(pl.program_id(0) % steps_per_batch == 0)
    def _():
        carry_ref[...] = jnp.zeros_like(carry_ref)

    xb = x_ref[...].astype(BF16)
    d_inner = xs_ref.shape[1]
    gn = c_ref.shape[1]
    nconv = wxbc_ref.shape[1] // col_tile
    z_chunks = list(range(0, wz_ref.shape[1], col_tile))

    def project(i):
        cs = slice(i * col_tile, (i + 1) * col_tile)
        buf = bufs[i % 2]
        r = jnp.dot(xb, wxbc_ref[:, cs], preferred_element_type=F32)
        buf[0:halo, :] = carry_ref[:, cs]
        buf[halo:halo + tm, :] = r
        carry_ref[:, cs] = r[tm - halo:tm, :]

    def project_z():
        if z_chunks:
            zs = slice(z_chunks[0], z_chunks.pop(0) + col_tile)
            z_ref[:, zs] = jnp.dot(xb, wz_ref[:, zs], preferred_element_type=F32).astype(BF16)

    def conv_act(i):
        c0 = i * col_tile
        cs = slice(c0, c0 + col_tile)
        buf = bufs[i % 2]
        conv = cb_ref[:, cs] + cw_ref[kconv - 1:kconv, cs] * buf[halo:halo + tm, :]
        for k in range(kconv - 1):
            off = halo - (kconv - 1) + k
            conv = conv + cw_ref[k:k + 1, cs] * buf[off:off + tm, :]
        act = conv * _sigmoid(conv)
        if c0 < d_inner:
            xs_ref[:, cs] = act.astype(BF16)
        elif c0 < d_inner + gn:
            bt_ref[0, c0 - d_inner:c0 - d_inner + col_tile, :] = act.T.astype(BF16)
        else:
            c_ref[:, c0 - d_inner - gn:c0 - d_inner - gn + col_tile] = act.astype(BF16)

    project(0)
    for i in range(nconv):
        if i + 1 < nconv:
            project(i + 1)
        else:
            project_z()
        conv_act(i)
    while z_chunks:
        project_z()
    dt = _softplus(jnp.dot(xb, wdt_ref[...], preferred_element_type=F32) + dtb_ref[...])
    dt_ref[...] = dt
    dtt_ref[0] = dt.T[0:dtt_ref.shape[1], :]


def _ssd_proj(xf, wz, wxbc, wdt, cw, cb, dtb, bsz, seq, gn, nheads):
    m, d = xf.shape
    d_inner = wz.shape[1]
    tm = min(ROW_TILE, seq)
    nblk = seq // tm
    col_tile = min(1024, gn)
    row = lambda i: (i, 0)
    tposed = lambda i: (i // nblk, 0, i % nblk)
    out_shape = [jax.ShapeDtypeStruct((m, d_inner), BF16), jax.ShapeDtypeStruct((m, d_inner), BF16),
                 jax.ShapeDtypeStruct((bsz, gn, seq), BF16), jax.ShapeDtypeStruct((m, gn), BF16),
                 jax.ShapeDtypeStruct((m, LANES), F32),
                 jax.ShapeDtypeStruct((bsz, nheads, seq), F32)]
    return pl.pallas_call(
        functools.partial(_ssd_proj_kernel, steps_per_batch=seq // tm, col_tile=col_tile),
        grid=(m // tm,),
        in_specs=[pl.BlockSpec((tm, d), row), _resident(wz.shape), _resident(wxbc.shape),
                  _resident(wdt.shape), _resident(cw.shape), _resident(cb.shape),
                  _resident(dtb.shape)],
        out_specs=[pl.BlockSpec((tm, d_inner), row), pl.BlockSpec((tm, d_inner), row),
                   pl.BlockSpec((1, gn, tm), tposed), pl.BlockSpec((tm, gn), row),
                   pl.BlockSpec((tm, LANES), row), pl.BlockSpec((1, nheads, tm), tposed)],
        out_shape=out_shape,
        scratch_shapes=[pltpu.VMEM((tm + SUBLANES, col_tile), F32),
                        pltpu.VMEM((tm + SUBLANES, col_tile), F32),
                        pltpu.VMEM((SUBLANES, wxbc.shape[1]), F32)],
        compiler_params=_params("arbitrary"),
        name="ssd_proj",
    )(xf, wz, wxbc, wdt, cw, cb, dtb)


def _expand_heads(cols, lane_head):
    out = cols[0]
    for r in range(1, len(cols)):
        out = jnp.where(lane_head >= r, cols[r], out)
    return out


def _ssd_scan_kernel(xs_ref, bt_ref, c_ref, z_ref, dt_ref, dtt_ref, arow_ref, acol_ref,
                     dskip_ref, nw_ref, ltri_ref, utri_ref, y_ref, state_ref, *, heads_per_group):
    q = xs_ref.shape[1]
    n = SSM_STATE
    hp_ = SSM_HEAD_DIM
    gw = heads_per_group * hp_
    ngroups = xs_ref.shape[2] // gw

    @pl.when(pl.program_id(1) == 0)
    def _():
        state_ref[...] = jnp.zeros_like(state_ref)

    dt_col = dt_ref[0]
    dt_row = dtt_ref[0]
    nheads = dt_row.shape[0]
    a_col = dt_col * (arow_ref[...] * LOG2E)
    pc = jnp.dot(ltri_ref[...], jnp.concatenate(_split3(a_col), axis=1),
                 preferred_element_type=F32)
    acum_col = pc[:, 0:LANES] + pc[:, LANES:2 * LANES] + pc[:, 2 * LANES:3 * LANES]
    a_row = dt_row * (acol_ref[...] * LOG2E)
    pr = jnp.dot(jnp.concatenate(_split3(a_row), axis=0), utri_ref[...],
                 preferred_element_type=F32)
    acum_row = pr[0:nheads] + pr[nheads:2 * nheads] + pr[2 * nheads:3 * nheads]
    dec_row = jnp.exp2(acum_col[q - 1:q, :])
    w_row = jnp.exp2(acum_row[:, q - 1:q] - acum_row) * dt_row
    src_row = acum_row - jnp.log2(dt_row)

    tri = (lax.broadcasted_iota(jnp.int32, (q, q), 0) >= lax.broadcasted_iota(jnp.int32, (q, q), 1))
    lane_head = lax.broadcasted_iota(jnp.int32, (1, gw), 1) // hp_
    low_half = lax.broadcasted_iota(jnp.int32, (1, LANES), 1) < hp_

    for g in range(ngroups):
        gs = slice(g * gw, (g + 1) * gw)
        ns = slice(g * n, (g + 1) * n)
        cg = c_ref[0, :, ns]
        btg = bt_ref[0, ns, :]
        btg_f = btg.astype(F32)
        xg_b = xs_ref[0, :, gs]
        heads = range(g * heads_per_group, (g + 1) * heads_per_group)
        cb = jnp.dot(cg, btg, preferred_element_type=F32)
        state = state_ref[g]
        y_inter = jnp.dot(cg, state.astype(BF16), preferred_element_type=F32)
        e_cols, y_intra, st_new = [], None, None
        for r, h in enumerate(heads):
            acol_b = jnp.broadcast_to(acum_col[:, h:h + 1], (q, LANES))
            e_cols.append(jnp.exp2(acol_b))
            seg = jnp.concatenate([acol_b] * (q // LANES), axis=1) - src_row[h:h + 1, :]
            mh = (cb * jnp.exp2(jnp.where(tri, seg, -jnp.inf))).astype(BF16)
            xh = jnp.where(lane_head == r, xg_b, jnp.zeros_like(xg_b))
            part = jnp.dot(mh, xh, preferred_element_type=F32)
            y_intra = part if y_intra is None else y_intra + part
            btw = (btg_f * w_row[h:h + 1, :]).astype(BF16)
            part = jnp.dot(btw, xh, preferred_element_type=F32)
            st_new = part if st_new is None else st_new + part
        e_x = jnp.concatenate([jnp.where(low_half, e_cols[2 * i], e_cols[2 * i + 1])
                               for i in range(gw // LANES)], axis=1)
        y_g = y_inter * e_x + y_intra
        d_x = _expand_heads([dec_row[:, h:h + 1] for h in heads], lane_head)
        state_ref[g] = state * d_x + st_new
        xg = xg_b.astype(F32)
        zg = z_ref[0, :, gs].astype(F32)
        yv = (y_g + dskip_ref[:, gs] * xg) * (zg * _sigmoid(zg))
        ms = jnp.mean(yv * yv, axis=-1, keepdims=True)
        y_ref[0, :, gs] = (yv * lax.rsqrt(ms + RMS_EPS) * nw_ref[:, gs]).astype(BF16)


def _ssd_scan(xs, bmt, cm, z, dt, dtt, arow, acol, dskip, nw, ltri, utri, q):
    b, s, d_inner = xs.shape
    gn = cm.shape[2]
    nheads = dtt.shape[1]
    ngroups = gn // SSM_STATE
    heads_per_group = nheads // ngroups
    gw = heads_per_group * SSM_HEAD_DIM
    blk = lambda w: pl.BlockSpec((1, q, w), lambda bi, ci: (bi, ci, 0))
    return pl.pallas_call(
        functools.partial(_ssd_scan_kernel, heads_per_group=heads_per_group),
        grid=(b, s // q),
        in_specs=[blk(d_inner), pl.BlockSpec((1, gn, q), lambda bi, ci: (bi, 0, ci)),
                  blk(gn), blk(d_inner), blk(LANES),
                  pl.BlockSpec((1, nheads, q), lambda bi, ci: (bi, 0, ci)),
                  _resident(arow.shape), _resident(acol.shape), _resident(dskip.shape),
                  _resident(nw.shape), _resident(ltri.shape), _resident(utri.shape)],
        out_specs=blk(d_inner),
        out_shape=jax.ShapeDtypeStruct((b, s, d_inner), BF16),
        scratch_shapes=[pltpu.VMEM((ngroups, SSM_STATE, gw), F32)],
        compiler_params=_params("arbitrary", "arbitrary"),
        name="ssd_scan",
    )(xs, bmt, cm, z, dt, dtt, arow, acol, dskip, nw, ltri, utri)


def _pad_cols(a, width):
    return jnp.pad(a, ((0, 0), (0, width - a.shape[1])))


def _fox_layer(xf, bsz, seq, w_in, b_f, w_out, g, b, alpha):
    m, d = xf.shape
    nh = d // ATTN_HEAD_DIM
    scale = LOG2E / math.sqrt(ATTN_HEAD_DIM)
    wqt = (w_in[:, :d] * scale).T.astype(BF16)
    wk = w_in[:, d:2 * d].astype(BF16)
    wvt = w_in[:, 2 * d:3 * d].T.astype(BF16)
    wft = w_in[:, 3 * d:].T.astype(BF16)
    utri = jnp.triu(jnp.ones((CUMSUM_CHUNK, CUMSUM_CHUNK), F32)).astype(BF16)
    blk = min(ATTN_BLOCK, seq)
    k, qt, vt, qta, ka = _fox_proj(xf, wk, wqt, wvt, wft, b_f.reshape(nh, 1), utri, bsz, seq, blk)
    o = _fox_attn(qt, qta, k.reshape(bsz, seq, d), ka.reshape(bsz, seq, -1), vt, blk)
    return _outproj_ln(o.reshape(m, d), w_out.astype(BF16), xf, g.reshape(1, d), b.reshape(1, d),
                       alpha, seq)


def _ssd_layer(xf, bsz, seq, w_in, conv_w, conv_b, dt_bias, a_log, d_skip, norm_w, w_out, g, b,
               alpha):
    m, d = xf.shape
    nheads = dt_bias.shape[0]
    d_inner = nheads * SSM_HEAD_DIM
    gn = SSM_GROUPS * SSM_STATE
    wz = w_in[:, :d_inner].astype(BF16)
    wxbc = w_in[:, d_inner:2 * d_inner + 2 * gn].astype(BF16)
    wdt = _pad_cols(w_in[:, 2 * d_inner + 2 * gn:], LANES).astype(BF16)
    dtb = _pad_cols(dt_bias.reshape(1, nheads), LANES)
    z, xs, bmt, cm, dt, dtt = _ssd_proj(xf, wz, wxbc, wdt, conv_w, conv_b.reshape(1, -1), dtb,
                                        bsz, seq, gn, nheads)
    a = -jnp.exp(a_log.astype(F32))
    arow = _pad_cols(a.reshape(1, nheads), LANES)
    acol = a.reshape(nheads, 1)
    dskip = jnp.repeat(d_skip, SSM_HEAD_DIM).reshape(1, d_inner)
    q = min(SSD_CHUNK, seq)
    ltri = jnp.tril(jnp.ones((q, q), F32)).astype(BF16)
    utri = jnp.triu(jnp.ones((q, q), F32)).astype(BF16)
    y = _ssd_scan(xs.reshape(bsz, seq, d_inner), bmt, cm.reshape(bsz, seq, gn),
                  z.reshape(bsz, seq, d_inner), dt.reshape(bsz, seq, LANES), dtt, arow, acol,
                  dskip, norm_w.reshape(1, d_inner), ltri, utri, q)
    return _outproj_ln(y.reshape(m, d_inner), w_out.astype(BF16), xf, g.reshape(1, d),
                       b.reshape(1, d), alpha, seq)


def kernel(x, p, attn_w_in, attn_b_f, attn_w_out, ssm_w_in, ssm_conv_w, ssm_conv_b, ssm_dt_bias, ssm_A_log, ssm_D, ssm_norm_w, ssm_w_out, ln_mix_g, ln_mix_b, ffn_w_up, ffn_conv_w, ffn_conv_b, ffn_w_down, ln_ffn_g, ln_ffn_b, ple_w_proj, ple_w_gate, ple_b_gate):
    bsz, seq, d = x.shape
    depth = p.shape[0]
    n_mixers = 2
    alpha = (2 * depth) ** 0.25
    xf = x.reshape(bsz * seq, d)
    for i in range(depth):
        j = i // n_mixers
        if i % n_mixers == 0:
            xf = _fox_layer(xf, bsz, seq, attn_w_in[j], attn_b_f[j], attn_w_out[j],
                            ln_mix_g[i], ln_mix_b[i], alpha)
        else:
            xf = _ssd_layer(xf, bsz, seq, ssm_w_in[j], ssm_conv_w[j], ssm_conv_b[j],
                            ssm_dt_bias[j], ssm_A_log[j], ssm_D[j], ssm_norm_w[j], ssm_w_out[j],
                            ln_mix_g[i], ln_mix_b[i], alpha)
        f = ffn_conv_w.shape[-1]
        xf = _ffn(xf, ffn_w_up[i][:, :f].astype(BF16), ffn_w_up[i][:, f:].astype(BF16),
                  ffn_conv_w[i], ffn_conv_b[i].reshape(1, f), ffn_w_down[i].astype(BF16),
                  ln_ffn_g[i].reshape(1, d), ln_ffn_b[i].reshape(1, d),
                  p[i].reshape(bsz * seq, -1), ple_w_proj[i].astype(BF16),
                  ple_w_gate[i].astype(BF16), ple_b_gate[i].reshape(1, d), alpha, seq)
    return xf.reshape(bsz, seq, d)
```

```python
import functools
import math

import jax
import jax.numpy as jnp
import numpy as np
from jax import lax
from jax.experimental import pallas as pl
from jax.experimental.pallas import tpu as pltpu

F32 = jnp.float32
BF16 = jnp.bfloat16

ATTN_HEAD_DIM = 64
SSM_HEAD_DIM = 64
SSM_GROUPS = 8
SSM_STATE = 128
LN_EPS = 1e-5
RMS_EPS = 1e-5

LANES = 128
SUBLANES = 8
VMEM_LIMIT_BYTES = 56 * 1024 * 1024

ROW_TILE = 512
ATTN_BLOCK = 512
ATTN_UNROLL = 4
SSD_CHUNK = 256
CUMSUM_CHUNK = 256
MASK_VALUE = -1e30
LOG2E = 1.4426950408889634
AUG_ROWS = 16
ONES_ROWS = 16


def _params(*sem):
    return pltpu.CompilerParams(dimension_semantics=sem, vmem_limit_bytes=VMEM_LIMIT_BYTES)


def _resident(shape):
    nd = len(shape)
    return pl.BlockSpec(shape, lambda *_: (0,) * nd, pipeline_mode=pl.Buffered(1))


def _split3(v):
    hi = v.astype(BF16)
    r1 = v - hi.astype(F32)
    mid = r1.astype(BF16)
    lo = (r1 - mid.astype(F32)).astype(BF16)
    return hi, mid, lo


def _softplus(v):
    return jnp.maximum(v, 0.0) + jnp.log1p(jnp.exp(-jnp.abs(v)))


def _sigmoid(v):
    return 1.0 / (1.0 + jnp.exp2(v * (-LOG2E)))


def _layer_norm(h, g, b):
    mu = jnp.mean(h, axis=-1, keepdims=True)
    d = h - mu
    var = jnp.mean(d * d, axis=-1, keepdims=True)
    return d * lax.rsqrt(var + LN_EPS) * g + b


_NT = (((1,), (1,)), ((), ()))


def _bias_placement(nh):
    place_q = np.zeros((nh * AUG_ROWS, LANES), np.float32)
    ones_q = np.zeros((nh * AUG_ROWS, 1), np.float32)
    place_k = np.zeros((LANES, (nh // 2) * LANES), np.float32)
    ones_k = np.zeros((1, (nh // 2) * LANES), np.float32)
    for h in range(nh):
        off = 6 * (h % 2)
        for p in range(3):
            ones_q[h * AUG_ROWS + off + p, 0] = 1.0
            place_q[h * AUG_ROWS + off + 3 + p, p * nh + h] = 1.0
            place_k[p * nh + h, (h // 2) * LANES + off + p] = -1.0
            ones_k[0, (h // 2) * LANES + off + 3 + p] = 1.0
    return (jnp.asarray(place_q, BF16), jnp.asarray(ones_q), jnp.asarray(place_k, BF16),
            jnp.asarray(ones_k))


def _fox_proj_kernel(x_ref, wk_ref, wqt_ref, wvt_ref, wft_ref, bf_ref, ut_ref, pq_ref, oq_ref,
                     pk_ref, ok_ref, k_ref, qt_ref, vt_ref, qta_ref, ka_ref, carry_ref, *,
                     steps_per_batch):
    tm, d = x_ref.shape

    @pl.when(pl.program_id(0) % steps_per_batch == 0)
    def _():
        carry_ref[...] = jnp.zeros_like(carry_ref)

    xb = x_ref[...].astype(BF16)
    k_ref[...] = jnp.dot(xb, wk_ref[...], preferred_element_type=F32).astype(BF16)
    qt_ref[0] = lax.dot_general(wqt_ref[...], xb, _NT, preferred_element_type=F32).astype(BF16)
    vt_ref[0, 0] = lax.dot_general(wvt_ref[...], xb, _NT, preferred_element_type=F32).astype(BF16)

    fl = lax.dot_general(wft_ref[...], xb, _NT, preferred_element_type=F32) + bf_ref[...]
    logf = -_softplus(-fl) * LOG2E
    nh = logf.shape[0]
    parts = jnp.concatenate(_split3(logf), axis=0)
    carry = carry_ref[...]
    zrows = jnp.zeros((LANES - 3 * nh, CUMSUM_CHUNK), F32)
    stacks = []
    for j in range(tm // CUMSUM_CHUNK):
        sl = slice(j * CUMSUM_CHUNK, (j + 1) * CUMSUM_CHUNK)
        pj = jnp.dot(parts[:, sl], ut_ref[...], preferred_element_type=F32)
        cj = (pj[0:nh] + pj[nh:2 * nh] + pj[2 * nh:3 * nh]) + carry
        stacks.append(jnp.concatenate([cp.astype(F32) for cp in _split3(cj)] + [zrows], axis=0))
        carry = cj[:, CUMSUM_CHUNK - 1:CUMSUM_CHUNK]
    carry_ref[...] = carry
    stack = jnp.concatenate(stacks, axis=1)
    qta_ref[0] = (jnp.dot(pq_ref[...], stack.astype(BF16), preferred_element_type=F32)
                  + oq_ref[...]).astype(BF16)
    ka_ref[...] = (jnp.dot(stack.T.astype(BF16), pk_ref[...], preferred_element_type=F32)
                   + ok_ref[...]).astype(BF16)


def _fox_proj(xf, wk, wqt, wvt, wft, bf_col, utri, bsz, seq, tm):
    m, d = xf.shape
    nh = wft.shape[0]
    nblk = seq // tm
    row = lambda i: (i, 0)
    place = _bias_placement(nh)
    out_shape = [jax.ShapeDtypeStruct((m, d), BF16), jax.ShapeDtypeStruct((bsz, d, seq), BF16),
                 jax.ShapeDtypeStruct((bsz, nblk, d, tm), BF16),
                 jax.ShapeDtypeStruct((bsz, nh * AUG_ROWS, seq), BF16),
                 jax.ShapeDtypeStruct((m, (nh // 2) * LANES), BF16)]
    return pl.pallas_call(
        functools.partial(_fox_proj_kernel, steps_per_batch=nblk),
        grid=(m // tm,),
        in_specs=[pl.BlockSpec((tm, d), row), _resident(wk.shape), _resident(wqt.shape),
                  _resident(wvt.shape), _resident(wft.shape), _resident(bf_col.shape),
                  _resident(utri.shape)] + [_resident(a.shape) for a in place],
        out_specs=[pl.BlockSpec((tm, d), row),
                   pl.BlockSpec((1, d, tm), lambda i: (i // nblk, 0, i % nblk)),
                   pl.BlockSpec((1, 1, d, tm), lambda i: (i // nblk, i % nblk, 0, 0)),
                   pl.BlockSpec((1, nh * AUG_ROWS, tm), lambda i: (i // nblk, 0, i % nblk)),
                   pl.BlockSpec((tm, (nh // 2) * LANES), row)],
        out_shape=out_shape,
        scratch_shapes=[pltpu.VMEM((nh, 1), F32)],
        compiler_params=_params("arbitrary"),
        name="fox_proj",
    )(xf, wk, wqt, wvt, wft, bf_col, utri, *place)


def _fox_attn_kernel(qt_ref, qta_ref, k_ref, ka_ref, vt_ref, o_ref, m_scr, acc_scr,
                     sa_scr, sb_scr, ma_scr, mb_scr, *, blk):
    qi = pl.program_id(2)
    hd = ATTN_HEAD_DIM
    qt2 = qt_ref[0]
    zhead = jnp.zeros((hd, blk), BF16)
    zpad = jnp.zeros((LANES - AUG_ROWS, blk), BF16)
    qts = [jnp.concatenate([qt2[0:hd], zhead, qta_ref[0, 0:AUG_ROWS], zpad], axis=0),
           jnp.concatenate([zhead, qt2[hd:2 * hd], qta_ref[0, AUG_ROWS:2 * AUG_ROWS], zpad], axis=0)]
    ones_rows = jnp.ones((ONES_ROWS, blk), BF16)
    key_id = lax.broadcasted_iota(jnp.int32, (blk, blk), 0)
    qry_id = lax.broadcasted_iota(jnp.int32, (blk, blk), 1)
    m_scr[...] = jnp.full_like(m_scr, MASK_VALUE)
    acc_scr[...] = jnp.zeros_like(acc_scr)

    def keys(j):
        start = pl.multiple_of(j * blk, blk)
        return jnp.concatenate([k_ref[0, pl.ds(start, blk), :], ka_ref[0, pl.ds(start, blk), :]],
                               axis=1)

    def values(j):
        vt2 = vt_ref[0, j]
        return [jnp.concatenate([vt2[h * hd:(h + 1) * hd], ones_rows], axis=0) for h in range(2)]

    def produce(kk, h, s_buf, m_buf):
        s = jnp.dot(kk, qts[h], preferred_element_type=F32)
        s_buf[h] = s
        m_buf[h] = jnp.max(s, axis=0, keepdims=True)

    def consume(vt, h, s_buf, m_buf, masked):
        s = s_buf[h]
        if masked:
            s = jnp.where(key_id <= qry_id, s, MASK_VALUE)
            m_blk = jnp.max(s, axis=0, keepdims=True)
        else:
            m_blk = m_buf[h]
        m_old = m_scr[h]
        m_new = jnp.maximum(m_old, m_blk)
        p = jnp.exp2(s - m_new).astype(BF16)
        alpha = jnp.exp2(m_old - m_new)
        acc_scr[h] = alpha * acc_scr[h] + jnp.dot(vt[h], p, preferred_element_type=F32)
        m_scr[h] = m_new

    kk0 = keys(0)
    for h in range(2):
        produce(kk0, h, sa_scr, ma_scr)

    bufs = ((sa_scr, ma_scr), (sb_scr, mb_scr))

    def advance(j0, nsteps):
        for i in range(nsteps):
            kk, vt = keys(j0 + i + 1), values(j0 + i)
            for h in range(2):
                produce(kk, h, *bufs[(i + 1) % 2])
                consume(vt, h, *bufs[i % 2], False)

    def body(t, carry):
        advance(ATTN_UNROLL * t, ATTN_UNROLL)
        return carry

    lax.fori_loop(0, qi // ATTN_UNROLL, body, 0)

    for rem in range(ATTN_UNROLL):
        @pl.when(qi % ATTN_UNROLL == rem)
        def _(rem=rem):
            advance(qi - rem, rem)
            vt = values(qi)
            for h in range(2):
                consume(vt, h, *bufs[rem % 2], True)

    outs = []
    for h in range(2):
        acc = acc_scr[h]
        outs.append(acc[0:hd] * (1.0 / acc[hd:hd + 1]))
    o_ref[0] = jnp.concatenate(outs, axis=0).T.astype(BF16)


def _fox_attn(qt, qta, k, ka, vt, blk):
    b, d, s = qt.shape
    nhp = d // LANES
    nblk = s // blk
    return pl.pallas_call(
        functools.partial(_fox_attn_kernel, blk=blk),
        grid=(b, nhp, nblk),
        in_specs=[
            pl.BlockSpec((1, LANES, blk), lambda bi, hp, qi: (bi, hp, qi)),
            pl.BlockSpec((1, 2 * AUG_ROWS, blk), lambda bi, hp, qi: (bi, hp, qi)),
            pl.BlockSpec((1, s, LANES), lambda bi, hp, qi: (bi, 0, hp)),
            pl.BlockSpec((1, s, LANES), lambda bi, hp, qi: (bi, 0, hp)),
            pl.BlockSpec((1, nblk, LANES, blk), lambda bi, hp, qi: (bi, 0, hp, 0)),
        ],
        out_specs=pl.BlockSpec((1, blk, LANES), lambda bi, hp, qi: (bi, qi, hp)),
        out_shape=jax.ShapeDtypeStruct((b, s, d), BF16),
        scratch_shapes=[pltpu.VMEM((2, 1, blk), F32),
                        pltpu.VMEM((2, ATTN_HEAD_DIM + ONES_ROWS, blk), F32),
                        pltpu.VMEM((2, blk, blk), F32), pltpu.VMEM((2, blk, blk), F32),
                        pltpu.VMEM((2, 1, blk), F32), pltpu.VMEM((2, 1, blk), F32)],
        compiler_params=_params("arbitrary", "arbitrary", "arbitrary"),
        name="fox_attn",
    )(qt, qta, k, ka, vt)


def _ffn_kernel(y_ref, wo_ref, x_ref, g1_ref, b1_ref, wu_ref, wg_ref, cw_ref, cb_ref, wd_ref,
                g_ref, b_ref, p_ref, wproj_ref, wgate_ref, bgate_ref, o_ref, gbuf_ref, carry_ref,
                *, chunks, steps_per_batch, alpha):
    tm = x_ref.shape[0]
    halo = SUBLANES

    @pl.when(pl.program_id(0) % steps_per_batch == 0)
    def _():
        carry_ref[...] = jnp.zeros_like(carry_ref)

    mix = jnp.dot(y_ref[...], wo_ref[...], preferred_element_type=F32)
    x = _layer_norm(alpha * x_ref[...] + mix, g1_ref[...], b1_ref[...])
    xb = x.astype(BF16)
    acc = None
    for c0, cw in chunks:
        cs = slice(c0, c0 + cw)
        u = jnp.dot(xb, wu_ref[:, cs], preferred_element_type=F32)
        g = jnp.dot(xb, wg_ref[:, cs], preferred_element_type=F32)
        gbuf_ref[0:halo, 0:cw] = carry_ref[:, cs]
        gbuf_ref[halo:halo + tm, 0:cw] = g
        carry_ref[:, cs] = g[tm - halo:tm, :]
        conv = (cb_ref[:, cs] + cw_ref[2:3, cs] * g
                + cw_ref[1:2, cs] * gbuf_ref[halo - 1:halo - 1 + tm, 0:cw]
                + cw_ref[0:1, cs] * gbuf_ref[halo - 2:halo - 2 + tm, 0:cw])
        gelu = 0.5 * conv * (1.0 + lax.erf(conv * (1.0 / math.sqrt(2.0))))
        part = jnp.dot((gelu * u).astype(BF16), wd_ref[cs, :], preferred_element_type=F32)
        acc = part if acc is None else acc + part
    x2 = _layer_norm(alpha * x + acc, g_ref[...], b_ref[...])
    gate = _sigmoid(jnp.dot(x2.astype(BF16), wgate_ref[...], preferred_element_type=F32)
                    + bgate_ref[...])
    emb = jnp.dot(p_ref[...].astype(BF16), wproj_ref[...], preferred_element_type=F32)
    o_ref[...] = x2 + gate * emb


def _ffn_chunks(f):
    tile = 2 * LANES
    if f <= 6 * tile or f % tile:
        return ((0, f),)
    first = (f // tile + 1) // 2 * tile
    return ((0, first), (first, f - first))


def _ffn(y, wo, xf, g1, b1, wu, wg, cw, cb, wd, g, b, pf, wproj, wgate, bgate, alpha, seq):
    m, d = xf.shape
    f = wu.shape[1]
    pdim = pf.shape[1]
    kdim = y.shape[1]
    tm = min(ROW_TILE, seq)
    chunks = _ffn_chunks(f)
    wmax = max(c[1] for c in chunks)
    row = lambda i: (i, 0)
    return pl.pallas_call(
        functools.partial(_ffn_kernel, chunks=chunks, steps_per_batch=seq // tm, alpha=alpha),
        grid=(m // tm,),
        in_specs=[pl.BlockSpec((tm, kdim), row), _resident(wo.shape), pl.BlockSpec((tm, d), row),
                  _resident(g1.shape), _resident(b1.shape), _resident(wu.shape), _resident(wg.shape),
                  _resident(cw.shape), _resident(cb.shape), _resident(wd.shape),
                  _resident(g.shape), _resident(b.shape), pl.BlockSpec((tm, pdim), row),
                  _resident(wproj.shape), _resident(wgate.shape), _resident(bgate.shape)],
        out_specs=pl.BlockSpec((tm, d), row),
        out_shape=jax.ShapeDtypeStruct((m, d), F32),
        scratch_shapes=[pltpu.VMEM((tm + SUBLANES, wmax), F32), pltpu.VMEM((SUBLANES, f), F32)],
        compiler_params=_params("arbitrary"),
        name="conv_ffn",
    )(y, wo, xf, g1, b1, wu, wg, cw, cb, wd, g, b, pf, wproj, wgate, bgate)


def _ssd_proj_kernel(x_ref, wz_ref, wxbc_ref, wdt_ref, cw_ref, cb_ref, dtb_ref,
                     z_ref, xs_ref, bt_ref, c_ref, dt_ref, dtt_ref, buf0_ref, buf1_ref, carry_ref,
                     *, steps_per_batch, col_tile):
    tm = x_ref.shape[0]
    halo = SUBLANES
    kconv = cw_ref.shape[0]
    bufs = (buf0_ref, buf1_ref)

    @pl.when(pl.program_id(0) % steps_per_batch == 0)
    def _():
        carry_ref[...] = jnp.zeros_like(carry_ref)

    xb = x_ref[...].astype(BF16)
    d_inner = xs_ref.shape[1]
    gn = c_ref.shape[1]
    nconv = wxbc_ref.shape[1] // col_tile
    z_chunks = list(range(0, wz_ref.shape[1], col_tile))

    def project(i):
        cs = slice(i * col_tile, (i + 1) * col_tile)
        buf = bufs[i % 2]
        r = jnp.dot(xb, wxbc_ref[:, cs], preferred_element_type=F32)
        buf[0:halo, :] = carry_ref[:, cs]
        buf[halo:halo + tm, :] = r
        carry_ref[:, cs] = r[tm - halo:tm, :]

    def project_z():
        if z_chunks:
            zs = slice(z_chunks[0], z_chunks.pop(0) + col_tile)
            z_ref[:, zs] = jnp.dot(xb, wz_ref[:, zs], preferred_element_type=F32).astype(BF16)

    def conv_act(i):
        c0 = i * col_tile
        cs = slice(c0, c0 + col_tile)
        buf = bufs[i % 2]
        conv = cb_ref[:, cs] + cw_ref[kconv - 1:kconv, cs] * buf[halo:halo + tm, :]
        for k in range(kconv - 1):
            off = halo - (kconv - 1) + k
            conv = conv + cw_ref[k:k + 1, cs] * buf[off:off + tm, :]
        act = conv * _sigmoid(conv)
        if c0 < d_inner:
            xs_ref[:, cs] = act.astype(BF16)
        elif c0 < d_inner + gn:
            bt_ref[0, c0 - d_inner:c0 - d_inner + col_tile, :] = act.T.astype(BF16)
        else:
            c_ref[:, c0 - d_inner - gn:c0 - d_inner - gn + col_tile] = act.astype(BF16)

    project(0)
    for i in range(nconv):
        if i + 1 < nconv:
            project(i + 1)
        else:
            project_z()
        conv_act(i)
    while z_chunks:
        project_z()
    dt = _softplus(jnp.dot(xb, wdt_ref[...], preferred_element_type=F32) + dtb_ref[...])
    dt_ref[...] = dt
    dtt_ref[0] = dt.T[0:dtt_ref.shape[1], :]


def _ssd_proj(xf, wz, wxbc, wdt, cw, cb, dtb, bsz, seq, gn, nheads):
    m, d = xf.shape
    d_inner = wz.shape[1]
    tm = min(ROW_TILE, seq)
    nblk = seq // tm
    col_tile = min(1024, gn)
    row = lambda i: (i, 0)
    tposed = lambda i: (i // nblk, 0, i % nblk)
    out_shape = [jax.ShapeDtypeStruct((m, d_inner), BF16), jax.ShapeDtypeStruct((m, d_inner), BF16),
                 jax.ShapeDtypeStruct((bsz, gn, seq), BF16), jax.ShapeDtypeStruct((m, gn), BF16),
                 jax.ShapeDtypeStruct((m, LANES), F32),
                 jax.ShapeDtypeStruct((bsz, nheads, seq), F32)]
    return pl.pallas_call(
        functools.partial(_ssd_proj_kernel, steps_per_batch=seq // tm, col_tile=col_tile),
        grid=(m // tm,),
        in_specs=[pl.BlockSpec((tm, d), row), _resident(wz.shape), _resident(wxbc.shape),
                  _resident(wdt.shape), _resident(cw.shape), _resident(cb.shape),
                  _resident(dtb.shape)],
        out_specs=[pl.BlockSpec((tm, d_inner), row), pl.BlockSpec((tm, d_inner), row),
                   pl.BlockSpec((1, gn, tm), tposed), pl.BlockSpec((tm, gn), row),
                   pl.BlockSpec((tm, LANES), row), pl.BlockSpec((1, nheads, tm), tposed)],
        out_shape=out_shape,
        scratch_shapes=[pltpu.VMEM((tm + SUBLANES, col_tile), F32),
                        pltpu.VMEM((tm + SUBLANES, col_tile), F32),
                        pltpu.VMEM((SUBLANES, wxbc.shape[1]), F32)],
        compiler_params=_params("arbitrary"),
        name="ssd_proj",
    )(xf, wz, wxbc, wdt, cw, cb, dtb)


def _expand_heads(cols, lane_head):
    out = cols[0]
    for r in range(1, len(cols)):
        out = jnp.where(lane_head >= r, cols[r], out)
    return out


def _ssd_scan_kernel(xs_ref, bt_ref, c_ref, z_ref, dt_ref, dtt_ref, arow_ref, acol_ref,
                     dskip_ref, nw_ref, ltri_ref, utri_ref, y_ref, state_ref, *, heads_per_group):
    q = xs_ref.shape[1]
    n = SSM_STATE
    hp_ = SSM_HEAD_DIM
    gw = heads_per_group * hp_
    ngroups = xs_ref.shape[2] // gw

    @pl.when(pl.program_id(1) == 0)
    def _():
        state_ref[...] = jnp.zeros_like(state_ref)

    dt_col = dt_ref[0]
    dt_row = dtt_ref[0]
    nheads = dt_row.shape[0]
    a_col = dt_col * (arow_ref[...] * LOG2E)
    pc = jnp.dot(ltri_ref[...], jnp.concatenate(_split3(a_col), axis=1),
                 preferred_element_type=F32)
    acum_col = pc[:, 0:LANES] + pc[:, LANES:2 * LANES] + pc[:, 2 * LANES:3 * LANES]
    a_row = dt_row * (acol_ref[...] * LOG2E)
    pr = jnp.dot(jnp.concatenate(_split3(a_row), axis=0), utri_ref[...],
                 preferred_element_type=F32)
    acum_row = pr[0:nheads] + pr[nheads:2 * nheads] + pr[2 * nheads:3 * nheads]
    dec_row = jnp.exp2(acum_col[q - 1:q, :])
    w_row = jnp.exp2(acum_row[:, q - 1:q] - acum_row) * dt_row
    src_row = acum_row - jnp.log2(dt_row)

    tri = (lax.broadcasted_iota(jnp.int32, (q, q), 0) >= lax.broadcasted_iota(jnp.int32, (q, q), 1))
    lane_head = lax.broadcasted_iota(jnp.int32, (1, gw), 1) // hp_
    low_half = lax.broadcasted_iota(jnp.int32, (1, LANES), 1) < hp_

    for g in range(ngroups):
        gs = slice(g * gw, (g + 1) * gw)
        ns = slice(g * n, (g + 1) * n)
        cg = c_ref[0, :, ns]
        btg = bt_ref[0, ns, :]
        btg_f = btg.astype(F32)
        xg_b = xs_ref[0, :, gs]
        heads = range(g * heads_per_group, (g + 1) * heads_per_group)
        cb = jnp.dot(cg, btg, preferred_element_type=F32)
        state = state_ref[g]
        y_inter = jnp.dot(cg, state.astype(BF16), preferred_element_type=F32)
        e_cols, y_intra, st_new = [], None, None
        for r, h in enumerate(heads):
            acol_b = jnp.broadcast_to(acum_col[:, h:h + 1], (q, LANES))
            e_cols.append(jnp.exp2(acol_b))
            seg = jnp.concatenate([acol_b] * (q // LANES), axis=1) - src_row[h:h + 1, :]
            mh = (cb * jnp.exp2(jnp.where(tri, seg, -jnp.inf))).astype(BF16)
            xh = jnp.where(lane_head == r, xg_b, jnp.zeros_like(xg_b))
            part = jnp.dot(mh, xh, preferred_element_type=F32)
            y_intra = part if y_intra is None else y_intra + part
            btw = (btg_f * w_row[h:h + 1, :]).astype(BF16)
            part = jnp.dot(btw, xh, preferred_element_type=F32)
            st_new = part if st_new is None else st_new + part
        e_x = jnp.concatenate([jnp.where(low_half, e_cols[2 * i], e_cols[2 * i + 1])
                               for i in range(gw // LANES)], axis=1)
        y_g = y_inter * e_x + y_intra
        d_x = _expand_heads([dec_row[:, h:h + 1] for h in heads], lane_head)
        state_ref[g] = state * d_x + st_new
        xg = xg_b.astype(F32)
        zg = z_ref[0, :, gs].astype(F32)
        yv = (y_g + dskip_ref[:, gs] * xg) * (zg * _sigmoid(zg))
        ms = jnp.mean(yv * yv, axis=-1, keepdims=True)
        y_ref[0, :, gs] = (yv * lax.rsqrt(ms + RMS_EPS) * nw_ref[:, gs]).astype(BF16)


def _ssd_scan(xs, bmt, cm, z, dt, dtt, arow, acol, dskip, nw, ltri, utri, q):
    b, s, d_inner = xs.shape
    gn = cm.shape[2]
    nheads = dtt.shape[1]
    ngroups = gn // SSM_STATE
    heads_per_group = nheads // ngroups
    gw = heads_per_group * SSM_HEAD_DIM
    blk = lambda w: pl.BlockSpec((1, q, w), lambda bi, ci: (bi, ci, 0))
    return pl.pallas_call(
        functools.partial(_ssd_scan_kernel, heads_per_group=heads_per_group),
        grid=(b, s // q),
        in_specs=[blk(d_inner), pl.BlockSpec((1, gn, q), lambda bi, ci: (bi, 0, ci)),
                  blk(gn), blk(d_inner), blk(LANES),
                  pl.BlockSpec((1, nheads, q), lambda bi, ci: (bi, 0, ci)),
                  _resident(arow.shape), _resident(acol.shape), _resident(dskip.shape),
                  _resident(nw.shape), _resident(ltri.shape), _resident(utri.shape)],
        out_specs=blk(d_inner),
        out_shape=jax.ShapeDtypeStruct((b, s, d_inner), BF16),
        scratch_shapes=[pltpu.VMEM((ngroups, SSM_STATE, gw), F32)],
        compiler_params=_params("arbitrary", "arbitrary"),
        name="ssd_scan",
    )(xs, bmt, cm, z, dt, dtt, arow, acol, dskip, nw, ltri, utri)


def _pad_cols(a, width):
    return jnp.pad(a, ((0, 0), (0, width - a.shape[1])))


def _fox_mixer(xf, bsz, seq, w_in, b_f):
    m, d = xf.shape
    nh = d // ATTN_HEAD_DIM
    scale = LOG2E / math.sqrt(ATTN_HEAD_DIM)
    wqt = (w_in[:, :d] * scale).T.astype(BF16)
    wk = w_in[:, d:2 * d].astype(BF16)
    wvt = w_in[:, 2 * d:3 * d].T.astype(BF16)
    wft = w_in[:, 3 * d:].T.astype(BF16)
    utri = jnp.triu(jnp.ones((CUMSUM_CHUNK, CUMSUM_CHUNK), F32)).astype(BF16)
    blk = min(ATTN_BLOCK, seq)
    k, qt, vt, qta, ka = _fox_proj(xf, wk, wqt, wvt, wft, b_f.reshape(nh, 1), utri, bsz, seq, blk)
    o = _fox_attn(qt, qta, k.reshape(bsz, seq, d), ka.reshape(bsz, seq, -1), vt, blk)
    return o.reshape(m, d)


def _ssd_mixer(xf, bsz, seq, w_in, conv_w, conv_b, dt_bias, a_log, d_skip, norm_w):
    m, d = xf.shape
    nheads = dt_bias.shape[0]
    d_inner = nheads * SSM_HEAD_DIM
    gn = SSM_GROUPS * SSM_STATE
    wz = w_in[:, :d_inner].astype(BF16)
    wxbc = w_in[:, d_inner:2 * d_inner + 2 * gn].astype(BF16)
    wdt = _pad_cols(w_in[:, 2 * d_inner + 2 * gn:], LANES).astype(BF16)
    dtb = _pad_cols(dt_bias.reshape(1, nheads), LANES)
    z, xs, bmt, cm, dt, dtt = _ssd_proj(xf, wz, wxbc, wdt, conv_w, conv_b.reshape(1, -1), dtb,
                                        bsz, seq, gn, nheads)
    a = -jnp.exp(a_log.astype(F32))
    arow = _pad_cols(a.reshape(1, nheads), LANES)
    acol = a.reshape(nheads, 1)
    dskip = jnp.repeat(d_skip, SSM_HEAD_DIM).reshape(1, d_inner)
    q = min(SSD_CHUNK, seq)
    ltri = jnp.tril(jnp.ones((q, q), F32)).astype(BF16)
    utri = jnp.triu(jnp.ones((q, q), F32)).astype(BF16)
    y = _ssd_scan(xs.reshape(bsz, seq, d_inner), bmt, cm.reshape(bsz, seq, gn),
                  z.reshape(bsz, seq, d_inner), dt.reshape(bsz, seq, LANES), dtt, arow, acol,
                  dskip, norm_w.reshape(1, d_inner), ltri, utri, q)
    return y.reshape(m, d_inner)


def kernel(x, p, attn_w_in, attn_b_f, attn_w_out, ssm_w_in, ssm_conv_w, ssm_conv_b, ssm_dt_bias, ssm_A_log, ssm_D, ssm_norm_w, ssm_w_out, ln_mix_g, ln_mix_b, ffn_w_up, ffn_conv_w, ffn_conv_b, ffn_w_down, ln_ffn_g, ln_ffn_b, ple_w_proj, ple_w_gate, ple_b_gate):
    bsz, seq, d = x.shape
    depth = p.shape[0]
    n_mixers = 2
    alpha = (2 * depth) ** 0.25
    xf = x.reshape(bsz * seq, d)
    for i in range(depth):
        j = i // n_mixers
        if i % n_mixers == 0:
            y, w_out = _fox_mixer(xf, bsz, seq, attn_w_in[j], attn_b_f[j]), attn_w_out[j]
        else:
            y = _ssd_mixer(xf, bsz, seq, ssm_w_in[j], ssm_conv_w[j], ssm_conv_b[j],
                           ssm_dt_bias[j], ssm_A_log[j], ssm_D[j], ssm_norm_w[j])
            w_out = ssm_w_out[j]
        f = ffn_conv_w.shape[-1]
        xf = _ffn(y, w_out.astype(BF16), xf, ln_mix_g[i].reshape(1, d), ln_mix_b[i].reshape(1, d),
                  ffn_w_up[i][:, :f].astype(BF16), ffn_w_up[i][:, f:].astype(BF16),
                  ffn_conv_w[i], ffn_conv_b[i].reshape(1, f), ffn_w_down[i].astype(BF16),
                  ln_ffn_g[i].reshape(1, d), ln_ffn_b[i].reshape(1, d),
                  p[i].reshape(bsz * seq, -1), ple_w_proj[i].astype(BF16),
                  ple_w_gate[i].astype(BF16), ple_b_gate[i].reshape(1, d), alpha, seq)
    return xf.reshape(bsz, seq, d)
```

```python
import functools
import math

import jax
import jax.numpy as jnp
import numpy as np
from jax import lax
from jax.experimental import pallas as pl
from jax.experimental.pallas import tpu as pltpu

F32 = jnp.float32
BF16 = jnp.bfloat16

ATTN_HEAD_DIM = 64
SSM_HEAD_DIM = 64
SSM_GROUPS = 8
SSM_STATE = 128
LN_EPS = 1e-5
RMS_EPS = 1e-5

LANES = 128
SUBLANES = 8
VMEM_LIMIT_BYTES = 56 * 1024 * 1024

ROW_TILE = 512
ATTN_BLOCK = 512
ATTN_UNROLL = 4
SSD_CHUNK = 256
CUMSUM_CHUNK = 256
MASK_VALUE = -1e30
LOG2E = 1.4426950408889634
AUG_ROWS = 16
ONES_ROWS = 16


def _params(*sem):
    return pltpu.CompilerParams(dimension_semantics=sem, vmem_limit_bytes=VMEM_LIMIT_BYTES)


def _resident(shape):
    nd = len(shape)
    return pl.BlockSpec(shape, lambda *_: (0,) * nd, pipeline_mode=pl.Buffered(1))


def _split3(v):
    hi = v.astype(BF16)
    r1 = v - hi.astype(F32)
    mid = r1.astype(BF16)
    lo = (r1 - mid.astype(F32)).astype(BF16)
    return hi, mid, lo


def _softplus(v):
    return jnp.maximum(v, 0.0) + jnp.log1p(jnp.exp(-jnp.abs(v)))


def _sigmoid(v):
    return 1.0 / (1.0 + jnp.exp2(v * (-LOG2E)))


def _layer_norm(h, g, b):
    mu = jnp.mean(h, axis=-1, keepdims=True)
    d = h - mu
    var = jnp.mean(d * d, axis=-1, keepdims=True)
    return d * lax.rsqrt(var + LN_EPS) * g + b


_NT = (((1,), (1,)), ((), ()))


def _bias_placement(nh):
    place_q = np.zeros((nh * AUG_ROWS, LANES), np.float32)
    ones_q = np.zeros((nh * AUG_ROWS, 1), np.float32)
    place_k = np.zeros((LANES, (nh // 2) * LANES), np.float32)
    ones_k = np.zeros((1, (nh // 2) * LANES), np.float32)
    for h in range(nh):
        off = 6 * (h % 2)
        for p in range(3):
            ones_q[h * AUG_ROWS + off + p, 0] = 1.0
            place_q[h * AUG_ROWS + off + 3 + p, p * nh + h] = 1.0
            place_k[p * nh + h, (h // 2) * LANES + off + p] = -1.0
            ones_k[0, (h // 2) * LANES + off + 3 + p] = 1.0
    return (jnp.asarray(place_q, BF16), jnp.asarray(ones_q), jnp.asarray(place_k, BF16),
            jnp.asarray(ones_k))


def _fox_proj_kernel(x_ref, wk_ref, wqt_ref, wvt_ref, wft_ref, bf_ref, ut_ref, pq_ref, oq_ref,
                     pk_ref, ok_ref, k_ref, qt_ref, vt_ref, qta_ref, ka_ref, carry_ref, *,
                     steps_per_batch):
    tm, d = x_ref.shape

    @pl.when(pl.program_id(0) % steps_per_batch == 0)
    def _():
        carry_ref[...] = jnp.zeros_like(carry_ref)

    xb = x_ref[...].astype(BF16)
    k_ref[...] = jnp.dot(xb, wk_ref[...], preferred_element_type=F32).astype(BF16)
    qt_ref[0] = lax.dot_general(wqt_ref[...], xb, _NT, preferred_element_type=F32).astype(BF16)
    vt_ref[0, 0] = lax.dot_general(wvt_ref[...], xb, _NT, preferred_element_type=F32).astype(BF16)

    fl = lax.dot_general(wft_ref[...], xb, _NT, preferred_element_type=F32) + bf_ref[...]
    logf = -_softplus(-fl) * LOG2E
    nh = logf.shape[0]
    parts = jnp.concatenate(_split3(logf), axis=0)
    carry = carry_ref[...]
    zrows = jnp.zeros((LANES - 3 * nh, CUMSUM_CHUNK), F32)
    stacks = []
    for j in range(tm // CUMSUM_CHUNK):
        sl = slice(j * CUMSUM_CHUNK, (j + 1) * CUMSUM_CHUNK)
        pj = jnp.dot(parts[:, sl], ut_ref[...], preferred_element_type=F32)
        cj = (pj[0:nh] + pj[nh:2 * nh] + pj[2 * nh:3 * nh]) + carry
        stacks.append(jnp.concatenate([cp.astype(F32) for cp in _split3(cj)] + [zrows], axis=0))
        carry = cj[:, CUMSUM_CHUNK - 1:CUMSUM_CHUNK]
    carry_ref[...] = carry
    stack = jnp.concatenate(stacks, axis=1)
    qta_ref[0] = (jnp.dot(pq_ref[...], stack.astype(BF16), preferred_element_type=F32)
                  + oq_ref[...]).astype(BF16)
    ka_ref[...] = (jnp.dot(stack.T.astype(BF16), pk_ref[...], preferred_element_type=F32)
                   + ok_ref[...]).astype(BF16)


def _fox_proj(xf, wk, wqt, wvt, wft, bf_col, utri, bsz, seq, tm):
    m, d = xf.shape
    nh = wft.shape[0]
    nblk = seq // tm
    row = lambda i: (i, 0)
    place = _bias_placement(nh)
    out_shape = [jax.ShapeDtypeStruct((m, d), BF16), jax.ShapeDtypeStruct((bsz, d, seq), BF16),
                 jax.ShapeDtypeStruct((bsz, nblk, d, tm), BF16),
                 jax.ShapeDtypeStruct((bsz, nh * AUG_ROWS, seq), BF16),
                 jax.ShapeDtypeStruct((m, (nh // 2) * LANES), BF16)]
    return pl.pallas_call(
        functools.partial(_fox_proj_kernel, steps_per_batch=nblk),
        grid=(m // tm,),
        in_specs=[pl.BlockSpec((tm, d), row), _resident(wk.shape), _resident(wqt.shape),
                  _resident(wvt.shape), _resident(wft.shape), _resident(bf_col.shape),
                  _resident(utri.shape)] + [_resident(a.shape) for a in place],
        out_specs=[pl.BlockSpec((tm, d), row),
                   pl.BlockSpec((1, d, tm), lambda i: (i // nblk, 0, i % nblk)),
                   pl.BlockSpec((1, 1, d, tm), lambda i: (i // nblk, i % nblk, 0, 0)),
                   pl.BlockSpec((1, nh * AUG_ROWS, tm), lambda i: (i // nblk, 0, i % nblk)),
                   pl.BlockSpec((tm, (nh // 2) * LANES), row)],
        out_shape=out_shape,
        scratch_shapes=[pltpu.VMEM((nh, 1), F32)],
        compiler_params=_params("arbitrary"),
        name="fox_proj",
    )(xf, wk, wqt, wvt, wft, bf_col, utri, *place)


def _fox_attn_kernel(qt_ref, qta_ref, k_ref, ka_ref, vt_ref, o_ref, m_scr, acc_scr,
                     sa_scr, sb_scr, ma_scr, mb_scr, *, blk):
    qi = pl.program_id(2)
    hd = ATTN_HEAD_DIM
    qt2 = qt_ref[0]
    zhead = jnp.zeros((hd, blk), BF16)
    zpad = jnp.zeros((LANES - AUG_ROWS, blk), BF16)
    qts = [jnp.concatenate([qt2[0:hd], zhead, qta_ref[0, 0:AUG_ROWS], zpad], axis=0),
           jnp.concatenate([zhead, qt2[hd:2 * hd], qta_ref[0, AUG_ROWS:2 * AUG_ROWS], zpad], axis=0)]
    ones_rows = jnp.ones((ONES_ROWS, blk), BF16)
    key_id = lax.broadcasted_iota(jnp.int32, (blk, blk), 0)
    qry_id = lax.broadcasted_iota(jnp.int32, (blk, blk), 1)
    m_scr[...] = jnp.full_like(m_scr, MASK_VALUE)
    acc_scr[...] = jnp.zeros_like(acc_scr)

    def keys(j):
        start = pl.multiple_of(j * blk, blk)
        return jnp.concatenate([k_ref[0, pl.ds(start, blk), :], ka_ref[0, pl.ds(start, blk), :]],
                               axis=1)

    def values(j):
        vt2 = vt_ref[0, j]
        return [jnp.concatenate([vt2[h * hd:(h + 1) * hd], ones_rows], axis=0) for h in range(2)]

    def produce(kk, h, s_buf, m_buf):
        s = jnp.dot(kk, qts[h], preferred_element_type=F32)
        s_buf[h] = s
        m_buf[h] = jnp.max(s, axis=0, keepdims=True)

    def consume(vt, h, s_buf, m_buf, masked):
        s = s_buf[h]
        if masked:
            s = jnp.where(key_id <= qry_id, s, MASK_VALUE)
            m_blk = jnp.max(s, axis=0, keepdims=True)
        else:
            m_blk = m_buf[h]
        m_old = m_scr[h]
        m_new = jnp.maximum(m_old, m_blk)
        p = jnp.exp2((s - m_new).astype(BF16))
        alpha = jnp.exp2(m_old - m_new)
        acc_scr[h] = alpha * acc_scr[h] + jnp.dot(vt[h], p, preferred_element_type=F32)
        m_scr[h] = m_new

    kk0 = keys(0)
    for h in range(2):
        produce(kk0, h, sa_scr, ma_scr)

    bufs = ((sa_scr, ma_scr), (sb_scr, mb_scr))

    def advance(j0, nsteps):
        for i in range(nsteps):
            kk, vt = keys(j0 + i + 1), values(j0 + i)
            for h in range(2):
                produce(kk, h, *bufs[(i + 1) % 2])
                consume(vt, h, *bufs[i % 2], False)

    def body(t, carry):
        advance(ATTN_UNROLL * t, ATTN_UNROLL)
        return carry

    lax.fori_loop(0, qi // ATTN_UNROLL, body, 0)

    for rem in range(ATTN_UNROLL):
        @pl.when(qi % ATTN_UNROLL == rem)
        def _(rem=rem):
            advance(qi - rem, rem)
            vt = values(qi)
            for h in range(2):
                consume(vt, h, *bufs[rem % 2], True)

    outs = []
    for h in range(2):
        acc = acc_scr[h]
        outs.append(acc[0:hd] * (1.0 / acc[hd:hd + 1]))
    o_ref[0] = jnp.concatenate(outs, axis=0).T.astype(BF16)


def _fox_attn(qt, qta, k, ka, vt, blk):
    b, d, s = qt.shape
    nhp = d // LANES
    nblk = s // blk
    return pl.pallas_call(
        functools.partial(_fox_attn_kernel, blk=blk),
        grid=(b, nhp, nblk),
        in_specs=[
            pl.BlockSpec((1, LANES, blk), lambda bi, hp, qi: (bi, hp, qi)),
            pl.BlockSpec((1, 2 * AUG_ROWS, blk), lambda bi, hp, qi: (bi, hp, qi)),
            pl.BlockSpec((1, s, LANES), lambda bi, hp, qi: (bi, 0, hp)),
            pl.BlockSpec((1, s, LANES), lambda bi, hp, qi: (bi, 0, hp)),
            pl.BlockSpec((1, nblk, LANES, blk), lambda bi, hp, qi: (bi, 0, hp, 0)),
        ],
        out_specs=pl.BlockSpec((1, blk, LANES), lambda bi, hp, qi: (bi, qi, hp)),
        out_shape=jax.ShapeDtypeStruct((b, s, d), BF16),
        scratch_shapes=[pltpu.VMEM((2, 1, blk), F32),
                        pltpu.VMEM((2, ATTN_HEAD_DIM + ONES_ROWS, blk), F32),
                        pltpu.VMEM((2, blk, blk), F32), pltpu.VMEM((2, blk, blk), F32),
                        pltpu.VMEM((2, 1, blk), F32), pltpu.VMEM((2, 1, blk), F32)],
        compiler_params=_params("arbitrary", "arbitrary", "arbitrary"),
        name="fox_attn",
    )(qt, qta, k, ka, vt)


def _ffn_kernel(y_ref, wo_ref, x_ref, g1_ref, b1_ref, wu_ref, wg_ref, cw_ref, cb_ref, wd_ref,
                g_ref, b_ref, p_ref, wproj_ref, wgate_ref, bgate_ref, o_ref, gbuf_ref, carry_ref,
                *, chunks, steps_per_batch, alpha):
    tm = x_ref.shape[0]
    halo = SUBLANES

    @pl.when(pl.program_id(0) % steps_per_batch == 0)
    def _():
        carry_ref[...] = jnp.zeros_like(carry_ref)

    mix = jnp.dot(y_ref[...], wo_ref[...], preferred_element_type=F32)
    x = _layer_norm(alpha * x_ref[...] + mix, g1_ref[...], b1_ref[...])
    xb = x.astype(BF16)
    acc = None
    for c0, cw in chunks:
        cs = slice(c0, c0 + cw)
        u = jnp.dot(xb, wu_ref[:, cs], preferred_element_type=F32)
        g = jnp.dot(xb, wg_ref[:, cs], preferred_element_type=F32)
        gbuf_ref[0:halo, 0:cw] = carry_ref[:, cs]
        gbuf_ref[halo:halo + tm, 0:cw] = g
        carry_ref[:, cs] = g[tm - halo:tm, :]
        conv = (cb_ref[:, cs] + cw_ref[2:3, cs] * g
                + cw_ref[1:2, cs] * gbuf_ref[halo - 1:halo - 1 + tm, 0:cw]
                + cw_ref[0:1, cs] * gbuf_ref[halo - 2:halo - 2 + tm, 0:cw])
        gelu = 0.5 * conv * (1.0 + lax.erf(conv * (1.0 / math.sqrt(2.0))))
        part = jnp.dot((gelu * u).astype(BF16), wd_ref[cs, :], preferred_element_type=F32)
        acc = part if acc is None else acc + part
    x2 = _layer_norm(alpha * x + acc, g_ref[...], b_ref[...])
    gate = _sigmoid(jnp.dot(x2.astype(BF16), wgate_ref[...], preferred_element_type=F32)
                    + bgate_ref[...])
    emb = jnp.dot(p_ref[...].astype(BF16), wproj_ref[...], preferred_element_type=F32)
    o_ref[...] = x2 + gate * emb


def _ffn_chunks(f):
    tile = 2 * LANES
    if f <= 6 * tile or f % tile:
        return ((0, f),)
    first = (f // tile + 1) // 2 * tile
    return ((0, first), (first, f - first))


def _ffn(y, wo, xf, g1, b1, wu, wg, cw, cb, wd, g, b, pf, wproj, wgate, bgate, alpha, seq):
    m, d = xf.shape
    f = wu.shape[1]
    pdim = pf.shape[1]
    kdim = y.shape[1]
    tm = min(ROW_TILE, seq)
    chunks = _ffn_chunks(f)
    wmax = max(c[1] for c in chunks)
    row = lambda i: (i, 0)
    return pl.pallas_call(
        functools.partial(_ffn_kernel, chunks=chunks, steps_per_batch=seq // tm, alpha=alpha),
        grid=(m // tm,),
        in_specs=[pl.BlockSpec((tm, kdim), row), _resident(wo.shape), pl.BlockSpec((tm, d), row),
                  _resident(g1.shape), _resident(b1.shape), _resident(wu.shape), _resident(wg.shape),
                  _resident(cw.shape), _resident(cb.shape), _resident(wd.shape),
                  _resident(g.shape), _resident(b.shape), pl.BlockSpec((tm, pdim), row),
                  _resident(wproj.shape), _resident(wgate.shape), _resident(bgate.shape)],
        out_specs=pl.BlockSpec((tm, d), row),
        out_shape=jax.ShapeDtypeStruct((m, d), F32),
        scratch_shapes=[pltpu.VMEM((tm + SUBLANES, wmax), F32), pltpu.VMEM((SUBLANES, f), F32)],
        compiler_params=_params("arbitrary"),
        name="conv_ffn",
    )(y, wo, xf, g1, b1, wu, wg, cw, cb, wd, g, b, pf, wproj, wgate, bgate)


def _ssd_proj_kernel(x_ref, wz_ref, wxbc_ref, wdt_ref, cw_ref, cb_ref, dtb_ref,
                     z_ref, xs_ref, bt_ref, c_ref, dt_ref, dtt_ref, buf0_ref, buf1_ref, carry_ref,
                     *, steps_per_batch, col_tile):
    tm = x_ref.shape[0]
    halo = SUBLANES
    kconv = cw_ref.shape[0]
    bufs = (buf0_ref, buf1_ref)

    @pl.when(pl.program_id(0) % steps_per_batch == 0)
    def _():
        carry_ref[...] = jnp.zeros_like(carry_ref)

    xb = x_ref[...].astype(BF16)
    d_inner = xs_ref.shape[1]
    gn = c_ref.shape[1]
    nconv = wxbc_ref.shape[1] // col_tile
    z_chunks = list(range(0, wz_ref.shape[1], col_tile))

    def project(i):
        cs = slice(i * col_tile, (i + 1) * col_tile)
        buf = bufs[i % 2]
        r = jnp.dot(xb, wxbc_ref[:, cs], preferred_element_type=F32)
        buf[0:halo, :] = carry_ref[:, cs]
        buf[halo:halo + tm, :] = r
        carry_ref[:, cs] = r[tm - halo:tm, :]

    def project_z():
        if z_chunks:
            zs = slice(z_chunks[0], z_chunks.pop(0) + col_tile)
            z_ref[:, zs] = jnp.dot(xb, wz_ref[:, zs], preferred_element_type=F32).astype(BF16)

    def conv_act(i):
        c0 = i * col_tile
        cs = slice(c0, c0 + col_tile)
        buf = bufs[i % 2]
        conv = cb_ref[:, cs] + cw_ref[kconv - 1:kconv, cs] * buf[halo:halo + tm, :]
        for k in range(kconv - 1):
            off = halo - (kconv - 1) + k
            conv = conv + cw_ref[k:k + 1, cs] * buf[off:off + tm, :]
        act = conv * _sigmoid(conv)
        if c0 < d_inner:
            xs_ref[:, cs] = act.astype(BF16)
        elif c0 < d_inner + gn:
            bt_ref[0, c0 - d_inner:c0 - d_inner + col_tile, :] = act.T.astype(BF16)
        else:
            c_ref[:, c0 - d_inner - gn:c0 - d_inner - gn + col_tile] = act.astype(BF16)

    project(0)
    for i in range(nconv):
        if i + 1 < nconv:
            project(i + 1)
        else:
            project_z()
        conv_act(i)
    while z_chunks:
        project_z()
    dt = _softplus(jnp.dot(xb, wdt_ref[...], preferred_element_type=F32) + dtb_ref[...])
    dt_ref[...] = dt
    dtt_ref[0] = dt.T[0:dtt_ref.shape[1], :]


def _ssd_proj(xf, wz, wxbc, wdt, cw, cb, dtb, bsz, seq, gn, nheads):
    m, d = xf.shape
    d_inner = wz.shape[1]
    tm = min(ROW_TILE, seq)
    nblk = seq // tm
    col_tile = min(1024, gn)
    row = lambda i: (i, 0)
    tposed = lambda i: (i // nblk, 0, i % nblk)
    out_shape = [jax.ShapeDtypeStruct((m, d_inner), BF16), jax.ShapeDtypeStruct((m, d_inner), BF16),
                 jax.ShapeDtypeStruct((bsz, gn, seq), BF16), jax.ShapeDtypeStruct((m, gn), BF16),
                 jax.ShapeDtypeStruct((m, LANES), F32),
                 jax.ShapeDtypeStruct((bsz, nheads, seq), F32)]
    return pl.pallas_call(
        functools.partial(_ssd_proj_kernel, steps_per_batch=seq // tm, col_tile=col_tile),
        grid=(m // tm,),
        in_specs=[pl.BlockSpec((tm, d), row), _resident(wz.shape), _resident(wxbc.shape),
                  _resident(wdt.shape), _resident(cw.shape), _resident(cb.shape),
                  _resident(dtb.shape)],
        out_specs=[pl.BlockSpec((tm, d_inner), row), pl.BlockSpec((tm, d_inner), row),
                   pl.BlockSpec((1, gn, tm), tposed), pl.BlockSpec((tm, gn), row),
                   pl.BlockSpec((tm, LANES), row), pl.BlockSpec((1, nheads, tm), tposed)],
        out_shape=out_shape,
        scratch_shapes=[pltpu.VMEM((tm + SUBLANES, col_tile), F32),
                        pltpu.VMEM((tm + SUBLANES, col_tile), F32),
                        pltpu.VMEM((SUBLANES, wxbc.shape[1]), F32)],
        compiler_params=_params("arbitrary"),
        name="ssd_proj",
    )(xf, wz, wxbc, wdt, cw, cb, dtb)


def _expand_heads(cols, lane_head):
    out = cols[0]
    for r in range(1, len(cols)):
        out = jnp.where(lane_head >= r, cols[r], out)
    return out


def _ssd_scan_kernel(xs_ref, bt_ref, c_ref, z_ref, dt_ref, dtt_ref, arow_ref, acol_ref,
                     dskip_ref, nw_ref, ltri_ref, utri_ref, y_ref, state_ref, *, heads_per_group):
    q = xs_ref.shape[1]
    n = SSM_STATE
    hp_ = SSM_HEAD_DIM
    gw = heads_per_group * hp_
    ngroups = xs_ref.shape[2] // gw

    @pl.when(pl.program_id(1) == 0)
    def _():
        state_ref[...] = jnp.zeros_like(state_ref)

    dt_col = dt_ref[0]
    dt_row = dtt_ref[0]
    nheads = dt_row.shape[0]
    a_col = dt_col * (arow_ref[...] * LOG2E)
    pc = jnp.dot(ltri_ref[...], jnp.concatenate(_split3(a_col), axis=1),
                 preferred_element_type=F32)
    acum_col = pc[:, 0:LANES] + pc[:, LANES:2 * LANES] + pc[:, 2 * LANES:3 * LANES]
    a_row = dt_row * (acol_ref[...] * LOG2E)
    pr = jnp.dot(jnp.concatenate(_split3(a_row), axis=0), utri_ref[...],
                 preferred_element_type=F32)
    acum_row = pr[0:nheads] + pr[nheads:2 * nheads] + pr[2 * nheads:3 * nheads]
    dec_row = jnp.exp2(acum_col[q - 1:q, :])
    w_row = jnp.exp2(acum_row[:, q - 1:q] - acum_row) * dt_row
    src_row = acum_row - jnp.log2(dt_row)

    tri = (lax.broadcasted_iota(jnp.int32, (q, q), 0) >= lax.broadcasted_iota(jnp.int32, (q, q), 1))
    lane_head = lax.broadcasted_iota(jnp.int32, (1, gw), 1) // hp_
    low_half = lax.broadcasted_iota(jnp.int32, (1, LANES), 1) < hp_

    for g in range(ngroups):
        gs = slice(g * gw, (g + 1) * gw)
        ns = slice(g * n, (g + 1) * n)
        cg = c_ref[0, :, ns]
        btg = bt_ref[0, ns, :]
        btg_f = btg.astype(F32)
        xg_b = xs_ref[0, :, gs]
        heads = range(g * heads_per_group, (g + 1) * heads_per_group)
        cb = jnp.dot(cg, btg, preferred_element_type=F32)
        state = state_ref[g]
        y_inter = jnp.dot(cg, state.astype(BF16), preferred_element_type=F32)
        e_cols, y_intra, st_new = [], None, None
        for r, h in enumerate(heads):
            acol_b = jnp.broadcast_to(acum_col[:, h:h + 1], (q, LANES))
            e_cols.append(jnp.exp2(acol_b))
            seg = jnp.concatenate([acol_b] * (q // LANES), axis=1) - src_row[h:h + 1, :]
            mh = (cb * jnp.exp2(jnp.where(tri, seg, -jnp.inf))).astype(BF16)
            xh = jnp.where(lane_head == r, xg_b, jnp.zeros_like(xg_b))
            part = jnp.dot(mh, xh, preferred_element_type=F32)
            y_intra = part if y_intra is None else y_intra + part
            btw = (btg_f * w_row[h:h + 1, :]).astype(BF16)
            part = jnp.dot(btw, xh, preferred_element_type=F32)
            st_new = part if st_new is None else st_new + part
        e_x = jnp.concatenate([jnp.where(low_half, e_cols[2 * i], e_cols[2 * i + 1])
                               for i in range(gw // LANES)], axis=1)
        y_g = y_inter * e_x + y_intra
        d_x = _expand_heads([dec_row[:, h:h + 1] for h in heads], lane_head)
        state_ref[g] = state * d_x + st_new
        xg = xg_b.astype(F32)
        zg = z_ref[0, :, gs].astype(F32)
        yv = (y_g + dskip_ref[:, gs] * xg) * (zg * _sigmoid(zg))
        ms = jnp.mean(yv * yv, axis=-1, keepdims=True)
        y_ref[0, :, gs] = (yv * lax.rsqrt(ms + RMS_EPS) * nw_ref[:, gs]).astype(BF16)


def _ssd_scan(xs, bmt, cm, z, dt, dtt, arow, acol, dskip, nw, ltri, utri, q):
    b, s, d_inner = xs.shape
    gn = cm.shape[2]
    nheads = dtt.shape[1]
    ngroups = gn // SSM_STATE
    heads_per_group = nheads // ngroups
    gw = heads_per_group * SSM_HEAD_DIM
    blk = lambda w: pl.BlockSpec((1, q, w), lambda bi, ci: (bi, ci, 0))
    return pl.pallas_call(
        functools.partial(_ssd_scan_kernel, heads_per_group=heads_per_group),
        grid=(b, s // q),
        in_specs=[blk(d_inner), pl.BlockSpec((1, gn, q), lambda bi, ci: (bi, 0, ci)),
                  blk(gn), blk(d_inner), blk(LANES),
                  pl.BlockSpec((1, nheads, q), lambda bi, ci: (bi, 0, ci)),
                  _resident(arow.shape), _resident(acol.shape), _resident(dskip.shape),
                  _resident(nw.shape), _resident(ltri.shape), _resident(utri.shape)],
        out_specs=blk(d_inner),
        out_shape=jax.ShapeDtypeStruct((b, s, d_inner), BF16),
        scratch_shapes=[pltpu.VMEM((ngroups, SSM_STATE, gw), F32)],
        compiler_params=_params("arbitrary", "arbitrary"),
        name="ssd_scan",
    )(xs, bmt, cm, z, dt, dtt, arow, acol, dskip, nw, ltri, utri)


def _pad_cols(a, width):
    return jnp.pad(a, ((0, 0), (0, width - a.shape[1])))


def _fox_mixer(xf, bsz, seq, w_in, b_f):
    m, d = xf.shape
    nh = d // ATTN_HEAD_DIM
    scale = LOG2E / math.sqrt(ATTN_HEAD_DIM)
    wqt = (w_in[:, :d] * scale).T.astype(BF16)
    wk = w_in[:, d:2 * d].astype(BF16)
    wvt = w_in[:, 2 * d:3 * d].T.astype(BF16)
    wft = w_in[:, 3 * d:].T.astype(BF16)
    utri = jnp.triu(jnp.ones((CUMSUM_CHUNK, CUMSUM_CHUNK), F32)).astype(BF16)
    blk = min(ATTN_BLOCK, seq)
    k, qt, vt, qta, ka = _fox_proj(xf, wk, wqt, wvt, wft, b_f.reshape(nh, 1), utri, bsz, seq, blk)
    o = _fox_attn(qt, qta, k.reshape(bsz, seq, d), ka.reshape(bsz, seq, -1), vt, blk)
    return o.reshape(m, d)


def _ssd_mixer(xf, bsz, seq, w_in, conv_w, conv_b, dt_bias, a_log, d_skip, norm_w):
    m, d = xf.shape
    nheads = dt_bias.shape[0]
    d_inner = nheads * SSM_HEAD_DIM
    gn = SSM_GROUPS * SSM_STATE
    wz = w_in[:, :d_inner].astype(BF16)
    wxbc = w_in[:, d_inner:2 * d_inner + 2 * gn].astype(BF16)
    wdt = _pad_cols(w_in[:, 2 * d_inner + 2 * gn:], LANES).astype(BF16)
    dtb = _pad_cols(dt_bias.reshape(1, nheads), LANES)
    z, xs, bmt, cm, dt, dtt = _ssd_proj(xf, wz, wxbc, wdt, conv_w, conv_b.reshape(1, -1), dtb,
                                        bsz, seq, gn, nheads)
    a = -jnp.exp(a_log.astype(F32))
    arow = _pad_cols(a.reshape(1, nheads), LANES)
    acol = a.reshape(nheads, 1)
    dskip = jnp.repeat(d_skip, SSM_HEAD_DIM).reshape(1, d_inner)
    q = min(SSD_CHUNK, seq)
    ltri = jnp.tril(jnp.ones((q, q), F32)).astype(BF16)
    utri = jnp.triu(jnp.ones((q, q), F32)).astype(BF16)
    y = _ssd_scan(xs.reshape(bsz, seq, d_inner), bmt, cm.reshape(bsz, seq, gn),
                  z.reshape(bsz, seq, d_inner), dt.reshape(bsz, seq, LANES), dtt, arow, acol,
                  dskip, norm_w.reshape(1, d_inner), ltri, utri, q)
    return y.reshape(m, d_inner)


def kernel(x, p, attn_w_in, attn_b_f, attn_w_out, ssm_w_in, ssm_conv_w, ssm_conv_b, ssm_dt_bias, ssm_A_log, ssm_D, ssm_norm_w, ssm_w_out, ln_mix_g, ln_mix_b, ffn_w_up, ffn_conv_w, ffn_conv_b, ffn_w_down, ln_ffn_g, ln_ffn_b, ple_w_proj, ple_w_gate, ple_b_gate):
    bsz, seq, d = x.shape
    depth = p.shape[0]
    n_mixers = 2
    alpha = (2 * depth) ** 0.25
    xf = x.reshape(bsz * seq, d)
    for i in range(depth):
        j = i // n_mixers
        if i % n_mixers == 0:
            y, w_out = _fox_mixer(xf, bsz, seq, attn_w_in[j], attn_b_f[j]), attn_w_out[j]
        else:
            y = _ssd_mixer(xf, bsz, seq, ssm_w_in[j], ssm_conv_w[j], ssm_conv_b[j],
                           ssm_dt_bias[j], ssm_A_log[j], ssm_D[j], ssm_norm_w[j])
            w_out = ssm_w_out[j]
        f = ffn_conv_w.shape[-1]
        xf = _ffn(y, w_out.astype(BF16), xf, ln_mix_g[i].reshape(1, d), ln_mix_b[i].reshape(1, d),
                  ffn_w_up[i][:, :f].astype(BF16), ffn_w_up[i][:, f:].astype(BF16),
                  ffn_conv_w[i], ffn_conv_b[i].reshape(1, f), ffn_w_down[i].astype(BF16),
                  ln_ffn_g[i].reshape(1, d), ln_ffn_b[i].reshape(1, d),
                  p[i].reshape(bsz * seq, -1), ple_w_proj[i].astype(BF16),
                  ple_w_gate[i].astype(BF16), ple_b_gate[i].reshape(1, d), alpha, seq)
    return xf.reshape(bsz, seq, d)
```

```python
import functools
import math

import jax
import jax.numpy as jnp
import numpy as np
from jax import lax
from jax.experimental import pallas as pl
from jax.experimental.pallas import tpu as pltpu

F32 = jnp.float32
BF16 = jnp.bfloat16

ATTN_HEAD_DIM = 64
SSM_HEAD_DIM = 64
SSM_GROUPS = 8
SSM_STATE = 128
LN_EPS = 1e-5
RMS_EPS = 1e-5

LANES = 128
SUBLANES = 8
VMEM_LIMIT_BYTES = 56 * 1024 * 1024

ROW_TILE = 512
ATTN_BLOCK = 512
ATTN_UNROLL = 4
SSD_CHUNK = 256
CUMSUM_CHUNK = 256
MASK_VALUE = -1e30
LOG2E = 1.4426950408889634
AUG_ROWS = 16
ONES_ROWS = 16


def _params(*sem):
    return pltpu.CompilerParams(dimension_semantics=sem, vmem_limit_bytes=VMEM_LIMIT_BYTES)


def _resident(shape):
    nd = len(shape)
    return pl.BlockSpec(shape, lambda *_: (0,) * nd, pipeline_mode=pl.Buffered(1))


def _split3(v):
    hi = v.astype(BF16)
    r1 = v - hi.astype(F32)
    mid = r1.astype(BF16)
    lo = (r1 - mid.astype(F32)).astype(BF16)
    return hi, mid, lo


def _softplus(v):
    return jnp.maximum(v, 0.0) + jnp.log1p(jnp.exp(-jnp.abs(v)))


def _sigmoid(v):
    return 1.0 / (1.0 + jnp.exp2(v * (-LOG2E)))


def _layer_norm(h, g, b):
    mu = jnp.mean(h, axis=-1, keepdims=True)
    d = h - mu
    var = jnp.mean(d * d, axis=-1, keepdims=True)
    return d * lax.rsqrt(var + LN_EPS) * g + b


_NT = (((1,), (1,)), ((), ()))


def _bias_placement(nh):
    place_q = np.zeros((nh * AUG_ROWS, LANES), np.float32)
    ones_q = np.zeros((nh * AUG_ROWS, 1), np.float32)
    place_k = np.zeros((LANES, (nh // 2) * LANES), np.float32)
    ones_k = np.zeros((1, (nh // 2) * LANES), np.float32)
    for h in range(nh):
        off = 6 * (h % 2)
        for p in range(3):
            ones_q[h * AUG_ROWS + off + p, 0] = 1.0
            place_q[h * AUG_ROWS + off + 3 + p, p * nh + h] = 1.0
            place_k[p * nh + h, (h // 2) * LANES + off + p] = -1.0
            ones_k[0, (h // 2) * LANES + off + 3 + p] = 1.0
    return (jnp.asarray(place_q, BF16), jnp.asarray(ones_q), jnp.asarray(place_k, BF16),
            jnp.asarray(ones_k))


def _fox_proj_kernel(x_ref, wk_ref, wqt_ref, wvt_ref, wft_ref, bf_ref, ut_ref, pq_ref, oq_ref,
                     pk_ref, ok_ref, k_ref, qt_ref, vt_ref, qta_ref, ka_ref, carry_ref, *,
                     steps_per_batch):
    tm, d = x_ref.shape

    @pl.when(pl.program_id(0) % steps_per_batch == 0)
    def _():
        carry_ref[...] = jnp.zeros_like(carry_ref)

    xb = x_ref[...].astype(BF16)
    k_ref[...] = jnp.dot(xb, wk_ref[...], preferred_element_type=F32).astype(BF16)
    qt_ref[0] = lax.dot_general(wqt_ref[...], xb, _NT, preferred_element_type=F32).astype(BF16)
    vt_ref[0, 0] = lax.dot_general(wvt_ref[...], xb, _NT, preferred_element_type=F32).astype(BF16)

    fl = lax.dot_general(wft_ref[...], xb, _NT, preferred_element_type=F32) + bf_ref[...]
    logf = -_softplus(-fl) * LOG2E
    nh = logf.shape[0]
    parts = jnp.concatenate(_split3(logf), axis=0)
    carry = carry_ref[...]
    zrows = jnp.zeros((LANES - 3 * nh, CUMSUM_CHUNK), F32)
    stacks = []
    for j in range(tm // CUMSUM_CHUNK):
        sl = slice(j * CUMSUM_CHUNK, (j + 1) * CUMSUM_CHUNK)
        pj = jnp.dot(parts[:, sl], ut_ref[...], preferred_element_type=F32)
        cj = (pj[0:nh] + pj[nh:2 * nh] + pj[2 * nh:3 * nh]) + carry
        stacks.append(jnp.concatenate([cp.astype(F32) for cp in _split3(cj)] + [zrows], axis=0))
        carry = cj[:, CUMSUM_CHUNK - 1:CUMSUM_CHUNK]
    carry_ref[...] = carry
    stack = jnp.concatenate(stacks, axis=1)
    qta_ref[0] = (jnp.dot(pq_ref[...], stack.astype(BF16), preferred_element_type=F32)
                  + oq_ref[...]).astype(BF16)
    ka_ref[...] = (jnp.dot(stack.T.astype(BF16), pk_ref[...], preferred_element_type=F32)
                   + ok_ref[...]).astype(BF16)


def _fox_proj(xf, wk, wqt, wvt, wft, bf_col, utri, bsz, seq, tm):
    m, d = xf.shape
    nh = wft.shape[0]
    nblk = seq // tm
    row = lambda i: (i, 0)
    place = _bias_placement(nh)
    out_shape = [jax.ShapeDtypeStruct((m, d), BF16), jax.ShapeDtypeStruct((bsz, d, seq), BF16),
                 jax.ShapeDtypeStruct((bsz, nblk, d, tm), BF16),
                 jax.ShapeDtypeStruct((bsz, nh * AUG_ROWS, seq), BF16),
                 jax.ShapeDtypeStruct((m, (nh // 2) * LANES), BF16)]
    return pl.pallas_call(
        functools.partial(_fox_proj_kernel, steps_per_batch=nblk),
        grid=(m // tm,),
        in_specs=[pl.BlockSpec((tm, d), row), _resident(wk.shape), _resident(wqt.shape),
                  _resident(wvt.shape), _resident(wft.shape), _resident(bf_col.shape),
                  _resident(utri.shape)] + [_resident(a.shape) for a in place],
        out_specs=[pl.BlockSpec((tm, d), row),
                   pl.BlockSpec((1, d, tm), lambda i: (i // nblk, 0, i % nblk)),
                   pl.BlockSpec((1, 1, d, tm), lambda i: (i // nblk, i % nblk, 0, 0)),
                   pl.BlockSpec((1, nh * AUG_ROWS, tm), lambda i: (i // nblk, 0, i % nblk)),
                   pl.BlockSpec((tm, (nh // 2) * LANES), row)],
        out_shape=out_shape,
        scratch_shapes=[pltpu.VMEM((nh, 1), F32)],
        compiler_params=_params("arbitrary"),
        name="fox_proj",
    )(xf, wk, wqt, wvt, wft, bf_col, utri, *place)


def _fox_attn_kernel(qt_ref, qta_ref, k_ref, ka_ref, vt_ref, o_ref, m_scr, acc_scr,
                     sa_scr, sb_scr, ma_scr, mb_scr, *, blk):
    qi = pl.program_id(2)
    hd = ATTN_HEAD_DIM
    qt2 = qt_ref[0]
    zhead = jnp.zeros((hd, blk), BF16)
    zpad = jnp.zeros((LANES - AUG_ROWS, blk), BF16)
    qts = [jnp.concatenate([qt2[0:hd], zhead, qta_ref[0, 0:AUG_ROWS], zpad], axis=0),
           jnp.concatenate([zhead, qt2[hd:2 * hd], qta_ref[0, AUG_ROWS:2 * AUG_ROWS], zpad], axis=0)]
    ones_rows = jnp.ones((ONES_ROWS, blk), BF16)
    key_id = lax.broadcasted_iota(jnp.int32, (blk, blk), 0)
    qry_id = lax.broadcasted_iota(jnp.int32, (blk, blk), 1)
    m_scr[...] = jnp.full_like(m_scr, MASK_VALUE)
    acc_scr[...] = jnp.zeros_like(acc_scr)

    def keys(j):
        start = pl.multiple_of(j * blk, blk)
        return jnp.concatenate([k_ref[0, pl.ds(start, blk), :], ka_ref[0, pl.ds(start, blk), :]],
                               axis=1)

    def values(j):
        vt2 = vt_ref[0, j]
        return [jnp.concatenate([vt2[h * hd:(h + 1) * hd], ones_rows], axis=0) for h in range(2)]

    def produce(kk, h, s_buf, m_buf):
        s = jnp.dot(kk, qts[h], preferred_element_type=F32)
        s_buf[h] = s
        m_buf[h] = jnp.max(s, axis=0, keepdims=True)

    def consume(vt, h, s_buf, m_buf, masked):
        s = s_buf[h]
        if masked:
            s = jnp.where(key_id <= qry_id, s, MASK_VALUE)
            m_blk = jnp.max(s, axis=0, keepdims=True)
        else:
            m_blk = m_buf[h]
        m_old = m_scr[h]
        m_new = jnp.maximum(m_old, m_blk)
        p = jnp.exp2(s - m_new).astype(BF16)
        alpha = jnp.exp2(m_old - m_new)
        acc_scr[h] = alpha * acc_scr[h] + jnp.dot(vt[h], p, preferred_element_type=F32)
        m_scr[h] = m_new

    kk0 = keys(0)
    for h in range(2):
        produce(kk0, h, sa_scr, ma_scr)

    bufs = ((sa_scr, ma_scr), (sb_scr, mb_scr))

    def advance(j0, nsteps):
        for i in range(nsteps):
            kk, vt = keys(j0 + i + 1), values(j0 + i)
            for h in range(2):
                produce(kk, h, *bufs[(i + 1) % 2])
                consume(vt, h, *bufs[i % 2], False)

    def body(t, carry):
        advance(ATTN_UNROLL * t, ATTN_UNROLL)
        return carry

    lax.fori_loop(0, qi // ATTN_UNROLL, body, 0)

    for rem in range(ATTN_UNROLL):
        @pl.when(qi % ATTN_UNROLL == rem)
        def _(rem=rem):
            advance(qi - rem, rem)
            vt = values(qi)
            for h in range(2):
                consume(vt, h, *bufs[rem % 2], True)

    outs = []
    for h in range(2):
        acc = acc_scr[h]
        outs.append(acc[0:hd] * (1.0 / acc[hd:hd + 1]))
    o_ref[0] = jnp.concatenate(outs, axis=0).T.astype(BF16)


def _fox_attn(qt, qta, k, ka, vt, blk):
    b, d, s = qt.shape
    nhp = d // LANES
    nblk = s // blk
    return pl.pallas_call(
        functools.partial(_fox_attn_kernel, blk=blk),
        grid=(b, nhp, nblk),
        in_specs=[
            pl.BlockSpec((1, LANES, blk), lambda bi, hp, qi: (bi, hp, qi)),
            pl.BlockSpec((1, 2 * AUG_ROWS, blk), lambda bi, hp, qi: (bi, hp, qi)),
            pl.BlockSpec((1, s, LANES), lambda bi, hp, qi: (bi, 0, hp)),
            pl.BlockSpec((1, s, LANES), lambda bi, hp, qi: (bi, 0, hp)),
            pl.BlockSpec((1, nblk, LANES, blk), lambda bi, hp, qi: (bi, 0, hp, 0)),
        ],
        out_specs=pl.BlockSpec((1, blk, LANES), lambda bi, hp, qi: (bi, qi, hp)),
        out_shape=jax.ShapeDtypeStruct((b, s, d), BF16),
        scratch_shapes=[pltpu.VMEM((2, 1, blk), F32),
                        pltpu.VMEM((2, ATTN_HEAD_DIM + ONES_ROWS, blk), F32),
                        pltpu.VMEM((2, blk, blk), F32), pltpu.VMEM((2, blk, blk), F32),
                        pltpu.VMEM((2, 1, blk), F32), pltpu.VMEM((2, 1, blk), F32)],
        compiler_params=_params("arbitrary", "arbitrary", "arbitrary"),
        name="fox_attn",
    )(qt, qta, k, ka, vt)


def _ffn_kernel(y_ref, wo_ref, x_ref, g1_ref, b1_ref, wu_ref, wg_ref, cw_ref, cb_ref, wd_ref,
                g_ref, b_ref, p_ref, wproj_ref, wgate_ref, bgate_ref, o_ref, gbuf0_ref, gbuf1_ref,
                carry_ref, *, chunks, steps_per_batch, alpha):
    tm = x_ref.shape[0]
    halo = SUBLANES
    kconv = cw_ref.shape[0]
    gbufs = (gbuf0_ref, gbuf1_ref)

    @pl.when(pl.program_id(0) % steps_per_batch == 0)
    def _():
        carry_ref[...] = jnp.zeros_like(carry_ref)

    mix = jnp.dot(y_ref[...], wo_ref[...], preferred_element_type=F32)
    x = _layer_norm(alpha * x_ref[...] + mix, g1_ref[...], b1_ref[...])
    xb = x.astype(BF16)

    def up(i):
        c0, cw = chunks[i]
        cs = slice(c0, c0 + cw)
        buf = gbufs[i % 2]
        u = jnp.dot(xb, wu_ref[:, cs], preferred_element_type=F32)
        g = jnp.dot(xb, wg_ref[:, cs], preferred_element_type=F32)
        buf[0:halo, 0:cw] = carry_ref[:, cs]
        buf[halo:halo + tm, 0:cw] = g
        carry_ref[:, cs] = g[tm - halo:tm, :]
        return u

    def down(i, u):
        c0, cw = chunks[i]
        cs = slice(c0, c0 + cw)
        buf = gbufs[i % 2]
        conv = cb_ref[:, cs] + cw_ref[kconv - 1:kconv, cs] * buf[halo:halo + tm, 0:cw]
        for k in range(kconv - 1):
            off = halo - (kconv - 1) + k
            conv = conv + cw_ref[k:k + 1, cs] * buf[off:off + tm, 0:cw]
        gelu = 0.5 * conv * (1.0 + lax.erf(conv * (1.0 / math.sqrt(2.0))))
        return jnp.dot((gelu * u).astype(BF16), wd_ref[cs, :], preferred_element_type=F32)

    acc = None
    u_cur = up(0)
    for i in range(len(chunks)):
        u_next = up(i + 1) if i + 1 < len(chunks) else None
        part = down(i, u_cur)
        acc = part if acc is None else acc + part
        u_cur = u_next
    x2 = _layer_norm(alpha * x + acc, g_ref[...], b_ref[...])
    gate = _sigmoid(jnp.dot(x2.astype(BF16), wgate_ref[...], preferred_element_type=F32)
                    + bgate_ref[...])
    emb = jnp.dot(p_ref[...].astype(BF16), wproj_ref[...], preferred_element_type=F32)
    o_ref[...] = x2 + gate * emb


def _ffn_chunks(f):
    tile = 2 * LANES
    if f % tile:
        return ((0, f),)
    step = 4 * tile
    return tuple((c0, min(step, f - c0)) for c0 in range(0, f, step))


def _ffn(y, wo, xf, g1, b1, wu, wg, cw, cb, wd, g, b, pf, wproj, wgate, bgate, alpha, seq):
    m, d = xf.shape
    f = wu.shape[1]
    pdim = pf.shape[1]
    kdim = y.shape[1]
    tm = min(ROW_TILE, seq)
    chunks = _ffn_chunks(f)
    wmax = max(c[1] for c in chunks)
    row = lambda i: (i, 0)
    return pl.pallas_call(
        functools.partial(_ffn_kernel, chunks=chunks, steps_per_batch=seq // tm, alpha=alpha),
        grid=(m // tm,),
        in_specs=[pl.BlockSpec((tm, kdim), row), _resident(wo.shape), pl.BlockSpec((tm, d), row),
                  _resident(g1.shape), _resident(b1.shape), _resident(wu.shape), _resident(wg.shape),
                  _resident(cw.shape), _resident(cb.shape), _resident(wd.shape),
                  _resident(g.shape), _resident(b.shape), pl.BlockSpec((tm, pdim), row),
                  _resident(wproj.shape), _resident(wgate.shape), _resident(bgate.shape)],
        out_specs=pl.BlockSpec((tm, d), row),
        out_shape=jax.ShapeDtypeStruct((m, d), F32),
        scratch_shapes=[pltpu.VMEM((tm + SUBLANES, wmax), F32), pltpu.VMEM((tm + SUBLANES, wmax), F32),
                        pltpu.VMEM((SUBLANES, f), F32)],
        compiler_params=_params("arbitrary"),
        name="conv_ffn",
    )(y, wo, xf, g1, b1, wu, wg, cw, cb, wd, g, b, pf, wproj, wgate, bgate)


def _ssd_proj_kernel(x_ref, wz_ref, wxbc_ref, wdt_ref, cw_ref, cb_ref, dtb_ref,
                     z_ref, xs_ref, bt_ref, c_ref, dt_ref, dtt_ref, buf0_ref, buf1_ref, carry_ref,
                     *, steps_per_batch, col_tile):
    tm = x_ref.shape[0]
    halo = SUBLANES
    kconv = cw_ref.shape[0]
    bufs = (buf0_ref, buf1_ref)

    @pl.when(pl.program_id(0) % steps_per_batch == 0)
    def _():
        carry_ref[...] = jnp.zeros_like(carry_ref)

    xb = x_ref[...].astype(BF16)
    d_inner = xs_ref.shape[1]
    gn = c_ref.shape[1]
    nconv = wxbc_ref.shape[1] // col_tile
    z_chunks = list(range(0, wz_ref.shape[1], col_tile))

    def project(i):
        cs = slice(i * col_tile, (i + 1) * col_tile)
        buf = bufs[i % 2]
        r = jnp.dot(xb, wxbc_ref[:, cs], preferred_element_type=F32)
        buf[0:halo, :] = carry_ref[:, cs]
        buf[halo:halo + tm, :] = r
        carry_ref[:, cs] = r[tm - halo:tm, :]

    def project_z():
        if z_chunks:
            zs = slice(z_chunks[0], z_chunks.pop(0) + col_tile)
            z_ref[:, zs] = jnp.dot(xb, wz_ref[:, zs], preferred_element_type=F32).astype(BF16)

    def conv_act(i):
        c0 = i * col_tile
        cs = slice(c0, c0 + col_tile)
        buf = bufs[i % 2]
        conv = cb_ref[:, cs] + cw_ref[kconv - 1:kconv, cs] * buf[halo:halo + tm, :]
        for k in range(kconv - 1):
            off = halo - (kconv - 1) + k
            conv = conv + cw_ref[k:k + 1, cs] * buf[off:off + tm, :]
        act = conv * _sigmoid(conv)
        if c0 < d_inner:
            xs_ref[:, cs] = act.astype(BF16)
        elif c0 < d_inner + gn:
            bt_ref[0, c0 - d_inner:c0 - d_inner + col_tile, :] = act.T.astype(BF16)
        else:
            c_ref[:, c0 - d_inner - gn:c0 - d_inner - gn + col_tile] = act.astype(BF16)

    project(0)
    for i in range(nconv):
        if i + 1 < nconv:
            project(i + 1)
        else:
            project_z()
        conv_act(i)
    while z_chunks:
        project_z()
    dt = _softplus(jnp.dot(xb, wdt_ref[...], preferred_element_type=F32) + dtb_ref[...])
    dt_ref[...] = dt
    dtt_ref[0] = dt.T[0:dtt_ref.shape[1], :]


def _ssd_proj(xf, wz, wxbc, wdt, cw, cb, dtb, bsz, seq, gn, nheads):
    m, d = xf.shape
    d_inner = wz.shape[1]
    tm = min(ROW_TILE, seq)
    nblk = seq // tm
    col_tile = min(1024, gn)
    row = lambda i: (i, 0)
    tposed = lambda i: (i // nblk, 0, i % nblk)
    out_shape = [jax.ShapeDtypeStruct((m, d_inner), BF16), jax.ShapeDtypeStruct((m, d_inner), BF16),
                 jax.ShapeDtypeStruct((bsz, gn, seq), BF16), jax.ShapeDtypeStruct((m, gn), BF16),
                 jax.ShapeDtypeStruct((m, LANES), F32),
                 jax.ShapeDtypeStruct((bsz, nheads, seq), F32)]
    return pl.pallas_call(
        functools.partial(_ssd_proj_kernel, steps_per_batch=seq // tm, col_tile=col_tile),
        grid=(m // tm,),
        in_specs=[pl.BlockSpec((tm, d), row), _resident(wz.shape), _resident(wxbc.shape),
                  _resident(wdt.shape), _resident(cw.shape), _resident(cb.shape),
                  _resident(dtb.shape)],
        out_specs=[pl.BlockSpec((tm, d_inner), row), pl.BlockSpec((tm, d_inner), row),
                   pl.BlockSpec((1, gn, tm), tposed), pl.BlockSpec((tm, gn), row),
                   pl.BlockSpec((tm, LANES), row), pl.BlockSpec((1, nheads, tm), tposed)],
        out_shape=out_shape,
        scratch_shapes=[pltpu.VMEM((tm + SUBLANES, col_tile), F32),
                        pltpu.VMEM((tm + SUBLANES, col_tile), F32),
                        pltpu.VMEM((SUBLANES, wxbc.shape[1]), F32)],
        compiler_params=_params("arbitrary"),
        name="ssd_proj",
    )(xf, wz, wxbc, wdt, cw, cb, dtb)


def _expand_heads(cols, lane_head):
    out = cols[0]
    for r in range(1, len(cols)):
        out = jnp.where(lane_head >= r, cols[r], out)
    return out


def _ssd_scan_kernel(xs_ref, bt_ref, c_ref, z_ref, dt_ref, dtt_ref, arow_ref, acol_ref,
                     dskip_ref, nw_ref, ltri_ref, utri_ref, y_ref, state_ref, *, heads_per_group):
    q = xs_ref.shape[1]
    n = SSM_STATE
    hp_ = SSM_HEAD_DIM
    gw = heads_per_group * hp_
    ngroups = xs_ref.shape[2] // gw

    @pl.when(pl.program_id(1) == 0)
    def _():
        state_ref[...] = jnp.zeros_like(state_ref)

    dt_col = dt_ref[0]
    dt_row = dtt_ref[0]
    nheads = dt_row.shape[0]
    a_col = dt_col * (arow_ref[...] * LOG2E)
    pc = jnp.dot(ltri_ref[...], jnp.concatenate(_split3(a_col), axis=1),
                 preferred_element_type=F32)
    acum_col = pc[:, 0:LANES] + pc[:, LANES:2 * LANES] + pc[:, 2 * LANES:3 * LANES]
    a_row = dt_row * (acol_ref[...] * LOG2E)
    pr = jnp.dot(jnp.concatenate(_split3(a_row), axis=0), utri_ref[...],
                 preferred_element_type=F32)
    acum_row = pr[0:nheads] + pr[nheads:2 * nheads] + pr[2 * nheads:3 * nheads]
    dec_row = jnp.exp2(acum_col[q - 1:q, :])
    w_row = jnp.exp2(acum_row[:, q - 1:q] - acum_row) * dt_row
    src_row = acum_row - jnp.log2(dt_row)

    nsub = q // LANES
    tri = (lax.broadcasted_iota(jnp.int32, (LANES, LANES), 0)
           >= lax.broadcasted_iota(jnp.int32, (LANES, LANES), 1))
    lane_head = lax.broadcasted_iota(jnp.int32, (1, gw), 1) // hp_
    low_half = lax.broadcasted_iota(jnp.int32, (1, LANES), 1) < hp_

    for g in range(ngroups):
        gs = slice(g * gw, (g + 1) * gw)
        ns = slice(g * n, (g + 1) * n)
        cg = c_ref[0, :, ns]
        btg = bt_ref[0, ns, :]
        btg_f = btg.astype(F32)
        xg_b = xs_ref[0, :, gs]
        heads = range(g * heads_per_group, (g + 1) * heads_per_group)
        cb = jnp.dot(cg, btg, preferred_element_type=F32)
        state = state_ref[g]
        y_inter = jnp.dot(cg, state.astype(BF16), preferred_element_type=F32)
        e_cols, y_intra, st_new = [], None, None
        for r, h in enumerate(heads):
            acol_b = jnp.broadcast_to(acum_col[:, h:h + 1], (q, LANES))
            e_cols.append(jnp.exp2(acol_b))
            rows = []
            for ti in range(nsub):
                tsl = slice(ti * LANES, (ti + 1) * LANES)
                tiles = []
                for si in range(nsub):
                    ssl = slice(si * LANES, (si + 1) * LANES)
                    if si > ti:
                        tiles.append(jnp.zeros((LANES, LANES), BF16))
                        continue
                    seg = acol_b[tsl] - src_row[h:h + 1, ssl]
                    if si == ti:
                        seg = jnp.where(tri, seg, -jnp.inf)
                    tiles.append((cb[tsl, ssl] * jnp.exp2(seg)).astype(BF16))
                rows.append(jnp.concatenate(tiles, axis=1))
            mh = jnp.concatenate(rows, axis=0)
            xh = jnp.where(lane_head == r, xg_b, jnp.zeros_like(xg_b))
            part = jnp.dot(mh, xh, preferred_element_type=F32)
            y_intra = part if y_intra is None else y_intra + part
            btw = (btg_f * w_row[h:h + 1, :]).astype(BF16)
            part = jnp.dot(btw, xh, preferred_element_type=F32)
            st_new = part if st_new is None else st_new + part
        e_x = jnp.concatenate([jnp.where(low_half, e_cols[2 * i], e_cols[2 * i + 1])
                               for i in range(gw // LANES)], axis=1)
        y_g = y_inter * e_x + y_intra
        d_x = _expand_heads([dec_row[:, h:h + 1] for h in heads], lane_head)
        state_ref[g] = state * d_x + st_new
        xg = xg_b.astype(F32)
        zg = z_ref[0, :, gs].astype(F32)
        yv = (y_g + dskip_ref[:, gs] * xg) * (zg * _sigmoid(zg))
        ms = jnp.mean(yv * yv, axis=-1, keepdims=True)
        y_ref[0, :, gs] = (yv * lax.rsqrt(ms + RMS_EPS) * nw_ref[:, gs]).astype(BF16)


def _ssd_scan(xs, bmt, cm, z, dt, dtt, arow, acol, dskip, nw, ltri, utri, q):
    b, s, d_inner = xs.shape
    gn = cm.shape[2]
    nheads = dtt.shape[1]
    ngroups = gn // SSM_STATE
    heads_per_group = nheads // ngroups
    gw = heads_per_group * SSM_HEAD_DIM
    blk = lambda w: pl.BlockSpec((1, q, w), lambda bi, ci: (bi, ci, 0))
    return pl.pallas_call(
        functools.partial(_ssd_scan_kernel, heads_per_group=heads_per_group),
        grid=(b, s // q),
        in_specs=[blk(d_inner), pl.BlockSpec((1, gn, q), lambda bi, ci: (bi, 0, ci)),
                  blk(gn), blk(d_inner), blk(LANES),
                  pl.BlockSpec((1, nheads, q), lambda bi, ci: (bi, 0, ci)),
                  _resident(arow.shape), _resident(acol.shape), _resident(dskip.shape),
                  _resident(nw.shape), _resident(ltri.shape), _resident(utri.shape)],
        out_specs=blk(d_inner),
        out_shape=jax.ShapeDtypeStruct((b, s, d_inner), BF16),
        scratch_shapes=[pltpu.VMEM((ngroups, SSM_STATE, gw), F32)],
        compiler_params=_params("arbitrary", "arbitrary"),
        name="ssd_scan",
    )(xs, bmt, cm, z, dt, dtt, arow, acol, dskip, nw, ltri, utri)


def _pad_cols(a, width):
    return jnp.pad(a, ((0, 0), (0, width - a.shape[1])))


def _fox_mixer(xf, bsz, seq, w_in, b_f):
    m, d = xf.shape
    nh = d // ATTN_HEAD_DIM
    scale = LOG2E / math.sqrt(ATTN_HEAD_DIM)
    wqt = (w_in[:, :d] * scale).T.astype(BF16)
    wk = w_in[:, d:2 * d].astype(BF16)
    wvt = w_in[:, 2 * d:3 * d].T.astype(BF16)
    wft = w_in[:, 3 * d:].T.astype(BF16)
    utri = jnp.triu(jnp.ones((CUMSUM_CHUNK, CUMSUM_CHUNK), F32)).astype(BF16)
    blk = min(ATTN_BLOCK, seq)
    k, qt, vt, qta, ka = _fox_proj(xf, wk, wqt, wvt, wft, b_f.reshape(nh, 1), utri, bsz, seq, blk)
    o = _fox_attn(qt, qta, k.reshape(bsz, seq, d), ka.reshape(bsz, seq, -1), vt, blk)
    return o.reshape(m, d)


def _ssd_mixer(xf, bsz, seq, w_in, conv_w, conv_b, dt_bias, a_log, d_skip, norm_w):
    m, d = xf.shape
    nheads = dt_bias.shape[0]
    d_inner = nheads * SSM_HEAD_DIM
    gn = SSM_GROUPS * SSM_STATE
    wz = w_in[:, :d_inner].astype(BF16)
    wxbc = w_in[:, d_inner:2 * d_inner + 2 * gn].astype(BF16)
    wdt = _pad_cols(w_in[:, 2 * d_inner + 2 * gn:], LANES).astype(BF16)
    dtb = _pad_cols(dt_bias.reshape(1, nheads), LANES)
    z, xs, bmt, cm, dt, dtt = _ssd_proj(xf, wz, wxbc, wdt, conv_w, conv_b.reshape(1, -1), dtb,
                                        bsz, seq, gn, nheads)
    a = -jnp.exp(a_log.astype(F32))
    arow = _pad_cols(a.reshape(1, nheads), LANES)
    acol = a.reshape(nheads, 1)
    dskip = jnp.repeat(d_skip, SSM_HEAD_DIM).reshape(1, d_inner)
    q = min(SSD_CHUNK, seq)
    ltri = jnp.tril(jnp.ones((q, q), F32)).astype(BF16)
    utri = jnp.triu(jnp.ones((q, q), F32)).astype(BF16)
    y = _ssd_scan(xs.reshape(bsz, seq, d_inner), bmt, cm.reshape(bsz, seq, gn),
                  z.reshape(bsz, seq, d_inner), dt.reshape(bsz, seq, LANES), dtt, arow, acol,
                  dskip, norm_w.reshape(1, d_inner), ltri, utri, q)
    return y.reshape(m, d_inner)


def kernel(x, p, attn_w_in, attn_b_f, attn_w_out, ssm_w_in, ssm_conv_w, ssm_conv_b, ssm_dt_bias, ssm_A_log, ssm_D, ssm_norm_w, ssm_w_out, ln_mix_g, ln_mix_b, ffn_w_up, ffn_conv_w, ffn_conv_b, ffn_w_down, ln_ffn_g, ln_ffn_b, ple_w_proj, ple_w_gate, ple_b_gate):
    bsz, seq, d = x.shape
    depth = p.shape[0]
    n_mixers = 2
    alpha = (2 * depth) ** 0.25
    xf = x.reshape(bsz * seq, d)
    for i in range(depth):
        j = i // n_mixers
        if i % n_mixers == 0:
            y, w_out = _fox_mixer(xf, bsz, seq, attn_w_in[j], attn_b_f[j]), attn_w_out[j]
        else:
            y = _ssd_mixer(xf, bsz, seq, ssm_w_in[j], ssm_conv_w[j], ssm_conv_b[j],
                           ssm_dt_bias[j], ssm_A_log[j], ssm_D[j], ssm_norm_w[j])
            w_out = ssm_w_out[j]
        f = ffn_conv_w.shape[-1]
        xf = _ffn(y, w_out.astype(BF16), xf, ln_mix_g[i].reshape(1, d), ln_mix_b[i].reshape(1, d),
                  ffn_w_up[i][:, :f].astype(BF16), ffn_w_up[i][:, f:].astype(BF16),
                  ffn_conv_w[i], ffn_conv_b[i].reshape(1, f), ffn_w_down[i].astype(BF16),
                  ln_ffn_g[i].reshape(1, d), ln_ffn_b[i].reshape(1, d),
                  p[i].reshape(bsz * seq, -1), ple_w_proj[i].astype(BF16),
                  ple_w_gate[i].astype(BF16), ple_b_gate[i].reshape(1, d), alpha, seq)
    return xf.reshape(bsz, seq, d)
```

```python
import functools
import math

import jax
import jax.numpy as jnp
import numpy as np
from jax import lax
from jax.experimental import pallas as pl
from jax.experimental.pallas import tpu as pltpu

F32 = jnp.float32
BF16 = jnp.bfloat16

ATTN_HEAD_DIM = 64
SSM_HEAD_DIM = 64
SSM_GROUPS = 8
SSM_STATE = 128
LN_EPS = 1e-5
RMS_EPS = 1e-5

LANES = 128
SUBLANES = 8
VMEM_LIMIT_BYTES = 56 * 1024 * 1024

ROW_TILE = 512
ATTN_BLOCK = 512
ATTN_UNROLL = 4
SSD_CHUNK = 256
CUMSUM_CHUNK = 256
MASK_VALUE = -1e30
LOG2E = 1.4426950408889634
AUG_ROWS = 16
ONES_ROWS = 16


def _params(*sem):
    return pltpu.CompilerParams(dimension_semantics=sem, vmem_limit_bytes=VMEM_LIMIT_BYTES)


def _resident(shape):
    nd = len(shape)
    return pl.BlockSpec(shape, lambda *_: (0,) * nd, pipeline_mode=pl.Buffered(1))


def _split3(v):
    hi = v.astype(BF16)
    r1 = v - hi.astype(F32)
    mid = r1.astype(BF16)
    lo = (r1 - mid.astype(F32)).astype(BF16)
    return hi, mid, lo


def _softplus(v):
    return jnp.maximum(v, 0.0) + jnp.log1p(jnp.exp(-jnp.abs(v)))


def _sigmoid(v):
    return 1.0 / (1.0 + jnp.exp2(v * (-LOG2E)))


def _layer_norm(h, g, b):
    mu = jnp.mean(h, axis=-1, keepdims=True)
    d = h - mu
    var = jnp.mean(d * d, axis=-1, keepdims=True)
    return d * lax.rsqrt(var + LN_EPS) * g + b


_NT = (((1,), (1,)), ((), ()))


def _bias_placement(nh):
    place_q = np.zeros((nh * AUG_ROWS, LANES), np.float32)
    ones_q = np.zeros((nh * AUG_ROWS, 1), np.float32)
    place_k = np.zeros((LANES, (nh // 2) * LANES), np.float32)
    ones_k = np.zeros((1, (nh // 2) * LANES), np.float32)
    for h in range(nh):
        off = 6 * (h % 2)
        for p in range(3):
            ones_q[h * AUG_ROWS + off + p, 0] = 1.0
            place_q[h * AUG_ROWS + off + 3 + p, p * nh + h] = 1.0
            place_k[p * nh + h, (h // 2) * LANES + off + p] = -1.0
            ones_k[0, (h // 2) * LANES + off + 3 + p] = 1.0
    return (jnp.asarray(place_q, BF16), jnp.asarray(ones_q), jnp.asarray(place_k, BF16),
            jnp.asarray(ones_k))


def _fox_proj_kernel(x_ref, wk_ref, wqt_ref, wvt_ref, wft_ref, bf_ref, ut_ref, pq_ref, oq_ref,
                     pk_ref, ok_ref, k_ref, qt_ref, vt_ref, qta_ref, ka_ref, carry_ref, *,
                     steps_per_batch):
    tm, d = x_ref.shape

    @pl.when(pl.program_id(0) % steps_per_batch == 0)
    def _():
        carry_ref[...] = jnp.zeros_like(carry_ref)

    xb = x_ref[...].astype(BF16)
    k_ref[...] = jnp.dot(xb, wk_ref[...], preferred_element_type=F32).astype(BF16)
    qt_ref[0] = lax.dot_general(wqt_ref[...], xb, _NT, preferred_element_type=F32).astype(BF16)
    vt_ref[0, 0] = lax.dot_general(wvt_ref[...], xb, _NT, preferred_element_type=F32).astype(BF16)

    fl = lax.dot_general(wft_ref[...], xb, _NT, preferred_element_type=F32) + bf_ref[...]
    logf = -_softplus(-fl) * LOG2E
    nh = logf.shape[0]
    parts = jnp.concatenate(_split3(logf), axis=0)
    carry = carry_ref[...]
    zrows = jnp.zeros((LANES - 3 * nh, CUMSUM_CHUNK), F32)
    stacks = []
    for j in range(tm // CUMSUM_CHUNK):
        sl = slice(j * CUMSUM_CHUNK, (j + 1) * CUMSUM_CHUNK)
        pj = jnp.dot(parts[:, sl], ut_ref[...], preferred_element_type=F32)
        cj = (pj[0:nh] + pj[nh:2 * nh] + pj[2 * nh:3 * nh]) + carry
        stacks.append(jnp.concatenate([cp.astype(F32) for cp in _split3(cj)] + [zrows], axis=0))
        carry = cj[:, CUMSUM_CHUNK - 1:CUMSUM_CHUNK]
    carry_ref[...] = carry
    stack = jnp.concatenate(stacks, axis=1)
    qta_ref[0] = (jnp.dot(pq_ref[...], stack.astype(BF16), preferred_element_type=F32)
                  + oq_ref[...]).astype(BF16)
    ka_ref[...] = (jnp.dot(stack.T.astype(BF16), pk_ref[...], preferred_element_type=F32)
                   + ok_ref[...]).astype(BF16)


def _fox_proj(xf, wk, wqt, wvt, wft, bf_col, utri, bsz, seq, tm):
    m, d = xf.shape
    nh = wft.shape[0]
    nblk = seq // tm
    row = lambda i: (i, 0)
    place = _bias_placement(nh)
    out_shape = [jax.ShapeDtypeStruct((m, d), BF16), jax.ShapeDtypeStruct((bsz, d, seq), BF16),
                 jax.ShapeDtypeStruct((bsz, nblk, d, tm), BF16),
                 jax.ShapeDtypeStruct((bsz, nh * AUG_ROWS, seq), BF16),
                 jax.ShapeDtypeStruct((m, (nh // 2) * LANES), BF16)]
    return pl.pallas_call(
        functools.partial(_fox_proj_kernel, steps_per_batch=nblk),
        grid=(m // tm,),
        in_specs=[pl.BlockSpec((tm, d), row), _resident(wk.shape), _resident(wqt.shape),
                  _resident(wvt.shape), _resident(wft.shape), _resident(bf_col.shape),
                  _resident(utri.shape)] + [_resident(a.shape) for a in place],
        out_specs=[pl.BlockSpec((tm, d), row),
                   pl.BlockSpec((1, d, tm), lambda i: (i // nblk, 0, i % nblk)),
                   pl.BlockSpec((1, 1, d, tm), lambda i: (i // nblk, i % nblk, 0, 0)),
                   pl.BlockSpec((1, nh * AUG_ROWS, tm), lambda i: (i // nblk, 0, i % nblk)),
                   pl.BlockSpec((tm, (nh // 2) * LANES), row)],
        out_shape=out_shape,
        scratch_shapes=[pltpu.VMEM((nh, 1), F32)],
        compiler_params=_params("arbitrary"),
        name="fox_proj",
    )(xf, wk, wqt, wvt, wft, bf_col, utri, *place)


def _fox_attn_kernel(qt_ref, qta_ref, k_ref, ka_ref, vt_ref, o_ref, m_scr, acc_scr,
                     sa_scr, sb_scr, ma_scr, mb_scr, *, blk):
    qi = pl.program_id(2)
    hd = ATTN_HEAD_DIM
    qt2 = qt_ref[0]
    zhead = jnp.zeros((hd, blk), BF16)
    zpad = jnp.zeros((LANES - AUG_ROWS, blk), BF16)
    qts = [jnp.concatenate([qt2[0:hd], zhead, qta_ref[0, 0:AUG_ROWS], zpad], axis=0),
           jnp.concatenate([zhead, qt2[hd:2 * hd], qta_ref[0, AUG_ROWS:2 * AUG_ROWS], zpad], axis=0)]
    ones_rows = jnp.ones((ONES_ROWS, blk), BF16)
    key_id = lax.broadcasted_iota(jnp.int32, (blk, blk), 0)
    qry_id = lax.broadcasted_iota(jnp.int32, (blk, blk), 1)
    m_scr[...] = jnp.full_like(m_scr, MASK_VALUE)
    acc_scr[...] = jnp.zeros_like(acc_scr)

    def keys(j):
        start = pl.multiple_of(j * blk, blk)
        return jnp.concatenate([k_ref[0, pl.ds(start, blk), :], ka_ref[0, pl.ds(start, blk), :]],
                               axis=1)

    def values(j):
        vt2 = vt_ref[0, j]
        return [jnp.concatenate([vt2[h * hd:(h + 1) * hd], ones_rows], axis=0) for h in range(2)]

    def produce(kk, h, s_buf, m_buf):
        s = jnp.dot(kk, qts[h], preferred_element_type=F32)
        s_buf[h] = s
        m_buf[h] = jnp.max(s, axis=0, keepdims=True)

    def consume(vt, h, s_buf, m_buf, masked):
        s = s_buf[h]
        if masked:
            s = jnp.where(key_id <= qry_id, s, MASK_VALUE)
            m_blk = jnp.max(s, axis=0, keepdims=True)
        else:
            m_blk = m_buf[h]
        m_old = m_scr[h]
        m_new = jnp.maximum(m_old, m_blk)
        p = jnp.exp2(s - m_new).astype(BF16)
        alpha = jnp.exp2(m_old - m_new)
        acc_scr[h] = alpha * acc_scr[h] + jnp.dot(vt[h], p, preferred_element_type=F32)
        m_scr[h] = m_new

    kk0 = keys(0)
    for h in range(2):
        produce(kk0, h, sa_scr, ma_scr)

    bufs = ((sa_scr, ma_scr), (sb_scr, mb_scr))

    def advance(j0, nsteps):
        for i in range(nsteps):
            kk, vt = keys(j0 + i + 1), values(j0 + i)
            for h in range(2):
                produce(kk, h, *bufs[(i + 1) % 2])
                consume(vt, h, *bufs[i % 2], False)

    def body(t, carry):
        advance(ATTN_UNROLL * t, ATTN_UNROLL)
        return carry

    lax.fori_loop(0, qi // ATTN_UNROLL, body, 0)

    for rem in range(ATTN_UNROLL):
        @pl.when(qi % ATTN_UNROLL == rem)
        def _(rem=rem):
            advance(qi - rem, rem)
            vt = values(qi)
            for h in range(2):
                consume(vt, h, *bufs[rem % 2], True)

    outs = []
    for h in range(2):
        acc = acc_scr[h]
        outs.append(acc[0:hd] * (1.0 / acc[hd:hd + 1]))
    o_ref[0] = jnp.concatenate(outs, axis=0).T.astype(BF16)


def _fox_attn(qt, qta, k, ka, vt, blk):
    b, d, s = qt.shape
    nhp = d // LANES
    nblk = s // blk
    return pl.pallas_call(
        functools.partial(_fox_attn_kernel, blk=blk),
        grid=(b, nhp, nblk),
        in_specs=[
            pl.BlockSpec((1, LANES, blk), lambda bi, hp, qi: (bi, hp, qi)),
            pl.BlockSpec((1, 2 * AUG_ROWS, blk), lambda bi, hp, qi: (bi, hp, qi)),
            pl.BlockSpec((1, s, LANES), lambda bi, hp, qi: (bi, 0, hp)),
            pl.BlockSpec((1, s, LANES), lambda bi, hp, qi: (bi, 0, hp)),
            pl.BlockSpec((1, nblk, LANES, blk), lambda bi, hp, qi: (bi, 0, hp, 0)),
        ],
        out_specs=pl.BlockSpec((1, blk, LANES), lambda bi, hp, qi: (bi, qi, hp)),
        out_shape=jax.ShapeDtypeStruct((b, s, d), BF16),
        scratch_shapes=[pltpu.VMEM((2, 1, blk), F32),
                        pltpu.VMEM((2, ATTN_HEAD_DIM + ONES_ROWS, blk), F32),
                        pltpu.VMEM((2, blk, blk), F32), pltpu.VMEM((2, blk, blk), F32),
                        pltpu.VMEM((2, 1, blk), F32), pltpu.VMEM((2, 1, blk), F32)],
        compiler_params=_params("arbitrary", "arbitrary", "arbitrary"),
        name="fox_attn",
    )(qt, qta, k, ka, vt)


def _ffn_kernel(y_ref, wo_ref, x_ref, g1_ref, b1_ref, wu_ref, wg_ref, cw_ref, cb_ref, wd_ref,
                g_ref, b_ref, p_ref, wproj_ref, wgate_ref, bgate_ref, o_ref, gbuf0_ref, gbuf1_ref,
                carry_ref, *, chunks, steps_per_batch, alpha):
    tm = x_ref.shape[0]
    halo = SUBLANES
    kconv = cw_ref.shape[0]
    gbufs = (gbuf0_ref, gbuf1_ref)
    halves = (slice(0, tm // 2), slice(tm // 2, tm))
    last = len(chunks) - 1

    @pl.when(pl.program_id(0) % steps_per_batch == 0)
    def _():
        carry_ref[...] = jnp.zeros_like(carry_ref)

    mixes = [jnp.dot(y_ref[rs, :], wo_ref[...], preferred_element_type=F32) for rs in halves]
    xh = [_layer_norm(alpha * x_ref[rs, :] + mix, g1_ref[...], b1_ref[...])
          for rs, mix in zip(halves, mixes)]
    xbh = [v.astype(BF16) for v in xh]

    def stage(i, g):
        c0, cw = chunks[i]
        cs = slice(c0, c0 + cw)
        buf = gbufs[i % 2]
        buf[0:halo, 0:cw] = carry_ref[:, cs]
        buf[halo:halo + tm, 0:cw] = g
        carry_ref[:, cs] = g[tm - halo:tm, :]

    def up(i, lhs):
        c0, cw = chunks[i]
        cs = slice(c0, c0 + cw)
        return (jnp.dot(lhs, wu_ref[:, cs], preferred_element_type=F32),
                jnp.dot(lhs, wg_ref[:, cs], preferred_element_type=F32))

    def gated(i, u):
        c0, cw = chunks[i]
        cs = slice(c0, c0 + cw)
        buf = gbufs[i % 2]
        conv = cb_ref[:, cs] + cw_ref[kconv - 1:kconv, cs] * buf[halo:halo + tm, 0:cw]
        for k in range(kconv - 1):
            off = halo - (kconv - 1) + k
            conv = conv + cw_ref[k:k + 1, cs] * buf[off:off + tm, 0:cw]
        gelu = 0.5 * conv * (1.0 + lax.erf(conv * (1.0 / math.sqrt(2.0))))
        return (gelu * u).astype(BF16)

    first = [up(0, xbh[0]), up(0, xbh[1])]
    u_cur = jnp.concatenate([first[0][0], first[1][0]], axis=0)
    stage(0, jnp.concatenate([first[0][1], first[1][1]], axis=0))
    xb = jnp.concatenate(xbh, axis=0)

    acc = None
    for i in range(last):
        u_next, g_next = up(i + 1, xb)
        stage(i + 1, g_next)
        cs = slice(chunks[i][0], chunks[i][0] + chunks[i][1])
        part = jnp.dot(gated(i, u_cur), wd_ref[cs, :], preferred_element_type=F32)
        acc = part if acc is None else acc + part
        u_cur = u_next

    hid = gated(last, u_cur)
    cs = slice(chunks[last][0], chunks[last][0] + chunks[last][1])
    parts = [jnp.dot(hid[rs, :], wd_ref[cs, :], preferred_element_type=F32) for rs in halves]
    for rs, part, xin in zip(halves, parts, xh):
        total = part if acc is None else acc[rs, :] + part
        x2 = _layer_norm(alpha * xin + total, g_ref[...], b_ref[...])
        gate = _sigmoid(jnp.dot(x2.astype(BF16), wgate_ref[...], preferred_element_type=F32)
                        + bgate_ref[...])
        emb = jnp.dot(p_ref[rs, :].astype(BF16), wproj_ref[...], preferred_element_type=F32)
        o_ref[rs, :] = x2 + gate * emb


def _ffn_chunks(f):
    tile = 2 * LANES
    if f % tile:
        return ((0, f),)
    step = 4 * tile
    return tuple((c0, min(step, f - c0)) for c0 in range(0, f, step))


def _ffn(y, wo, xf, g1, b1, wu, wg, cw, cb, wd, g, b, pf, wproj, wgate, bgate, alpha, seq):
    m, d = xf.shape
    f = wu.shape[1]
    pdim = pf.shape[1]
    kdim = y.shape[1]
    tm = min(ROW_TILE, seq)
    chunks = _ffn_chunks(f)
    wmax = max(c[1] for c in chunks)
    row = lambda i: (i, 0)
    return pl.pallas_call(
        functools.partial(_ffn_kernel, chunks=chunks, steps_per_batch=seq // tm, alpha=alpha),
        grid=(m // tm,),
        in_specs=[pl.BlockSpec((tm, kdim), row), _resident(wo.shape), pl.BlockSpec((tm, d), row),
                  _resident(g1.shape), _resident(b1.shape), _resident(wu.shape), _resident(wg.shape),
                  _resident(cw.shape), _resident(cb.shape), _resident(wd.shape),
                  _resident(g.shape), _resident(b.shape), pl.BlockSpec((tm, pdim), row),
                  _resident(wproj.shape), _resident(wgate.shape), _resident(bgate.shape)],
        out_specs=pl.BlockSpec((tm, d), row),
        out_shape=jax.ShapeDtypeStruct((m, d), F32),
        scratch_shapes=[pltpu.VMEM((tm + SUBLANES, wmax), F32), pltpu.VMEM((tm + SUBLANES, wmax), F32),
                        pltpu.VMEM((SUBLANES, f), F32)],
        compiler_params=_params("arbitrary"),
        name="conv_ffn",
    )(y, wo, xf, g1, b1, wu, wg, cw, cb, wd, g, b, pf, wproj, wgate, bgate)


def _ssd_proj_kernel(x_ref, wz_ref, wxbc_ref, wdt_ref, cw_ref, cb_ref, dtb_ref,
                     z_ref, xs_ref, bt_ref, c_ref, dt_ref, dtt_ref, buf0_ref, buf1_ref, carry_ref,
                     *, steps_per_batch, col_tile):
    tm = x_ref.shape[0]
    halo = SUBLANES
    kconv = cw_ref.shape[0]
    bufs = (buf0_ref, buf1_ref)

    @pl.when(pl.program_id(0) % steps_per_batch == 0)
    def _():
        carry_ref[...] = jnp.zeros_like(carry_ref)

    xb = x_ref[...].astype(BF16)
    d_inner = xs_ref.shape[1]
    gn = c_ref.shape[1]
    nconv = wxbc_ref.shape[1] // col_tile
    z_chunks = list(range(0, wz_ref.shape[1], col_tile))

    def project(i):
        cs = slice(i * col_tile, (i + 1) * col_tile)
        buf = bufs[i % 2]
        r = jnp.dot(xb, wxbc_ref[:, cs], preferred_element_type=F32)
        buf[0:halo, :] = carry_ref[:, cs]
        buf[halo:halo + tm, :] = r
        carry_ref[:, cs] = r[tm - halo:tm, :]

    def project_z():
        if z_chunks:
            zs = slice(z_chunks[0], z_chunks.pop(0) + col_tile)
            z_ref[:, zs] = jnp.dot(xb, wz_ref[:, zs], preferred_element_type=F32).astype(BF16)

    def conv_act(i):
        c0 = i * col_tile
        cs = slice(c0, c0 + col_tile)
        buf = bufs[i % 2]
        conv = cb_ref[:, cs] + cw_ref[kconv - 1:kconv, cs] * buf[halo:halo + tm, :]
        for k in range(kconv - 1):
            off = halo - (kconv - 1) + k
            conv = conv + cw_ref[k:k + 1, cs] * buf[off:off + tm, :]
        act = conv * _sigmoid(conv)
        if c0 < d_inner:
            xs_ref[:, cs] = act.astype(BF16)
        elif c0 < d_inner + gn:
            bt_ref[0, c0 - d_inner:c0 - d_inner + col_tile, :] = act.T.astype(BF16)
        else:
            c_ref[:, c0 - d_inner - gn:c0 - d_inner - gn + col_tile] = act.astype(BF16)

    project(0)
    for i in range(nconv):
        if i + 1 < nconv:
            project(i + 1)
        else:
            project_z()
        conv_act(i)
    while z_chunks:
        project_z()
    dt = _softplus(jnp.dot(xb, wdt_ref[...], preferred_element_type=F32) + dtb_ref[...])
    dt_ref[...] = dt
    dtt_ref[0] = dt.T[0:dtt_ref.shape[1], :]


def _ssd_proj(xf, wz, wxbc, wdt, cw, cb, dtb, bsz, seq, gn, nheads):
    m, d = xf.shape
    d_inner = wz.shape[1]
    tm = min(ROW_TILE, seq)
    nblk = seq // tm
    col_tile = min(1024, gn)
    row = lambda i: (i, 0)
    tposed = lambda i: (i // nblk, 0, i % nblk)
    out_shape = [jax.ShapeDtypeStruct((m, d_inner), BF16), jax.ShapeDtypeStruct((m, d_inner), BF16),
                 jax.ShapeDtypeStruct((bsz, gn, seq), BF16), jax.ShapeDtypeStruct((m, gn), BF16),
                 jax.ShapeDtypeStruct((m, LANES), F32),
                 jax.ShapeDtypeStruct((bsz, nheads, seq), F32)]
    return pl.pallas_call(
        functools.partial(_ssd_proj_kernel, steps_per_batch=seq // tm, col_tile=col_tile),
        grid=(m // tm,),
        in_specs=[pl.BlockSpec((tm, d), row), _resident(wz.shape), _resident(wxbc.shape),
                  _resident(wdt.shape), _resident(cw.shape), _resident(cb.shape),
                  _resident(dtb.shape)],
        out_specs=[pl.BlockSpec((tm, d_inner), row), pl.BlockSpec((tm, d_inner), row),
                   pl.BlockSpec((1, gn, tm), tposed), pl.BlockSpec((tm, gn), row),
                   pl.BlockSpec((tm, LANES), row), pl.BlockSpec((1, nheads, tm), tposed)],
        out_shape=out_shape,
        scratch_shapes=[pltpu.VMEM((tm + SUBLANES, col_tile), F32),
                        pltpu.VMEM((tm + SUBLANES, col_tile), F32),
                        pltpu.VMEM((SUBLANES, wxbc.shape[1]), F32)],
        compiler_params=_params("arbitrary"),
        name="ssd_proj",
    )(xf, wz, wxbc, wdt, cw, cb, dtb)


def _expand_heads(cols, lane_head):
    out = cols[0]
    for r in range(1, len(cols)):
        out = jnp.where(lane_head >= r, cols[r], out)
    return out


def _ssd_scan_kernel(xs_ref, bt_ref, c_ref, z_ref, dt_ref, dtt_ref, arow_ref, acol_ref,
                     dskip_ref, nw_ref, ltri_ref, utri_ref, y_ref, state_ref, *, heads_per_group):
    q = xs_ref.shape[1]
    n = SSM_STATE
    hp_ = SSM_HEAD_DIM
    gw = heads_per_group * hp_
    ngroups = xs_ref.shape[2] // gw

    @pl.when(pl.program_id(1) == 0)
    def _():
        state_ref[...] = jnp.zeros_like(state_ref)

    dt_col = dt_ref[0]
    dt_row = dtt_ref[0]
    nheads = dt_row.shape[0]
    a_col = dt_col * (arow_ref[...] * LOG2E)
    pc = jnp.dot(ltri_ref[...], jnp.concatenate(_split3(a_col), axis=1),
                 preferred_element_type=F32)
    acum_col = pc[:, 0:LANES] + pc[:, LANES:2 * LANES] + pc[:, 2 * LANES:3 * LANES]
    a_row = dt_row * (acol_ref[...] * LOG2E)
    pr = jnp.dot(jnp.concatenate(_split3(a_row), axis=0), utri_ref[...],
                 preferred_element_type=F32)
    acum_row = pr[0:nheads] + pr[nheads:2 * nheads] + pr[2 * nheads:3 * nheads]
    dec_row = jnp.exp2(acum_col[q - 1:q, :])
    w_row = jnp.exp2(acum_row[:, q - 1:q] - acum_row) * dt_row
    src_row = acum_row - jnp.log2(dt_row)

    nsub = q // LANES
    tri = (lax.broadcasted_iota(jnp.int32, (LANES, LANES), 0)
           >= lax.broadcasted_iota(jnp.int32, (LANES, LANES), 1))
    lane_head = lax.broadcasted_iota(jnp.int32, (1, gw), 1) // hp_
    low_half = lax.broadcasted_iota(jnp.int32, (1, LANES), 1) < hp_

    for g in range(ngroups):
        gs = slice(g * gw, (g + 1) * gw)
        ns = slice(g * n, (g + 1) * n)
        cg = c_ref[0, :, ns]
        btg = bt_ref[0, ns, :]
        btg_f = btg.astype(F32)
        xg_b = xs_ref[0, :, gs]
        heads = range(g * heads_per_group, (g + 1) * heads_per_group)
        cb = jnp.dot(cg, btg, preferred_element_type=F32)
        state = state_ref[g]
        y_inter = jnp.dot(cg, state.astype(BF16), preferred_element_type=F32)
        e_cols, y_intra, st_new = [], None, None
        for r, h in enumerate(heads):
            acol_b = jnp.broadcast_to(acum_col[:, h:h + 1], (q, LANES))
            e_cols.append(jnp.exp2(acol_b))
            rows = []
            for ti in range(nsub):
                tsl = slice(ti * LANES, (ti + 1) * LANES)
                tiles = []
                for si in range(nsub):
                    ssl = slice(si * LANES, (si + 1) * LANES)
                    if si > ti:
                        tiles.append(jnp.zeros((LANES, LANES), BF16))
                        continue
                    seg = acol_b[tsl] - src_row[h:h + 1, ssl]
                    if si == ti:
                        seg = jnp.where(tri, seg, -jnp.inf)
                    tiles.append((cb[tsl, ssl] * jnp.exp2(seg)).astype(BF16))
                rows.append(jnp.concatenate(tiles, axis=1))
            mh = jnp.concatenate(rows, axis=0)
            xh = jnp.where(lane_head == r, xg_b, jnp.zeros_like(xg_b))
            part = jnp.dot(mh, xh, preferred_element_type=F32)
            y_intra = part if y_intra is None else y_intra + part
            btw = (btg_f * w_row[h:h + 1, :]).astype(BF16)
            part = jnp.dot(btw, xh, preferred_element_type=F32)
            st_new = part if st_new is None else st_new + part
        e_x = jnp.concatenate([jnp.where(low_half, e_cols[2 * i], e_cols[2 * i + 1])
                               for i in range(gw // LANES)], axis=1)
        y_g = y_inter * e_x + y_intra
        d_x = _expand_heads([dec_row[:, h:h + 1] for h in heads], lane_head)
        state_ref[g] = state * d_x + st_new
        xg = xg_b.astype(F32)
        zg = z_ref[0, :, gs].astype(F32)
        yv = (y_g + dskip_ref[:, gs] * xg) * (zg * _sigmoid(zg))
        ms = jnp.mean(yv * yv, axis=-1, keepdims=True)
        y_ref[0, :, gs] = (yv * lax.rsqrt(ms + RMS_EPS) * nw_ref[:, gs]).astype(BF16)


def _ssd_scan(xs, bmt, cm, z, dt, dtt, arow, acol, dskip, nw, ltri, utri, q):
    b, s, d_inner = xs.shape
    gn = cm.shape[2]
    nheads = dtt.shape[1]
    ngroups = gn // SSM_STATE
    heads_per_group = nheads // ngroups
    gw = heads_per_group * SSM_HEAD_DIM
    blk = lambda w: pl.BlockSpec((1, q, w), lambda bi, ci: (bi, ci, 0))
    return pl.pallas_call(
        functools.partial(_ssd_scan_kernel, heads_per_group=heads_per_group),
        grid=(b, s // q),
        in_specs=[blk(d_inner), pl.BlockSpec((1, gn, q), lambda bi, ci: (bi, 0, ci)),
                  blk(gn), blk(d_inner), blk(LANES),
                  pl.BlockSpec((1, nheads, q), lambda bi, ci: (bi, 0, ci)),
                  _resident(arow.shape), _resident(acol.shape), _resident(dskip.shape),
                  _resident(nw.shape), _resident(ltri.shape), _resident(utri.shape)],
        out_specs=blk(d_inner),
        out_shape=jax.ShapeDtypeStruct((b, s, d_inner), BF16),
        scratch_shapes=[pltpu.VMEM((ngroups, SSM_STATE, gw), F32)],
        compiler_params=_params("arbitrary", "arbitrary"),
        name="ssd_scan",
    )(xs, bmt, cm, z, dt, dtt, arow, acol, dskip, nw, ltri, utri)


def _pad_cols(a, width):
    return jnp.pad(a, ((0, 0), (0, width - a.shape[1])))


def _fox_mixer(xf, bsz, seq, w_in, b_f):
    m, d = xf.shape
    nh = d // ATTN_HEAD_DIM
    scale = LOG2E / math.sqrt(ATTN_HEAD_DIM)
    wqt = (w_in[:, :d] * scale).T.astype(BF16)
    wk = w_in[:, d:2 * d].astype(BF16)
    wvt = w_in[:, 2 * d:3 * d].T.astype(BF16)
    wft = w_in[:, 3 * d:].T.astype(BF16)
    utri = jnp.triu(jnp.ones((CUMSUM_CHUNK, CUMSUM_CHUNK), F32)).astype(BF16)
    blk = min(ATTN_BLOCK, seq)
    k, qt, vt, qta, ka = _fox_proj(xf, wk, wqt, wvt, wft, b_f.reshape(nh, 1), utri, bsz, seq, blk)
    o = _fox_attn(qt, qta, k.reshape(bsz, seq, d), ka.reshape(bsz, seq, -1), vt, blk)
    return o.reshape(m, d)


def _ssd_mixer(xf, bsz, seq, w_in, conv_w, conv_b, dt_bias, a_log, d_skip, norm_w):
    m, d = xf.shape
    nheads = dt_bias.shape[0]
    d_inner = nheads * SSM_HEAD_DIM
    gn = SSM_GROUPS * SSM_STATE
    wz = w_in[:, :d_inner].astype(BF16)
    wxbc = w_in[:, d_inner:2 * d_inner + 2 * gn].astype(BF16)
    wdt = _pad_cols(w_in[:, 2 * d_inner + 2 * gn:], LANES).astype(BF16)
    dtb = _pad_cols(dt_bias.reshape(1, nheads), LANES)
    z, xs, bmt, cm, dt, dtt = _ssd_proj(xf, wz, wxbc, wdt, conv_w, conv_b.reshape(1, -1), dtb,
                                        bsz, seq, gn, nheads)
    a = -jnp.exp(a_log.astype(F32))
    arow = _pad_cols(a.reshape(1, nheads), LANES)
    acol = a.reshape(nheads, 1)
    dskip = jnp.repeat(d_skip, SSM_HEAD_DIM).reshape(1, d_inner)
    q = min(SSD_CHUNK, seq)
    ltri = jnp.tril(jnp.ones((q, q), F32)).astype(BF16)
    utri = jnp.triu(jnp.ones((q, q), F32)).astype(BF16)
    y = _ssd_scan(xs.reshape(bsz, seq, d_inner), bmt, cm.reshape(bsz, seq, gn),
                  z.reshape(bsz, seq, d_inner), dt.reshape(bsz, seq, LANES), dtt, arow, acol,
                  dskip, norm_w.reshape(1, d_inner), ltri, utri, q)
    return y.reshape(m, d_inner)


def kernel(x, p, attn_w_in, attn_b_f, attn_w_out, ssm_w_in, ssm_conv_w, ssm_conv_b, ssm_dt_bias, ssm_A_log, ssm_D, ssm_norm_w, ssm_w_out, ln_mix_g, ln_mix_b, ffn_w_up, ffn_conv_w, ffn_conv_b, ffn_w_down, ln_ffn_g, ln_ffn_b, ple_w_proj, ple_w_gate, ple_b_gate):
    bsz, seq, d = x.shape
    depth = p.shape[0]
    n_mixers = 2
    alpha = (2 * depth) ** 0.25
    xf = x.reshape(bsz * seq, d)
    for i in range(depth):
        j = i // n_mixers
        if i % n_mixers == 0:
            y, w_out = _fox_mixer(xf, bsz, seq, attn_w_in[j], attn_b_f[j]), attn_w_out[j]
        else:
            y = _ssd_mixer(xf, bsz, seq, ssm_w_in[j], ssm_conv_w[j], ssm_conv_b[j],
                           ssm_dt_bias[j], ssm_A_log[j], ssm_D[j], ssm_norm_w[j])
            w_out = ssm_w_out[j]
        f = ffn_conv_w.shape[-1]
        xf = _ffn(y, w_out.astype(BF16), xf, ln_mix_g[i].reshape(1, d), ln_mix_b[i].reshape(1, d),
                  ffn_w_up[i][:, :f].astype(BF16), ffn_w_up[i][:, f:].astype(BF16),
                  ffn_conv_w[i], ffn_conv_b[i].reshape(1, f), ffn_w_down[i].astype(BF16),
                  ln_ffn_g[i].reshape(1, d), ln_ffn_b[i].reshape(1, d),
                  p[i].reshape(bsz * seq, -1), ple_w_proj[i].astype(BF16),
                  ple_w_gate[i].astype(BF16), ple_b_gate[i].reshape(1, d), alpha, seq)
    return xf.reshape(bsz, seq, d)
```

```python
import functools
import math

import jax
import jax.numpy as jnp
import numpy as np
from jax import lax
from jax.experimental import pallas as pl
from jax.experimental.pallas import tpu as pltpu

F32 = jnp.float32
BF16 = jnp.bfloat16

ATTN_HEAD_DIM = 64
SSM_HEAD_DIM = 64
SSM_GROUPS = 8
SSM_STATE = 128
LN_EPS = 1e-5
RMS_EPS = 1e-5

LANES = 128
SUBLANES = 8
VMEM_LIMIT_BYTES = 56 * 1024 * 1024

ROW_TILE = 512
ATTN_BLOCK = 512
ATTN_UNROLL = 4
SSD_CHUNK = 256
CUMSUM_CHUNK = 256
MASK_VALUE = -1e30
LOG2E = 1.4426950408889634
AUG_ROWS = 16
ONES_ROWS = 16


def _params(*sem):
    return pltpu.CompilerParams(dimension_semantics=sem, vmem_limit_bytes=VMEM_LIMIT_BYTES)


def _resident(shape):
    nd = len(shape)
    return pl.BlockSpec(shape, lambda *_: (0,) * nd, pipeline_mode=pl.Buffered(1))


def _split3(v):
    hi = v.astype(BF16)
    r1 = v - hi.astype(F32)
    mid = r1.astype(BF16)
    lo = (r1 - mid.astype(F32)).astype(BF16)
    return hi, mid, lo


def _softplus(v):
    return jnp.maximum(v, 0.0) + jnp.log1p(jnp.exp(-jnp.abs(v)))


def _sigmoid(v):
    return 1.0 / (1.0 + jnp.exp2(v * (-LOG2E)))


def _layer_norm(h, g, b):
    mu = jnp.mean(h, axis=-1, keepdims=True)
    d = h - mu
    var = jnp.mean(d * d, axis=-1, keepdims=True)
    return d * lax.rsqrt(var + LN_EPS) * g + b


_NT = (((1,), (1,)), ((), ()))


def _bias_placement(nh):
    place_q = np.zeros((nh * AUG_ROWS, LANES), np.float32)
    ones_q = np.zeros((nh * AUG_ROWS, 1), np.float32)
    place_k = np.zeros((LANES, (nh // 2) * LANES), np.float32)
    ones_k = np.zeros((1, (nh // 2) * LANES), np.float32)
    for h in range(nh):
        off = 6 * (h % 2)
        for p in range(3):
            ones_q[h * AUG_ROWS + off + p, 0] = 1.0
            place_q[h * AUG_ROWS + off + 3 + p, p * nh + h] = 1.0
            place_k[p * nh + h, (h // 2) * LANES + off + p] = -1.0
            ones_k[0, (h // 2) * LANES + off + 3 + p] = 1.0
    return (jnp.asarray(place_q, BF16), jnp.asarray(ones_q), jnp.asarray(place_k, BF16),
            jnp.asarray(ones_k))


def _fox_proj_kernel(x_ref, wk_ref, wqt_ref, wvt_ref, wft_ref, bf_ref, ut_ref, pq_ref, oq_ref,
                     pk_ref, ok_ref, k_ref, qt_ref, vt_ref, qta_ref, ka_ref, carry_ref, *,
                     steps_per_batch):
    tm, d = x_ref.shape

    @pl.when(pl.program_id(0) % steps_per_batch == 0)
    def _():
        carry_ref[...] = jnp.zeros_like(carry_ref)

    xb = x_ref[...].astype(BF16)
    fl = lax.dot_general(wft_ref[...], xb, _NT, preferred_element_type=F32) + bf_ref[...]
    k_ref[...] = jnp.dot(xb, wk_ref[...], preferred_element_type=F32).astype(BF16)
    logf = -_softplus(-fl) * LOG2E
    nh = logf.shape[0]
    parts = jnp.concatenate(_split3(logf), axis=0)
    sums = [jnp.dot(parts[:, j * CUMSUM_CHUNK:(j + 1) * CUMSUM_CHUNK], ut_ref[...],
                    preferred_element_type=F32) for j in range(tm // CUMSUM_CHUNK)]
    qt_ref[0] = lax.dot_general(wqt_ref[...], xb, _NT, preferred_element_type=F32).astype(BF16)
    carry = carry_ref[...]
    zrows = jnp.zeros((LANES - 3 * nh, CUMSUM_CHUNK), F32)
    stacks = []
    for pj in sums:
        cj = (pj[0:nh] + pj[nh:2 * nh] + pj[2 * nh:3 * nh]) + carry
        stacks.append(jnp.concatenate([cp.astype(F32) for cp in _split3(cj)] + [zrows], axis=0))
        carry = cj[:, CUMSUM_CHUNK - 1:CUMSUM_CHUNK]
    carry_ref[...] = carry
    stack = jnp.concatenate(stacks, axis=1)
    qta_ref[0] = (jnp.dot(pq_ref[...], stack.astype(BF16), preferred_element_type=F32)
                  + oq_ref[...]).astype(BF16)
    ka_ref[...] = (jnp.dot(stack.T.astype(BF16), pk_ref[...], preferred_element_type=F32)
                   + ok_ref[...]).astype(BF16)
    vt_ref[0, 0] = lax.dot_general(wvt_ref[...], xb, _NT, preferred_element_type=F32).astype(BF16)


def _fox_proj(xf, wk, wqt, wvt, wft, bf_col, utri, bsz, seq, tm):
    m, d = xf.shape
    nh = wft.shape[0]
    nblk = seq // tm
    row = lambda i: (i, 0)
    place = _bias_placement(nh)
    out_shape = [jax.ShapeDtypeStruct((m, d), BF16), jax.ShapeDtypeStruct((bsz, d, seq), BF16),
                 jax.ShapeDtypeStruct((bsz, nblk, d, tm), BF16),
                 jax.ShapeDtypeStruct((bsz, nh * AUG_ROWS, seq), BF16),
                 jax.ShapeDtypeStruct((m, (nh // 2) * LANES), BF16)]
    return pl.pallas_call(
        functools.partial(_fox_proj_kernel, steps_per_batch=nblk),
        grid=(m // tm,),
        in_specs=[pl.BlockSpec((tm, d), row), _resident(wk.shape), _resident(wqt.shape),
                  _resident(wvt.shape), _resident(wft.shape), _resident(bf_col.shape),
                  _resident(utri.shape)] + [_resident(a.shape) for a in place],
        out_specs=[pl.BlockSpec((tm, d), row),
                   pl.BlockSpec((1, d, tm), lambda i: (i // nblk, 0, i % nblk)),
                   pl.BlockSpec((1, 1, d, tm), lambda i: (i // nblk, i % nblk, 0, 0)),
                   pl.BlockSpec((1, nh * AUG_ROWS, tm), lambda i: (i // nblk, 0, i % nblk)),
                   pl.BlockSpec((tm, (nh // 2) * LANES), row)],
        out_shape=out_shape,
        scratch_shapes=[pltpu.VMEM((nh, 1), F32)],
        compiler_params=_params("arbitrary"),
        name="fox_proj",
    )(xf, wk, wqt, wvt, wft, bf_col, utri, *place)


def _fox_attn_kernel(qt_ref, qta_ref, k_ref, ka_ref, vt_ref, o_ref, m_scr, acc_scr,
                     sa_scr, sb_scr, ma_scr, mb_scr, *, blk):
    qi = pl.program_id(2)
    hd = ATTN_HEAD_DIM
    qt2 = qt_ref[0]
    zhead = jnp.zeros((hd, blk), BF16)
    zpad = jnp.zeros((LANES - AUG_ROWS, blk), BF16)
    qts = [jnp.concatenate([qt2[0:hd], zhead, qta_ref[0, 0:AUG_ROWS], zpad], axis=0),
           jnp.concatenate([zhead, qt2[hd:2 * hd], qta_ref[0, AUG_ROWS:2 * AUG_ROWS], zpad], axis=0)]
    ones_rows = jnp.ones((ONES_ROWS, blk), BF16)
    key_id = lax.broadcasted_iota(jnp.int32, (blk, blk), 0)
    qry_id = lax.broadcasted_iota(jnp.int32, (blk, blk), 1)
    m_scr[...] = jnp.full_like(m_scr, MASK_VALUE)
    acc_scr[...] = jnp.zeros_like(acc_scr)

    def keys(j):
        start = pl.multiple_of(j * blk, blk)
        return jnp.concatenate([k_ref[0, pl.ds(start, blk), :], ka_ref[0, pl.ds(start, blk), :]],
                               axis=1)

    def values(j):
        vt2 = vt_ref[0, j]
        return [jnp.concatenate([vt2[h * hd:(h + 1) * hd], ones_rows], axis=0) for h in range(2)]

    def produce(kk, h, s_buf, m_buf):
        s = jnp.dot(kk, qts[h], preferred_element_type=F32)
        s_buf[h] = s
        m_buf[h] = jnp.max(s, axis=0, keepdims=True)

    def consume(vt, h, s_buf, m_buf, masked):
        s = s_buf[h]
        if masked:
            s = jnp.where(key_id <= qry_id, s, MASK_VALUE)
            m_blk = jnp.max(s, axis=0, keepdims=True)
        else:
            m_blk = m_buf[h]
        m_old = m_scr[h]
        m_new = jnp.maximum(m_old, m_blk)
        p = jnp.exp2(s - m_new).astype(BF16)
        alpha = jnp.exp2(m_old - m_new)
        acc_scr[h] = alpha * acc_scr[h] + jnp.dot(vt[h], p, preferred_element_type=F32)
        m_scr[h] = m_new

    kk0 = keys(0)
    for h in range(2):
        produce(kk0, h, sa_scr, ma_scr)

    bufs = ((sa_scr, ma_scr), (sb_scr, mb_scr))

    def advance(j0, nsteps):
        for i in range(nsteps):
            kk, vt = keys(j0 + i + 1), values(j0 + i)
            for h in range(2):
                produce(kk, h, *bufs[(i + 1) % 2])
                consume(vt, h, *bufs[i % 2], False)

    def body(t, carry):
        advance(ATTN_UNROLL * t, ATTN_UNROLL)
        return carry

    lax.fori_loop(0, qi // ATTN_UNROLL, body, 0)

    for rem in range(ATTN_UNROLL):
        @pl.when(qi % ATTN_UNROLL == rem)
        def _(rem=rem):
            advance(qi - rem, rem)
            vt = values(qi)
            for h in range(2):
                consume(vt, h, *bufs[rem % 2], True)

    outs = []
    for h in range(2):
        acc = acc_scr[h]
        outs.append(acc[0:hd] * (1.0 / acc[hd:hd + 1]))
    o_ref[0] = jnp.concatenate(outs, axis=0).T.astype(BF16)


def _fox_attn(qt, qta, k, ka, vt, blk):
    b, d, s = qt.shape
    nhp = d // LANES
    nblk = s // blk
    return pl.pallas_call(
        functools.partial(_fox_attn_kernel, blk=blk),
        grid=(b, nhp, nblk),
        in_specs=[
            pl.BlockSpec((1, LANES, blk), lambda bi, hp, qi: (bi, hp, qi)),
            pl.BlockSpec((1, 2 * AUG_ROWS, blk), lambda bi, hp, qi: (bi, hp, qi)),
            pl.BlockSpec((1, s, LANES), lambda bi, hp, qi: (bi, 0, hp)),
            pl.BlockSpec((1, s, LANES), lambda bi, hp, qi: (bi, 0, hp)),
            pl.BlockSpec((1, nblk, LANES, blk), lambda bi, hp, qi: (bi, 0, hp, 0)),
        ],
        out_specs=pl.BlockSpec((1, blk, LANES), lambda bi, hp, qi: (bi, qi, hp)),
        out_shape=jax.ShapeDtypeStruct((b, s, d), BF16),
        scratch_shapes=[pltpu.VMEM((2, 1, blk), F32),
                        pltpu.VMEM((2, ATTN_HEAD_DIM + ONES_ROWS, blk), F32),
                        pltpu.VMEM((2, blk, blk), F32), pltpu.VMEM((2, blk, blk), F32),
                        pltpu.VMEM((2, 1, blk), F32), pltpu.VMEM((2, 1, blk), F32)],
        compiler_params=_params("arbitrary", "arbitrary", "arbitrary"),
        name="fox_attn",
    )(qt, qta, k, ka, vt)


def _ffn_kernel(y_ref, wo_ref, x_ref, g1_ref, b1_ref, wu_ref, wg_ref, cw_ref, cb_ref, wd_ref,
                g_ref, b_ref, p_ref, wproj_ref, wgate_ref, bgate_ref, o_ref, gbuf0_ref, gbuf1_ref,
                carry_ref, *, chunks, steps_per_batch, alpha):
    tm = x_ref.shape[0]
    halo = SUBLANES
    kconv = cw_ref.shape[0]
    gbufs = (gbuf0_ref, gbuf1_ref)
    halves = (slice(0, tm // 2), slice(tm // 2, tm))
    last = len(chunks) - 1

    @pl.when(pl.program_id(0) % steps_per_batch == 0)
    def _():
        carry_ref[...] = jnp.zeros_like(carry_ref)

    mixes = [jnp.dot(y_ref[rs, :], wo_ref[...], preferred_element_type=F32) for rs in halves]
    xh = [_layer_norm(alpha * x_ref[rs, :] + mix, g1_ref[...], b1_ref[...])
          for rs, mix in zip(halves, mixes)]
    xbh = [v.astype(BF16) for v in xh]

    def stage(i, g):
        c0, cw = chunks[i]
        cs = slice(c0, c0 + cw)
        buf = gbufs[i % 2]
        buf[0:halo, 0:cw] = carry_ref[:, cs]
        buf[halo:halo + tm, 0:cw] = g
        carry_ref[:, cs] = g[tm - halo:tm, :]

    def up(i, lhs):
        c0, cw = chunks[i]
        cs = slice(c0, c0 + cw)
        return (jnp.dot(lhs, wu_ref[:, cs], preferred_element_type=F32),
                jnp.dot(lhs, wg_ref[:, cs], preferred_element_type=F32))

    def gated(i, u):
        c0, cw = chunks[i]
        cs = slice(c0, c0 + cw)
        buf = gbufs[i % 2]
        conv = cb_ref[:, cs] + cw_ref[kconv - 1:kconv, cs] * buf[halo:halo + tm, 0:cw]
        for k in range(kconv - 1):
            off = halo - (kconv - 1) + k
            conv = conv + cw_ref[k:k + 1, cs] * buf[off:off + tm, 0:cw]
        gelu = 0.5 * conv * (1.0 + lax.erf(conv * (1.0 / math.sqrt(2.0))))
        return (gelu * u).astype(BF16)

    first = [up(0, xbh[0]), up(0, xbh[1])]
    u_cur = jnp.concatenate([first[0][0], first[1][0]], axis=0)
    stage(0, jnp.concatenate([first[0][1], first[1][1]], axis=0))
    xb = jnp.concatenate(xbh, axis=0)

    acc = None
    for i in range(last):
        u_next, g_next = up(i + 1, xb)
        stage(i + 1, g_next)
        cs = slice(chunks[i][0], chunks[i][0] + chunks[i][1])
        part = jnp.dot(gated(i, u_cur), wd_ref[cs, :], preferred_element_type=F32)
        acc = part if acc is None else acc + part
        u_cur = u_next

    hid = gated(last, u_cur)
    cs = slice(chunks[last][0], chunks[last][0] + chunks[last][1])
    parts = [jnp.dot(hid[rs, :], wd_ref[cs, :], preferred_element_type=F32) for rs in halves]
    for rs, part, xin in zip(halves, parts, xh):
        total = part if acc is None else acc[rs, :] + part
        x2 = _layer_norm(alpha * xin + total, g_ref[...], b_ref[...])
        gate = _sigmoid(jnp.dot(x2.astype(BF16), wgate_ref[...], preferred_element_type=F32)
                        + bgate_ref[...])
        emb = jnp.dot(p_ref[rs, :].astype(BF16), wproj_ref[...], preferred_element_type=F32)
        o_ref[rs, :] = x2 + gate * emb


def _ffn_chunks(f):
    tile = 2 * LANES
    if f % tile:
        return ((0, f),)
    step = 4 * tile
    return tuple((c0, min(step, f - c0)) for c0 in range(0, f, step))


def _ffn(y, wo, xf, g1, b1, wu, wg, cw, cb, wd, g, b, pf, wproj, wgate, bgate, alpha, seq):
    m, d = xf.shape
    f = wu.shape[1]
    pdim = pf.shape[1]
    kdim = y.shape[1]
    tm = min(ROW_TILE, seq)
    chunks = _ffn_chunks(f)
    wmax = max(c[1] for c in chunks)
    row = lambda i: (i, 0)
    return pl.pallas_call(
        functools.partial(_ffn_kernel, chunks=chunks, steps_per_batch=seq // tm, alpha=alpha),
        grid=(m // tm,),
        in_specs=[pl.BlockSpec((tm, kdim), row), _resident(wo.shape), pl.BlockSpec((tm, d), row),
                  _resident(g1.shape), _resident(b1.shape), _resident(wu.shape), _resident(wg.shape),
                  _resident(cw.shape), _resident(cb.shape), _resident(wd.shape),
                  _resident(g.shape), _resident(b.shape), pl.BlockSpec((tm, pdim), row),
                  _resident(wproj.shape), _resident(wgate.shape), _resident(bgate.shape)],
        out_specs=pl.BlockSpec((tm, d), row),
        out_shape=jax.ShapeDtypeStruct((m, d), F32),
        scratch_shapes=[pltpu.VMEM((tm + SUBLANES, wmax), F32), pltpu.VMEM((tm + SUBLANES, wmax), F32),
                        pltpu.VMEM((SUBLANES, f), F32)],
        compiler_params=_params("arbitrary"),
        name="conv_ffn",
    )(y, wo, xf, g1, b1, wu, wg, cw, cb, wd, g, b, pf, wproj, wgate, bgate)


def _ssd_proj_kernel(x_ref, wz_ref, wxbc_ref, wdt_ref, cw_ref, cb_ref, dtb_ref,
                     z_ref, xs_ref, bt_ref, c_ref, dt_ref, dtt_ref, buf0_ref, buf1_ref, carry_ref,
                     *, steps_per_batch, col_tile):
    tm = x_ref.shape[0]
    halo = SUBLANES
    kconv = cw_ref.shape[0]
    bufs = (buf0_ref, buf1_ref)

    @pl.when(pl.program_id(0) % steps_per_batch == 0)
    def _():
        carry_ref[...] = jnp.zeros_like(carry_ref)

    xb = x_ref[...].astype(BF16)
    d_inner = xs_ref.shape[1]
    gn = c_ref.shape[1]
    nconv = wxbc_ref.shape[1] // col_tile
    z_chunks = list(range(0, wz_ref.shape[1], col_tile))

    def project(i):
        cs = slice(i * col_tile, (i + 1) * col_tile)
        buf = bufs[i % 2]
        r = jnp.dot(xb, wxbc_ref[:, cs], preferred_element_type=F32)
        buf[0:halo, :] = carry_ref[:, cs]
        buf[halo:halo + tm, :] = r
        carry_ref[:, cs] = r[tm - halo:tm, :]

    def project_z():
        if z_chunks:
            zs = slice(z_chunks[0], z_chunks.pop(0) + col_tile)
            z_ref[:, zs] = jnp.dot(xb, wz_ref[:, zs], preferred_element_type=F32).astype(BF16)

    def conv_act(i):
        c0 = i * col_tile
        cs = slice(c0, c0 + col_tile)
        buf = bufs[i % 2]
        conv = cb_ref[:, cs] + cw_ref[kconv - 1:kconv, cs] * buf[halo:halo + tm, :]
        for k in range(kconv - 1):
            off = halo - (kconv - 1) + k
            conv = conv + cw_ref[k:k + 1, cs] * buf[off:off + tm, :]
        act = conv * _sigmoid(conv)
        if c0 < d_inner:
            xs_ref[:, cs] = act.astype(BF16)
        elif c0 < d_inner + gn:
            bt_ref[0, c0 - d_inner:c0 - d_inner + col_tile, :] = act.T.astype(BF16)
        else:
            c_ref[:, c0 - d_inner - gn:c0 - d_inner - gn + col_tile] = act.astype(BF16)

    project(0)
    for i in range(nconv):
        if i + 1 < nconv:
            project(i + 1)
        else:
            project_z()
        conv_act(i)
    while z_chunks:
        project_z()
    dt = _softplus(jnp.dot(xb, wdt_ref[...], preferred_element_type=F32) + dtb_ref[...])
    dt_ref[...] = dt
    dtt_ref[0] = dt.T[0:dtt_ref.shape[1], :]


def _ssd_proj(xf, wz, wxbc, wdt, cw, cb, dtb, bsz, seq, gn, nheads):
    m, d = xf.shape
    d_inner = wz.shape[1]
    tm = min(ROW_TILE, seq)
    nblk = seq // tm
    col_tile = min(1024, gn)
    row = lambda i: (i, 0)
    tposed = lambda i: (i // nblk, 0, i % nblk)
    out_shape = [jax.ShapeDtypeStruct((m, d_inner), BF16), jax.ShapeDtypeStruct((m, d_inner), BF16),
                 jax.ShapeDtypeStruct((bsz, gn, seq), BF16), jax.ShapeDtypeStruct((m, gn), BF16),
                 jax.ShapeDtypeStruct((m, LANES), F32),
                 jax.ShapeDtypeStruct((bsz, nheads, seq), F32)]
    return pl.pallas_call(
        functools.partial(_ssd_proj_kernel, steps_per_batch=seq // tm, col_tile=col_tile),
        grid=(m // tm,),
        in_specs=[pl.BlockSpec((tm, d), row), _resident(wz.shape), _resident(wxbc.shape),
                  _resident(wdt.shape), _resident(cw.shape), _resident(cb.shape),
                  _resident(dtb.shape)],
        out_specs=[pl.BlockSpec((tm, d_inner), row), pl.BlockSpec((tm, d_inner), row),
                   pl.BlockSpec((1, gn, tm), tposed), pl.BlockSpec((tm, gn), row),
                   pl.BlockSpec((tm, LANES), row), pl.BlockSpec((1, nheads, tm), tposed)],
        out_shape=out_shape,
        scratch_shapes=[pltpu.VMEM((tm + SUBLANES, col_tile), F32),
                        pltpu.VMEM((tm + SUBLANES, col_tile), F32),
                        pltpu.VMEM((SUBLANES, wxbc.shape[1]), F32)],
        compiler_params=_params("arbitrary"),
        name="ssd_proj",
    )(xf, wz, wxbc, wdt, cw, cb, dtb)


def _expand_heads(cols, lane_head):
    out = cols[0]
    for r in range(1, len(cols)):
        out = jnp.where(lane_head >= r, cols[r], out)
    return out


def _ssd_scan_kernel(xs_ref, bt_ref, c_ref, z_ref, dt_ref, dtt_ref, arow_ref, acol_ref,
                     dskip_ref, nw_ref, ltri_ref, utri_ref, y_ref, state_ref, *, heads_per_group):
    q = xs_ref.shape[1]
    n = SSM_STATE
    hp_ = SSM_HEAD_DIM
    gw = heads_per_group * hp_
    ngroups = xs_ref.shape[2] // gw

    @pl.when(pl.program_id(1) == 0)
    def _():
        state_ref[...] = jnp.zeros_like(state_ref)

    dt_col = dt_ref[0]
    dt_row = dtt_ref[0]
    nheads = dt_row.shape[0]
    a_col = dt_col * (arow_ref[...] * LOG2E)
    pc = jnp.dot(ltri_ref[...], jnp.concatenate(_split3(a_col), axis=1),
                 preferred_element_type=F32)
    acum_col = pc[:, 0:LANES] + pc[:, LANES:2 * LANES] + pc[:, 2 * LANES:3 * LANES]
    a_row = dt_row * (acol_ref[...] * LOG2E)
    pr = jnp.dot(jnp.concatenate(_split3(a_row), axis=0), utri_ref[...],
                 preferred_element_type=F32)
    acum_row = pr[0:nheads] + pr[nheads:2 * nheads] + pr[2 * nheads:3 * nheads]
    dec_row = jnp.exp2(acum_col[q - 1:q, :])
    w_row = jnp.exp2(acum_row[:, q - 1:q] - acum_row) * dt_row
    src_row = acum_row - jnp.log2(dt_row)

    nsub = q // LANES
    tri = (lax.broadcasted_iota(jnp.int32, (LANES, LANES), 0)
           >= lax.broadcasted_iota(jnp.int32, (LANES, LANES), 1))
    lane_head = lax.broadcasted_iota(jnp.int32, (1, gw), 1) // hp_
    low_half = lax.broadcasted_iota(jnp.int32, (1, LANES), 1) < hp_

    for g in range(ngroups):
        gs = slice(g * gw, (g + 1) * gw)
        ns = slice(g * n, (g + 1) * n)
        cg = c_ref[0, :, ns]
        btg = bt_ref[0, ns, :]
        btg_f = btg.astype(F32)
        xg_b = xs_ref[0, :, gs]
        heads = range(g * heads_per_group, (g + 1) * heads_per_group)
        cb = jnp.dot(cg, btg, preferred_element_type=F32)
        state = state_ref[g]
        y_inter = jnp.dot(cg, state.astype(BF16), preferred_element_type=F32)
        e_cols, y_intra, st_new = [], None, None
        for r, h in enumerate(heads):
            acol_b = jnp.broadcast_to(acum_col[:, h:h + 1], (q, LANES))
            e_cols.append(jnp.exp2(acol_b))
            rows = []
            for ti in range(nsub):
                tsl = slice(ti * LANES, (ti + 1) * LANES)
                tiles = []
                for si in range(nsub):
                    ssl = slice(si * LANES, (si + 1) * LANES)
                    if si > ti:
                        tiles.append(jnp.zeros((LANES, LANES), BF16))
                        continue
                    seg = acol_b[tsl] - src_row[h:h + 1, ssl]
                    if si == ti:
                        seg = jnp.where(tri, seg, -jnp.inf)
                    tiles.append((cb[tsl, ssl] * jnp.exp2(seg)).astype(BF16))
                rows.append(jnp.concatenate(tiles, axis=1))
            mh = jnp.concatenate(rows, axis=0)
            xh = jnp.where(lane_head == r, xg_b, jnp.zeros_like(xg_b))
            part = jnp.dot(mh, xh, preferred_element_type=F32)
            y_intra = part if y_intra is None else y_intra + part
            btw = (btg_f * w_row[h:h + 1, :]).astype(BF16)
            part = jnp.dot(btw, xh, preferred_element_type=F32)
            st_new = part if st_new is None else st_new + part
        e_x = jnp.concatenate([jnp.where(low_half, e_cols[2 * i], e_cols[2 * i + 1])
                               for i in range(gw // LANES)], axis=1)
        y_g = y_inter * e_x + y_intra
        d_x = _expand_heads([dec_row[:, h:h + 1] for h in heads], lane_head)
        state_ref[g] = state * d_x + st_new
        xg = xg_b.astype(F32)
        zg = z_ref[0, :, gs].astype(F32)
        yv = (y_g + dskip_ref[:, gs] * xg) * (zg * _sigmoid(zg))
        ms = jnp.mean(yv * yv, axis=-1, keepdims=True)
        y_ref[0, :, gs] = (yv * lax.rsqrt(ms + RMS_EPS) * nw_ref[:, gs]).astype(BF16)


def _ssd_scan(xs, bmt, cm, z, dt, dtt, arow, acol, dskip, nw, ltri, utri, q):
    b, s, d_inner = xs.shape
    gn = cm.shape[2]
    nheads = dtt.shape[1]
    ngroups = gn // SSM_STATE
    heads_per_group = nheads // ngroups
    gw = heads_per_group * SSM_HEAD_DIM
    blk = lambda w: pl.BlockSpec((1, q, w), lambda bi, ci: (bi, ci, 0))
    return pl.pallas_call(
        functools.partial(_ssd_scan_kernel, heads_per_group=heads_per_group),
        grid=(b, s // q),
        in_specs=[blk(d_inner), pl.BlockSpec((1, gn, q), lambda bi, ci: (bi, 0, ci)),
                  blk(gn), blk(d_inner), blk(LANES),
                  pl.BlockSpec((1, nheads, q), lambda bi, ci: (bi, 0, ci)),
                  _resident(arow.shape), _resident(acol.shape), _resident(dskip.shape),
                  _resident(nw.shape), _resident(ltri.shape), _resident(utri.shape)],
        out_specs=blk(d_inner),
        out_shape=jax.ShapeDtypeStruct((b, s, d_inner), BF16),
        scratch_shapes=[pltpu.VMEM((ngroups, SSM_STATE, gw), F32)],
        compiler_params=_params("arbitrary", "arbitrary"),
        name="ssd_scan",
    )(xs, bmt, cm, z, dt, dtt, arow, acol, dskip, nw, ltri, utri)


def _pad_cols(a, width):
    return jnp.pad(a, ((0, 0), (0, width - a.shape[1])))


def _fox_mixer(xf, bsz, seq, w_in, b_f):
    m, d = xf.shape
    nh = d // ATTN_HEAD_DIM
    scale = LOG2E / math.sqrt(ATTN_HEAD_DIM)
    wqt = (w_in[:, :d] * scale).T.astype(BF16)
    wk = w_in[:, d:2 * d].astype(BF16)
    wvt = w_in[:, 2 * d:3 * d].T.astype(BF16)
    wft = w_in[:, 3 * d:].T.astype(BF16)
    utri = jnp.triu(jnp.ones((CUMSUM_CHUNK, CUMSUM_CHUNK), F32)).astype(BF16)
    blk = min(ATTN_BLOCK, seq)
    k, qt, vt, qta, ka = _fox_proj(xf, wk, wqt, wvt, wft, b_f.reshape(nh, 1), utri, bsz, seq, blk)
    o = _fox_attn(qt, qta, k.reshape(bsz, seq, d), ka.reshape(bsz, seq, -1), vt, blk)
    return o.reshape(m, d)


def _ssd_mixer(xf, bsz, seq, w_in, conv_w, conv_b, dt_bias, a_log, d_skip, norm_w):
    m, d = xf.shape
    nheads = dt_bias.shape[0]
    d_inner = nheads * SSM_HEAD_DIM
    gn = SSM_GROUPS * SSM_STATE
    wz = w_in[:, :d_inner].astype(BF16)
    wxbc = w_in[:, d_inner:2 * d_inner + 2 * gn].astype(BF16)
    wdt = _pad_cols(w_in[:, 2 * d_inner + 2 * gn:], LANES).astype(BF16)
    dtb = _pad_cols(dt_bias.reshape(1, nheads), LANES)
    z, xs, bmt, cm, dt, dtt = _ssd_proj(xf, wz, wxbc, wdt, conv_w, conv_b.reshape(1, -1), dtb,
                                        bsz, seq, gn, nheads)
    a = -jnp.exp(a_log.astype(F32))
    arow = _pad_cols(a.reshape(1, nheads), LANES)
    acol = a.reshape(nheads, 1)
    dskip = jnp.repeat(d_skip, SSM_HEAD_DIM).reshape(1, d_inner)
    q = min(SSD_CHUNK, seq)
    ltri = jnp.tril(jnp.ones((q, q), F32)).astype(BF16)
    utri = jnp.triu(jnp.ones((q, q), F32)).astype(BF16)
    y = _ssd_scan(xs.reshape(bsz, seq, d_inner), bmt, cm.reshape(bsz, seq, gn),
                  z.reshape(bsz, seq, d_inner), dt.reshape(bsz, seq, LANES), dtt, arow, acol,
                  dskip, norm_w.reshape(1, d_inner), ltri, utri, q)
    return y.reshape(m, d_inner)


def kernel(x, p, attn_w_in, attn_b_f, attn_w_out, ssm_w_in, ssm_conv_w, ssm_conv_b, ssm_dt_bias, ssm_A_log, ssm_D, ssm_norm_w, ssm_w_out, ln_mix_g, ln_mix_b, ffn_w_up, ffn_conv_w, ffn_conv_b, ffn_w_down, ln_ffn_g, ln_ffn_b, ple_w_proj, ple_w_gate, ple_b_gate):
    bsz, seq, d = x.shape
    depth = p.shape[0]
    n_mixers = 2
    alpha = (2 * depth) ** 0.25
    xf = x.reshape(bsz * seq, d)
    for i in range(depth):
        j = i // n_mixers
        if i % n_mixers == 0:
            y, w_out = _fox_mixer(xf, bsz, seq, attn_w_in[j], attn_b_f[j]), attn_w_out[j]
        else:
            y = _ssd_mixer(xf, bsz, seq, ssm_w_in[j], ssm_conv_w[j], ssm_conv_b[j],
                           ssm_dt_bias[j], ssm_A_log[j], ssm_D[j], ssm_norm_w[j])
            w_out = ssm_w_out[j]
        f = ffn_conv_w.shape[-1]
        xf = _ffn(y, w_out.astype(BF16), xf, ln_mix_g[i].reshape(1, d), ln_mix_b[i].reshape(1, d),
                  ffn_w_up[i][:, :f].astype(BF16), ffn_w_up[i][:, f:].astype(BF16),
                  ffn_conv_w[i], ffn_conv_b[i].reshape(1, f), ffn_w_down[i].astype(BF16),
                  ln_ffn_g[i].reshape(1, d), ln_ffn_b[i].reshape(1, d),
                  p[i].reshape(bsz * seq, -1), ple_w_proj[i].astype(BF16),
                  ple_w_gate[i].astype(BF16), ple_b_gate[i].reshape(1, d), alpha, seq)
    return xf.reshape(bsz, seq, d)
```

```python
import functools
import math

import jax
import jax.numpy as jnp
import numpy as np
from jax import lax
from jax.experimental import pallas as pl
from jax.experimental.pallas import tpu as pltpu

F32 = jnp.float32
BF16 = jnp.bfloat16

ATTN_HEAD_DIM = 64
SSM_HEAD_DIM = 64
SSM_GROUPS = 8
SSM_STATE = 128
LN_EPS = 1e-5
RMS_EPS = 1e-5

LANES = 128
SUBLANES = 8
VMEM_LIMIT_BYTES = 56 * 1024 * 1024

ROW_TILE = 512
ATTN_BLOCK = 512
ATTN_UNROLL = 4
SSD_CHUNK = 256
CUMSUM_CHUNK = 256
MASK_VALUE = -1e30
LOG2E = 1.4426950408889634
AUG_ROWS = 16
ONES_ROWS = 16


def _params(*sem):
    return pltpu.CompilerParams(dimension_semantics=sem, vmem_limit_bytes=VMEM_LIMIT_BYTES)


def _resident(shape):
    nd = len(shape)
    return pl.BlockSpec(shape, lambda *_: (0,) * nd, pipeline_mode=pl.Buffered(1))


def _split3(v):
    hi = v.astype(BF16)
    r1 = v - hi.astype(F32)
    mid = r1.astype(BF16)
    lo = (r1 - mid.astype(F32)).astype(BF16)
    return hi, mid, lo


def _softplus(v):
    return jnp.maximum(v, 0.0) + jnp.log1p(jnp.exp(-jnp.abs(v)))


def _sigmoid(v):
    return 1.0 / (1.0 + jnp.exp2(v * (-LOG2E)))


def _layer_norm(h, g, b):
    mu = jnp.mean(h, axis=-1, keepdims=True)
    d = h - mu
    var = jnp.mean(d * d, axis=-1, keepdims=True)
    return d * lax.rsqrt(var + LN_EPS) * g + b


_NT = (((1,), (1,)), ((), ()))


def _bias_placement(nh):
    place_q = np.zeros((nh * AUG_ROWS, LANES), np.float32)
    ones_q = np.zeros((nh * AUG_ROWS, 1), np.float32)
    place_k = np.zeros((LANES, (nh // 2) * LANES), np.float32)
    ones_k = np.zeros((1, (nh // 2) * LANES), np.float32)
    for h in range(nh):
        off = 6 * (h % 2)
        for p in range(3):
            ones_q[h * AUG_ROWS + off + p, 0] = 1.0
            place_q[h * AUG_ROWS + off + 3 + p, p * nh + h] = 1.0
            place_k[p * nh + h, (h // 2) * LANES + off + p] = -1.0
            ones_k[0, (h // 2) * LANES + off + 3 + p] = 1.0
    return (jnp.asarray(place_q, BF16), jnp.asarray(ones_q), jnp.asarray(place_k, BF16),
            jnp.asarray(ones_k))


def _fox_proj_kernel(x_ref, wk_ref, wqt_ref, wvt_ref, wft_ref, bf_ref, ut_ref, pq_ref, oq_ref,
                     pk_ref, ok_ref, k_ref, qt_ref, vt_ref, qta_ref, ka_ref, carry_ref, *,
                     steps_per_batch):
    tm, d = x_ref.shape

    @pl.when(pl.program_id(0) % steps_per_batch == 0)
    def _():
        carry_ref[...] = jnp.zeros_like(carry_ref)

    xb = x_ref[...].astype(BF16)
    fl = lax.dot_general(wft_ref[...], xb, _NT, preferred_element_type=F32) + bf_ref[...]
    k_ref[...] = jnp.dot(xb, wk_ref[...], preferred_element_type=F32).astype(BF16)
    logf = -_softplus(-fl) * LOG2E
    nh = logf.shape[0]
    parts = jnp.concatenate(_split3(logf), axis=0)
    sums = [jnp.dot(parts[:, j * CUMSUM_CHUNK:(j + 1) * CUMSUM_CHUNK], ut_ref[...],
                    preferred_element_type=F32) for j in range(tm // CUMSUM_CHUNK)]
    qt_ref[0] = lax.dot_general(wqt_ref[...], xb, _NT, preferred_element_type=F32).astype(BF16)
    carry = carry_ref[...]
    zrows = jnp.zeros((LANES - 3 * nh, CUMSUM_CHUNK), F32)
    stacks = []
    for pj in sums:
        cj = (pj[0:nh] + pj[nh:2 * nh] + pj[2 * nh:3 * nh]) + carry
        stacks.append(jnp.concatenate([cp.astype(F32) for cp in _split3(cj)] + [zrows], axis=0))
        carry = cj[:, CUMSUM_CHUNK - 1:CUMSUM_CHUNK]
    carry_ref[...] = carry
    stack = jnp.concatenate(stacks, axis=1)
    qta_ref[0] = (jnp.dot(pq_ref[...], stack.astype(BF16), preferred_element_type=F32)
                  + oq_ref[...]).astype(BF16)
    ka_ref[...] = (jnp.dot(stack.T.astype(BF16), pk_ref[...], preferred_element_type=F32)
                   + ok_ref[...]).astype(BF16)
    vt_ref[0, 0] = lax.dot_general(wvt_ref[...], xb, _NT, preferred_element_type=F32).astype(BF16)


def _fox_proj(xf, wk, wqt, wvt, wft, bf_col, utri, bsz, seq, tm):
    m, d = xf.shape
    nh = wft.shape[0]
    nblk = seq // tm
    row = lambda i: (i, 0)
    place = _bias_placement(nh)
    out_shape = [jax.ShapeDtypeStruct((m, d), BF16), jax.ShapeDtypeStruct((bsz, d, seq), BF16),
                 jax.ShapeDtypeStruct((bsz, nblk, d, tm), BF16),
                 jax.ShapeDtypeStruct((bsz, nh * AUG_ROWS, seq), BF16),
                 jax.ShapeDtypeStruct((m, (nh // 2) * LANES), BF16)]
    return pl.pallas_call(
        functools.partial(_fox_proj_kernel, steps_per_batch=nblk),
        grid=(m // tm,),
        in_specs=[pl.BlockSpec((tm, d), row), _resident(wk.shape), _resident(wqt.shape),
                  _resident(wvt.shape), _resident(wft.shape), _resident(bf_col.shape),
                  _resident(utri.shape)] + [_resident(a.shape) for a in place],
        out_specs=[pl.BlockSpec((tm, d), row),
                   pl.BlockSpec((1, d, tm), lambda i: (i // nblk, 0, i % nblk)),
                   pl.BlockSpec((1, 1, d, tm), lambda i: (i // nblk, i % nblk, 0, 0)),
                   pl.BlockSpec((1, nh * AUG_ROWS, tm), lambda i: (i // nblk, 0, i % nblk)),
                   pl.BlockSpec((tm, (nh // 2) * LANES), row)],
        out_shape=out_shape,
        scratch_shapes=[pltpu.VMEM((nh, 1), F32)],
        compiler_params=_params("arbitrary"),
        name="fox_proj",
    )(xf, wk, wqt, wvt, wft, bf_col, utri, *place)


def _fox_attn_kernel(qt_ref, qta_ref, k_ref, ka_ref, vt_ref, o_ref, m_scr, acc_scr,
                     sa_scr, sb_scr, ma_scr, mb_scr, *, blk):
    qi = pl.program_id(2)
    hd = ATTN_HEAD_DIM
    qt2 = qt_ref[0]
    zhead = jnp.zeros((hd, blk), BF16)
    zpad = jnp.zeros((LANES - AUG_ROWS, blk), BF16)
    qts = [jnp.concatenate([qt2[0:hd], zhead, qta_ref[0, 0:AUG_ROWS], zpad], axis=0),
           jnp.concatenate([zhead, qt2[hd:2 * hd], qta_ref[0, AUG_ROWS:2 * AUG_ROWS], zpad], axis=0)]
    ones_rows = jnp.ones((ONES_ROWS, blk), BF16)
    key_id = lax.broadcasted_iota(jnp.int32, (blk, blk), 0)
    qry_id = lax.broadcasted_iota(jnp.int32, (blk, blk), 1)
    m_scr[...] = jnp.full_like(m_scr, MASK_VALUE)
    acc_scr[...] = jnp.zeros_like(acc_scr)

    def keys(j):
        start = pl.multiple_of(j * blk, blk)
        return jnp.concatenate([k_ref[0, pl.ds(start, blk), :], ka_ref[0, pl.ds(start, blk), :]],
                               axis=1)

    def values(j):
        vt2 = vt_ref[0, j]
        return [jnp.concatenate([vt2[h * hd:(h + 1) * hd], ones_rows], axis=0) for h in range(2)]

    half = blk // 2

    def produce(kk, h, s_buf, m_buf, diagonal=False):
        if diagonal:
            s_buf[h, 0:half, :] = jnp.dot(kk[0:half], qts[h], preferred_element_type=F32)
            s_buf[h, half:blk, half:blk] = jnp.dot(kk[half:blk], qts[h][:, half:blk],
                                                  preferred_element_type=F32)
            return
        s = jnp.dot(kk, qts[h], preferred_element_type=F32)
        s_buf[h] = s
        m_buf[h] = jnp.max(s, axis=0, keepdims=True)

    def consume(vt, h, s_buf, m_buf):
        s = s_buf[h]
        m_old = m_scr[h]
        m_new = jnp.maximum(m_old, m_buf[h])
        p = jnp.exp2(s - m_new).astype(BF16)
        alpha = jnp.exp2(m_old - m_new)
        acc_scr[h] = alpha * acc_scr[h] + jnp.dot(vt[h], p, preferred_element_type=F32)
        m_scr[h] = m_new

    def consume_diagonal(vt, h, s_buf):
        tri = (lax.broadcasted_iota(jnp.int32, (half, half), 0)
               <= lax.broadcasted_iota(jnp.int32, (half, half), 1))
        early, late = slice(0, half), slice(half, blk)
        for qs in (early, late):
            s_tri = jnp.where(tri, s_buf[h, qs, qs], MASK_VALUE)
            m_blk = jnp.max(s_tri, axis=0, keepdims=True)
            if qs is late:
                s_full = s_buf[h, early, late]
                m_blk = jnp.maximum(m_blk, jnp.max(s_full, axis=0, keepdims=True))
            m_old = m_scr[h, :, qs]
            m_new = jnp.maximum(m_old, m_blk)
            p = jnp.exp2(s_tri - m_new).astype(BF16)
            v_t = vt[h][:, qs]
            if qs is late:
                p = jnp.concatenate([jnp.exp2(s_full - m_new).astype(BF16), p], axis=0)
                v_t = vt[h]
            acc_scr[h, :, qs] = (jnp.exp2(m_old - m_new) * acc_scr[h, :, qs]
                                 + jnp.dot(v_t, p, preferred_element_type=F32))
            m_scr[h, :, qs] = m_new

    kk0 = keys(0)
    for h in range(2):
        produce(kk0, h, sa_scr, ma_scr)

    bufs = ((sa_scr, ma_scr), (sb_scr, mb_scr))

    def advance(j0, nsteps, ends_on_diagonal=False):
        for i in range(nsteps):
            kk, vt = keys(j0 + i + 1), values(j0 + i)
            for h in range(2):
                produce(kk, h, *bufs[(i + 1) % 2], diagonal=ends_on_diagonal and i == nsteps - 1)
                consume(vt, h, *bufs[i % 2])

    def body(t, carry):
        advance(ATTN_UNROLL * t, ATTN_UNROLL)
        return carry

    lax.fori_loop(0, qi // ATTN_UNROLL, body, 0)

    for rem in range(ATTN_UNROLL):
        @pl.when(qi % ATTN_UNROLL == rem)
        def _(rem=rem):
            advance(qi - rem, rem, ends_on_diagonal=True)
            vt = values(qi)
            for h in range(2):
                consume_diagonal(vt, h, bufs[rem % 2][0])

    outs = []
    for h in range(2):
        acc = acc_scr[h]
        outs.append(acc[0:hd] * (1.0 / acc[hd:hd + 1]))
    o_ref[0] = jnp.concatenate(outs, axis=0).T.astype(BF16)


def _fox_attn(qt, qta, k, ka, vt, blk):
    b, d, s = qt.shape
    nhp = d // LANES
    nblk = s // blk
    return pl.pallas_call(
        functools.partial(_fox_attn_kernel, blk=blk),
        grid=(b, nhp, nblk),
        in_specs=[
            pl.BlockSpec((1, LANES, blk), lambda bi, hp, qi: (bi, hp, qi)),
            pl.BlockSpec((1, 2 * AUG_ROWS, blk), lambda bi, hp, qi: (bi, hp, qi)),
            pl.BlockSpec((1, s, LANES), lambda bi, hp, qi: (bi, 0, hp)),
            pl.BlockSpec((1, s, LANES), lambda bi, hp, qi: (bi, 0, hp)),
            pl.BlockSpec((1, nblk, LANES, blk), lambda bi, hp, qi: (bi, 0, hp, 0)),
        ],
        out_specs=pl.BlockSpec((1, blk, LANES), lambda bi, hp, qi: (bi, qi, hp)),
        out_shape=jax.ShapeDtypeStruct((b, s, d), BF16),
        scratch_shapes=[pltpu.VMEM((2, 1, blk), F32),
                        pltpu.VMEM((2, ATTN_HEAD_DIM + ONES_ROWS, blk), F32),
                        pltpu.VMEM((2, blk, blk), F32), pltpu.VMEM((2, blk, blk), F32),
                        pltpu.VMEM((2, 1, blk), F32), pltpu.VMEM((2, 1, blk), F32)],
        compiler_params=_params("arbitrary", "arbitrary", "arbitrary"),
        name="fox_attn",
    )(qt, qta, k, ka, vt)


def _ffn_kernel(y_ref, wo_ref, x_ref, g1_ref, b1_ref, wu_ref, wg_ref, cw_ref, cb_ref, wd_ref,
                g_ref, b_ref, p_ref, wproj_ref, wgate_ref, bgate_ref, o_ref, gbuf0_ref, gbuf1_ref,
                carry_ref, *, chunks, steps_per_batch, alpha):
    tm = x_ref.shape[0]
    halo = SUBLANES
    kconv = cw_ref.shape[0]
    gbufs = (gbuf0_ref, gbuf1_ref)
    halves = (slice(0, tm // 2), slice(tm // 2, tm))
    last = len(chunks) - 1

    @pl.when(pl.program_id(0) % steps_per_batch == 0)
    def _():
        carry_ref[...] = jnp.zeros_like(carry_ref)

    mixes = [jnp.dot(y_ref[rs, :], wo_ref[...], preferred_element_type=F32) for rs in halves]
    xh = [_layer_norm(alpha * x_ref[rs, :] + mix, g1_ref[...], b1_ref[...])
          for rs, mix in zip(halves, mixes)]
    xbh = [v.astype(BF16) for v in xh]

    def stage(i, g):
        c0, cw = chunks[i]
        cs = slice(c0, c0 + cw)
        buf = gbufs[i % 2]
        buf[0:halo, 0:cw] = carry_ref[:, cs]
        buf[halo:halo + tm, 0:cw] = g
        carry_ref[:, cs] = g[tm - halo:tm, :]

    def up(i, lhs):
        c0, cw = chunks[i]
        cs = slice(c0, c0 + cw)
        return (jnp.dot(lhs, wu_ref[:, cs], preferred_element_type=F32),
                jnp.dot(lhs, wg_ref[:, cs], preferred_element_type=F32))

    def gated(i, u):
        c0, cw = chunks[i]
        cs = slice(c0, c0 + cw)
        buf = gbufs[i % 2]
        conv = cb_ref[:, cs] + cw_ref[kconv - 1:kconv, cs] * buf[halo:halo + tm, 0:cw]
        for k in range(kconv - 1):
            off = halo - (kconv - 1) + k
            conv = conv + cw_ref[k:k + 1, cs] * buf[off:off + tm, 0:cw]
        gelu = 0.5 * conv * (1.0 + lax.erf(conv * (1.0 / math.sqrt(2.0))))
        return (gelu * u).astype(BF16)

    first = [up(0, xbh[0]), up(0, xbh[1])]
    u_cur = jnp.concatenate([first[0][0], first[1][0]], axis=0)
    stage(0, jnp.concatenate([first[0][1], first[1][1]], axis=0))
    xb = jnp.concatenate(xbh, axis=0)

    acc = None
    for i in range(last):
        u_next, g_next = up(i + 1, xb)
        stage(i + 1, g_next)
        cs = slice(chunks[i][0], chunks[i][0] + chunks[i][1])
        part = jnp.dot(gated(i, u_cur), wd_ref[cs, :], preferred_element_type=F32)
        acc = part if acc is None else acc + part
        u_cur = u_next

    hid = gated(last, u_cur)
    cs = slice(chunks[last][0], chunks[last][0] + chunks[last][1])
    parts = [jnp.dot(hid[rs, :], wd_ref[cs, :], preferred_element_type=F32) for rs in halves]
    for rs, part, xin in zip(halves, parts, xh):
        total = part if acc is None else acc[rs, :] + part
        x2 = _layer_norm(alpha * xin + total, g_ref[...], b_ref[...])
        gate = _sigmoid(jnp.dot(x2.astype(BF16), wgate_ref[...], preferred_element_type=F32)
                        + bgate_ref[...])
        emb = jnp.dot(p_ref[rs, :].astype(BF16), wproj_ref[...], preferred_element_type=F32)
        o_ref[rs, :] = x2 + gate * emb


def _ffn_chunks(f):
    tile = 2 * LANES
    if f % tile:
        return ((0, f),)
    step = 4 * tile
    return tuple((c0, min(step, f - c0)) for c0 in range(0, f, step))


def _ffn(y, wo, xf, g1, b1, wu, wg, cw, cb, wd, g, b, pf, wproj, wgate, bgate, alpha, seq):
    m, d = xf.shape
    f = wu.shape[1]
    pdim = pf.shape[1]
    kdim = y.shape[1]
    tm = min(ROW_TILE, seq)
    chunks = _ffn_chunks(f)
    wmax = max(c[1] for c in chunks)
    row = lambda i: (i, 0)
    return pl.pallas_call(
        functools.partial(_ffn_kernel, chunks=chunks, steps_per_batch=seq // tm, alpha=alpha),
        grid=(m // tm,),
        in_specs=[pl.BlockSpec((tm, kdim), row), _resident(wo.shape), pl.BlockSpec((tm, d), row),
                  _resident(g1.shape), _resident(b1.shape), _resident(wu.shape), _resident(wg.shape),
                  _resident(cw.shape), _resident(cb.shape), _resident(wd.shape),
                  _resident(g.shape), _resident(b.shape), pl.BlockSpec((tm, pdim), row),
                  _resident(wproj.shape), _resident(wgate.shape), _resident(bgate.shape)],
        out_specs=pl.BlockSpec((tm, d), row),
        out_shape=jax.ShapeDtypeStruct((m, d), F32),
        scratch_shapes=[pltpu.VMEM((tm + SUBLANES, wmax), F32), pltpu.VMEM((tm + SUBLANES, wmax), F32),
                        pltpu.VMEM((SUBLANES, f), F32)],
        compiler_params=_params("arbitrary"),
        name="conv_ffn",
    )(y, wo, xf, g1, b1, wu, wg, cw, cb, wd, g, b, pf, wproj, wgate, bgate)


def _ssd_proj_kernel(x_ref, wz_ref, wxbc_ref, wdt_ref, cw_ref, cb_ref, dtb_ref,
                     z_ref, xs_ref, bt_ref, c_ref, dt_ref, dtt_ref, buf0_ref, buf1_ref, carry_ref,
                     *, steps_per_batch, col_tile):
    tm = x_ref.shape[0]
    halo = SUBLANES
    kconv = cw_ref.shape[0]
    bufs = (buf0_ref, buf1_ref)

    @pl.when(pl.program_id(0) % steps_per_batch == 0)
    def _():
        carry_ref[...] = jnp.zeros_like(carry_ref)

    xb = x_ref[...].astype(BF16)
    d_inner = xs_ref.shape[1]
    gn = c_ref.shape[1]
    nconv = wxbc_ref.shape[1] // col_tile
    z_chunks = list(range(0, wz_ref.shape[1], col_tile))

    def project(i):
        cs = slice(i * col_tile, (i + 1) * col_tile)
        buf = bufs[i % 2]
        r = jnp.dot(xb, wxbc_ref[:, cs], preferred_element_type=F32)
        buf[0:halo, :] = carry_ref[:, cs]
        buf[halo:halo + tm, :] = r
        carry_ref[:, cs] = r[tm - halo:tm, :]

    def project_z():
        if z_chunks:
            zs = slice(z_chunks[0], z_chunks.pop(0) + col_tile)
            z_ref[:, zs] = jnp.dot(xb, wz_ref[:, zs], preferred_element_type=F32).astype(BF16)

    def conv_act(i):
        c0 = i * col_tile
        cs = slice(c0, c0 + col_tile)
        buf = bufs[i % 2]
        conv = cb_ref[:, cs] + cw_ref[kconv - 1:kconv, cs] * buf[halo:halo + tm, :]
        for k in range(kconv - 1):
            off = halo - (kconv - 1) + k
            conv = conv + cw_ref[k:k + 1, cs] * buf[off:off + tm, :]
        act = conv * _sigmoid(conv)
        if c0 < d_inner:
            xs_ref[:, cs] = act.astype(BF16)
        elif c0 < d_inner + gn:
            bt_ref[0, c0 - d_inner:c0 - d_inner + col_tile, :] = act.T.astype(BF16)
        else:
            c_ref[:, c0 - d_inner - gn:c0 - d_inner - gn + col_tile] = act.astype(BF16)

    project(0)
    for i in range(nconv):
        if i + 1 < nconv:
            project(i + 1)
        else:
            project_z()
        conv_act(i)
    while z_chunks:
        project_z()
    dt = _softplus(jnp.dot(xb, wdt_ref[...], preferred_element_type=F32) + dtb_ref[...])
    dt_ref[...] = dt
    dtt_ref[0] = dt.T[0:dtt_ref.shape[1], :]


def _ssd_proj(xf, wz, wxbc, wdt, cw, cb, dtb, bsz, seq, gn, nheads):
    m, d = xf.shape
    d_inner = wz.shape[1]
    tm = min(ROW_TILE, seq)
    nblk = seq // tm
    col_tile = min(1024, gn)
    row = lambda i: (i, 0)
    tposed = lambda i: (i // nblk, 0, i % nblk)
    out_shape = [jax.ShapeDtypeStruct((m, d_inner), BF16), jax.ShapeDtypeStruct((m, d_inner), BF16),
                 jax.ShapeDtypeStruct((bsz, gn, seq), BF16), jax.ShapeDtypeStruct((m, gn), BF16),
                 jax.ShapeDtypeStruct((m, LANES), F32),
                 jax.ShapeDtypeStruct((bsz, nheads, seq), F32)]
    return pl.pallas_call(
        functools.partial(_ssd_proj_kernel, steps_per_batch=seq // tm, col_tile=col_tile),
        grid=(m // tm,),
        in_specs=[pl.BlockSpec((tm, d), row), _resident(wz.shape), _resident(wxbc.shape),
                  _resident(wdt.shape), _resident(cw.shape), _resident(cb.shape),
                  _resident(dtb.shape)],
        out_specs=[pl.BlockSpec((tm, d_inner), row), pl.BlockSpec((tm, d_inner), row),
                   pl.BlockSpec((1, gn, tm), tposed), pl.BlockSpec((tm, gn), row),
                   pl.BlockSpec((tm, LANES), row), pl.BlockSpec((1, nheads, tm), tposed)],
        out_shape=out_shape,
        scratch_shapes=[pltpu.VMEM((tm + SUBLANES, col_tile), F32),
                        pltpu.VMEM((tm + SUBLANES, col_tile), F32),
                        pltpu.VMEM((SUBLANES, wxbc.shape[1]), F32)],
        compiler_params=_params("arbitrary"),
        name="ssd_proj",
    )(xf, wz, wxbc, wdt, cw, cb, dtb)


def _expand_heads(cols, lane_head):
    out = cols[0]
    for r in range(1, len(cols)):
        out = jnp.where(lane_head >= r, cols[r], out)
    return out


def _ssd_scan_kernel(xs_ref, bt_ref, c_ref, z_ref, dt_ref, dtt_ref, arow_ref, acol_ref,
                     dskip_ref, nw_ref, ltri_ref, utri_ref, y_ref, state_ref, *, heads_per_group):
    q = xs_ref.shape[1]
    n = SSM_STATE
    hp_ = SSM_HEAD_DIM
    gw = heads_per_group * hp_
    ngroups = xs_ref.shape[2] // gw

    @pl.when(pl.program_id(1) == 0)
    def _():
        state_ref[...] = jnp.zeros_like(state_ref)

    dt_col = dt_ref[0]
    dt_row = dtt_ref[0]
    nheads = dt_row.shape[0]
    a_col = dt_col * (arow_ref[...] * LOG2E)
    pc = jnp.dot(ltri_ref[...], jnp.concatenate(_split3(a_col), axis=1),
                 preferred_element_type=F32)
    acum_col = pc[:, 0:LANES] + pc[:, LANES:2 * LANES] + pc[:, 2 * LANES:3 * LANES]
    a_row = dt_row * (acol_ref[...] * LOG2E)
    pr = jnp.dot(jnp.concatenate(_split3(a_row), axis=0), utri_ref[...],
                 preferred_element_type=F32)
    acum_row = pr[0:nheads] + pr[nheads:2 * nheads] + pr[2 * nheads:3 * nheads]
    dec_row = jnp.exp2(acum_col[q - 1:q, :])
    w_row = jnp.exp2(acum_row[:, q - 1:q] - acum_row) * dt_row
    src_row = acum_row - jnp.log2(dt_row)

    nsub = q // LANES
    tri = (lax.broadcasted_iota(jnp.int32, (LANES, LANES), 0)
           >= lax.broadcasted_iota(jnp.int32, (LANES, LANES), 1))
    lane_head = lax.broadcasted_iota(jnp.int32, (1, gw), 1) // hp_
    low_half = lax.broadcasted_iota(jnp.int32, (1, LANES), 1) < hp_

    for g in range(ngroups):
        gs = slice(g * gw, (g + 1) * gw)
        ns = slice(g * n, (g + 1) * n)
        cg = c_ref[0, :, ns]
        btg = bt_ref[0, ns, :]
        btg_f = btg.astype(F32)
        xg_b = xs_ref[0, :, gs]
        heads = range(g * heads_per_group, (g + 1) * heads_per_group)
        cb = jnp.dot(cg, btg, preferred_element_type=F32)
        state = state_ref[g]
        y_inter = jnp.dot(cg, state.astype(BF16), preferred_element_type=F32)
        e_cols, y_intra, st_new = [], None, None
        for r, h in enumerate(heads):
            acol_b = jnp.broadcast_to(acum_col[:, h:h + 1], (q, LANES))
            e_cols.append(jnp.exp2(acol_b))
            rows = []
            for ti in range(nsub):
                tsl = slice(ti * LANES, (ti + 1) * LANES)
                tiles = []
                for si in range(nsub):
                    ssl = slice(si * LANES, (si + 1) * LANES)
                    if si > ti:
                        tiles.append(jnp.zeros((LANES, LANES), BF16))
                        continue
                    seg = acol_b[tsl] - src_row[h:h + 1, ssl]
                    if si == ti:
                        seg = jnp.where(tri, seg, -jnp.inf)
                    tiles.append((cb[tsl, ssl] * jnp.exp2(seg)).astype(BF16))
                rows.append(jnp.concatenate(tiles, axis=1))
            mh = jnp.concatenate(rows, axis=0)
            xh = jnp.where(lane_head == r, xg_b, jnp.zeros_like(xg_b))
            part = jnp.dot(mh, xh, preferred_element_type=F32)
            y_intra = part if y_intra is None else y_intra + part
            btw = (btg_f * w_row[h:h + 1, :]).astype(BF16)
            part = jnp.dot(btw, xh, preferred_element_type=F32)
            st_new = part if st_new is None else st_new + part
        e_x = jnp.concatenate([jnp.where(low_half, e_cols[2 * i], e_cols[2 * i + 1])
                               for i in range(gw // LANES)], axis=1)
        y_g = y_inter * e_x + y_intra
        d_x = _expand_heads([dec_row[:, h:h + 1] for h in heads], lane_head)
        state_ref[g] = state * d_x + st_new
        xg = xg_b.astype(F32)
        zg = z_ref[0, :, gs].astype(F32)
        yv = (y_g + dskip_ref[:, gs] * xg) * (zg * _sigmoid(zg))
        ms = jnp.mean(yv * yv, axis=-1, keepdims=True)
        y_ref[0, :, gs] = (yv * lax.rsqrt(ms + RMS_EPS) * nw_ref[:, gs]).astype(BF16)


def _ssd_scan(xs, bmt, cm, z, dt, dtt, arow, acol, dskip, nw, ltri, utri, q):
    b, s, d_inner = xs.shape
    gn = cm.shape[2]
    nheads = dtt.shape[1]
    ngroups = gn // SSM_STATE
    heads_per_group = nheads // ngroups
    gw = heads_per_group * SSM_HEAD_DIM
    blk = lambda w: pl.BlockSpec((1, q, w), lambda bi, ci: (bi, ci, 0))
    return pl.pallas_call(
        functools.partial(_ssd_scan_kernel, heads_per_group=heads_per_group),
        grid=(b, s // q),
        in_specs=[blk(d_inner), pl.BlockSpec((1, gn, q), lambda bi, ci: (bi, 0, ci)),
                  blk(gn), blk(d_inner), blk(LANES),
                  pl.BlockSpec((1, nheads, q), lambda bi, ci: (bi, 0, ci)),
                  _resident(arow.shape), _resident(acol.shape), _resident(dskip.shape),
                  _resident(nw.shape), _resident(ltri.shape), _resident(utri.shape)],
        out_specs=blk(d_inner),
        out_shape=jax.ShapeDtypeStruct((b, s, d_inner), BF16),
        scratch_shapes=[pltpu.VMEM((ngroups, SSM_STATE, gw), F32)],
        compiler_params=_params("arbitrary", "arbitrary"),
        name="ssd_scan",
    )(xs, bmt, cm, z, dt, dtt, arow, acol, dskip, nw, ltri, utri)


def _pad_cols(a, width):
    return jnp.pad(a, ((0, 0), (0, width - a.shape[1])))


def _fox_mixer(xf, bsz, seq, w_in, b_f):
    m, d = xf.shape
    nh = d // ATTN_HEAD_DIM
    scale = LOG2E / math.sqrt(ATTN_HEAD_DIM)
    wqt = (w_in[:, :d] * scale).T.astype(BF16)
    wk = w_in[:, d:2 * d].astype(BF16)
    wvt = w_in[:, 2 * d:3 * d].T.astype(BF16)
    wft = w_in[:, 3 * d:].T.astype(BF16)
    utri = jnp.triu(jnp.ones((CUMSUM_CHUNK, CUMSUM_CHUNK), F32)).astype(BF16)
    blk = min(ATTN_BLOCK, seq)
    k, qt, vt, qta, ka = _fox_proj(xf, wk, wqt, wvt, wft, b_f.reshape(nh, 1), utri, bsz, seq, blk)
    o = _fox_attn(qt, qta, k.reshape(bsz, seq, d), ka.reshape(bsz, seq, -1), vt, blk)
    return o.reshape(m, d)


def _ssd_mixer(xf, bsz, seq, w_in, conv_w, conv_b, dt_bias, a_log, d_skip, norm_w):
    m, d = xf.shape
    nheads = dt_bias.shape[0]
    d_inner = nheads * SSM_HEAD_DIM
    gn = SSM_GROUPS * SSM_STATE
    wz = w_in[:, :d_inner].astype(BF16)
    wxbc = w_in[:, d_inner:2 * d_inner + 2 * gn].astype(BF16)
    wdt = _pad_cols(w_in[:, 2 * d_inner + 2 * gn:], LANES).astype(BF16)
    dtb = _pad_cols(dt_bias.reshape(1, nheads), LANES)
    z, xs, bmt, cm, dt, dtt = _ssd_proj(xf, wz, wxbc, wdt, conv_w, conv_b.reshape(1, -1), dtb,
                                        bsz, seq, gn, nheads)
    a = -jnp.exp(a_log.astype(F32))
    arow = _pad_cols(a.reshape(1, nheads), LANES)
    acol = a.reshape(nheads, 1)
    dskip = jnp.repeat(d_skip, SSM_HEAD_DIM).reshape(1, d_inner)
    q = min(SSD_CHUNK, seq)
    ltri = jnp.tril(jnp.ones((q, q), F32)).astype(BF16)
    utri = jnp.triu(jnp.ones((q, q), F32)).astype(BF16)
    y = _ssd_scan(xs.reshape(bsz, seq, d_inner), bmt, cm.reshape(bsz, seq, gn),
                  z.reshape(bsz, seq, d_inner), dt.reshape(bsz, seq, LANES), dtt, arow, acol,
                  dskip, norm_w.reshape(1, d_inner), ltri, utri, q)
    return y.reshape(m, d_inner)


def kernel(x, p, attn_w_in, attn_b_f, attn_w_out, ssm_w_in, ssm_conv_w, ssm_conv_b, ssm_dt_bias, ssm_A_log, ssm_D, ssm_norm_w, ssm_w_out, ln_mix_g, ln_mix_b, ffn_w_up, ffn_conv_w, ffn_conv_b, ffn_w_down, ln_ffn_g, ln_ffn_b, ple_w_proj, ple_w_gate, ple_b_gate):
    bsz, seq, d = x.shape
    depth = p.shape[0]
    n_mixers = 2
    alpha = (2 * depth) ** 0.25
    xf = x.reshape(bsz * seq, d)
    for i in range(depth):
        j = i // n_mixers
        if i % n_mixers == 0:
            y, w_out = _fox_mixer(xf, bsz, seq, attn_w_in[j], attn_b_f[j]), attn_w_out[j]
        else:
            y = _ssd_mixer(xf, bsz, seq, ssm_w_in[j], ssm_conv_w[j], ssm_conv_b[j],
                           ssm_dt_bias[j], ssm_A_log[j], ssm_D[j], ssm_norm_w[j])
            w_out = ssm_w_out[j]
        f = ffn_conv_w.shape[-1]
        xf = _ffn(y, w_out.astype(BF16), xf, ln_mix_g[i].reshape(1, d), ln_mix_b[i].reshape(1, d),
                  ffn_w_up[i][:, :f].astype(BF16), ffn_w_up[i][:, f:].astype(BF16),
                  ffn_conv_w[i], ffn_conv_b[i].reshape(1, f), ffn_w_down[i].astype(BF16),
                  ln_ffn_g[i].reshape(1, d), ln_ffn_b[i].reshape(1, d),
                  p[i].reshape(bsz * seq, -1), ple_w_proj[i].astype(BF16),
                  ple_w_gate[i].astype(BF16), ple_b_gate[i].reshape(1, d), alpha, seq)
    return xf.reshape(bsz, seq, d)
```

```python
import functools
import math

import jax
import jax.numpy as jnp
import numpy as np
from jax import lax
from jax.experimental import pallas as pl
from jax.experimental.pallas import tpu as pltpu

F32 = jnp.float32
BF16 = jnp.bfloat16

ATTN_HEAD_DIM = 64
SSM_HEAD_DIM = 64
SSM_GROUPS = 8
SSM_STATE = 128
LN_EPS = 1e-5
RMS_EPS = 1e-5

LANES = 128
SUBLANES = 8
VMEM_LIMIT_BYTES = 56 * 1024 * 1024

ROW_TILE = 512
ATTN_BLOCK = 512
ATTN_UNROLL = 4
SSD_COL_TILE = 1024
FFN_CHUNK_TILES = 4
SSD_CHUNK = 256
CUMSUM_CHUNK = 256
MASK_VALUE = -1e30
LOG2E = 1.4426950408889634
SPLIT_PIECES = 3
AUG_ROWS = 16
ONES_ROWS = 16


def _params(*sem):
    return pltpu.CompilerParams(dimension_semantics=sem, vmem_limit_bytes=VMEM_LIMIT_BYTES)


def _resident(shape):
    nd = len(shape)
    return pl.BlockSpec(shape, lambda *_: (0,) * nd, pipeline_mode=pl.Buffered(1))


def _split3(v):
    hi = v.astype(BF16)
    r1 = v - hi.astype(F32)
    mid = r1.astype(BF16)
    lo = (r1 - mid.astype(F32)).astype(BF16)
    return hi, mid, lo


def _softplus(v):
    return jnp.maximum(v, 0.0) + jnp.log1p(jnp.exp(-jnp.abs(v)))


def _sigmoid(v):
    return 1.0 / (1.0 + jnp.exp2(v * (-LOG2E)))


def _layer_norm(h, g, b):
    mu = jnp.mean(h, axis=-1, keepdims=True)
    d = h - mu
    var = jnp.mean(d * d, axis=-1, keepdims=True)
    return d * lax.rsqrt(var + LN_EPS) * g + b


_NT = (((1,), (1,)), ((), ()))


def _bias_placement(nh):
    npieces = SPLIT_PIECES
    place_q = np.zeros((nh * AUG_ROWS, LANES), np.float32)
    ones_q = np.zeros((nh * AUG_ROWS, 1), np.float32)
    place_k = np.zeros((LANES, (nh // 2) * LANES), np.float32)
    ones_k = np.zeros((1, (nh // 2) * LANES), np.float32)
    for h in range(nh):
        off = 2 * npieces * (h % 2)
        for p in range(npieces):
            ones_q[h * AUG_ROWS + off + p, 0] = 1.0
            place_q[h * AUG_ROWS + off + npieces + p, p * nh + h] = 1.0
            place_k[p * nh + h, (h // 2) * LANES + off + p] = -1.0
            ones_k[0, (h // 2) * LANES + off + npieces + p] = 1.0
    return (jnp.asarray(place_q, BF16), jnp.asarray(ones_q), jnp.asarray(place_k, BF16),
            jnp.asarray(ones_k))


def _fox_proj_kernel(x_ref, wk_ref, wqt_ref, wvt_ref, wft_ref, bf_ref, ut_ref, pq_ref, oq_ref,
                     pk_ref, ok_ref, k_ref, qt_ref, vt_ref, qta_ref, ka_ref, carry_ref, *,
                     steps_per_batch):
    tm, d = x_ref.shape

    @pl.when(pl.program_id(0) % steps_per_batch == 0)
    def _():
        carry_ref[...] = jnp.zeros_like(carry_ref)

    xb = x_ref[...].astype(BF16)
    fl = lax.dot_general(wft_ref[...], xb, _NT, preferred_element_type=F32) + bf_ref[...]
    k_ref[...] = jnp.dot(xb, wk_ref[...], preferred_element_type=F32).astype(BF16)
    logf = -_softplus(-fl) * LOG2E
    nh = logf.shape[0]
    parts = jnp.concatenate(_split3(logf), axis=0)
    sums = [jnp.dot(parts[:, j * CUMSUM_CHUNK:(j + 1) * CUMSUM_CHUNK], ut_ref[...],
                    preferred_element_type=F32) for j in range(tm // CUMSUM_CHUNK)]
    qt_ref[0] = lax.dot_general(wqt_ref[...], xb, _NT, preferred_element_type=F32).astype(BF16)
    carry = carry_ref[...]
    zrows = jnp.zeros((LANES - 3 * nh, CUMSUM_CHUNK), F32)
    stacks = []
    for pj in sums:
        cj = (pj[0:nh] + pj[nh:2 * nh] + pj[2 * nh:3 * nh]) + carry
        stacks.append(jnp.concatenate([cp.astype(F32) for cp in _split3(cj)] + [zrows], axis=0))
        carry = cj[:, CUMSUM_CHUNK - 1:CUMSUM_CHUNK]
    carry_ref[...] = carry
    stack = jnp.concatenate(stacks, axis=1)
    qta_ref[0] = (jnp.dot(pq_ref[...], stack.astype(BF16), preferred_element_type=F32)
                  + oq_ref[...]).astype(BF16)
    ka_ref[...] = (jnp.dot(stack.T.astype(BF16), pk_ref[...], preferred_element_type=F32)
                   + ok_ref[...]).astype(BF16)
    vt_ref[0, 0] = lax.dot_general(wvt_ref[...], xb, _NT, preferred_element_type=F32).astype(BF16)


def _fox_proj(xf, wk, wqt, wvt, wft, bf_col, utri, bsz, seq, tm):
    m, d = xf.shape
    nh = wft.shape[0]
    nblk = seq // tm
    row = lambda i: (i, 0)
    place = _bias_placement(nh)
    out_shape = [jax.ShapeDtypeStruct((m, d), BF16), jax.ShapeDtypeStruct((bsz, d, seq), BF16),
                 jax.ShapeDtypeStruct((bsz, nblk, d, tm), BF16),
                 jax.ShapeDtypeStruct((bsz, nh * AUG_ROWS, seq), BF16),
                 jax.ShapeDtypeStruct((m, (nh // 2) * LANES), BF16)]
    return pl.pallas_call(
        functools.partial(_fox_proj_kernel, steps_per_batch=nblk),
        grid=(m // tm,),
        in_specs=[pl.BlockSpec((tm, d), row), _resident(wk.shape), _resident(wqt.shape),
                  _resident(wvt.shape), _resident(wft.shape), _resident(bf_col.shape),
                  _resident(utri.shape)] + [_resident(a.shape) for a in place],
        out_specs=[pl.BlockSpec((tm, d), row),
                   pl.BlockSpec((1, d, tm), lambda i: (i // nblk, 0, i % nblk)),
                   pl.BlockSpec((1, 1, d, tm), lambda i: (i // nblk, i % nblk, 0, 0)),
                   pl.BlockSpec((1, nh * AUG_ROWS, tm), lambda i: (i // nblk, 0, i % nblk)),
                   pl.BlockSpec((tm, (nh // 2) * LANES), row)],
        out_shape=out_shape,
        scratch_shapes=[pltpu.VMEM((nh, 1), F32)],
        compiler_params=_params("arbitrary"),
        name="fox_proj",
    )(xf, wk, wqt, wvt, wft, bf_col, utri, *place)


def _fox_attn_kernel(qt_ref, qta_ref, k_ref, ka_ref, vt_ref, o_ref, m_scr, acc_scr,
                     sa_scr, sb_scr, ma_scr, mb_scr, *, blk):
    qi = pl.program_id(2)
    hd = ATTN_HEAD_DIM
    qt2 = qt_ref[0]
    zhead = jnp.zeros((hd, blk), BF16)
    zpad = jnp.zeros((LANES - AUG_ROWS, blk), BF16)
    qts = [jnp.concatenate([qt2[0:hd], zhead, qta_ref[0, 0:AUG_ROWS], zpad], axis=0),
           jnp.concatenate([zhead, qt2[hd:2 * hd], qta_ref[0, AUG_ROWS:2 * AUG_ROWS], zpad], axis=0)]
    ones_rows = jnp.ones((ONES_ROWS, blk), BF16)
    key_id = lax.broadcasted_iota(jnp.int32, (blk, blk), 0)
    qry_id = lax.broadcasted_iota(jnp.int32, (blk, blk), 1)
    m_scr[...] = jnp.full_like(m_scr, MASK_VALUE)
    acc_scr[...] = jnp.zeros_like(acc_scr)

    def keys(j):
        start = pl.multiple_of(j * blk, blk)
        return jnp.concatenate([k_ref[0, pl.ds(start, blk), :], ka_ref[0, pl.ds(start, blk), :]],
                               axis=1)

    def values(j):
        vt2 = vt_ref[0, j]
        return [jnp.concatenate([vt2[h * hd:(h + 1) * hd], ones_rows], axis=0) for h in range(2)]

    half = blk // 2

    def produce(kk, h, s_buf, m_buf, diagonal=False):
        if diagonal:
            s_buf[h, 0:half, :] = jnp.dot(kk[0:half], qts[h], preferred_element_type=F32)
            s_buf[h, half:blk, half:blk] = jnp.dot(kk[half:blk], qts[h][:, half:blk],
                                                  preferred_element_type=F32)
            return
        s = jnp.dot(kk, qts[h], preferred_element_type=F32)
        s_buf[h] = s
        m_buf[h] = jnp.max(s, axis=0, keepdims=True)

    def consume(vt, h, s_buf, m_buf):
        s = s_buf[h]
        m_old = m_scr[h]
        m_new = jnp.maximum(m_old, m_buf[h])
        p = jnp.exp2(s - m_new).astype(BF16)
        alpha = jnp.exp2(m_old - m_new)
        acc_scr[h] = alpha * acc_scr[h] + jnp.dot(vt[h], p, preferred_element_type=F32)
        m_scr[h] = m_new

    def consume_diagonal(vt, h, s_buf):
        tri = (lax.broadcasted_iota(jnp.int32, (half, half), 0)
               <= lax.broadcasted_iota(jnp.int32, (half, half), 1))
        early, late = slice(0, half), slice(half, blk)
        for qs in (early, late):
            s_tri = jnp.where(tri, s_buf[h, qs, qs], MASK_VALUE)
            m_blk = jnp.max(s_tri, axis=0, keepdims=True)
            if qs is late:
                s_full = s_buf[h, early, late]
                m_blk = jnp.maximum(m_blk, jnp.max(s_full, axis=0, keepdims=True))
            m_old = m_scr[h, :, qs]
            m_new = jnp.maximum(m_old, m_blk)
            p = jnp.exp2(s_tri - m_new).astype(BF16)
            v_t = vt[h][:, qs]
            if qs is late:
                p = jnp.concatenate([jnp.exp2(s_full - m_new).astype(BF16), p], axis=0)
                v_t = vt[h]
            acc_scr[h, :, qs] = (jnp.exp2(m_old - m_new) * acc_scr[h, :, qs]
                                 + jnp.dot(v_t, p, preferred_element_type=F32))
            m_scr[h, :, qs] = m_new

    kk0 = keys(0)
    for h in range(2):
        produce(kk0, h, sa_scr, ma_scr)

    bufs = ((sa_scr, ma_scr), (sb_scr, mb_scr))

    def advance(j0, nsteps, ends_on_diagonal=False):
        for i in range(nsteps):
            kk, vt = keys(j0 + i + 1), values(j0 + i)
            for h in range(2):
                produce(kk, h, *bufs[(i + 1) % 2], diagonal=ends_on_diagonal and i == nsteps - 1)
                consume(vt, h, *bufs[i % 2])

    def body(t, carry):
        advance(ATTN_UNROLL * t, ATTN_UNROLL)
        return carry

    lax.fori_loop(0, qi // ATTN_UNROLL, body, 0)

    for rem in range(ATTN_UNROLL):
        @pl.when(qi % ATTN_UNROLL == rem)
        def _(rem=rem):
            advance(qi - rem, rem, ends_on_diagonal=True)
            vt = values(qi)
            for h in range(2):
                consume_diagonal(vt, h, bufs[rem % 2][0])

    outs = []
    for h in range(2):
        acc = acc_scr[h]
        outs.append(acc[0:hd] * (1.0 / acc[hd:hd + 1]))
    o_ref[0] = jnp.concatenate(outs, axis=0).T.astype(BF16)


def _fox_attn(qt, qta, k, ka, vt, blk):
    b, d, s = qt.shape
    nhp = d // LANES
    nblk = s // blk
    return pl.pallas_call(
        functools.partial(_fox_attn_kernel, blk=blk),
        grid=(b, nhp, nblk),
        in_specs=[
            pl.BlockSpec((1, LANES, blk), lambda bi, hp, qi: (bi, hp, qi)),
            pl.BlockSpec((1, 2 * AUG_ROWS, blk), lambda bi, hp, qi: (bi, hp, qi)),
            pl.BlockSpec((1, s, LANES), lambda bi, hp, qi: (bi, 0, hp)),
            pl.BlockSpec((1, s, LANES), lambda bi, hp, qi: (bi, 0, hp)),
            pl.BlockSpec((1, nblk, LANES, blk), lambda bi, hp, qi: (bi, 0, hp, 0)),
        ],
        out_specs=pl.BlockSpec((1, blk, LANES), lambda bi, hp, qi: (bi, qi, hp)),
        out_shape=jax.ShapeDtypeStruct((b, s, d), BF16),
        scratch_shapes=[pltpu.VMEM((2, 1, blk), F32),
                        pltpu.VMEM((2, ATTN_HEAD_DIM + ONES_ROWS, blk), F32),
                        pltpu.VMEM((2, blk, blk), F32), pltpu.VMEM((2, blk, blk), F32),
                        pltpu.VMEM((2, 1, blk), F32), pltpu.VMEM((2, 1, blk), F32)],
        compiler_params=_params("arbitrary", "arbitrary", "arbitrary"),
        name="fox_attn",
    )(qt, qta, k, ka, vt)


def _ffn_kernel(y_ref, wo_ref, x_ref, g1_ref, b1_ref, wu_ref, wg_ref, cw_ref, cb_ref, wd_ref,
                g_ref, b_ref, p_ref, wproj_ref, wgate_ref, bgate_ref, o_ref, gbuf0_ref, gbuf1_ref,
                carry_ref, *, chunks, steps_per_batch, alpha):
    tm = x_ref.shape[0]
    halo = SUBLANES
    kconv = cw_ref.shape[0]
    gbufs = (gbuf0_ref, gbuf1_ref)
    halves = (slice(0, tm // 2), slice(tm // 2, tm))
    last = len(chunks) - 1

    @pl.when(pl.program_id(0) % steps_per_batch == 0)
    def _():
        carry_ref[...] = jnp.zeros_like(carry_ref)

    mixes = [jnp.dot(y_ref[rs, :], wo_ref[...], preferred_element_type=F32) for rs in halves]
    xh = [_layer_norm(alpha * x_ref[rs, :] + mix, g1_ref[...], b1_ref[...])
          for rs, mix in zip(halves, mixes)]
    xbh = [v.astype(BF16) for v in xh]

    def stage(i, g):
        c0, cw = chunks[i]
        cs = slice(c0, c0 + cw)
        buf = gbufs[i % 2]
        buf[0:halo, 0:cw] = carry_ref[:, cs]
        buf[halo:halo + tm, 0:cw] = g
        carry_ref[:, cs] = g[tm - halo:tm, :]

    def up(i, lhs):
        c0, cw = chunks[i]
        cs = slice(c0, c0 + cw)
        return (jnp.dot(lhs, wu_ref[:, cs], preferred_element_type=F32),
                jnp.dot(lhs, wg_ref[:, cs], preferred_element_type=F32))

    def gated(i, u):
        c0, cw = chunks[i]
        cs = slice(c0, c0 + cw)
        buf = gbufs[i % 2]
        conv = cb_ref[:, cs] + cw_ref[kconv - 1:kconv, cs] * buf[halo:halo + tm, 0:cw]
        for k in range(kconv - 1):
            off = halo - (kconv - 1) + k
            conv = conv + cw_ref[k:k + 1, cs] * buf[off:off + tm, 0:cw]
        gelu = 0.5 * conv * (1.0 + lax.erf(conv * (1.0 / math.sqrt(2.0))))
        return (gelu * u).astype(BF16)

    first = [up(0, xbh[0]), up(0, xbh[1])]
    u_cur = jnp.concatenate([first[0][0], first[1][0]], axis=0)
    stage(0, jnp.concatenate([first[0][1], first[1][1]], axis=0))
    xb = jnp.concatenate(xbh, axis=0)

    acc = None
    for i in range(last):
        u_next, g_next = up(i + 1, xb)
        stage(i + 1, g_next)
        cs = slice(chunks[i][0], chunks[i][0] + chunks[i][1])
        part = jnp.dot(gated(i, u_cur), wd_ref[cs, :], preferred_element_type=F32)
        acc = part if acc is None else acc + part
        u_cur = u_next

    hid = gated(last, u_cur)
    cs = slice(chunks[last][0], chunks[last][0] + chunks[last][1])
    parts = [jnp.dot(hid[rs, :], wd_ref[cs, :], preferred_element_type=F32) for rs in halves]
    for rs, part, xin in zip(halves, parts, xh):
        total = part if acc is None else acc[rs, :] + part
        x2 = _layer_norm(alpha * xin + total, g_ref[...], b_ref[...])
        gate = _sigmoid(jnp.dot(x2.astype(BF16), wgate_ref[...], preferred_element_type=F32)
                        + bgate_ref[...])
        emb = jnp.dot(p_ref[rs, :].astype(BF16), wproj_ref[...], preferred_element_type=F32)
        o_ref[rs, :] = x2 + gate * emb


def _ffn_chunks(f):
    tile = 2 * LANES
    if f % tile:
        return ((0, f),)
    step = FFN_CHUNK_TILES * tile
    return tuple((c0, min(step, f - c0)) for c0 in range(0, f, step))


def _ffn(y, wo, xf, g1, b1, wu, wg, cw, cb, wd, g, b, pf, wproj, wgate, bgate, alpha, seq):
    m, d = xf.shape
    f = wu.shape[1]
    pdim = pf.shape[1]
    kdim = y.shape[1]
    tm = min(ROW_TILE, seq)
    chunks = _ffn_chunks(f)
    wmax = max(c[1] for c in chunks)
    row = lambda i: (i, 0)
    return pl.pallas_call(
        functools.partial(_ffn_kernel, chunks=chunks, steps_per_batch=seq // tm, alpha=alpha),
        grid=(m // tm,),
        in_specs=[pl.BlockSpec((tm, kdim), row), _resident(wo.shape), pl.BlockSpec((tm, d), row),
                  _resident(g1.shape), _resident(b1.shape), _resident(wu.shape), _resident(wg.shape),
                  _resident(cw.shape), _resident(cb.shape), _resident(wd.shape),
                  _resident(g.shape), _resident(b.shape), pl.BlockSpec((tm, pdim), row),
                  _resident(wproj.shape), _resident(wgate.shape), _resident(bgate.shape)],
        out_specs=pl.BlockSpec((tm, d), row),
        out_shape=jax.ShapeDtypeStruct((m, d), F32),
        scratch_shapes=[pltpu.VMEM((tm + SUBLANES, wmax), F32), pltpu.VMEM((tm + SUBLANES, wmax), F32),
                        pltpu.VMEM((SUBLANES, f), F32)],
        compiler_params=_params("arbitrary"),
        name="conv_ffn",
    )(y, wo, xf, g1, b1, wu, wg, cw, cb, wd, g, b, pf, wproj, wgate, bgate)


def _ssd_proj_kernel(x_ref, wz_ref, wxbc_ref, wdt_ref, cw_ref, cb_ref, dtb_ref,
                     z_ref, xs_ref, bt_ref, c_ref, dt_ref, dtt_ref, buf0_ref, buf1_ref, carry_ref,
                     *, steps_per_batch, col_tile):
    tm = x_ref.shape[0]
    halo = SUBLANES
    kconv = cw_ref.shape[0]
    bufs = (buf0_ref, buf1_ref)

    @pl.when(pl.program_id(0) % steps_per_batch == 0)
    def _():
        carry_ref[...] = jnp.zeros_like(carry_ref)

    xb = x_ref[...].astype(BF16)
    d_inner = xs_ref.shape[1]
    gn = c_ref.shape[1]
    nconv = wxbc_ref.shape[1] // col_tile
    z_chunks = list(range(0, wz_ref.shape[1], col_tile))

    def project(i):
        cs = slice(i * col_tile, (i + 1) * col_tile)
        buf = bufs[i % 2]
        r = jnp.dot(xb, wxbc_ref[:, cs], preferred_element_type=F32)
        buf[0:halo, :] = carry_ref[:, cs]
        buf[halo:halo + tm, :] = r
        carry_ref[:, cs] = r[tm - halo:tm, :]

    def project_z():
        if z_chunks:
            zs = slice(z_chunks[0], z_chunks.pop(0) + col_tile)
            z_ref[:, zs] = jnp.dot(xb, wz_ref[:, zs], preferred_element_type=F32).astype(BF16)

    def conv_act(i):
        c0 = i * col_tile
        cs = slice(c0, c0 + col_tile)
        buf = bufs[i % 2]
        conv = cb_ref[:, cs] + cw_ref[kconv - 1:kconv, cs] * buf[halo:halo + tm, :]
        for k in range(kconv - 1):
            off = halo - (kconv - 1) + k
            conv = conv + cw_ref[k:k + 1, cs] * buf[off:off + tm, :]
        act = conv * _sigmoid(conv)
        if c0 < d_inner:
            xs_ref[:, cs] = act.astype(BF16)
        elif c0 < d_inner + gn:
            bt_ref[0, c0 - d_inner:c0 - d_inner + col_tile, :] = act.T.astype(BF16)
        else:
            c_ref[:, c0 - d_inner - gn:c0 - d_inner - gn + col_tile] = act.astype(BF16)

    project(0)
    for i in range(nconv):
        if i + 1 < nconv:
            project(i + 1)
        else:
            project_z()
        conv_act(i)
    while z_chunks:
        project_z()
    dt = _softplus(jnp.dot(xb, wdt_ref[...], preferred_element_type=F32) + dtb_ref[...])
    dt_ref[...] = dt
    dtt_ref[0] = dt.T[0:dtt_ref.shape[1], :]


def _ssd_proj(xf, wz, wxbc, wdt, cw, cb, dtb, bsz, seq, gn, nheads):
    m, d = xf.shape
    d_inner = wz.shape[1]
    tm = min(ROW_TILE, seq)
    nblk = seq // tm
    col_tile = min(SSD_COL_TILE, gn)
    row = lambda i: (i, 0)
    tposed = lambda i: (i // nblk, 0, i % nblk)
    out_shape = [jax.ShapeDtypeStruct((m, d_inner), BF16), jax.ShapeDtypeStruct((m, d_inner), BF16),
                 jax.ShapeDtypeStruct((bsz, gn, seq), BF16), jax.ShapeDtypeStruct((m, gn), BF16),
                 jax.ShapeDtypeStruct((m, LANES), F32),
                 jax.ShapeDtypeStruct((bsz, nheads, seq), F32)]
    return pl.pallas_call(
        functools.partial(_ssd_proj_kernel, steps_per_batch=seq // tm, col_tile=col_tile),
        grid=(m // tm,),
        in_specs=[pl.BlockSpec((tm, d), row), _resident(wz.shape), _resident(wxbc.shape),
                  _resident(wdt.shape), _resident(cw.shape), _resident(cb.shape),
                  _resident(dtb.shape)],
        out_specs=[pl.BlockSpec((tm, d_inner), row), pl.BlockSpec((tm, d_inner), row),
                   pl.BlockSpec((1, gn, tm), tposed), pl.BlockSpec((tm, gn), row),
                   pl.BlockSpec((tm, LANES), row), pl.BlockSpec((1, nheads, tm), tposed)],
        out_shape=out_shape,
        scratch_shapes=[pltpu.VMEM((tm + SUBLANES, col_tile), F32),
                        pltpu.VMEM((tm + SUBLANES, col_tile), F32),
                        pltpu.VMEM((SUBLANES, wxbc.shape[1]), F32)],
        compiler_params=_params("arbitrary"),
        name="ssd_proj",
    )(xf, wz, wxbc, wdt, cw, cb, dtb)


def _expand_heads(cols, lane_head):
    out = cols[0]
    for r in range(1, len(cols)):
        out = jnp.where(lane_head >= r, cols[r], out)
    return out


def _ssd_scan_kernel(xs_ref, bt_ref, c_ref, z_ref, dt_ref, dtt_ref, arow_ref, acol_ref,
                     dskip_ref, nw_ref, ltri_ref, utri_ref, y_ref, state_ref, *, heads_per_group):
    q = xs_ref.shape[1]
    n = SSM_STATE
    hp_ = SSM_HEAD_DIM
    gw = heads_per_group * hp_
    ngroups = xs_ref.shape[2] // gw

    @pl.when(pl.program_id(1) == 0)
    def _():
        state_ref[...] = jnp.zeros_like(state_ref)

    dt_col = dt_ref[0]
    dt_row = dtt_ref[0]
    nheads = dt_row.shape[0]
    a_col = dt_col * (arow_ref[...] * LOG2E)
    pc = jnp.dot(ltri_ref[...], jnp.concatenate(_split3(a_col), axis=1),
                 preferred_element_type=F32)
    acum_col = pc[:, 0:LANES] + pc[:, LANES:2 * LANES] + pc[:, 2 * LANES:3 * LANES]
    a_row = dt_row * (acol_ref[...] * LOG2E)
    pr = jnp.dot(jnp.concatenate(_split3(a_row), axis=0), utri_ref[...],
                 preferred_element_type=F32)
    acum_row = pr[0:nheads] + pr[nheads:2 * nheads] + pr[2 * nheads:3 * nheads]
    dec_row = jnp.exp2(acum_col[q - 1:q, :])
    w_row = jnp.exp2(acum_row[:, q - 1:q] - acum_row) * dt_row
    src_row = acum_row - jnp.log2(dt_row)

    nsub = q // LANES
    tri = (lax.broadcasted_iota(jnp.int32, (LANES, LANES), 0)
           >= lax.broadcasted_iota(jnp.int32, (LANES, LANES), 1))
    lane_head = lax.broadcasted_iota(jnp.int32, (1, gw), 1) // hp_
    low_half = lax.broadcasted_iota(jnp.int32, (1, LANES), 1) < hp_

    for g in range(ngroups):
        gs = slice(g * gw, (g + 1) * gw)
        ns = slice(g * n, (g + 1) * n)
        cg = c_ref[0, :, ns]
        btg = bt_ref[0, ns, :]
        btg_f = btg.astype(F32)
        xg_b = xs_ref[0, :, gs]
        heads = range(g * heads_per_group, (g + 1) * heads_per_group)
        cb = jnp.dot(cg, btg, preferred_element_type=F32)
        state = state_ref[g]
        y_inter = jnp.dot(cg, state.astype(BF16), preferred_element_type=F32)
        e_cols, y_intra, st_new = [], None, None
        for r, h in enumerate(heads):
            acol_b = jnp.broadcast_to(acum_col[:, h:h + 1], (q, LANES))
            e_cols.append(jnp.exp2(acol_b))
            rows = []
            for ti in range(nsub):
                tsl = slice(ti * LANES, (ti + 1) * LANES)
                tiles = []
                for si in range(nsub):
                    ssl = slice(si * LANES, (si + 1) * LANES)
                    if si > ti:
                        tiles.append(jnp.zeros((LANES, LANES), BF16))
                        continue
                    seg = acol_b[tsl] - src_row[h:h + 1, ssl]
                    if si == ti:
                        seg = jnp.where(tri, seg, -jnp.inf)
                    tiles.append((cb[tsl, ssl] * jnp.exp2(seg)).astype(BF16))
                rows.append(jnp.concatenate(tiles, axis=1))
            mh = jnp.concatenate(rows, axis=0)
            xh = jnp.where(lane_head == r, xg_b, jnp.zeros_like(xg_b))
            part = jnp.dot(mh, xh, preferred_element_type=F32)
            y_intra = part if y_intra is None else y_intra + part
            btw = (btg_f * w_row[h:h + 1, :]).astype(BF16)
            part = jnp.dot(btw, xh, preferred_element_type=F32)
            st_new = part if st_new is None else st_new + part
        e_x = jnp.concatenate([jnp.where(low_half, e_cols[2 * i], e_cols[2 * i + 1])
                               for i in range(gw // LANES)], axis=1)
        y_g = y_inter * e_x + y_intra
        d_x = _expand_heads([dec_row[:, h:h + 1] for h in heads], lane_head)
        state_ref[g] = state * d_x + st_new
        xg = xg_b.astype(F32)
        zg = z_ref[0, :, gs].astype(F32)
        yv = (y_g + dskip_ref[:, gs] * xg) * (zg * _sigmoid(zg))
        ms = jnp.mean(yv * yv, axis=-1, keepdims=True)
        y_ref[0, :, gs] = (yv * lax.rsqrt(ms + RMS_EPS) * nw_ref[:, gs]).astype(BF16)


def _ssd_scan(xs, bmt, cm, z, dt, dtt, arow, acol, dskip, nw, ltri, utri, q):
    b, s, d_inner = xs.shape
    gn = cm.shape[2]
    nheads = dtt.shape[1]
    ngroups = gn // SSM_STATE
    heads_per_group = nheads // ngroups
    gw = heads_per_group * SSM_HEAD_DIM
    blk = lambda w: pl.BlockSpec((1, q, w), lambda bi, ci: (bi, ci, 0))
    return pl.pallas_call(
        functools.partial(_ssd_scan_kernel, heads_per_group=heads_per_group),
        grid=(b, s // q),
        in_specs=[blk(d_inner), pl.BlockSpec((1, gn, q), lambda bi, ci: (bi, 0, ci)),
                  blk(gn), blk(d_inner), blk(LANES),
                  pl.BlockSpec((1, nheads, q), lambda bi, ci: (bi, 0, ci)),
                  _resident(arow.shape), _resident(acol.shape), _resident(dskip.shape),
                  _resident(nw.shape), _resident(ltri.shape), _resident(utri.shape)],
        out_specs=blk(d_inner),
        out_shape=jax.ShapeDtypeStruct((b, s, d_inner), BF16),
        scratch_shapes=[pltpu.VMEM((ngroups, SSM_STATE, gw), F32)],
        compiler_params=_params("arbitrary", "arbitrary"),
        name="ssd_scan",
    )(xs, bmt, cm, z, dt, dtt, arow, acol, dskip, nw, ltri, utri)


def _pad_cols(a, width):
    return jnp.pad(a, ((0, 0), (0, width - a.shape[1])))


def _fox_mixer(xf, bsz, seq, w_in, b_f):
    m, d = xf.shape
    nh = d // ATTN_HEAD_DIM
    scale = LOG2E / math.sqrt(ATTN_HEAD_DIM)
    wqt = (w_in[:, :d] * scale).T.astype(BF16)
    wk = w_in[:, d:2 * d].astype(BF16)
    wvt = w_in[:, 2 * d:3 * d].T.astype(BF16)
    wft = w_in[:, 3 * d:].T.astype(BF16)
    utri = jnp.triu(jnp.ones((CUMSUM_CHUNK, CUMSUM_CHUNK), F32)).astype(BF16)
    blk = min(ATTN_BLOCK, seq)
    k, qt, vt, qta, ka = _fox_proj(xf, wk, wqt, wvt, wft, b_f.reshape(nh, 1), utri, bsz, seq, blk)
    o = _fox_attn(qt, qta, k.reshape(bsz, seq, d), ka.reshape(bsz, seq, -1), vt, blk)
    return o.reshape(m, d)


def _ssd_mixer(xf, bsz, seq, w_in, conv_w, conv_b, dt_bias, a_log, d_skip, norm_w):
    m, d = xf.shape
    nheads = dt_bias.shape[0]
    d_inner = nheads * SSM_HEAD_DIM
    gn = SSM_GROUPS * SSM_STATE
    wz = w_in[:, :d_inner].astype(BF16)
    wxbc = w_in[:, d_inner:2 * d_inner + 2 * gn].astype(BF16)
    wdt = _pad_cols(w_in[:, 2 * d_inner + 2 * gn:], LANES).astype(BF16)
    dtb = _pad_cols(dt_bias.reshape(1, nheads), LANES)
    z, xs, bmt, cm, dt, dtt = _ssd_proj(xf, wz, wxbc, wdt, conv_w, conv_b.reshape(1, -1), dtb,
                                        bsz, seq, gn, nheads)
    a = -jnp.exp(a_log.astype(F32))
    arow = _pad_cols(a.reshape(1, nheads), LANES)
    acol = a.reshape(nheads, 1)
    dskip = jnp.repeat(d_skip, SSM_HEAD_DIM).reshape(1, d_inner)
    q = min(SSD_CHUNK, seq)
    ltri = jnp.tril(jnp.ones((q, q), F32)).astype(BF16)
    utri = jnp.triu(jnp.ones((q, q), F32)).astype(BF16)
    y = _ssd_scan(xs.reshape(bsz, seq, d_inner), bmt, cm.reshape(bsz, seq, gn),
                  z.reshape(bsz, seq, d_inner), dt.reshape(bsz, seq, LANES), dtt, arow, acol,
                  dskip, norm_w.reshape(1, d_inner), ltri, utri, q)
    return y.reshape(m, d_inner)


def kernel(x, p, attn_w_in, attn_b_f, attn_w_out, ssm_w_in, ssm_conv_w, ssm_conv_b, ssm_dt_bias, ssm_A_log, ssm_D, ssm_norm_w, ssm_w_out, ln_mix_g, ln_mix_b, ffn_w_up, ffn_conv_w, ffn_conv_b, ffn_w_down, ln_ffn_g, ln_ffn_b, ple_w_proj, ple_w_gate, ple_b_gate):
    bsz, seq, d = x.shape
    depth = p.shape[0]
    n_mixers = 2
    alpha = (2 * depth) ** 0.25
    xf = x.reshape(bsz * seq, d)
    for i in range(depth):
        j = i // n_mixers
        if i % n_mixers == 0:
            y, w_out = _fox_mixer(xf, bsz, seq, attn_w_in[j], attn_b_f[j]), attn_w_out[j]
        else:
            y = _ssd_mixer(xf, bsz, seq, ssm_w_in[j], ssm_conv_w[j], ssm_conv_b[j],
                           ssm_dt_bias[j], ssm_A_log[j], ssm_D[j], ssm_norm_w[j])
            w_out = ssm_w_out[j]
        f = ffn_conv_w.shape[-1]
        xf = _ffn(y, w_out.astype(BF16), xf, ln_mix_g[i].reshape(1, d), ln_mix_b[i].reshape(1, d),
                  ffn_w_up[i][:, :f].astype(BF16), ffn_w_up[i][:, f:].astype(BF16),
                  ffn_conv_w[i], ffn_conv_b[i].reshape(1, f), ffn_w_down[i].astype(BF16),
                  ln_ffn_g[i].reshape(1, d), ln_ffn_b[i].reshape(1, d),
                  p[i].reshape(bsz * seq, -1), ple_w_proj[i].astype(BF16),
                  ple_w_gate[i].astype(BF16), ple_b_gate[i].reshape(1, d), alpha, seq)
    return xf.reshape(bsz, seq, d)
```

```python
import functools
import math

import jax
import jax.numpy as jnp
import numpy as np
from jax import lax
from jax.experimental import pallas as pl
from jax.experimental.pallas import tpu as pltpu

F32 = jnp.float32
BF16 = jnp.bfloat16

ATTN_HEAD_DIM = 64
SSM_HEAD_DIM = 64
SSM_GROUPS = 8
SSM_STATE = 128
LN_EPS = 1e-5
RMS_EPS = 1e-5

LANES = 128
SUBLANES = 8
VMEM_LIMIT_BYTES = 56 * 1024 * 1024

ROW_TILE = 512
ATTN_BLOCK = 512
ATTN_UNROLL = 4
SSD_COL_TILE = 1024
FFN_CHUNK_TILES = 4
SSD_CHUNK = 256
CUMSUM_CHUNK = 256
MASK_VALUE = -1e30
LOG2E = 1.4426950408889634
SPLIT_PIECES = 3
AUG_ROWS = 16
ONES_ROWS = 16


def _params(*sem):
    return pltpu.CompilerParams(dimension_semantics=sem, vmem_limit_bytes=VMEM_LIMIT_BYTES)


def _resident(shape):
    nd = len(shape)
    return pl.BlockSpec(shape, lambda *_: (0,) * nd, pipeline_mode=pl.Buffered(1))


def _split3(v):
    hi = v.astype(BF16)
    r1 = v - hi.astype(F32)
    mid = r1.astype(BF16)
    lo = (r1 - mid.astype(F32)).astype(BF16)
    return hi, mid, lo


def _softplus(v):
    return jnp.maximum(v, 0.0) + jnp.log1p(jnp.exp(-jnp.abs(v)))


def _sigmoid(v):
    return 1.0 / (1.0 + jnp.exp2(v * (-LOG2E)))


def _layer_norm(h, g, b):
    mu = jnp.mean(h, axis=-1, keepdims=True)
    d = h - mu
    var = jnp.mean(d * d, axis=-1, keepdims=True)
    return d * lax.rsqrt(var + LN_EPS) * g + b


_NT = (((1,), (1,)), ((), ()))


def _bias_placement(nh):
    npieces = SPLIT_PIECES
    place_q = np.zeros((nh * AUG_ROWS, LANES), np.float32)
    ones_q = np.zeros((nh * AUG_ROWS, 1), np.float32)
    place_k = np.zeros((LANES, (nh // 2) * LANES), np.float32)
    ones_k = np.zeros((1, (nh // 2) * LANES), np.float32)
    for h in range(nh):
        off = 2 * npieces * (h % 2)
        for p in range(npieces):
            ones_q[h * AUG_ROWS + off + p, 0] = 1.0
            place_q[h * AUG_ROWS + off + npieces + p, p * nh + h] = 1.0
            place_k[p * nh + h, (h // 2) * LANES + off + p] = -1.0
            ones_k[0, (h // 2) * LANES + off + npieces + p] = 1.0
    return (jnp.asarray(place_q, BF16), jnp.asarray(ones_q), jnp.asarray(place_k, BF16),
            jnp.asarray(ones_k))


def _fox_proj_kernel(x_ref, wk_ref, wqt_ref, wvt_ref, wft_ref, bf_ref, ut_ref, pq_ref, oq_ref,
                     pk_ref, ok_ref, k_ref, qt_ref, vt_ref, qta_ref, ka_ref, carry_ref, *,
                     steps_per_batch):
    tm, d = x_ref.shape

    @pl.when(pl.program_id(0) % steps_per_batch == 0)
    def _():
        carry_ref[...] = jnp.zeros_like(carry_ref)

    xb = x_ref[...].astype(BF16)
    fl = lax.dot_general(wft_ref[...], xb, _NT, preferred_element_type=F32) + bf_ref[...]
    k_ref[...] = jnp.dot(xb, wk_ref[...], preferred_element_type=F32).astype(BF16)
    logf = -_softplus(-fl) * LOG2E
    nh = logf.shape[0]
    parts = jnp.concatenate(_split3(logf), axis=0)
    sums = [jnp.dot(parts[:, j * CUMSUM_CHUNK:(j + 1) * CUMSUM_CHUNK], ut_ref[...],
                    preferred_element_type=F32) for j in range(tm // CUMSUM_CHUNK)]
    qt_ref[0] = lax.dot_general(wqt_ref[...], xb, _NT, preferred_element_type=F32).astype(BF16)
    carry = carry_ref[...]
    zrows = jnp.zeros((LANES - 3 * nh, CUMSUM_CHUNK), F32)
    stacks = []
    for pj in sums:
        cj = (pj[0:nh] + pj[nh:2 * nh] + pj[2 * nh:3 * nh]) + carry
        stacks.append(jnp.concatenate([cp.astype(F32) for cp in _split3(cj)] + [zrows], axis=0))
        carry = cj[:, CUMSUM_CHUNK - 1:CUMSUM_CHUNK]
    carry_ref[...] = carry
    stack = jnp.concatenate(stacks, axis=1)
    qta_ref[0] = (jnp.dot(pq_ref[...], stack.astype(BF16), preferred_element_type=F32)
                  + oq_ref[...]).astype(BF16)
    ka_ref[...] = (jnp.dot(stack.T.astype(BF16), pk_ref[...], preferred_element_type=F32)
                   + ok_ref[...]).astype(BF16)
    vt_ref[0, 0] = lax.dot_general(wvt_ref[...], xb, _NT, preferred_element_type=F32).astype(BF16)


def _fox_proj(xf, wk, wqt, wvt, wft, bf_col, utri, bsz, seq, tm):
    m, d = xf.shape
    nh = wft.shape[0]
    nblk = seq // tm
    row = lambda i: (i, 0)
    place = _bias_placement(nh)
    out_shape = [jax.ShapeDtypeStruct((m, d), BF16), jax.ShapeDtypeStruct((bsz, d, seq), BF16),
                 jax.ShapeDtypeStruct((bsz, nblk, d, tm), BF16),
                 jax.ShapeDtypeStruct((bsz, nh * AUG_ROWS, seq), BF16),
                 jax.ShapeDtypeStruct((m, (nh // 2) * LANES), BF16)]
    return pl.pallas_call(
        functools.partial(_fox_proj_kernel, steps_per_batch=nblk),
        grid=(m // tm,),
        in_specs=[pl.BlockSpec((tm, d), row), _resident(wk.shape), _resident(wqt.shape),
                  _resident(wvt.shape), _resident(wft.shape), _resident(bf_col.shape),
                  _resident(utri.shape)] + [_resident(a.shape) for a in place],
        out_specs=[pl.BlockSpec((tm, d), row),
                   pl.BlockSpec((1, d, tm), lambda i: (i // nblk, 0, i % nblk)),
                   pl.BlockSpec((1, 1, d, tm), lambda i: (i // nblk, i % nblk, 0, 0)),
                   pl.BlockSpec((1, nh * AUG_ROWS, tm), lambda i: (i // nblk, 0, i % nblk)),
                   pl.BlockSpec((tm, (nh // 2) * LANES), row)],
        out_shape=out_shape,
        scratch_shapes=[pltpu.VMEM((nh, 1), F32)],
        compiler_params=_params("arbitrary"),
        name="fox_proj",
    )(xf, wk, wqt, wvt, wft, bf_col, utri, *place)


def _fox_attn_kernel(qt_ref, qta_ref, k_ref, ka_ref, vt_ref, o_ref, m_scr, acc_scr,
                     sa_scr, sb_scr, ma_scr, mb_scr, *, blk):
    qi = pl.program_id(2)
    hd = ATTN_HEAD_DIM
    qt2 = qt_ref[0]
    zhead = jnp.zeros((hd, blk), BF16)
    zpad = jnp.zeros((LANES - AUG_ROWS, blk), BF16)
    qts = [jnp.concatenate([qt2[0:hd], zhead, qta_ref[0, 0:AUG_ROWS], zpad], axis=0),
           jnp.concatenate([zhead, qt2[hd:2 * hd], qta_ref[0, AUG_ROWS:2 * AUG_ROWS], zpad], axis=0)]
    ones_rows = jnp.ones((ONES_ROWS, blk), BF16)
    key_id = lax.broadcasted_iota(jnp.int32, (blk, blk), 0)
    qry_id = lax.broadcasted_iota(jnp.int32, (blk, blk), 1)
    m_scr[...] = jnp.full_like(m_scr, MASK_VALUE)
    acc_scr[...] = jnp.zeros_like(acc_scr)

    def keys(j):
        start = pl.multiple_of(j * blk, blk)
        return jnp.concatenate([k_ref[0, pl.ds(start, blk), :], ka_ref[0, pl.ds(start, blk), :]],
                               axis=1)

    def values(j):
        vt2 = vt_ref[0, j]
        return [jnp.concatenate([vt2[h * hd:(h + 1) * hd], ones_rows], axis=0) for h in range(2)]

    half = blk // 2

    def produce(kk, h, s_buf, m_buf, diagonal=False):
        if diagonal:
            s_buf[h, 0:half, :] = jnp.dot(kk[0:half], qts[h], preferred_element_type=F32)
            s_buf[h, half:blk, half:blk] = jnp.dot(kk[half:blk], qts[h][:, half:blk],
                                                  preferred_element_type=F32)
            return
        s = jnp.dot(kk, qts[h], preferred_element_type=F32)
        s_buf[h] = s
        m_buf[h] = jnp.max(s, axis=0, keepdims=True)

    def consume(vt, h, s_buf, m_buf):
        s = s_buf[h]
        m_old = m_scr[h]
        m_new = jnp.maximum(m_old, m_buf[h])
        p = jnp.exp2(s - m_new).astype(BF16)
        alpha = jnp.exp2(m_old - m_new)
        acc_scr[h] = alpha * acc_scr[h] + jnp.dot(vt[h], p, preferred_element_type=F32)
        m_scr[h] = m_new

    def consume_diagonal(vt, h, s_buf):
        tri = (lax.broadcasted_iota(jnp.int32, (half, half), 0)
               <= lax.broadcasted_iota(jnp.int32, (half, half), 1))
        early, late = slice(0, half), slice(half, blk)
        for qs in (early, late):
            s_tri = jnp.where(tri, s_buf[h, qs, qs], MASK_VALUE)
            m_blk = jnp.max(s_tri, axis=0, keepdims=True)
            if qs is late:
                s_full = s_buf[h, early, late]
                m_blk = jnp.maximum(m_blk, jnp.max(s_full, axis=0, keepdims=True))
            m_old = m_scr[h, :, qs]
            m_new = jnp.maximum(m_old, m_blk)
            p = jnp.exp2(s_tri - m_new).astype(BF16)
            v_t = vt[h][:, qs]
            if qs is late:
                p = jnp.concatenate([jnp.exp2(s_full - m_new).astype(BF16), p], axis=0)
                v_t = vt[h]
            acc_scr[h, :, qs] = (jnp.exp2(m_old - m_new) * acc_scr[h, :, qs]
                                 + jnp.dot(v_t, p, preferred_element_type=F32))
            m_scr[h, :, qs] = m_new

    kk0 = keys(0)
    for h in range(2):
        produce(kk0, h, sa_scr, ma_scr)

    bufs = ((sa_scr, ma_scr), (sb_scr, mb_scr))

    def advance(j0, nsteps, ends_on_diagonal=False):
        for i in range(nsteps):
            kk, vt = keys(j0 + i + 1), values(j0 + i)
            for h in range(2):
                produce(kk, h, *bufs[(i + 1) % 2], diagonal=ends_on_diagonal and i == nsteps - 1)
                consume(vt, h, *bufs[i % 2])

    def body(t, carry):
        advance(ATTN_UNROLL * t, ATTN_UNROLL)
        return carry

    lax.fori_loop(0, qi // ATTN_UNROLL, body, 0)

    for rem in range(ATTN_UNROLL):
        @pl.when(qi % ATTN_UNROLL == rem)
        def _(rem=rem):
            advance(qi - rem, rem, ends_on_diagonal=True)
            vt = values(qi)
            for h in range(2):
                consume_diagonal(vt, h, bufs[rem % 2][0])

    outs = []
    for h in range(2):
        acc = acc_scr[h]
        outs.append(acc[0:hd] * (1.0 / acc[hd:hd + 1]))
    o_ref[0] = jnp.concatenate(outs, axis=0).astype(BF16)


def _fox_attn(qt, qta, k, ka, vt, blk):
    b, d, s = qt.shape
    nhp = d // LANES
    nblk = s // blk
    return pl.pallas_call(
        functools.partial(_fox_attn_kernel, blk=blk),
        grid=(b, nhp, nblk),
        in_specs=[
            pl.BlockSpec((1, LANES, blk), lambda bi, hp, qi: (bi, hp, qi)),
            pl.BlockSpec((1, 2 * AUG_ROWS, blk), lambda bi, hp, qi: (bi, hp, qi)),
            pl.BlockSpec((1, s, LANES), lambda bi, hp, qi: (bi, 0, hp)),
            pl.BlockSpec((1, s, LANES), lambda bi, hp, qi: (bi, 0, hp)),
            pl.BlockSpec((1, nblk, LANES, blk), lambda bi, hp, qi: (bi, 0, hp, 0)),
        ],
        out_specs=pl.BlockSpec((1, LANES, blk), lambda bi, hp, qi: (bi, hp, qi)),
        out_shape=jax.ShapeDtypeStruct((b, d, s), BF16),
        scratch_shapes=[pltpu.VMEM((2, 1, blk), F32),
                        pltpu.VMEM((2, ATTN_HEAD_DIM + ONES_ROWS, blk), F32),
                        pltpu.VMEM((2, blk, blk), F32), pltpu.VMEM((2, blk, blk), F32),
                        pltpu.VMEM((2, 1, blk), F32), pltpu.VMEM((2, 1, blk), F32)],
        compiler_params=_params("arbitrary", "arbitrary", "arbitrary"),
        name="fox_attn",
    )(qt, qta, k, ka, vt)


def _ffn_kernel(y_ref, wo_ref, x_ref, g1_ref, b1_ref, wu_ref, wg_ref, cw_ref, cb_ref, wd_ref,
                g_ref, b_ref, p_ref, wproj_ref, wgate_ref, bgate_ref, o_ref, gbuf0_ref, gbuf1_ref,
                carry_ref, *, chunks, steps_per_batch, alpha, y_feature_major):
    tm = x_ref.shape[0]
    halo = SUBLANES
    kconv = cw_ref.shape[0]
    gbufs = (gbuf0_ref, gbuf1_ref)
    halves = (slice(0, tm // 2), slice(tm // 2, tm))
    last = len(chunks) - 1

    @pl.when(pl.program_id(0) % steps_per_batch == 0)
    def _():
        carry_ref[...] = jnp.zeros_like(carry_ref)

    if y_feature_major:
        mixes = [lax.dot_general(y_ref[0, :, rs], wo_ref[...], (((0,), (0,)), ((), ())),
                                 preferred_element_type=F32) for rs in halves]
    else:
        mixes = [jnp.dot(y_ref[rs, :], wo_ref[...], preferred_element_type=F32) for rs in halves]
    xh = [_layer_norm(alpha * x_ref[rs, :] + mix, g1_ref[...], b1_ref[...])
          for rs, mix in zip(halves, mixes)]
    xbh = [v.astype(BF16) for v in xh]

    def stage(i, g):
        c0, cw = chunks[i]
        cs = slice(c0, c0 + cw)
        buf = gbufs[i % 2]
        buf[0:halo, 0:cw] = carry_ref[:, cs]
        buf[halo:halo + tm, 0:cw] = g
        carry_ref[:, cs] = g[tm - halo:tm, :]

    def up(i, lhs):
        c0, cw = chunks[i]
        cs = slice(c0, c0 + cw)
        return (jnp.dot(lhs, wu_ref[:, cs], preferred_element_type=F32),
                jnp.dot(lhs, wg_ref[:, cs], preferred_element_type=F32))

    def gated(i, u):
        c0, cw = chunks[i]
        cs = slice(c0, c0 + cw)
        buf = gbufs[i % 2]
        conv = cb_ref[:, cs] + cw_ref[kconv - 1:kconv, cs] * buf[halo:halo + tm, 0:cw]
        for k in range(kconv - 1):
            off = halo - (kconv - 1) + k
            conv = conv + cw_ref[k:k + 1, cs] * buf[off:off + tm, 0:cw]
        gelu = 0.5 * conv * (1.0 + lax.erf(conv * (1.0 / math.sqrt(2.0))))
        return (gelu * u).astype(BF16)

    first = [up(0, xbh[0]), up(0, xbh[1])]
    u_cur = jnp.concatenate([first[0][0], first[1][0]], axis=0)
    stage(0, jnp.concatenate([first[0][1], first[1][1]], axis=0))
    xb = jnp.concatenate(xbh, axis=0)

    acc = None
    for i in range(last):
        u_next, g_next = up(i + 1, xb)
        stage(i + 1, g_next)
        cs = slice(chunks[i][0], chunks[i][0] + chunks[i][1])
        part = jnp.dot(gated(i, u_cur), wd_ref[cs, :], preferred_element_type=F32)
        acc = part if acc is None else acc + part
        u_cur = u_next

    hid = gated(last, u_cur)
    cs = slice(chunks[last][0], chunks[last][0] + chunks[last][1])
    parts = [jnp.dot(hid[rs, :], wd_ref[cs, :], preferred_element_type=F32) for rs in halves]
    for rs, part, xin in zip(halves, parts, xh):
        total = part if acc is None else acc[rs, :] + part
        x2 = _layer_norm(alpha * xin + total, g_ref[...], b_ref[...])
        gate = _sigmoid(jnp.dot(x2.astype(BF16), wgate_ref[...], preferred_element_type=F32)
                        + bgate_ref[...])
        emb = jnp.dot(p_ref[rs, :].astype(BF16), wproj_ref[...], preferred_element_type=F32)
        o_ref[rs, :] = x2 + gate * emb


def _ffn_chunks(f):
    tile = 2 * LANES
    if f % tile:
        return ((0, f),)
    step = FFN_CHUNK_TILES * tile
    return tuple((c0, min(step, f - c0)) for c0 in range(0, f, step))


def _ffn(y, wo, xf, g1, b1, wu, wg, cw, cb, wd, g, b, pf, wproj, wgate, bgate, alpha, seq):
    m, d = xf.shape
    f = wu.shape[1]
    pdim = pf.shape[1]
    y_feature_major = y.ndim == 3
    kdim = y.shape[1]
    tm = min(ROW_TILE, seq)
    nblk = seq // tm
    chunks = _ffn_chunks(f)
    wmax = max(c[1] for c in chunks)
    row = lambda i: (i, 0)
    y_spec = (pl.BlockSpec((1, kdim, tm), lambda i: (i // nblk, 0, i % nblk)) if y_feature_major
              else pl.BlockSpec((tm, kdim), row))
    return pl.pallas_call(
        functools.partial(_ffn_kernel, chunks=chunks, steps_per_batch=nblk, alpha=alpha,
                          y_feature_major=y_feature_major),
        grid=(m // tm,),
        in_specs=[y_spec, _resident(wo.shape), pl.BlockSpec((tm, d), row),
                  _resident(g1.shape), _resident(b1.shape), _resident(wu.shape), _resident(wg.shape),
                  _resident(cw.shape), _resident(cb.shape), _resident(wd.shape),
                  _resident(g.shape), _resident(b.shape), pl.BlockSpec((tm, pdim), row),
                  _resident(wproj.shape), _resident(wgate.shape), _resident(bgate.shape)],
        out_specs=pl.BlockSpec((tm, d), row),
        out_shape=jax.ShapeDtypeStruct((m, d), F32),
        scratch_shapes=[pltpu.VMEM((tm + SUBLANES, wmax), F32), pltpu.VMEM((tm + SUBLANES, wmax), F32),
                        pltpu.VMEM((SUBLANES, f), F32)],
        compiler_params=_params("arbitrary"),
        name="conv_ffn",
    )(y, wo, xf, g1, b1, wu, wg, cw, cb, wd, g, b, pf, wproj, wgate, bgate)


def _ssd_proj_kernel(x_ref, wz_ref, wxbc_ref, wdt_ref, cw_ref, cb_ref, dtb_ref,
                     z_ref, xs_ref, bt_ref, c_ref, dt_ref, dtt_ref, buf0_ref, buf1_ref, carry_ref,
                     *, steps_per_batch, col_tile):
    tm = x_ref.shape[0]
    halo = SUBLANES
    kconv = cw_ref.shape[0]
    bufs = (buf0_ref, buf1_ref)

    @pl.when(pl.program_id(0) % steps_per_batch == 0)
    def _():
        carry_ref[...] = jnp.zeros_like(carry_ref)

    xb = x_ref[...].astype(BF16)
    d_inner = xs_ref.shape[1]
    gn = c_ref.shape[1]
    nconv = wxbc_ref.shape[1] // col_tile
    z_chunks = list(range(0, wz_ref.shape[1], col_tile))

    def project(i):
        cs = slice(i * col_tile, (i + 1) * col_tile)
        buf = bufs[i % 2]
        r = jnp.dot(xb, wxbc_ref[:, cs], preferred_element_type=F32)
        buf[0:halo, :] = carry_ref[:, cs]
        buf[halo:halo + tm, :] = r
        carry_ref[:, cs] = r[tm - halo:tm, :]

    def project_z():
        if z_chunks:
            zs = slice(z_chunks[0], z_chunks.pop(0) + col_tile)
            z_ref[:, zs] = jnp.dot(xb, wz_ref[:, zs], preferred_element_type=F32).astype(BF16)

    def conv_act(i):
        c0 = i * col_tile
        cs = slice(c0, c0 + col_tile)
        buf = bufs[i % 2]
        conv = cb_ref[:, cs] + cw_ref[kconv - 1:kconv, cs] * buf[halo:halo + tm, :]
        for k in range(kconv - 1):
            off = halo - (kconv - 1) + k
            conv = conv + cw_ref[k:k + 1, cs] * buf[off:off + tm, :]
        act = conv * _sigmoid(conv)
        if c0 < d_inner:
            xs_ref[:, cs] = act.astype(BF16)
        elif c0 < d_inner + gn:
            bt_ref[0, c0 - d_inner:c0 - d_inner + col_tile, :] = act.T.astype(BF16)
        else:
            c_ref[:, c0 - d_inner - gn:c0 - d_inner - gn + col_tile] = act.astype(BF16)

    project(0)
    for i in range(nconv):
        if i + 1 < nconv:
            project(i + 1)
        else:
            project_z()
        conv_act(i)
    while z_chunks:
        project_z()
    dt = _softplus(jnp.dot(xb, wdt_ref[...], preferred_element_type=F32) + dtb_ref[...])
    dt_ref[...] = dt
    dtt_ref[0] = dt.T[0:dtt_ref.shape[1], :]


def _ssd_proj(xf, wz, wxbc, wdt, cw, cb, dtb, bsz, seq, gn, nheads):
    m, d = xf.shape
    d_inner = wz.shape[1]
    tm = min(ROW_TILE, seq)
    nblk = seq // tm
    col_tile = min(SSD_COL_TILE, gn)
    row = lambda i: (i, 0)
    tposed = lambda i: (i // nblk, 0, i % nblk)
    out_shape = [jax.ShapeDtypeStruct((m, d_inner), BF16), jax.ShapeDtypeStruct((m, d_inner), BF16),
                 jax.ShapeDtypeStruct((bsz, gn, seq), BF16), jax.ShapeDtypeStruct((m, gn), BF16),
                 jax.ShapeDtypeStruct((m, LANES), F32),
                 jax.ShapeDtypeStruct((bsz, nheads, seq), F32)]
    return pl.pallas_call(
        functools.partial(_ssd_proj_kernel, steps_per_batch=seq // tm, col_tile=col_tile),
        grid=(m // tm,),
        in_specs=[pl.BlockSpec((tm, d), row), _resident(wz.shape), _resident(wxbc.shape),
                  _resident(wdt.shape), _resident(cw.shape), _resident(cb.shape),
                  _resident(dtb.shape)],
        out_specs=[pl.BlockSpec((tm, d_inner), row), pl.BlockSpec((tm, d_inner), row),
                   pl.BlockSpec((1, gn, tm), tposed), pl.BlockSpec((tm, gn), row),
                   pl.BlockSpec((tm, LANES), row), pl.BlockSpec((1, nheads, tm), tposed)],
        out_shape=out_shape,
        scratch_shapes=[pltpu.VMEM((tm + SUBLANES, col_tile), F32),
                        pltpu.VMEM((tm + SUBLANES, col_tile), F32),
                        pltpu.VMEM((SUBLANES, wxbc.shape[1]), F32)],
        compiler_params=_params("arbitrary"),
        name="ssd_proj",
    )(xf, wz, wxbc, wdt, cw, cb, dtb)


def _expand_heads(cols, lane_head):
    out = cols[0]
    for r in range(1, len(cols)):
        out = jnp.where(lane_head >= r, cols[r], out)
    return out


def _ssd_scan_kernel(xs_ref, bt_ref, c_ref, z_ref, dt_ref, dtt_ref, arow_ref, acol_ref,
                     dskip_ref, nw_ref, ltri_ref, utri_ref, y_ref, state_ref, *, heads_per_group):
    q = xs_ref.shape[1]
    n = SSM_STATE
    hp_ = SSM_HEAD_DIM
    gw = heads_per_group * hp_
    ngroups = xs_ref.shape[2] // gw

    @pl.when(pl.program_id(1) == 0)
    def _():
        state_ref[...] = jnp.zeros_like(state_ref)

    dt_col = dt_ref[0]
    dt_row = dtt_ref[0]
    nheads = dt_row.shape[0]
    a_col = dt_col * (arow_ref[...] * LOG2E)
    pc = jnp.dot(ltri_ref[...], jnp.concatenate(_split3(a_col), axis=1),
                 preferred_element_type=F32)
    acum_col = pc[:, 0:LANES] + pc[:, LANES:2 * LANES] + pc[:, 2 * LANES:3 * LANES]
    a_row = dt_row * (acol_ref[...] * LOG2E)
    pr = jnp.dot(jnp.concatenate(_split3(a_row), axis=0), utri_ref[...],
                 preferred_element_type=F32)
    acum_row = pr[0:nheads] + pr[nheads:2 * nheads] + pr[2 * nheads:3 * nheads]
    dec_row = jnp.exp2(acum_col[q - 1:q, :])
    w_row = jnp.exp2(acum_row[:, q - 1:q] - acum_row) * dt_row
    src_row = acum_row - jnp.log2(dt_row)

    nsub = q // LANES
    tri = (lax.broadcasted_iota(jnp.int32, (LANES, LANES), 0)
           >= lax.broadcasted_iota(jnp.int32, (LANES, LANES), 1))
    lane_head = lax.broadcasted_iota(jnp.int32, (1, gw), 1) // hp_
    low_half = lax.broadcasted_iota(jnp.int32, (1, LANES), 1) < hp_

    for g in range(ngroups):
        gs = slice(g * gw, (g + 1) * gw)
        ns = slice(g * n, (g + 1) * n)
        cg = c_ref[0, :, ns]
        btg = bt_ref[0, ns, :]
        btg_f = btg.astype(F32)
        xg_b = xs_ref[0, :, gs]
        heads = range(g * heads_per_group, (g + 1) * heads_per_group)
        cb = jnp.dot(cg, btg, preferred_element_type=F32)
        state = state_ref[g]
        y_inter = jnp.dot(cg, state.astype(BF16), preferred_element_type=F32)
        e_cols, y_intra, st_new = [], None, None
        for r, h in enumerate(heads):
            acol_b = jnp.broadcast_to(acum_col[:, h:h + 1], (q, LANES))
            e_cols.append(jnp.exp2(acol_b))
            rows = []
            for ti in range(nsub):
                tsl = slice(ti * LANES, (ti + 1) * LANES)
                tiles = []
                for si in range(nsub):
                    ssl = slice(si * LANES, (si + 1) * LANES)
                    if si > ti:
                        tiles.append(jnp.zeros((LANES, LANES), BF16))
                        continue
                    seg = acol_b[tsl] - src_row[h:h + 1, ssl]
                    if si == ti:
                        seg = jnp.where(tri, seg, -jnp.inf)
                    tiles.append((cb[tsl, ssl] * jnp.exp2(seg)).astype(BF16))
                rows.append(jnp.concatenate(tiles, axis=1))
            mh = jnp.concatenate(rows, axis=0)
            xh = jnp.where(lane_head == r, xg_b, jnp.zeros_like(xg_b))
            part = jnp.dot(mh, xh, preferred_element_type=F32)
            y_intra = part if y_intra is None else y_intra + part
            btw = (btg_f * w_row[h:h + 1, :]).astype(BF16)
            part = jnp.dot(btw, xh, preferred_element_type=F32)
            st_new = part if st_new is None else st_new + part
        e_x = jnp.concatenate([jnp.where(low_half, e_cols[2 * i], e_cols[2 * i + 1])
                               for i in range(gw // LANES)], axis=1)
        y_g = y_inter * e_x + y_intra
        d_x = _expand_heads([dec_row[:, h:h + 1] for h in heads], lane_head)
        state_ref[g] = state * d_x + st_new
        xg = xg_b.astype(F32)
        zg = z_ref[0, :, gs].astype(F32)
        yv = (y_g + dskip_ref[:, gs] * xg) * (zg * _sigmoid(zg))
        ms = jnp.mean(yv * yv, axis=-1, keepdims=True)
        y_ref[0, :, gs] = (yv * lax.rsqrt(ms + RMS_EPS) * nw_ref[:, gs]).astype(BF16)


def _ssd_scan(xs, bmt, cm, z, dt, dtt, arow, acol, dskip, nw, ltri, utri, q):
    b, s, d_inner = xs.shape
    gn = cm.shape[2]
    nheads = dtt.shape[1]
    ngroups = gn // SSM_STATE
    heads_per_group = nheads // ngroups
    gw = heads_per_group * SSM_HEAD_DIM
    blk = lambda w: pl.BlockSpec((1, q, w), lambda bi, ci: (bi, ci, 0))
    return pl.pallas_call(
        functools.partial(_ssd_scan_kernel, heads_per_group=heads_per_group),
        grid=(b, s // q),
        in_specs=[blk(d_inner), pl.BlockSpec((1, gn, q), lambda bi, ci: (bi, 0, ci)),
                  blk(gn), blk(d_inner), blk(LANES),
                  pl.BlockSpec((1, nheads, q), lambda bi, ci: (bi, 0, ci)),
                  _resident(arow.shape), _resident(acol.shape), _resident(dskip.shape),
                  _resident(nw.shape), _resident(ltri.shape), _resident(utri.shape)],
        out_specs=blk(d_inner),
        out_shape=jax.ShapeDtypeStruct((b, s, d_inner), BF16),
        scratch_shapes=[pltpu.VMEM((ngroups, SSM_STATE, gw), F32)],
        compiler_params=_params("arbitrary", "arbitrary"),
        name="ssd_scan",
    )(xs, bmt, cm, z, dt, dtt, arow, acol, dskip, nw, ltri, utri)


def _pad_cols(a, width):
    return jnp.pad(a, ((0, 0), (0, width - a.shape[1])))


def _fox_mixer(xf, bsz, seq, w_in, b_f):
    m, d = xf.shape
    nh = d // ATTN_HEAD_DIM
    scale = LOG2E / math.sqrt(ATTN_HEAD_DIM)
    wqt = (w_in[:, :d] * scale).T.astype(BF16)
    wk = w_in[:, d:2 * d].astype(BF16)
    wvt = w_in[:, 2 * d:3 * d].T.astype(BF16)
    wft = w_in[:, 3 * d:].T.astype(BF16)
    utri = jnp.triu(jnp.ones((CUMSUM_CHUNK, CUMSUM_CHUNK), F32)).astype(BF16)
    blk = min(ATTN_BLOCK, seq)
    k, qt, vt, qta, ka = _fox_proj(xf, wk, wqt, wvt, wft, b_f.reshape(nh, 1), utri, bsz, seq, blk)
    o = _fox_attn(qt, qta, k.reshape(bsz, seq, d), ka.reshape(bsz, seq, -1), vt, blk)
    return o


def _ssd_mixer(xf, bsz, seq, w_in, conv_w, conv_b, dt_bias, a_log, d_skip, norm_w):
    m, d = xf.shape
    nheads = dt_bias.shape[0]
    d_inner = nheads * SSM_HEAD_DIM
    gn = SSM_GROUPS * SSM_STATE
    wz = w_in[:, :d_inner].astype(BF16)
    wxbc = w_in[:, d_inner:2 * d_inner + 2 * gn].astype(BF16)
    wdt = _pad_cols(w_in[:, 2 * d_inner + 2 * gn:], LANES).astype(BF16)
    dtb = _pad_cols(dt_bias.reshape(1, nheads), LANES)
    z, xs, bmt, cm, dt, dtt = _ssd_proj(xf, wz, wxbc, wdt, conv_w, conv_b.reshape(1, -1), dtb,
                                        bsz, seq, gn, nheads)
    a = -jnp.exp(a_log.astype(F32))
    arow = _pad_cols(a.reshape(1, nheads), LANES)
    acol = a.reshape(nheads, 1)
    dskip = jnp.repeat(d_skip, SSM_HEAD_DIM).reshape(1, d_inner)
    q = min(SSD_CHUNK, seq)
    ltri = jnp.tril(jnp.ones((q, q), F32)).astype(BF16)
    utri = jnp.triu(jnp.ones((q, q), F32)).astype(BF16)
    y = _ssd_scan(xs.reshape(bsz, seq, d_inner), bmt, cm.reshape(bsz, seq, gn),
                  z.reshape(bsz, seq, d_inner), dt.reshape(bsz, seq, LANES), dtt, arow, acol,
                  dskip, norm_w.reshape(1, d_inner), ltri, utri, q)
    return y.reshape(m, d_inner)


def kernel(x, p, attn_w_in, attn_b_f, attn_w_out, ssm_w_in, ssm_conv_w, ssm_conv_b, ssm_dt_bias, ssm_A_log, ssm_D, ssm_norm_w, ssm_w_out, ln_mix_g, ln_mix_b, ffn_w_up, ffn_conv_w, ffn_conv_b, ffn_w_down, ln_ffn_g, ln_ffn_b, ple_w_proj, ple_w_gate, ple_b_gate):
    bsz, seq, d = x.shape
    depth = p.shape[0]
    n_mixers = 2
    alpha = (2 * depth) ** 0.25
    xf = x.reshape(bsz * seq, d)
    for i in range(depth):
        j = i // n_mixers
        if i % n_mixers == 0:
            y, w_out = _fox_mixer(xf, bsz, seq, attn_w_in[j], attn_b_f[j]), attn_w_out[j]
        else:
            y = _ssd_mixer(xf, bsz, seq, ssm_w_in[j], ssm_conv_w[j], ssm_conv_b[j],
                           ssm_dt_bias[j], ssm_A_log[j], ssm_D[j], ssm_norm_w[j])
            w_out = ssm_w_out[j]
        f = ffn_conv_w.shape[-1]
        xf = _ffn(y, w_out.astype(BF16), xf, ln_mix_g[i].reshape(1, d), ln_mix_b[i].reshape(1, d),
                  ffn_w_up[i][:, :f].astype(BF16), ffn_w_up[i][:, f:].astype(BF16),
                  ffn_conv_w[i], ffn_conv_b[i].reshape(1, f), ffn_w_down[i].astype(BF16),
                  ln_ffn_g[i].reshape(1, d), ln_ffn_b[i].reshape(1, d),
                  p[i].reshape(bsz * seq, -1), ple_w_proj[i].astype(BF16),
                  ple_w_gate[i].astype(BF16), ple_b_gate[i].reshape(1, d), alpha, seq)
    return xf.reshape(bsz, seq, d)
```

```python
import functools
import math

import jax
import jax.numpy as jnp
import numpy as np
from jax import lax
from jax.experimental import pallas as pl
from jax.experimental.pallas import tpu as pltpu

F32 = jnp.float32
BF16 = jnp.bfloat16

ATTN_HEAD_DIM = 64
SSM_HEAD_DIM = 64
SSM_GROUPS = 8
SSM_STATE = 128
LN_EPS = 1e-5
RMS_EPS = 1e-5

LANES = 128
SUBLANES = 8
VMEM_LIMIT_BYTES = 56 * 1024 * 1024

ROW_TILE = 512
ATTN_BLOCK = 512
ATTN_UNROLL = 4
SSD_COL_TILE = 1024
FFN_CHUNK_TILES = 4
SSD_CHUNK = 256
CUMSUM_CHUNK = 256
MASK_VALUE = -1e30
LOG2E = 1.4426950408889634
SPLIT_PIECES = 3
AUG_ROWS = 16
ONES_ROWS = 16


def _params(*sem):
    return pltpu.CompilerParams(dimension_semantics=sem, vmem_limit_bytes=VMEM_LIMIT_BYTES)


def _resident(shape):
    nd = len(shape)
    return pl.BlockSpec(shape, lambda *_: (0,) * nd, pipeline_mode=pl.Buffered(1))


def _split3(v):
    hi = v.astype(BF16)
    r1 = v - hi.astype(F32)
    mid = r1.astype(BF16)
    lo = (r1 - mid.astype(F32)).astype(BF16)
    return hi, mid, lo


def _softplus(v):
    return jnp.maximum(v, 0.0) + jnp.log1p(jnp.exp(-jnp.abs(v)))


def _sigmoid(v):
    return 1.0 / (1.0 + jnp.exp2(v * (-LOG2E)))


def _layer_norm(h, g, b):
    mu = jnp.mean(h, axis=-1, keepdims=True)
    d = h - mu
    var = jnp.mean(d * d, axis=-1, keepdims=True)
    return d * lax.rsqrt(var + LN_EPS) * g + b


_NT = (((1,), (1,)), ((), ()))


def _bias_placement(nh):
    npieces = SPLIT_PIECES
    place_q = np.zeros((nh * AUG_ROWS, LANES), np.float32)
    ones_q = np.zeros((nh * AUG_ROWS, 1), np.float32)
    place_k = np.zeros((LANES, (nh // 2) * LANES), np.float32)
    ones_k = np.zeros((1, (nh // 2) * LANES), np.float32)
    for h in range(nh):
        off = 2 * npieces * (h % 2)
        for p in range(npieces):
            ones_q[h * AUG_ROWS + off + p, 0] = 1.0
            place_q[h * AUG_ROWS + off + npieces + p, p * nh + h] = 1.0
            place_k[p * nh + h, (h // 2) * LANES + off + p] = -1.0
            ones_k[0, (h // 2) * LANES + off + npieces + p] = 1.0
    return (jnp.asarray(place_q, BF16), jnp.asarray(ones_q), jnp.asarray(place_k, BF16),
            jnp.asarray(ones_k))


def _fox_proj_kernel(x_ref, wk_ref, wqt_ref, wvt_ref, wft_ref, bf_ref, ut_ref, pq_ref, oq_ref,
                     pk_ref, ok_ref, k_ref, qt_ref, vt_ref, qta_ref, ka_ref, carry_ref, *,
                     steps_per_batch):
    tm, d = x_ref.shape

    @pl.when(pl.program_id(0) % steps_per_batch == 0)
    def _():
        carry_ref[...] = jnp.zeros_like(carry_ref)

    xb = x_ref[...].astype(BF16)
    fl = lax.dot_general(wft_ref[...], xb, _NT, preferred_element_type=F32) + bf_ref[...]
    k_ref[...] = jnp.dot(xb, wk_ref[...], preferred_element_type=F32).astype(BF16)
    logf = -_softplus(-fl) * LOG2E
    nh = logf.shape[0]
    parts = jnp.concatenate(_split3(logf), axis=0)
    sums = [jnp.dot(parts[:, j * CUMSUM_CHUNK:(j + 1) * CUMSUM_CHUNK], ut_ref[...],
                    preferred_element_type=F32) for j in range(tm // CUMSUM_CHUNK)]
    qt_ref[0] = lax.dot_general(wqt_ref[...], xb, _NT, preferred_element_type=F32).astype(BF16)
    carry = carry_ref[...]
    zrows = jnp.zeros((LANES - 3 * nh, CUMSUM_CHUNK), F32)
    stacks = []
    for pj in sums:
        cj = (pj[0:nh] + pj[nh:2 * nh] + pj[2 * nh:3 * nh]) + carry
        stacks.append(jnp.concatenate([cp.astype(F32) for cp in _split3(cj)] + [zrows], axis=0))
        carry = cj[:, CUMSUM_CHUNK - 1:CUMSUM_CHUNK]
    carry_ref[...] = carry
    stack = jnp.concatenate(stacks, axis=1)
    qta_ref[0] = (jnp.dot(pq_ref[...], stack.astype(BF16), preferred_element_type=F32)
                  + oq_ref[...]).astype(BF16)
    ka_ref[...] = (jnp.dot(stack.T.astype(BF16), pk_ref[...], preferred_element_type=F32)
                   + ok_ref[...]).astype(BF16)
    vt_ref[0, 0] = lax.dot_general(wvt_ref[...], xb, _NT, preferred_element_type=F32).astype(BF16)


def _fox_proj(xf, wk, wqt, wvt, wft, bf_col, utri, bsz, seq, tm):
    m, d = xf.shape
    nh = wft.shape[0]
    nblk = seq // tm
    row = lambda i: (i, 0)
    place = _bias_placement(nh)
    out_shape = [jax.ShapeDtypeStruct((m, d), BF16), jax.ShapeDtypeStruct((bsz, d, seq), BF16),
                 jax.ShapeDtypeStruct((bsz, nblk, d, tm), BF16),
                 jax.ShapeDtypeStruct((bsz, nh * AUG_ROWS, seq), BF16),
                 jax.ShapeDtypeStruct((m, (nh // 2) * LANES), BF16)]
    return pl.pallas_call(
        functools.partial(_fox_proj_kernel, steps_per_batch=nblk),
        grid=(m // tm,),
        in_specs=[pl.BlockSpec((tm, d), row), _resident(wk.shape), _resident(wqt.shape),
                  _resident(wvt.shape), _resident(wft.shape), _resident(bf_col.shape),
                  _resident(utri.shape)] + [_resident(a.shape) for a in place],
        out_specs=[pl.BlockSpec((tm, d), row),
                   pl.BlockSpec((1, d, tm), lambda i: (i // nblk, 0, i % nblk)),
                   pl.BlockSpec((1, 1, d, tm), lambda i: (i // nblk, i % nblk, 0, 0)),
                   pl.BlockSpec((1, nh * AUG_ROWS, tm), lambda i: (i // nblk, 0, i % nblk)),
                   pl.BlockSpec((tm, (nh // 2) * LANES), row)],
        out_shape=out_shape,
        scratch_shapes=[pltpu.VMEM((nh, 1), F32)],
        compiler_params=_params("arbitrary"),
        name="fox_proj",
    )(xf, wk, wqt, wvt, wft, bf_col, utri, *place)


def _fox_attn_kernel(qt_ref, qta_ref, k_ref, ka_ref, vt_ref, o_ref, m_scr, acc_scr,
                     sa_scr, sb_scr, ma_scr, mb_scr, *, blk):
    qi = pl.program_id(2)
    hd = ATTN_HEAD_DIM
    qt2 = qt_ref[0]
    zhead = jnp.zeros((hd, blk), BF16)
    zpad = jnp.zeros((LANES - AUG_ROWS, blk), BF16)
    qts = [jnp.concatenate([qt2[0:hd], zhead, qta_ref[0, 0:AUG_ROWS], zpad], axis=0),
           jnp.concatenate([zhead, qt2[hd:2 * hd], qta_ref[0, AUG_ROWS:2 * AUG_ROWS], zpad], axis=0)]
    ones_rows = jnp.ones((ONES_ROWS, blk), BF16)
    key_id = lax.broadcasted_iota(jnp.int32, (blk, blk), 0)
    qry_id = lax.broadcasted_iota(jnp.int32, (blk, blk), 1)
    m_scr[...] = jnp.full_like(m_scr, MASK_VALUE)
    acc_scr[...] = jnp.zeros_like(acc_scr)

    def keys(j):
        start = pl.multiple_of(j * blk, blk)
        return jnp.concatenate([k_ref[0, pl.ds(start, blk), :], ka_ref[0, pl.ds(start, blk), :]],
                               axis=1)

    def values(j):
        vt2 = vt_ref[0, j]
        return [jnp.concatenate([vt2[h * hd:(h + 1) * hd], ones_rows], axis=0) for h in range(2)]

    half = blk // 2

    def produce(kk, h, s_buf, m_buf, diagonal=False):
        if diagonal:
            s_buf[h, 0:half, :] = jnp.dot(kk[0:half], qts[h], preferred_element_type=F32)
            s_buf[h, half:blk, half:blk] = jnp.dot(kk[half:blk], qts[h][:, half:blk],
                                                  preferred_element_type=F32)
            return
        s = jnp.dot(kk, qts[h], preferred_element_type=F32)
        s_buf[h] = s
        m_buf[h] = jnp.max(s, axis=0, keepdims=True)

    def consume(vt, h, s_buf, m_buf):
        s = s_buf[h]
        m_old = m_scr[h]
        m_new = jnp.maximum(m_old, m_buf[h])
        p = jnp.exp2(s - m_new).astype(BF16)
        alpha = jnp.exp2(m_old - m_new)
        acc_scr[h] = alpha * acc_scr[h] + jnp.dot(vt[h], p, preferred_element_type=F32)
        m_scr[h] = m_new

    def consume_diagonal(vt, h, s_buf):
        tri = (lax.broadcasted_iota(jnp.int32, (half, half), 0)
               <= lax.broadcasted_iota(jnp.int32, (half, half), 1))
        early, late = slice(0, half), slice(half, blk)
        for qs in (early, late):
            s_tri = jnp.where(tri, s_buf[h, qs, qs], MASK_VALUE)
            m_blk = jnp.max(s_tri, axis=0, keepdims=True)
            if qs is late:
                s_full = s_buf[h, early, late]
                m_blk = jnp.maximum(m_blk, jnp.max(s_full, axis=0, keepdims=True))
            m_old = m_scr[h, :, qs]
            m_new = jnp.maximum(m_old, m_blk)
            p = jnp.exp2(s_tri - m_new).astype(BF16)
            v_t = vt[h][:, qs]
            if qs is late:
                p = jnp.concatenate([jnp.exp2(s_full - m_new).astype(BF16), p], axis=0)
                v_t = vt[h]
            acc_scr[h, :, qs] = (jnp.exp2(m_old - m_new) * acc_scr[h, :, qs]
                                 + jnp.dot(v_t, p, preferred_element_type=F32))
            m_scr[h, :, qs] = m_new

    kk0 = keys(0)
    for h in range(2):
        produce(kk0, h, sa_scr, ma_scr)

    bufs = ((sa_scr, ma_scr), (sb_scr, mb_scr))

    def advance(j0, nsteps, ends_on_diagonal=False):
        for i in range(nsteps):
            kk, vt = keys(j0 + i + 1), values(j0 + i)
            for h in range(2):
                produce(kk, h, *bufs[(i + 1) % 2], diagonal=ends_on_diagonal and i == nsteps - 1)
                consume(vt, h, *bufs[i % 2])

    def body(t, carry):
        advance(ATTN_UNROLL * t, ATTN_UNROLL)
        return carry

    lax.fori_loop(0, qi // ATTN_UNROLL, body, 0)

    for rem in range(ATTN_UNROLL):
        @pl.when(qi % ATTN_UNROLL == rem)
        def _(rem=rem):
            advance(qi - rem, rem, ends_on_diagonal=True)
            vt = values(qi)
            for h in range(2):
                consume_diagonal(vt, h, bufs[rem % 2][0])

    outs = []
    for h in range(2):
        acc = acc_scr[h]
        outs.append(acc[0:hd] * (1.0 / acc[hd:hd + 1]))
    o_ref[0] = jnp.concatenate(outs, axis=0).astype(BF16)


def _fox_attn(qt, qta, k, ka, vt, blk):
    b, d, s = qt.shape
    nhp = d // LANES
    nblk = s // blk
    return pl.pallas_call(
        functools.partial(_fox_attn_kernel, blk=blk),
        grid=(b, nhp, nblk),
        in_specs=[
            pl.BlockSpec((1, LANES, blk), lambda bi, hp, qi: (bi, hp, qi)),
            pl.BlockSpec((1, 2 * AUG_ROWS, blk), lambda bi, hp, qi: (bi, hp, qi)),
            pl.BlockSpec((1, s, LANES), lambda bi, hp, qi: (bi, 0, hp)),
            pl.BlockSpec((1, s, LANES), lambda bi, hp, qi: (bi, 0, hp)),
            pl.BlockSpec((1, nblk, LANES, blk), lambda bi, hp, qi: (bi, 0, hp, 0)),
        ],
        out_specs=pl.BlockSpec((1, LANES, blk), lambda bi, hp, qi: (bi, hp, qi)),
        out_shape=jax.ShapeDtypeStruct((b, d, s), BF16),
        scratch_shapes=[pltpu.VMEM((2, 1, blk), F32),
                        pltpu.VMEM((2, ATTN_HEAD_DIM + ONES_ROWS, blk), F32),
                        pltpu.VMEM((2, blk, blk), F32), pltpu.VMEM((2, blk, blk), F32),
                        pltpu.VMEM((2, 1, blk), F32), pltpu.VMEM((2, 1, blk), F32)],
        compiler_params=_params("arbitrary", "arbitrary", "arbitrary"),
        name="fox_attn",
    )(qt, qta, k, ka, vt)


def _ffn_kernel(y_ref, wo_ref, x_ref, g1_ref, b1_ref, wu_ref, wg_ref, cw_ref, cb_ref, wd_ref,
                g_ref, b_ref, p_ref, wproj_ref, wgate_ref, bgate_ref, o_ref, gbuf0_ref, gbuf1_ref,
                carry_ref, *, chunks, steps_per_batch, alpha, y_feature_major):
    tm = x_ref.shape[0]
    halo = SUBLANES
    kconv = cw_ref.shape[0]
    gbufs = (gbuf0_ref, gbuf1_ref)
    halves = (slice(0, tm // 2), slice(tm // 2, tm))
    last = len(chunks) - 1

    @pl.when(pl.program_id(0) % steps_per_batch == 0)
    def _():
        carry_ref[...] = jnp.zeros_like(carry_ref)

    if y_feature_major:
        mixes = [lax.dot_general(y_ref[0, :, rs], wo_ref[...], (((0,), (0,)), ((), ())),
                                 preferred_element_type=F32) for rs in halves]
    else:
        mixes = [jnp.dot(y_ref[rs, :], wo_ref[...], preferred_element_type=F32) for rs in halves]
    xh = [_layer_norm(alpha * x_ref[rs, :] + mix, g1_ref[...], b1_ref[...])
          for rs, mix in zip(halves, mixes)]
    xbh = [v.astype(BF16) for v in xh]

    def stage(i, g):
        c0, cw = chunks[i]
        cs = slice(c0, c0 + cw)
        buf = gbufs[i % 2]
        buf[0:halo, 0:cw] = carry_ref[:, cs]
        buf[halo:halo + tm, 0:cw] = g
        carry_ref[:, cs] = g[tm - halo:tm, :]

    def up(i, lhs):
        c0, cw = chunks[i]
        cs = slice(c0, c0 + cw)
        return (jnp.dot(lhs, wu_ref[:, cs], preferred_element_type=F32),
                jnp.dot(lhs, wg_ref[:, cs], preferred_element_type=F32))

    def gated(i, u):
        c0, cw = chunks[i]
        cs = slice(c0, c0 + cw)
        buf = gbufs[i % 2]
        conv = cb_ref[:, cs] + cw_ref[kconv - 1:kconv, cs] * buf[halo:halo + tm, 0:cw]
        for k in range(kconv - 1):
            off = halo - (kconv - 1) + k
            conv = conv + cw_ref[k:k + 1, cs] * buf[off:off + tm, 0:cw]
        gelu = 0.5 * conv * (1.0 + lax.erf(conv * (1.0 / math.sqrt(2.0))))
        return (gelu * u).astype(BF16)

    first = [up(0, xbh[0]), up(0, xbh[1])]
    u_cur = jnp.concatenate([first[0][0], first[1][0]], axis=0)
    stage(0, jnp.concatenate([first[0][1], first[1][1]], axis=0))
    xb = jnp.concatenate(xbh, axis=0)

    acc = None
    for i in range(last):
        u_next, g_next = up(i + 1, xb)
        stage(i + 1, g_next)
        cs = slice(chunks[i][0], chunks[i][0] + chunks[i][1])
        part = jnp.dot(gated(i, u_cur), wd_ref[cs, :], preferred_element_type=F32)
        acc = part if acc is None else acc + part
        u_cur = u_next

    hid = gated(last, u_cur)
    cs = slice(chunks[last][0], chunks[last][0] + chunks[last][1])
    parts = [jnp.dot(hid[rs, :], wd_ref[cs, :], preferred_element_type=F32) for rs in halves]
    for rs, part, xin in zip(halves, parts, xh):
        total = part if acc is None else acc[rs, :] + part
        x2 = _layer_norm(alpha * xin + total, g_ref[...], b_ref[...])
        gate = _sigmoid(jnp.dot(x2.astype(BF16), wgate_ref[...], preferred_element_type=F32)
                        + bgate_ref[...])
        emb = jnp.dot(p_ref[rs, :].astype(BF16), wproj_ref[...], preferred_element_type=F32)
        o_ref[rs, :] = x2 + gate * emb


def _ffn_chunks(f):
    tile = 2 * LANES
    if f % tile:
        return ((0, f),)
    step = FFN_CHUNK_TILES * tile
    return tuple((c0, min(step, f - c0)) for c0 in range(0, f, step))


def _ffn(y, wo, xf, g1, b1, wu, wg, cw, cb, wd, g, b, pf, wproj, wgate, bgate, alpha, seq):
    m, d = xf.shape
    f = wu.shape[1]
    pdim = pf.shape[1]
    y_feature_major = y.ndim == 3
    kdim = y.shape[1]
    tm = min(ROW_TILE, seq)
    nblk = seq // tm
    chunks = _ffn_chunks(f)
    wmax = max(c[1] for c in chunks)
    row = lambda i: (i, 0)
    y_spec = (pl.BlockSpec((1, kdim, tm), lambda i: (i // nblk, 0, i % nblk)) if y_feature_major
              else pl.BlockSpec((tm, kdim), row))
    return pl.pallas_call(
        functools.partial(_ffn_kernel, chunks=chunks, steps_per_batch=nblk, alpha=alpha,
                          y_feature_major=y_feature_major),
        grid=(m // tm,),
        in_specs=[y_spec, _resident(wo.shape), pl.BlockSpec((tm, d), row),
                  _resident(g1.shape), _resident(b1.shape), _resident(wu.shape), _resident(wg.shape),
                  _resident(cw.shape), _resident(cb.shape), _resident(wd.shape),
                  _resident(g.shape), _resident(b.shape), pl.BlockSpec((tm, pdim), row),
                  _resident(wproj.shape), _resident(wgate.shape), _resident(bgate.shape)],
        out_specs=pl.BlockSpec((tm, d), row),
        out_shape=jax.ShapeDtypeStruct((m, d), F32),
        scratch_shapes=[pltpu.VMEM((tm + SUBLANES, wmax), F32), pltpu.VMEM((tm + SUBLANES, wmax), F32),
                        pltpu.VMEM((SUBLANES, f), F32)],
        compiler_params=_params("arbitrary"),
        name="conv_ffn",
    )(y, wo, xf, g1, b1, wu, wg, cw, cb, wd, g, b, pf, wproj, wgate, bgate)


def _ssd_proj_kernel(x_ref, wz_ref, wxbc_ref, wdt_ref, cw_ref, cb_ref, dtb_ref,
                     z_ref, xs_ref, bt_ref, c_ref, dt_ref, dtt_ref, buf0_ref, buf1_ref, carry_ref,
                     *, steps_per_batch, col_tile):
    tm = x_ref.shape[0]
    halo = SUBLANES
    kconv = cw_ref.shape[0]
    bufs = (buf0_ref, buf1_ref)

    @pl.when(pl.program_id(0) % steps_per_batch == 0)
    def _():
        carry_ref[...] = jnp.zeros_like(carry_ref)

    xb = x_ref[...].astype(BF16)
    d_inner = xs_ref.shape[1]
    gn = c_ref.shape[1]
    nconv = wxbc_ref.shape[1] // col_tile
    z_chunks = list(range(0, wz_ref.shape[1], col_tile))

    def project(i):
        cs = slice(i * col_tile, (i + 1) * col_tile)
        buf = bufs[i % 2]
        r = jnp.dot(xb, wxbc_ref[:, cs], preferred_element_type=F32)
        buf[0:halo, :] = carry_ref[:, cs]
        buf[halo:halo + tm, :] = r
        carry_ref[:, cs] = r[tm - halo:tm, :]

    def project_z():
        if z_chunks:
            zs = slice(z_chunks[0], z_chunks.pop(0) + col_tile)
            z_ref[:, zs] = jnp.dot(xb, wz_ref[:, zs], preferred_element_type=F32).astype(BF16)

    def conv_act(i):
        c0 = i * col_tile
        cs = slice(c0, c0 + col_tile)
        buf = bufs[i % 2]
        conv = cb_ref[:, cs] + cw_ref[kconv - 1:kconv, cs] * buf[halo:halo + tm, :]
        for k in range(kconv - 1):
            off = halo - (kconv - 1) + k
            conv = conv + cw_ref[k:k + 1, cs] * buf[off:off + tm, :]
        act = conv * _sigmoid(conv)
        if c0 < d_inner:
            xs_ref[:, cs] = act.astype(BF16)
        elif c0 < d_inner + gn:
            bt_ref[0, c0 - d_inner:c0 - d_inner + col_tile, :] = act.T.astype(BF16)
        else:
            c_ref[:, c0 - d_inner - gn:c0 - d_inner - gn + col_tile] = act.astype(BF16)

    project(0)
    for i in range(nconv):
        if i + 1 < nconv:
            project(i + 1)
        else:
            project_z()
        conv_act(i)
    while z_chunks:
        project_z()
    dt = _softplus(jnp.dot(xb, wdt_ref[...], preferred_element_type=F32) + dtb_ref[...])
    dt_ref[...] = dt
    dtt_ref[0] = dt.T[0:dtt_ref.shape[1], :]


def _ssd_proj(xf, wz, wxbc, wdt, cw, cb, dtb, bsz, seq, gn, nheads):
    m, d = xf.shape
    d_inner = wz.shape[1]
    tm = min(ROW_TILE, seq)
    nblk = seq // tm
    col_tile = min(SSD_COL_TILE, gn)
    row = lambda i: (i, 0)
    tposed = lambda i: (i // nblk, 0, i % nblk)
    out_shape = [jax.ShapeDtypeStruct((m, d_inner), BF16), jax.ShapeDtypeStruct((m, d_inner), BF16),
                 jax.ShapeDtypeStruct((bsz, gn, seq), BF16), jax.ShapeDtypeStruct((m, gn), BF16),
                 jax.ShapeDtypeStruct((m, LANES), F32),
                 jax.ShapeDtypeStruct((bsz, nheads, seq), F32)]
    return pl.pallas_call(
        functools.partial(_ssd_proj_kernel, steps_per_batch=seq // tm, col_tile=col_tile),
        grid=(m // tm,),
        in_specs=[pl.BlockSpec((tm, d), row), _resident(wz.shape), _resident(wxbc.shape),
                  _resident(wdt.shape), _resident(cw.shape), _resident(cb.shape),
                  _resident(dtb.shape)],
        out_specs=[pl.BlockSpec((tm, d_inner), row), pl.BlockSpec((tm, d_inner), row),
                   pl.BlockSpec((1, gn, tm), tposed), pl.BlockSpec((tm, gn), row),
                   pl.BlockSpec((tm, LANES), row), pl.BlockSpec((1, nheads, tm), tposed)],
        out_shape=out_shape,
        scratch_shapes=[pltpu.VMEM((tm + SUBLANES, col_tile), F32),
                        pltpu.VMEM((tm + SUBLANES, col_tile), F32),
                        pltpu.VMEM((SUBLANES, wxbc.shape[1]), F32)],
        compiler_params=_params("arbitrary"),
        name="ssd_proj",
    )(xf, wz, wxbc, wdt, cw, cb, dtb)


def _expand_heads(cols, lane_head):
    out = cols[0]
    for r in range(1, len(cols)):
        out = jnp.where(lane_head >= r, cols[r], out)
    return out


def _ssd_scan_kernel(xs_ref, bt_ref, c_ref, z_ref, dt_ref, dtt_ref, arow_ref, acol_ref,
                     dskip_ref, nw_ref, ltri_ref, utri_ref, y_ref, state_ref, *, heads_per_group):
    q = xs_ref.shape[1]
    n = SSM_STATE
    hp_ = SSM_HEAD_DIM
    gw = heads_per_group * hp_
    ngroups = xs_ref.shape[2] // gw

    @pl.when(pl.program_id(1) == 0)
    def _():
        state_ref[...] = jnp.zeros_like(state_ref)

    dt_col = dt_ref[0]
    dt_row = dtt_ref[0]
    nheads = dt_row.shape[0]
    a_col = dt_col * (arow_ref[...] * LOG2E)
    pc = jnp.dot(ltri_ref[...], jnp.concatenate(_split3(a_col), axis=1),
                 preferred_element_type=F32)
    acum_col = pc[:, 0:LANES] + pc[:, LANES:2 * LANES] + pc[:, 2 * LANES:3 * LANES]
    a_row = dt_row * (acol_ref[...] * LOG2E)
    pr = jnp.dot(jnp.concatenate(_split3(a_row), axis=0), utri_ref[...],
                 preferred_element_type=F32)
    acum_row = pr[0:nheads] + pr[nheads:2 * nheads] + pr[2 * nheads:3 * nheads]
    dec_row = jnp.exp2(acum_col[q - 1:q, :])
    w_row = jnp.exp2(acum_row[:, q - 1:q] - acum_row) * dt_row
    src_row = acum_row - jnp.log2(dt_row)

    nsub = q // LANES
    tri = (lax.broadcasted_iota(jnp.int32, (LANES, LANES), 0)
           >= lax.broadcasted_iota(jnp.int32, (LANES, LANES), 1))
    lane_head = lax.broadcasted_iota(jnp.int32, (1, gw), 1) // hp_
    low_half = lax.broadcasted_iota(jnp.int32, (1, LANES), 1) < hp_

    for g in range(ngroups):
        gs = slice(g * gw, (g + 1) * gw)
        ns = slice(g * n, (g + 1) * n)
        cg = c_ref[0, :, ns]
        btg = bt_ref[0, ns, :]
        btg_f = btg.astype(F32)
        xg_b = xs_ref[0, :, gs]
        heads = range(g * heads_per_group, (g + 1) * heads_per_group)
        cb = jnp.dot(cg, btg, preferred_element_type=F32).astype(BF16)
        state = state_ref[g]
        y_inter = jnp.dot(cg, state.astype(BF16), preferred_element_type=F32)
        e_cols, y_intra, st_new = [], None, None
        for r, h in enumerate(heads):
            acol_b = jnp.broadcast_to(acum_col[:, h:h + 1], (q, LANES))
            e_cols.append(jnp.exp2(acol_b))
            rows = []
            for ti in range(nsub):
                tsl = slice(ti * LANES, (ti + 1) * LANES)
                tiles = []
                for si in range(nsub):
                    ssl = slice(si * LANES, (si + 1) * LANES)
                    if si > ti:
                        tiles.append(jnp.zeros((LANES, LANES), BF16))
                        continue
                    seg = acol_b[tsl] - src_row[h:h + 1, ssl]
                    if si == ti:
                        seg = jnp.where(tri, seg, -jnp.inf)
                    tiles.append(cb[tsl, ssl] * jnp.exp2(seg).astype(BF16))
                rows.append(jnp.concatenate(tiles, axis=1))
            mh = jnp.concatenate(rows, axis=0)
            xh = jnp.where(lane_head == r, xg_b, jnp.zeros_like(xg_b))
            part = jnp.dot(mh, xh, preferred_element_type=F32)
            y_intra = part if y_intra is None else y_intra + part
            btw = (btg_f * w_row[h:h + 1, :]).astype(BF16)
            part = jnp.dot(btw, xh, preferred_element_type=F32)
            st_new = part if st_new is None else st_new + part
        e_x = jnp.concatenate([jnp.where(low_half, e_cols[2 * i], e_cols[2 * i + 1])
                               for i in range(gw // LANES)], axis=1)
        y_g = y_inter * e_x + y_intra
        d_x = _expand_heads([dec_row[:, h:h + 1] for h in heads], lane_head)
        state_ref[g] = state * d_x + st_new
        xg = xg_b.astype(F32)
        zg = z_ref[0, :, gs].astype(F32)
        yv = (y_g + dskip_ref[:, gs] * xg) * (zg * _sigmoid(zg))
        ms = jnp.mean(yv * yv, axis=-1, keepdims=True)
        y_ref[0, :, gs] = (yv * lax.rsqrt(ms + RMS_EPS) * nw_ref[:, gs]).astype(BF16)


def _ssd_scan(xs, bmt, cm, z, dt, dtt, arow, acol, dskip, nw, ltri, utri, q):
    b, s, d_inner = xs.shape
    gn = cm.shape[2]
    nheads = dtt.shape[1]
    ngroups = gn // SSM_STATE
    heads_per_group = nheads // ngroups
    gw = heads_per_group * SSM_HEAD_DIM
    blk = lambda w: pl.BlockSpec((1, q, w), lambda bi, ci: (bi, ci, 0))
    return pl.pallas_call(
        functools.partial(_ssd_scan_kernel, heads_per_group=heads_per_group),
        grid=(b, s // q),
        in_specs=[blk(d_inner), pl.BlockSpec((1, gn, q), lambda bi, ci: (bi, 0, ci)),
                  blk(gn), blk(d_inner), blk(LANES),
                  pl.BlockSpec((1, nheads, q), lambda bi, ci: (bi, 0, ci)),
                  _resident(arow.shape), _resident(acol.shape), _resident(dskip.shape),
                  _resident(nw.shape), _resident(ltri.shape), _resident(utri.shape)],
        out_specs=blk(d_inner),
        out_shape=jax.ShapeDtypeStruct((b, s, d_inner), BF16),
        scratch_shapes=[pltpu.VMEM((ngroups, SSM_STATE, gw), F32)],
        compiler_params=_params("arbitrary", "arbitrary"),
        name="ssd_scan",
    )(xs, bmt, cm, z, dt, dtt, arow, acol, dskip, nw, ltri, utri)


def _pad_cols(a, width):
    return jnp.pad(a, ((0, 0), (0, width - a.shape[1])))


def _fox_mixer(xf, bsz, seq, w_in, b_f):
    m, d = xf.shape
    nh = d // ATTN_HEAD_DIM
    scale = LOG2E / math.sqrt(ATTN_HEAD_DIM)
    wqt = (w_in[:, :d] * scale).T.astype(BF16)
    wk = w_in[:, d:2 * d].astype(BF16)
    wvt = w_in[:, 2 * d:3 * d].T.astype(BF16)
    wft = w_in[:, 3 * d:].T.astype(BF16)
    utri = jnp.triu(jnp.ones((CUMSUM_CHUNK, CUMSUM_CHUNK), F32)).astype(BF16)
    blk = min(ATTN_BLOCK, seq)
    k, qt, vt, qta, ka = _fox_proj(xf, wk, wqt, wvt, wft, b_f.reshape(nh, 1), utri, bsz, seq, blk)
    o = _fox_attn(qt, qta, k.reshape(bsz, seq, d), ka.reshape(bsz, seq, -1), vt, blk)
    return o


def _ssd_mixer(xf, bsz, seq, w_in, conv_w, conv_b, dt_bias, a_log, d_skip, norm_w):
    m, d = xf.shape
    nheads = dt_bias.shape[0]
    d_inner = nheads * SSM_HEAD_DIM
    gn = SSM_GROUPS * SSM_STATE
    wz = w_in[:, :d_inner].astype(BF16)
    wxbc = w_in[:, d_inner:2 * d_inner + 2 * gn].astype(BF16)
    wdt = _pad_cols(w_in[:, 2 * d_inner + 2 * gn:], LANES).astype(BF16)
    dtb = _pad_cols(dt_bias.reshape(1, nheads), LANES)
    z, xs, bmt, cm, dt, dtt = _ssd_proj(xf, wz, wxbc, wdt, conv_w, conv_b.reshape(1, -1), dtb,
                                        bsz, seq, gn, nheads)
    a = -jnp.exp(a_log.astype(F32))
    arow = _pad_cols(a.reshape(1, nheads), LANES)
    acol = a.reshape(nheads, 1)
    dskip = jnp.repeat(d_skip, SSM_HEAD_DIM).reshape(1, d_inner)
    q = min(SSD_CHUNK, seq)
    ltri = jnp.tril(jnp.ones((q, q), F32)).astype(BF16)
    utri = jnp.triu(jnp.ones((q, q), F32)).astype(BF16)
    y = _ssd_scan(xs.reshape(bsz, seq, d_inner), bmt, cm.reshape(bsz, seq, gn),
                  z.reshape(bsz, seq, d_inner), dt.reshape(bsz, seq, LANES), dtt, arow, acol,
                  dskip, norm_w.reshape(1, d_inner), ltri, utri, q)
    return y.reshape(m, d_inner)


def kernel(x, p, attn_w_in, attn_b_f, attn_w_out, ssm_w_in, ssm_conv_w, ssm_conv_b, ssm_dt_bias, ssm_A_log, ssm_D, ssm_norm_w, ssm_w_out, ln_mix_g, ln_mix_b, ffn_w_up, ffn_conv_w, ffn_conv_b, ffn_w_down, ln_ffn_g, ln_ffn_b, ple_w_proj, ple_w_gate, ple_b_gate):
    bsz, seq, d = x.shape
    depth = p.shape[0]
    n_mixers = 2
    alpha = (2 * depth) ** 0.25
    xf = x.reshape(bsz * seq, d)
    for i in range(depth):
        j = i // n_mixers
        if i % n_mixers == 0:
            y, w_out = _fox_mixer(xf, bsz, seq, attn_w_in[j], attn_b_f[j]), attn_w_out[j]
        else:
            y = _ssd_mixer(xf, bsz, seq, ssm_w_in[j], ssm_conv_w[j], ssm_conv_b[j],
                           ssm_dt_bias[j], ssm_A_log[j], ssm_D[j], ssm_norm_w[j])
            w_out = ssm_w_out[j]
        f = ffn_conv_w.shape[-1]
        xf = _ffn(y, w_out.astype(BF16), xf, ln_mix_g[i].reshape(1, d), ln_mix_b[i].reshape(1, d),
                  ffn_w_up[i][:, :f].astype(BF16), ffn_w_up[i][:, f:].astype(BF16),
                  ffn_conv_w[i], ffn_conv_b[i].reshape(1, f), ffn_w_down[i].astype(BF16),
                  ln_ffn_g[i].reshape(1, d), ln_ffn_b[i].reshape(1, d),
                  p[i].reshape(bsz * seq, -1), ple_w_proj[i].astype(BF16),
                  ple_w_gate[i].astype(BF16), ple_b_gate[i].reshape(1, d), alpha, seq)
    return xf.reshape(bsz, seq, d)
```

```python
import functools
import math

import jax
import jax.numpy as jnp
import numpy as np
from jax import lax
from jax.experimental import pallas as pl
from jax.experimental.pallas import tpu as pltpu

F32 = jnp.float32
BF16 = jnp.bfloat16

ATTN_HEAD_DIM = 64
SSM_HEAD_DIM = 64
SSM_GROUPS = 8
SSM_STATE = 128
LN_EPS = 1e-5
RMS_EPS = 1e-5

LANES = 128
SUBLANES = 8
VMEM_LIMIT_BYTES = 56 * 1024 * 1024

ROW_TILE = 512
ATTN_BLOCK = 512
ATTN_UNROLL = 8
SSD_COL_TILE = 1024
FFN_CHUNK_TILES = 4
SSD_CHUNK = 256
CUMSUM_CHUNK = 256
MASK_VALUE = -1e30
LOG2E = 1.4426950408889634
SPLIT_PIECES = 3
AUG_ROWS = 16
ONES_ROWS = 16


def _params(*sem):
    return pltpu.CompilerParams(dimension_semantics=sem, vmem_limit_bytes=VMEM_LIMIT_BYTES)


def _resident(shape):
    nd = len(shape)
    return pl.BlockSpec(shape, lambda *_: (0,) * nd, pipeline_mode=pl.Buffered(1))


def _split3(v):
    hi = v.astype(BF16)
    r1 = v - hi.astype(F32)
    mid = r1.astype(BF16)
    lo = (r1 - mid.astype(F32)).astype(BF16)
    return hi, mid, lo


def _softplus(v):
    return jnp.maximum(v, 0.0) + jnp.log1p(jnp.exp(-jnp.abs(v)))


def _sigmoid(v):
    return 1.0 / (1.0 + jnp.exp2(v * (-LOG2E)))


def _layer_norm(h, g, b):
    mu = jnp.mean(h, axis=-1, keepdims=True)
    d = h - mu
    var = jnp.mean(d * d, axis=-1, keepdims=True)
    return d * lax.rsqrt(var + LN_EPS) * g + b


_NT = (((1,), (1,)), ((), ()))


def _bias_placement(nh):
    npieces = SPLIT_PIECES
    place_q = np.zeros((nh * AUG_ROWS, LANES), np.float32)
    ones_q = np.zeros((nh * AUG_ROWS, 1), np.float32)
    place_k = np.zeros((LANES, (nh // 2) * LANES), np.float32)
    ones_k = np.zeros((1, (nh // 2) * LANES), np.float32)
    for h in range(nh):
        off = 2 * npieces * (h % 2)
        for p in range(npieces):
            ones_q[h * AUG_ROWS + off + p, 0] = 1.0
            place_q[h * AUG_ROWS + off + npieces + p, p * nh + h] = 1.0
            place_k[p * nh + h, (h // 2) * LANES + off + p] = -1.0
            ones_k[0, (h // 2) * LANES + off + npieces + p] = 1.0
    return (jnp.asarray(place_q, BF16), jnp.asarray(ones_q), jnp.asarray(place_k, BF16),
            jnp.asarray(ones_k))


def _fox_proj_kernel(x_ref, wk_ref, wqt_ref, wvt_ref, wft_ref, bf_ref, ut_ref, pq_ref, oq_ref,
                     pk_ref, ok_ref, k_ref, qt_ref, vt_ref, qta_ref, ka_ref, carry_ref, *,
                     steps_per_batch):
    tm, d = x_ref.shape

    @pl.when(pl.program_id(0) % steps_per_batch == 0)
    def _():
        carry_ref[...] = jnp.zeros_like(carry_ref)

    xb = x_ref[...].astype(BF16)
    fl = lax.dot_general(wft_ref[...], xb, _NT, preferred_element_type=F32) + bf_ref[...]
    k_ref[...] = jnp.dot(xb, wk_ref[...], preferred_element_type=F32).astype(BF16)
    logf = -_softplus(-fl) * LOG2E
    nh = logf.shape[0]
    parts = jnp.concatenate(_split3(logf), axis=0)
    sums = [jnp.dot(parts[:, j * CUMSUM_CHUNK:(j + 1) * CUMSUM_CHUNK], ut_ref[...],
                    preferred_element_type=F32) for j in range(tm // CUMSUM_CHUNK)]
    qt_ref[0] = lax.dot_general(wqt_ref[...], xb, _NT, preferred_element_type=F32).astype(BF16)
    carry = carry_ref[...]
    zrows = jnp.zeros((LANES - 3 * nh, CUMSUM_CHUNK), F32)
    stacks = []
    for pj in sums:
        cj = (pj[0:nh] + pj[nh:2 * nh] + pj[2 * nh:3 * nh]) + carry
        stacks.append(jnp.concatenate([cp.astype(F32) for cp in _split3(cj)] + [zrows], axis=0))
        carry = cj[:, CUMSUM_CHUNK - 1:CUMSUM_CHUNK]
    carry_ref[...] = carry
    stack = jnp.concatenate(stacks, axis=1)
    qta_ref[0] = (jnp.dot(pq_ref[...], stack.astype(BF16), preferred_element_type=F32)
                  + oq_ref[...]).astype(BF16)
    ka_ref[...] = (jnp.dot(stack.T.astype(BF16), pk_ref[...], preferred_element_type=F32)
                   + ok_ref[...]).astype(BF16)
    vt_ref[0, 0] = lax.dot_general(wvt_ref[...], xb, _NT, preferred_element_type=F32).astype(BF16)


def _fox_proj(xf, wk, wqt, wvt, wft, bf_col, utri, bsz, seq, tm):
    m, d = xf.shape
    nh = wft.shape[0]
    nblk = seq // tm
    row = lambda i: (i, 0)
    place = _bias_placement(nh)
    out_shape = [jax.ShapeDtypeStruct((m, d), BF16), jax.ShapeDtypeStruct((bsz, d, seq), BF16),
                 jax.ShapeDtypeStruct((bsz, nblk, d, tm), BF16),
                 jax.ShapeDtypeStruct((bsz, nh * AUG_ROWS, seq), BF16),
                 jax.ShapeDtypeStruct((m, (nh // 2) * LANES), BF16)]
    return pl.pallas_call(
        functools.partial(_fox_proj_kernel, steps_per_batch=nblk),
        grid=(m // tm,),
        in_specs=[pl.BlockSpec((tm, d), row), _resident(wk.shape), _resident(wqt.shape),
                  _resident(wvt.shape), _resident(wft.shape), _resident(bf_col.shape),
                  _resident(utri.shape)] + [_resident(a.shape) for a in place],
        out_specs=[pl.BlockSpec((tm, d), row),
                   pl.BlockSpec((1, d, tm), lambda i: (i // nblk, 0, i % nblk)),
                   pl.BlockSpec((1, 1, d, tm), lambda i: (i // nblk, i % nblk, 0, 0)),
                   pl.BlockSpec((1, nh * AUG_ROWS, tm), lambda i: (i // nblk, 0, i % nblk)),
                   pl.BlockSpec((tm, (nh // 2) * LANES), row)],
        out_shape=out_shape,
        scratch_shapes=[pltpu.VMEM((nh, 1), F32)],
        compiler_params=_params("arbitrary"),
        name="fox_proj",
    )(xf, wk, wqt, wvt, wft, bf_col, utri, *place)


def _fox_attn_kernel(qt_ref, qta_ref, k_ref, ka_ref, vt_ref, o_ref, m_scr, acc_scr,
                     sa_scr, sb_scr, ma_scr, mb_scr, *, blk):
    qi = pl.program_id(2)
    hd = ATTN_HEAD_DIM
    qt2 = qt_ref[0]
    zhead = jnp.zeros((hd, blk), BF16)
    zpad = jnp.zeros((LANES - AUG_ROWS, blk), BF16)
    qts = [jnp.concatenate([qt2[0:hd], zhead, qta_ref[0, 0:AUG_ROWS], zpad], axis=0),
           jnp.concatenate([zhead, qt2[hd:2 * hd], qta_ref[0, AUG_ROWS:2 * AUG_ROWS], zpad], axis=0)]
    ones_rows = jnp.ones((ONES_ROWS, blk), BF16)
    key_id = lax.broadcasted_iota(jnp.int32, (blk, blk), 0)
    qry_id = lax.broadcasted_iota(jnp.int32, (blk, blk), 1)
    m_scr[...] = jnp.full_like(m_scr, MASK_VALUE)
    acc_scr[...] = jnp.zeros_like(acc_scr)

    def keys(j):
        start = pl.multiple_of(j * blk, blk)
        return jnp.concatenate([k_ref[0, pl.ds(start, blk), :], ka_ref[0, pl.ds(start, blk), :]],
                               axis=1)

    def values(j):
        vt2 = vt_ref[0, j]
        return [jnp.concatenate([vt2[h * hd:(h + 1) * hd], ones_rows], axis=0) for h in range(2)]

    half = blk // 2

    def produce(kk, h, s_buf, m_buf, diagonal=False):
        if diagonal:
            s_buf[h, 0:half, :] = jnp.dot(kk[0:half], qts[h], preferred_element_type=F32)
            s_buf[h, half:blk, half:blk] = jnp.dot(kk[half:blk], qts[h][:, half:blk],
                                                  preferred_element_type=F32)
            return
        s = jnp.dot(kk, qts[h], preferred_element_type=F32)
        s_buf[h] = s
        m_buf[h] = jnp.max(s, axis=0, keepdims=True)

    def consume(vt, h, s_buf, m_buf):
        s = s_buf[h]
        m_old = m_scr[h]
        m_new = jnp.maximum(m_old, m_buf[h])
        p = jnp.exp2(s - m_new).astype(BF16)
        alpha = jnp.exp2(m_old - m_new)
        acc_scr[h] = alpha * acc_scr[h] + jnp.dot(vt[h], p, preferred_element_type=F32)
        m_scr[h] = m_new

    def consume_diagonal(vt, h, s_buf):
        tri = (lax.broadcasted_iota(jnp.int32, (half, half), 0)
               <= lax.broadcasted_iota(jnp.int32, (half, half), 1))
        early, late = slice(0, half), slice(half, blk)
        for qs in (early, late):
            s_tri = jnp.where(tri, s_buf[h, qs, qs], MASK_VALUE)
            m_blk = jnp.max(s_tri, axis=0, keepdims=True)
            if qs is late:
                s_full = s_buf[h, early, late]
                m_blk = jnp.maximum(m_blk, jnp.max(s_full, axis=0, keepdims=True))
            m_old = m_scr[h, :, qs]
            m_new = jnp.maximum(m_old, m_blk)
            p = jnp.exp2(s_tri - m_new).astype(BF16)
            v_t = vt[h][:, qs]
            if qs is late:
                p = jnp.concatenate([jnp.exp2(s_full - m_new).astype(BF16), p], axis=0)
                v_t = vt[h]
            acc_scr[h, :, qs] = (jnp.exp2(m_old - m_new) * acc_scr[h, :, qs]
                                 + jnp.dot(v_t, p, preferred_element_type=F32))
            m_scr[h, :, qs] = m_new

    kk0 = keys(0)
    for h in range(2):
        produce(kk0, h, sa_scr, ma_scr)

    bufs = ((sa_scr, ma_scr), (sb_scr, mb_scr))

    def advance(j0, nsteps, ends_on_diagonal=False):
        for i in range(nsteps):
            kk, vt = keys(j0 + i + 1), values(j0 + i)
            for h in range(2):
                produce(kk, h, *bufs[(i + 1) % 2], diagonal=ends_on_diagonal and i == nsteps - 1)
                consume(vt, h, *bufs[i % 2])

    def body(t, carry):
        advance(ATTN_UNROLL * t, ATTN_UNROLL)
        return carry

    lax.fori_loop(0, qi // ATTN_UNROLL, body, 0)

    for rem in range(ATTN_UNROLL):
        @pl.when(qi % ATTN_UNROLL == rem)
        def _(rem=rem):
            advance(qi - rem, rem, ends_on_diagonal=True)
            vt = values(qi)
            for h in range(2):
                consume_diagonal(vt, h, bufs[rem % 2][0])

    outs = []
    for h in range(2):
        acc = acc_scr[h]
        outs.append(acc[0:hd] * (1.0 / acc[hd:hd + 1]))
    o_ref[0] = jnp.concatenate(outs, axis=0).astype(BF16)


def _fox_attn(qt, qta, k, ka, vt, blk):
    b, d, s = qt.shape
    nhp = d // LANES
    nblk = s // blk
    return pl.pallas_call(
        functools.partial(_fox_attn_kernel, blk=blk),
        grid=(b, nhp, nblk),
        in_specs=[
            pl.BlockSpec((1, LANES, blk), lambda bi, hp, qi: (bi, hp, qi)),
            pl.BlockSpec((1, 2 * AUG_ROWS, blk), lambda bi, hp, qi: (bi, hp, qi)),
            pl.BlockSpec((1, s, LANES), lambda bi, hp, qi: (bi, 0, hp)),
            pl.BlockSpec((1, s, LANES), lambda bi, hp, qi: (bi, 0, hp)),
            pl.BlockSpec((1, nblk, LANES, blk), lambda bi, hp, qi: (bi, 0, hp, 0)),
        ],
        out_specs=pl.BlockSpec((1, LANES, blk), lambda bi, hp, qi: (bi, hp, qi)),
        out_shape=jax.ShapeDtypeStruct((b, d, s), BF16),
        scratch_shapes=[pltpu.VMEM((2, 1, blk), F32),
                        pltpu.VMEM((2, ATTN_HEAD_DIM + ONES_ROWS, blk), F32),
                        pltpu.VMEM((2, blk, blk), F32), pltpu.VMEM((2, blk, blk), F32),
                        pltpu.VMEM((2, 1, blk), F32), pltpu.VMEM((2, 1, blk), F32)],
        compiler_params=_params("arbitrary", "arbitrary", "arbitrary"),
        name="fox_attn",
    )(qt, qta, k, ka, vt)


def _ffn_kernel(y_ref, wo_ref, x_ref, g1_ref, b1_ref, wu_ref, wg_ref, cw_ref, cb_ref, wd_ref,
                g_ref, b_ref, p_ref, wproj_ref, wgate_ref, bgate_ref, o_ref, gbuf0_ref, gbuf1_ref,
                carry_ref, *, chunks, steps_per_batch, alpha, y_feature_major):
    tm = x_ref.shape[0]
    halo = SUBLANES
    kconv = cw_ref.shape[0]
    gbufs = (gbuf0_ref, gbuf1_ref)
    halves = (slice(0, tm // 2), slice(tm // 2, tm))
    last = len(chunks) - 1

    @pl.when(pl.program_id(0) % steps_per_batch == 0)
    def _():
        carry_ref[...] = jnp.zeros_like(carry_ref)

    if y_feature_major:
        mixes = [lax.dot_general(y_ref[0, :, rs], wo_ref[...], (((0,), (0,)), ((), ())),
                                 preferred_element_type=F32) for rs in halves]
    else:
        mixes = [jnp.dot(y_ref[rs, :], wo_ref[...], preferred_element_type=F32) for rs in halves]
    xh = [_layer_norm(alpha * x_ref[rs, :] + mix, g1_ref[...], b1_ref[...])
          for rs, mix in zip(halves, mixes)]
    xbh = [v.astype(BF16) for v in xh]

    def stage(i, g):
        c0, cw = chunks[i]
        cs = slice(c0, c0 + cw)
        buf = gbufs[i % 2]
        buf[0:halo, 0:cw] = carry_ref[:, cs]
        buf[halo:halo + tm, 0:cw] = g
        carry_ref[:, cs] = g[tm - halo:tm, :]

    def up(i, lhs):
        c0, cw = chunks[i]
        cs = slice(c0, c0 + cw)
        return (jnp.dot(lhs, wu_ref[:, cs], preferred_element_type=F32),
                jnp.dot(lhs, wg_ref[:, cs], preferred_element_type=F32))

    def gated(i, u):
        c0, cw = chunks[i]
        cs = slice(c0, c0 + cw)
        buf = gbufs[i % 2]
        conv = cb_ref[:, cs] + cw_ref[kconv - 1:kconv, cs] * buf[halo:halo + tm, 0:cw]
        for k in range(kconv - 1):
            off = halo - (kconv - 1) + k
            conv = conv + cw_ref[k:k + 1, cs] * buf[off:off + tm, 0:cw]
        gelu = 0.5 * conv * (1.0 + lax.erf(conv * (1.0 / math.sqrt(2.0))))
        return (gelu * u).astype(BF16)

    first = [up(0, xbh[0]), up(0, xbh[1])]
    u_cur = jnp.concatenate([first[0][0], first[1][0]], axis=0)
    stage(0, jnp.concatenate([first[0][1], first[1][1]], axis=0))
    xb = jnp.concatenate(xbh, axis=0)

    acc = None
    for i in range(last):
        u_next, g_next = up(i + 1, xb)
        stage(i + 1, g_next)
        cs = slice(chunks[i][0], chunks[i][0] + chunks[i][1])
        part = jnp.dot(gated(i, u_cur), wd_ref[cs, :], preferred_element_type=F32)
        acc = part if acc is None else acc + part
        u_cur = u_next

    hid = gated(last, u_cur)
    cs = slice(chunks[last][0], chunks[last][0] + chunks[last][1])
    parts = [jnp.dot(hid[rs, :], wd_ref[cs, :], preferred_element_type=F32) for rs in halves]
    for rs, part, xin in zip(halves, parts, xh):
        total = part if acc is None else acc[rs, :] + part
        x2 = _layer_norm(alpha * xin + total, g_ref[...], b_ref[...])
        gate = _sigmoid(jnp.dot(x2.astype(BF16), wgate_ref[...], preferred_element_type=F32)
                        + bgate_ref[...])
        emb = jnp.dot(p_ref[rs, :].astype(BF16), wproj_ref[...], preferred_element_type=F32)
        o_ref[rs, :] = x2 + gate * emb


def _ffn_chunks(f):
    tile = 2 * LANES
    if f % tile:
        return ((0, f),)
    step = FFN_CHUNK_TILES * tile
    return tuple((c0, min(step, f - c0)) for c0 in range(0, f, step))


def _ffn(y, wo, xf, g1, b1, wu, wg, cw, cb, wd, g, b, pf, wproj, wgate, bgate, alpha, seq):
    m, d = xf.shape
    f = wu.shape[1]
    pdim = pf.shape[1]
    y_feature_major = y.ndim == 3
    kdim = y.shape[1]
    tm = min(ROW_TILE, seq)
    nblk = seq // tm
    chunks = _ffn_chunks(f)
    wmax = max(c[1] for c in chunks)
    row = lambda i: (i, 0)
    y_spec = (pl.BlockSpec((1, kdim, tm), lambda i: (i // nblk, 0, i % nblk)) if y_feature_major
              else pl.BlockSpec((tm, kdim), row))
    return pl.pallas_call(
        functools.partial(_ffn_kernel, chunks=chunks, steps_per_batch=nblk, alpha=alpha,
                          y_feature_major=y_feature_major),
        grid=(m // tm,),
        in_specs=[y_spec, _resident(wo.shape), pl.BlockSpec((tm, d), row),
                  _resident(g1.shape), _resident(b1.shape), _resident(wu.shape), _resident(wg.shape),
                  _resident(cw.shape), _resident(cb.shape), _resident(wd.shape),
                  _resident(g.shape), _resident(b.shape), pl.BlockSpec((tm, pdim), row),
                  _resident(wproj.shape), _resident(wgate.shape), _resident(bgate.shape)],
        out_specs=pl.BlockSpec((tm, d), row),
        out_shape=jax.ShapeDtypeStruct((m, d), F32),
        scratch_shapes=[pltpu.VMEM((tm + SUBLANES, wmax), F32), pltpu.VMEM((tm + SUBLANES, wmax), F32),
                        pltpu.VMEM((SUBLANES, f), F32)],
        compiler_params=_params("arbitrary"),
        name="conv_ffn",
    )(y, wo, xf, g1, b1, wu, wg, cw, cb, wd, g, b, pf, wproj, wgate, bgate)


def _ssd_proj_kernel(x_ref, wz_ref, wxbc_ref, wdt_ref, cw_ref, cb_ref, dtb_ref,
                     z_ref, xs_ref, bt_ref, c_ref, dt_ref, dtt_ref, buf0_ref, buf1_ref, carry_ref,
                     *, steps_per_batch, col_tile):
    tm = x_ref.shape[0]
    halo = SUBLANES
    kconv = cw_ref.shape[0]
    bufs = (buf0_ref, buf1_ref)

    @pl.when(pl.program_id(0) % steps_per_batch == 0)
    def _():
        carry_ref[...] = jnp.zeros_like(carry_ref)

    xb = x_ref[...].astype(BF16)
    d_inner = xs_ref.shape[1]
    gn = c_ref.shape[1]
    nconv = wxbc_ref.shape[1] // col_tile
    z_chunks = list(range(0, wz_ref.shape[1], col_tile))

    def project(i):
        cs = slice(i * col_tile, (i + 1) * col_tile)
        buf = bufs[i % 2]
        r = jnp.dot(xb, wxbc_ref[:, cs], preferred_element_type=F32)
        buf[0:halo, :] = carry_ref[:, cs]
        buf[halo:halo + tm, :] = r
        carry_ref[:, cs] = r[tm - halo:tm, :]

    def project_z():
        if z_chunks:
            zs = slice(z_chunks[0], z_chunks.pop(0) + col_tile)
            z_ref[:, zs] = jnp.dot(xb, wz_ref[:, zs], preferred_element_type=F32).astype(BF16)

    def conv_act(i):
        c0 = i * col_tile
        cs = slice(c0, c0 + col_tile)
        buf = bufs[i % 2]
        conv = cb_ref[:, cs] + cw_ref[kconv - 1:kconv, cs] * buf[halo:halo + tm, :]
        for k in range(kconv - 1):
            off = halo - (kconv - 1) + k
            conv = conv + cw_ref[k:k + 1, cs] * buf[off:off + tm, :]
        act = conv * _sigmoid(conv)
        if c0 < d_inner:
            xs_ref[:, cs] = act.astype(BF16)
        elif c0 < d_inner + gn:
            bt_ref[0, c0 - d_inner:c0 - d_inner + col_tile, :] = act.T.astype(BF16)
        else:
            c_ref[:, c0 - d_inner - gn:c0 - d_inner - gn + col_tile] = act.astype(BF16)

    project(0)
    for i in range(nconv):
        if i + 1 < nconv:
            project(i + 1)
        else:
            project_z()
        conv_act(i)
    while z_chunks:
        project_z()
    dt = _softplus(jnp.dot(xb, wdt_ref[...], preferred_element_type=F32) + dtb_ref[...])
    dt_ref[...] = dt
    dtt_ref[0] = dt.T[0:dtt_ref.shape[1], :]


def _ssd_proj(xf, wz, wxbc, wdt, cw, cb, dtb, bsz, seq, gn, nheads):
    m, d = xf.shape
    d_inner = wz.shape[1]
    tm = min(ROW_TILE, seq)
    nblk = seq // tm
    col_tile = min(SSD_COL_TILE, gn)
    row = lambda i: (i, 0)
    tposed = lambda i: (i // nblk, 0, i % nblk)
    out_shape = [jax.ShapeDtypeStruct((m, d_inner), BF16), jax.ShapeDtypeStruct((m, d_inner), BF16),
                 jax.ShapeDtypeStruct((bsz, gn, seq), BF16), jax.ShapeDtypeStruct((m, gn), BF16),
                 jax.ShapeDtypeStruct((m, LANES), F32),
                 jax.ShapeDtypeStruct((bsz, nheads, seq), F32)]
    return pl.pallas_call(
        functools.partial(_ssd_proj_kernel, steps_per_batch=seq // tm, col_tile=col_tile),
        grid=(m // tm,),
        in_specs=[pl.BlockSpec((tm, d), row), _resident(wz.shape), _resident(wxbc.shape),
                  _resident(wdt.shape), _resident(cw.shape), _resident(cb.shape),
                  _resident(dtb.shape)],
        out_specs=[pl.BlockSpec((tm, d_inner), row), pl.BlockSpec((tm, d_inner), row),
                   pl.BlockSpec((1, gn, tm), tposed), pl.BlockSpec((tm, gn), row),
                   pl.BlockSpec((tm, LANES), row), pl.BlockSpec((1, nheads, tm), tposed)],
        out_shape=out_shape,
        scratch_shapes=[pltpu.VMEM((tm + SUBLANES, col_tile), F32),
                        pltpu.VMEM((tm + SUBLANES, col_tile), F32),
                        pltpu.VMEM((SUBLANES, wxbc.shape[1]), F32)],
        compiler_params=_params("arbitrary"),
        name="ssd_proj",
    )(xf, wz, wxbc, wdt, cw, cb, dtb)


def _expand_heads(cols, lane_head):
    out = cols[0]
    for r in range(1, len(cols)):
        out = jnp.where(lane_head >= r, cols[r], out)
    return out


def _ssd_scan_kernel(xs_ref, bt_ref, c_ref, z_ref, dt_ref, dtt_ref, arow_ref, acol_ref,
                     dskip_ref, nw_ref, ltri_ref, utri_ref, y_ref, state_ref, *, heads_per_group):
    q = xs_ref.shape[1]
    n = SSM_STATE
    hp_ = SSM_HEAD_DIM
    gw = heads_per_group * hp_
    ngroups = xs_ref.shape[2] // gw

    @pl.when(pl.program_id(1) == 0)
    def _():
        state_ref[...] = jnp.zeros_like(state_ref)

    dt_col = dt_ref[0]
    dt_row = dtt_ref[0]
    nheads = dt_row.shape[0]
    a_col = dt_col * (arow_ref[...] * LOG2E)
    pc = jnp.dot(ltri_ref[...], jnp.concatenate(_split3(a_col), axis=1),
                 preferred_element_type=F32)
    acum_col = pc[:, 0:LANES] + pc[:, LANES:2 * LANES] + pc[:, 2 * LANES:3 * LANES]
    a_row = dt_row * (acol_ref[...] * LOG2E)
    pr = jnp.dot(jnp.concatenate(_split3(a_row), axis=0), utri_ref[...],
                 preferred_element_type=F32)
    acum_row = pr[0:nheads] + pr[nheads:2 * nheads] + pr[2 * nheads:3 * nheads]
    dec_row = jnp.exp2(acum_col[q - 1:q, :])
    w_row = jnp.exp2(acum_row[:, q - 1:q] - acum_row) * dt_row
    src_row = acum_row - jnp.log2(dt_row)

    nsub = q // LANES
    tri = (lax.broadcasted_iota(jnp.int32, (LANES, LANES), 0)
           >= lax.broadcasted_iota(jnp.int32, (LANES, LANES), 1))
    lane_head = lax.broadcasted_iota(jnp.int32, (1, gw), 1) // hp_
    low_half = lax.broadcasted_iota(jnp.int32, (1, LANES), 1) < hp_

    for g in range(ngroups):
        gs = slice(g * gw, (g + 1) * gw)
        ns = slice(g * n, (g + 1) * n)
        cg = c_ref[0, :, ns]
        btg = bt_ref[0, ns, :]
        btg_f = btg.astype(F32)
        xg_b = xs_ref[0, :, gs]
        heads = range(g * heads_per_group, (g + 1) * heads_per_group)
        cb = jnp.dot(cg, btg, preferred_element_type=F32).astype(BF16)
        state = state_ref[g]
        y_inter = jnp.dot(cg, state.astype(BF16), preferred_element_type=F32)
        e_cols, y_intra, st_new = [], None, None
        for r, h in enumerate(heads):
            acol_b = jnp.broadcast_to(acum_col[:, h:h + 1], (q, LANES))
            e_cols.append(jnp.exp2(acol_b))
            rows = []
            for ti in range(nsub):
                tsl = slice(ti * LANES, (ti + 1) * LANES)
                tiles = []
                for si in range(nsub):
                    ssl = slice(si * LANES, (si + 1) * LANES)
                    if si > ti:
                        tiles.append(jnp.zeros((LANES, LANES), BF16))
                        continue
                    seg = acol_b[tsl] - src_row[h:h + 1, ssl]
                    if si == ti:
                        seg = jnp.where(tri, seg, -jnp.inf)
                    tiles.append(cb[tsl, ssl] * jnp.exp2(seg).astype(BF16))
                rows.append(jnp.concatenate(tiles, axis=1))
            mh = jnp.concatenate(rows, axis=0)
            xh = jnp.where(lane_head == r, xg_b, jnp.zeros_like(xg_b))
            part = jnp.dot(mh, xh, preferred_element_type=F32)
            y_intra = part if y_intra is None else y_intra + part
            btw = (btg_f * w_row[h:h + 1, :]).astype(BF16)
            part = jnp.dot(btw, xh, preferred_element_type=F32)
            st_new = part if st_new is None else st_new + part
        e_x = jnp.concatenate([jnp.where(low_half, e_cols[2 * i], e_cols[2 * i + 1])
                               for i in range(gw // LANES)], axis=1)
        y_g = y_inter * e_x + y_intra
        d_x = _expand_heads([dec_row[:, h:h + 1] for h in heads], lane_head)
        state_ref[g] = state * d_x + st_new
        xg = xg_b.astype(F32)
        zg = z_ref[0, :, gs].astype(F32)
        yv = (y_g + dskip_ref[:, gs] * xg) * (zg * _sigmoid(zg))
        ms = jnp.mean(yv * yv, axis=-1, keepdims=True)
        y_ref[0, :, gs] = (yv * lax.rsqrt(ms + RMS_EPS) * nw_ref[:, gs]).astype(BF16)


def _ssd_scan(xs, bmt, cm, z, dt, dtt, arow, acol, dskip, nw, ltri, utri, q):
    b, s, d_inner = xs.shape
    gn = cm.shape[2]
    nheads = dtt.shape[1]
    ngroups = gn // SSM_STATE
    heads_per_group = nheads // ngroups
    gw = heads_per_group * SSM_HEAD_DIM
    blk = lambda w: pl.BlockSpec((1, q, w), lambda bi, ci: (bi, ci, 0))
    return pl.pallas_call(
        functools.partial(_ssd_scan_kernel, heads_per_group=heads_per_group),
        grid=(b, s // q),
        in_specs=[blk(d_inner), pl.BlockSpec((1, gn, q), lambda bi, ci: (bi, 0, ci)),
                  blk(gn), blk(d_inner), blk(LANES),
                  pl.BlockSpec((1, nheads, q), lambda bi, ci: (bi, 0, ci)),
                  _resident(arow.shape), _resident(acol.shape), _resident(dskip.shape),
                  _resident(nw.shape), _resident(ltri.shape), _resident(utri.shape)],
        out_specs=blk(d_inner),
        out_shape=jax.ShapeDtypeStruct((b, s, d_inner), BF16),
        scratch_shapes=[pltpu.VMEM((ngroups, SSM_STATE, gw), F32)],
        compiler_params=_params("arbitrary", "arbitrary"),
        name="ssd_scan",
    )(xs, bmt, cm, z, dt, dtt, arow, acol, dskip, nw, ltri, utri)


def _pad_cols(a, width):
    return jnp.pad(a, ((0, 0), (0, width - a.shape[1])))


def _fox_mixer(xf, bsz, seq, w_in, b_f):
    m, d = xf.shape
    nh = d // ATTN_HEAD_DIM
    scale = LOG2E / math.sqrt(ATTN_HEAD_DIM)
    wqt = (w_in[:, :d] * scale).T.astype(BF16)
    wk = w_in[:, d:2 * d].astype(BF16)
    wvt = w_in[:, 2 * d:3 * d].T.astype(BF16)
    wft = w_in[:, 3 * d:].T.astype(BF16)
    utri = jnp.triu(jnp.ones((CUMSUM_CHUNK, CUMSUM_CHUNK), F32)).astype(BF16)
    blk = min(ATTN_BLOCK, seq)
    k, qt, vt, qta, ka = _fox_proj(xf, wk, wqt, wvt, wft, b_f.reshape(nh, 1), utri, bsz, seq, blk)
    o = _fox_attn(qt, qta, k.reshape(bsz, seq, d), ka.reshape(bsz, seq, -1), vt, blk)
    return o


def _ssd_mixer(xf, bsz, seq, w_in, conv_w, conv_b, dt_bias, a_log, d_skip, norm_w):
    m, d = xf.shape
    nheads = dt_bias.shape[0]
    d_inner = nheads * SSM_HEAD_DIM
    gn = SSM_GROUPS * SSM_STATE
    wz = w_in[:, :d_inner].astype(BF16)
    wxbc = w_in[:, d_inner:2 * d_inner + 2 * gn].astype(BF16)
    wdt = _pad_cols(w_in[:, 2 * d_inner + 2 * gn:], LANES).astype(BF16)
    dtb = _pad_cols(dt_bias.reshape(1, nheads), LANES)
    z, xs, bmt, cm, dt, dtt = _ssd_proj(xf, wz, wxbc, wdt, conv_w, conv_b.reshape(1, -1), dtb,
                                        bsz, seq, gn, nheads)
    a = -jnp.exp(a_log.astype(F32))
    arow = _pad_cols(a.reshape(1, nheads), LANES)
    acol = a.reshape(nheads, 1)
    dskip = jnp.repeat(d_skip, SSM_HEAD_DIM).reshape(1, d_inner)
    q = min(SSD_CHUNK, seq)
    ltri = jnp.tril(jnp.ones((q, q), F32)).astype(BF16)
    utri = jnp.triu(jnp.ones((q, q), F32)).astype(BF16)
    y = _ssd_scan(xs.reshape(bsz, seq, d_inner), bmt, cm.reshape(bsz, seq, gn),
                  z.reshape(bsz, seq, d_inner), dt.reshape(bsz, seq, LANES), dtt, arow, acol,
                  dskip, norm_w.reshape(1, d_inner), ltri, utri, q)
    return y.reshape(m, d_inner)


def kernel(x, p, attn_w_in, attn_b_f, attn_w_out, ssm_w_in, ssm_conv_w, ssm_conv_b, ssm_dt_bias, ssm_A_log, ssm_D, ssm_norm_w, ssm_w_out, ln_mix_g, ln_mix_b, ffn_w_up, ffn_conv_w, ffn_conv_b, ffn_w_down, ln_ffn_g, ln_ffn_b, ple_w_proj, ple_w_gate, ple_b_gate):
    bsz, seq, d = x.shape
    depth = p.shape[0]
    n_mixers = 2
    alpha = (2 * depth) ** 0.25
    xf = x.reshape(bsz * seq, d)
    for i in range(depth):
        j = i // n_mixers
        if i % n_mixers == 0:
            y, w_out = _fox_mixer(xf, bsz, seq, attn_w_in[j], attn_b_f[j]), attn_w_out[j]
        else:
            y = _ssd_mixer(xf, bsz, seq, ssm_w_in[j], ssm_conv_w[j], ssm_conv_b[j],
                           ssm_dt_bias[j], ssm_A_log[j], ssm_D[j], ssm_norm_w[j])
            w_out = ssm_w_out[j]
        f = ffn_conv_w.shape[-1]
        xf = _ffn(y, w_out.astype(BF16), xf, ln_mix_g[i].reshape(1, d), ln_mix_b[i].reshape(1, d),
                  ffn_w_up[i][:, :f].astype(BF16), ffn_w_up[i][:, f:].astype(BF16),
                  ffn_conv_w[i], ffn_conv_b[i].reshape(1, f), ffn_w_down[i].astype(BF16),
                  ln_ffn_g[i].reshape(1, d), ln_ffn_b[i].reshape(1, d),
                  p[i].reshape(bsz * seq, -1), ple_w_proj[i].astype(BF16),
                  ple_w_gate[i].astype(BF16), ple_b_gate[i].reshape(1, d), alpha, seq)
    return xf.reshape(bsz, seq, d)
```

```python
import functools
import math

import jax
import jax.numpy as jnp
import numpy as np
from jax import lax
from jax.experimental import pallas as pl
from jax.experimental.pallas import tpu as pltpu

F32 = jnp.float32
BF16 = jnp.bfloat16

ATTN_HEAD_DIM = 64
SSM_HEAD_DIM = 64
SSM_GROUPS = 8
SSM_STATE = 128
LN_EPS = 1e-5
RMS_EPS = 1e-5

LANES = 128
SUBLANES = 8
VMEM_LIMIT_BYTES = 56 * 1024 * 1024

ROW_TILE = 512
ATTN_BLOCK = 512
ATTN_UNROLL = 12
SSD_COL_TILE = 1024
FFN_CHUNK_TILES = 4
SSD_CHUNK = 256
CUMSUM_CHUNK = 256
MASK_VALUE = -1e30
LOG2E = 1.4426950408889634
SPLIT_PIECES = 3
AUG_ROWS = 16
ONES_ROWS = 16


def _params(*sem):
    return pltpu.CompilerParams(dimension_semantics=sem, vmem_limit_bytes=VMEM_LIMIT_BYTES)


def _resident(shape):
    nd = len(shape)
    return pl.BlockSpec(shape, lambda *_: (0,) * nd, pipeline_mode=pl.Buffered(1))


def _split3(v):
    hi = v.astype(BF16)
    r1 = v - hi.astype(F32)
    mid = r1.astype(BF16)
    lo = (r1 - mid.astype(F32)).astype(BF16)
    return hi, mid, lo


def _softplus(v):
    return jnp.maximum(v, 0.0) + jnp.log1p(jnp.exp(-jnp.abs(v)))


def _sigmoid(v):
    return 1.0 / (1.0 + jnp.exp2(v * (-LOG2E)))


def _layer_norm(h, g, b):
    mu = jnp.mean(h, axis=-1, keepdims=True)
    d = h - mu
    var = jnp.mean(d * d, axis=-1, keepdims=True)
    return d * lax.rsqrt(var + LN_EPS) * g + b


_NT = (((1,), (1,)), ((), ()))


def _bias_placement(nh):
    npieces = SPLIT_PIECES
    place_q = np.zeros((nh * AUG_ROWS, LANES), np.float32)
    ones_q = np.zeros((nh * AUG_ROWS, 1), np.float32)
    place_k = np.zeros((LANES, (nh // 2) * LANES), np.float32)
    ones_k = np.zeros((1, (nh // 2) * LANES), np.float32)
    for h in range(nh):
        off = 2 * npieces * (h % 2)
        for p in range(npieces):
            ones_q[h * AUG_ROWS + off + p, 0] = 1.0
            place_q[h * AUG_ROWS + off + npieces + p, p * nh + h] = 1.0
            place_k[p * nh + h, (h // 2) * LANES + off + p] = -1.0
            ones_k[0, (h // 2) * LANES + off + npieces + p] = 1.0
    return (jnp.asarray(place_q, BF16), jnp.asarray(ones_q), jnp.asarray(place_k, BF16),
            jnp.asarray(ones_k))


def _fox_proj_kernel(x_ref, wk_ref, wqt_ref, wvt_ref, wft_ref, bf_ref, ut_ref, pq_ref, oq_ref,
                     pk_ref, ok_ref, k_ref, qt_ref, vt_ref, qta_ref, ka_ref, carry_ref, *,
                     steps_per_batch):
    tm, d = x_ref.shape

    @pl.when(pl.program_id(0) % steps_per_batch == 0)
    def _():
        carry_ref[...] = jnp.zeros_like(carry_ref)

    xb = x_ref[...].astype(BF16)
    fl = lax.dot_general(wft_ref[...], xb, _NT, preferred_element_type=F32) + bf_ref[...]
    k_ref[...] = jnp.dot(xb, wk_ref[...], preferred_element_type=F32).astype(BF16)
    logf = -_softplus(-fl) * LOG2E
    nh = logf.shape[0]
    parts = jnp.concatenate(_split3(logf), axis=0)
    sums = [jnp.dot(parts[:, j * CUMSUM_CHUNK:(j + 1) * CUMSUM_CHUNK], ut_ref[...],
                    preferred_element_type=F32) for j in range(tm // CUMSUM_CHUNK)]
    qt_ref[0] = lax.dot_general(wqt_ref[...], xb, _NT, preferred_element_type=F32).astype(BF16)
    carry = carry_ref[...]
    zrows = jnp.zeros((LANES - 3 * nh, CUMSUM_CHUNK), F32)
    stacks = []
    for pj in sums:
        cj = (pj[0:nh] + pj[nh:2 * nh] + pj[2 * nh:3 * nh]) + carry
        stacks.append(jnp.concatenate([cp.astype(F32) for cp in _split3(cj)] + [zrows], axis=0))
        carry = cj[:, CUMSUM_CHUNK - 1:CUMSUM_CHUNK]
    carry_ref[...] = carry
    stack = jnp.concatenate(stacks, axis=1)
    qta_ref[0] = (jnp.dot(pq_ref[...], stack.astype(BF16), preferred_element_type=F32)
                  + oq_ref[...]).astype(BF16)
    ka_ref[...] = (jnp.dot(stack.T.astype(BF16), pk_ref[...], preferred_element_type=F32)
                   + ok_ref[...]).astype(BF16)
    vt_ref[0, 0] = lax.dot_general(wvt_ref[...], xb, _NT, preferred_element_type=F32).astype(BF16)


def _fox_proj(xf, wk, wqt, wvt, wft, bf_col, utri, bsz, seq, tm):
    m, d = xf.shape
    nh = wft.shape[0]
    nblk = seq // tm
    row = lambda i: (i, 0)
    place = _bias_placement(nh)
    out_shape = [jax.ShapeDtypeStruct((m, d), BF16), jax.ShapeDtypeStruct((bsz, d, seq), BF16),
                 jax.ShapeDtypeStruct((bsz, nblk, d, tm), BF16),
                 jax.ShapeDtypeStruct((bsz, nh * AUG_ROWS, seq), BF16),
                 jax.ShapeDtypeStruct((m, (nh // 2) * LANES), BF16)]
    return pl.pallas_call(
        functools.partial(_fox_proj_kernel, steps_per_batch=nblk),
        grid=(m // tm,),
        in_specs=[pl.BlockSpec((tm, d), row), _resident(wk.shape), _resident(wqt.shape),
                  _resident(wvt.shape), _resident(wft.shape), _resident(bf_col.shape),
                  _resident(utri.shape)] + [_resident(a.shape) for a in place],
        out_specs=[pl.BlockSpec((tm, d), row),
                   pl.BlockSpec((1, d, tm), lambda i: (i // nblk, 0, i % nblk)),
                   pl.BlockSpec((1, 1, d, tm), lambda i: (i // nblk, i % nblk, 0, 0)),
                   pl.BlockSpec((1, nh * AUG_ROWS, tm), lambda i: (i // nblk, 0, i % nblk)),
                   pl.BlockSpec((tm, (nh // 2) * LANES), row)],
        out_shape=out_shape,
        scratch_shapes=[pltpu.VMEM((nh, 1), F32)],
        compiler_params=_params("arbitrary"),
        name="fox_proj",
    )(xf, wk, wqt, wvt, wft, bf_col, utri, *place)


def _fox_attn_kernel(qt_ref, qta_ref, k_ref, ka_ref, vt_ref, o_ref, m_scr, acc_scr,
                     sa_scr, sb_scr, ma_scr, mb_scr, *, blk):
    qi = pl.program_id(2)
    hd = ATTN_HEAD_DIM
    qt2 = qt_ref[0]
    zhead = jnp.zeros((hd, blk), BF16)
    zpad = jnp.zeros((LANES - AUG_ROWS, blk), BF16)
    qts = [jnp.concatenate([qt2[0:hd], zhead, qta_ref[0, 0:AUG_ROWS], zpad], axis=0),
           jnp.concatenate([zhead, qt2[hd:2 * hd], qta_ref[0, AUG_ROWS:2 * AUG_ROWS], zpad], axis=0)]
    ones_rows = jnp.ones((ONES_ROWS, blk), BF16)
    key_id = lax.broadcasted_iota(jnp.int32, (blk, blk), 0)
    qry_id = lax.broadcasted_iota(jnp.int32, (blk, blk), 1)
    m_scr[...] = jnp.full_like(m_scr, MASK_VALUE)
    acc_scr[...] = jnp.zeros_like(acc_scr)

    def keys(j):
        start = pl.multiple_of(j * blk, blk)
        return jnp.concatenate([k_ref[0, pl.ds(start, blk), :], ka_ref[0, pl.ds(start, blk), :]],
                               axis=1)

    def values(j):
        vt2 = vt_ref[0, j]
        return [jnp.concatenate([vt2[h * hd:(h + 1) * hd], ones_rows], axis=0) for h in range(2)]

    half = blk // 2

    def produce(kk, h, s_buf, m_buf, diagonal=False):
        if diagonal:
            s_buf[h, 0:half, :] = jnp.dot(kk[0:half], qts[h], preferred_element_type=F32)
            s_buf[h, half:blk, half:blk] = jnp.dot(kk[half:blk], qts[h][:, half:blk],
                                                  preferred_element_type=F32)
            return
        s = jnp.dot(kk, qts[h], preferred_element_type=F32)
        s_buf[h] = s
        m_buf[h] = jnp.max(s, axis=0, keepdims=True)

    def consume(vt, h, s_buf, m_buf):
        s = s_buf[h]
        m_old = m_scr[h]
        m_new = jnp.maximum(m_old, m_buf[h])
        p = jnp.exp2(s - m_new).astype(BF16)
        alpha = jnp.exp2(m_old - m_new)
        acc_scr[h] = alpha * acc_scr[h] + jnp.dot(vt[h], p, preferred_element_type=F32)
        m_scr[h] = m_new

    def consume_diagonal(vt, h, s_buf):
        tri = (lax.broadcasted_iota(jnp.int32, (half, half), 0)
               <= lax.broadcasted_iota(jnp.int32, (half, half), 1))
        early, late = slice(0, half), slice(half, blk)
        for qs in (early, late):
            s_tri = jnp.where(tri, s_buf[h, qs, qs], MASK_VALUE)
            m_blk = jnp.max(s_tri, axis=0, keepdims=True)
            if qs is late:
                s_full = s_buf[h, early, late]
                m_blk = jnp.maximum(m_blk, jnp.max(s_full, axis=0, keepdims=True))
            m_old = m_scr[h, :, qs]
            m_new = jnp.maximum(m_old, m_blk)
            p = jnp.exp2(s_tri - m_new).astype(BF16)
            v_t = vt[h][:, qs]
            if qs is late:
                p = jnp.concatenate([jnp.exp2(s_full - m_new).astype(BF16), p], axis=0)
                v_t = vt[h]
            acc_scr[h, :, qs] = (jnp.exp2(m_old - m_new) * acc_scr[h, :, qs]
                                 + jnp.dot(v_t, p, preferred_element_type=F32))
            m_scr[h, :, qs] = m_new

    kk0 = keys(0)
    for h in range(2):
        produce(kk0, h, sa_scr, ma_scr)

    bufs = ((sa_scr, ma_scr), (sb_scr, mb_scr))

    def advance(j0, nsteps, ends_on_diagonal=False):
        for i in range(nsteps):
            kk, vt = keys(j0 + i + 1), values(j0 + i)
            for h in range(2):
                produce(kk, h, *bufs[(i + 1) % 2], diagonal=ends_on_diagonal and i == nsteps - 1)
                consume(vt, h, *bufs[i % 2])

    def body(t, carry):
        advance(ATTN_UNROLL * t, ATTN_UNROLL)
        return carry

    lax.fori_loop(0, qi // ATTN_UNROLL, body, 0)

    for rem in range(ATTN_UNROLL):
        @pl.when(qi % ATTN_UNROLL == rem)
        def _(rem=rem):
            advance(qi - rem, rem, ends_on_diagonal=True)
            vt = values(qi)
            for h in range(2):
                consume_diagonal(vt, h, bufs[rem % 2][0])

    outs = []
    for h in range(2):
        acc = acc_scr[h]
        outs.append(acc[0:hd] * (1.0 / acc[hd:hd + 1]))
    o_ref[0] = jnp.concatenate(outs, axis=0).astype(BF16)


def _fox_attn(qt, qta, k, ka, vt, blk):
    b, d, s = qt.shape
    nhp = d // LANES
    nblk = s // blk
    return pl.pallas_call(
        functools.partial(_fox_attn_kernel, blk=blk),
        grid=(b, nhp, nblk),
        in_specs=[
            pl.BlockSpec((1, LANES, blk), lambda bi, hp, qi: (bi, hp, qi)),
            pl.BlockSpec((1, 2 * AUG_ROWS, blk), lambda bi, hp, qi: (bi, hp, qi)),
            pl.BlockSpec((1, s, LANES), lambda bi, hp, qi: (bi, 0, hp)),
            pl.BlockSpec((1, s, LANES), lambda bi, hp, qi: (bi, 0, hp)),
            pl.BlockSpec((1, nblk, LANES, blk), lambda bi, hp, qi: (bi, 0, hp, 0)),
        ],
        out_specs=pl.BlockSpec((1, LANES, blk), lambda bi, hp, qi: (bi, hp, qi)),
        out_shape=jax.ShapeDtypeStruct((b, d, s), BF16),
        scratch_shapes=[pltpu.VMEM((2, 1, blk), F32),
                        pltpu.VMEM((2, ATTN_HEAD_DIM + ONES_ROWS, blk), F32),
                        pltpu.VMEM((2, blk, blk), F32), pltpu.VMEM((2, blk, blk), F32),
                        pltpu.VMEM((2, 1, blk), F32), pltpu.VMEM((2, 1, blk), F32)],
        compiler_params=_params("arbitrary", "arbitrary", "arbitrary"),
        name="fox_attn",
    )(qt, qta, k, ka, vt)


def _ffn_kernel(y_ref, wo_ref, x_ref, g1_ref, b1_ref, wu_ref, wg_ref, cw_ref, cb_ref, wd_ref,
                g_ref, b_ref, p_ref, wproj_ref, wgate_ref, bgate_ref, o_ref, gbuf0_ref, gbuf1_ref,
                carry_ref, *, chunks, steps_per_batch, alpha, y_feature_major):
    tm = x_ref.shape[0]
    halo = SUBLANES
    kconv = cw_ref.shape[0]
    gbufs = (gbuf0_ref, gbuf1_ref)
    halves = (slice(0, tm // 2), slice(tm // 2, tm))
    last = len(chunks) - 1

    @pl.when(pl.program_id(0) % steps_per_batch == 0)
    def _():
        carry_ref[...] = jnp.zeros_like(carry_ref)

    if y_feature_major:
        mixes = [lax.dot_general(y_ref[0, :, rs], wo_ref[...], (((0,), (0,)), ((), ())),
                                 preferred_element_type=F32) for rs in halves]
    else:
        mixes = [jnp.dot(y_ref[rs, :], wo_ref[...], preferred_element_type=F32) for rs in halves]
    xh = [_layer_norm(alpha * x_ref[rs, :] + mix, g1_ref[...], b1_ref[...])
          for rs, mix in zip(halves, mixes)]
    xbh = [v.astype(BF16) for v in xh]

    def stage(i, g):
        c0, cw = chunks[i]
        cs = slice(c0, c0 + cw)
        buf = gbufs[i % 2]
        buf[0:halo, 0:cw] = carry_ref[:, cs]
        buf[halo:halo + tm, 0:cw] = g
        carry_ref[:, cs] = g[tm - halo:tm, :]

    def up(i, lhs):
        c0, cw = chunks[i]
        cs = slice(c0, c0 + cw)
        return (jnp.dot(lhs, wu_ref[:, cs], preferred_element_type=F32),
                jnp.dot(lhs, wg_ref[:, cs], preferred_element_type=F32))

    def gated(i, u):
        c0, cw = chunks[i]
        cs = slice(c0, c0 + cw)
        buf = gbufs[i % 2]
        conv = cb_ref[:, cs] + cw_ref[kconv - 1:kconv, cs] * buf[halo:halo + tm, 0:cw]
        for k in range(kconv - 1):
            off = halo - (kconv - 1) + k
            conv = conv + cw_ref[k:k + 1, cs] * buf[off:off + tm, 0:cw]
        gelu = 0.5 * conv * (1.0 + lax.erf(conv * (1.0 / math.sqrt(2.0))))
        return (gelu * u).astype(BF16)

    first = [up(0, xbh[0]), up(0, xbh[1])]
    u_cur = jnp.concatenate([first[0][0], first[1][0]], axis=0)
    stage(0, jnp.concatenate([first[0][1], first[1][1]], axis=0))
    xb = jnp.concatenate(xbh, axis=0)

    acc = None
    for i in range(last):
        u_next, g_next = up(i + 1, xb)
        stage(i + 1, g_next)
        cs = slice(chunks[i][0], chunks[i][0] + chunks[i][1])
        part = jnp.dot(gated(i, u_cur), wd_ref[cs, :], preferred_element_type=F32)
        acc = part if acc is None else acc + part
        u_cur = u_next

    hid = gated(last, u_cur)
    cs = slice(chunks[last][0], chunks[last][0] + chunks[last][1])
    parts = [jnp.dot(hid[rs, :], wd_ref[cs, :], preferred_element_type=F32) for rs in halves]
    for rs, part, xin in zip(halves, parts, xh):
        total = part if acc is None else acc[rs, :] + part
        x2 = _layer_norm(alpha * xin + total, g_ref[...], b_ref[...])
        gate = _sigmoid(jnp.dot(x2.astype(BF16), wgate_ref[...], preferred_element_type=F32)
                        + bgate_ref[...])
        emb = jnp.dot(p_ref[rs, :].astype(BF16), wproj_ref[...], preferred_element_type=F32)
        o_ref[rs, :] = x2 + gate * emb


def _ffn_chunks(f):
    tile = 2 * LANES
    if f % tile:
        return ((0, f),)
    step = FFN_CHUNK_TILES * tile
    return tuple((c0, min(step, f - c0)) for c0 in range(0, f, step))


def _ffn(y, wo, xf, g1, b1, wu, wg, cw, cb, wd, g, b, pf, wproj, wgate, bgate, alpha, seq):
    m, d = xf.shape
    f = wu.shape[1]
    pdim = pf.shape[1]
    y_feature_major = y.ndim == 3
    kdim = y.shape[1]
    tm = min(ROW_TILE, seq)
    nblk = seq // tm
    chunks = _ffn_chunks(f)
    wmax = max(c[1] for c in chunks)
    row = lambda i: (i, 0)
    y_spec = (pl.BlockSpec((1, kdim, tm), lambda i: (i // nblk, 0, i % nblk)) if y_feature_major
              else pl.BlockSpec((tm, kdim), row))
    return pl.pallas_call(
        functools.partial(_ffn_kernel, chunks=chunks, steps_per_batch=nblk, alpha=alpha,
                          y_feature_major=y_feature_major),
        grid=(m // tm,),
        in_specs=[y_spec, _resident(wo.shape), pl.BlockSpec((tm, d), row),
                  _resident(g1.shape), _resident(b1.shape), _resident(wu.shape), _resident(wg.shape),
                  _resident(cw.shape), _resident(cb.shape), _resident(wd.shape),
                  _resident(g.shape), _resident(b.shape), pl.BlockSpec((tm, pdim), row),
                  _resident(wproj.shape), _resident(wgate.shape), _resident(bgate.shape)],
        out_specs=pl.BlockSpec((tm, d), row),
        out_shape=jax.ShapeDtypeStruct((m, d), F32),
        scratch_shapes=[pltpu.VMEM((tm + SUBLANES, wmax), F32), pltpu.VMEM((tm + SUBLANES, wmax), F32),
                        pltpu.VMEM((SUBLANES, f), F32)],
        compiler_params=_params("arbitrary"),
        name="conv_ffn",
    )(y, wo, xf, g1, b1, wu, wg, cw, cb, wd, g, b, pf, wproj, wgate, bgate)


def _ssd_proj_kernel(x_ref, wz_ref, wxbc_ref, wdt_ref, cw_ref, cb_ref, dtb_ref,
                     z_ref, xs_ref, bt_ref, c_ref, dt_ref, dtt_ref, buf0_ref, buf1_ref, carry_ref,
                     *, steps_per_batch, col_tile):
    tm = x_ref.shape[0]
    halo = SUBLANES
    kconv = cw_ref.shape[0]
    bufs = (buf0_ref, buf1_ref)

    @pl.when(pl.program_id(0) % steps_per_batch == 0)
    def _():
        carry_ref[...] = jnp.zeros_like(carry_ref)

    xb = x_ref[...].astype(BF16)
    d_inner = xs_ref.shape[1]
    gn = c_ref.shape[1]
    nconv = wxbc_ref.shape[1] // col_tile
    z_chunks = list(range(0, wz_ref.shape[1], col_tile))

    def project(i):
        cs = slice(i * col_tile, (i + 1) * col_tile)
        buf = bufs[i % 2]
        r = jnp.dot(xb, wxbc_ref[:, cs], preferred_element_type=F32)
        buf[0:halo, :] = carry_ref[:, cs]
        buf[halo:halo + tm, :] = r
        carry_ref[:, cs] = r[tm - halo:tm, :]

    def project_z():
        if z_chunks:
            zs = slice(z_chunks[0], z_chunks.pop(0) + col_tile)
            z_ref[:, zs] = jnp.dot(xb, wz_ref[:, zs], preferred_element_type=F32).astype(BF16)

    def conv_act(i):
        c0 = i * col_tile
        cs = slice(c0, c0 + col_tile)
        buf = bufs[i % 2]
        conv = cb_ref[:, cs] + cw_ref[kconv - 1:kconv, cs] * buf[halo:halo + tm, :]
        for k in range(kconv - 1):
            off = halo - (kconv - 1) + k
            conv = conv + cw_ref[k:k + 1, cs] * buf[off:off + tm, :]
        act = conv * _sigmoid(conv)
        if c0 < d_inner:
            xs_ref[:, cs] = act.astype(BF16)
        elif c0 < d_inner + gn:
            bt_ref[0, c0 - d_inner:c0 - d_inner + col_tile, :] = act.T.astype(BF16)
        else:
            c_ref[:, c0 - d_inner - gn:c0 - d_inner - gn + col_tile] = act.astype(BF16)

    project(0)
    for i in range(nconv):
        if i + 1 < nconv:
            project(i + 1)
        else:
            project_z()
        conv_act(i)
    while z_chunks:
        project_z()
    dt = _softplus(jnp.dot(xb, wdt_ref[...], preferred_element_type=F32) + dtb_ref[...])
    dt_ref[...] = dt
    dtt_ref[0] = dt.T[0:dtt_ref.shape[1], :]


def _ssd_proj(xf, wz, wxbc, wdt, cw, cb, dtb, bsz, seq, gn, nheads):
    m, d = xf.shape
    d_inner = wz.shape[1]
    tm = min(ROW_TILE, seq)
    nblk = seq // tm
    col_tile = min(SSD_COL_TILE, gn)
    row = lambda i: (i, 0)
    tposed = lambda i: (i // nblk, 0, i % nblk)
    out_shape = [jax.ShapeDtypeStruct((m, d_inner), BF16), jax.ShapeDtypeStruct((m, d_inner), BF16),
                 jax.ShapeDtypeStruct((bsz, gn, seq), BF16), jax.ShapeDtypeStruct((m, gn), BF16),
                 jax.ShapeDtypeStruct((m, LANES), F32),
                 jax.ShapeDtypeStruct((bsz, nheads, seq), F32)]
    return pl.pallas_call(
        functools.partial(_ssd_proj_kernel, steps_per_batch=seq // tm, col_tile=col_tile),
        grid=(m // tm,),
        in_specs=[pl.BlockSpec((tm, d), row), _resident(wz.shape), _resident(wxbc.shape),
                  _resident(wdt.shape), _resident(cw.shape), _resident(cb.shape),
                  _resident(dtb.shape)],
        out_specs=[pl.BlockSpec((tm, d_inner), row), pl.BlockSpec((tm, d_inner), row),
                   pl.BlockSpec((1, gn, tm), tposed), pl.BlockSpec((tm, gn), row),
                   pl.BlockSpec((tm, LANES), row), pl.BlockSpec((1, nheads, tm), tposed)],
        out_shape=out_shape,
        scratch_shapes=[pltpu.VMEM((tm + SUBLANES, col_tile), F32),
                        pltpu.VMEM((tm + SUBLANES, col_tile), F32),
                        pltpu.VMEM((SUBLANES, wxbc.shape[1]), F32)],
        compiler_params=_params("arbitrary"),
        name="ssd_proj",
    )(xf, wz, wxbc, wdt, cw, cb, dtb)


def _expand_heads(cols, lane_head):
    out = cols[0]
    for r in range(1, len(cols)):
        out = jnp.where(lane_head >= r, cols[r], out)
    return out


def _ssd_scan_kernel(xs_ref, bt_ref, c_ref, z_ref, dt_ref, dtt_ref, arow_ref, acol_ref,
                     dskip_ref, nw_ref, ltri_ref, utri_ref, y_ref, state_ref, *, heads_per_group):
    q = xs_ref.shape[1]
    n = SSM_STATE
    hp_ = SSM_HEAD_DIM
    gw = heads_per_group * hp_
    ngroups = xs_ref.shape[2] // gw

    @pl.when(pl.program_id(1) == 0)
    def _():
        state_ref[...] = jnp.zeros_like(state_ref)

    dt_col = dt_ref[0]
    dt_row = dtt_ref[0]
    nheads = dt_row.shape[0]
    a_col = dt_col * (arow_ref[...] * LOG2E)
    pc = jnp.dot(ltri_ref[...], jnp.concatenate(_split3(a_col), axis=1),
                 preferred_element_type=F32)
    acum_col = pc[:, 0:LANES] + pc[:, LANES:2 * LANES] + pc[:, 2 * LANES:3 * LANES]
    a_row = dt_row * (acol_ref[...] * LOG2E)
    pr = jnp.dot(jnp.concatenate(_split3(a_row), axis=0), utri_ref[...],
                 preferred_element_type=F32)
    acum_row = pr[0:nheads] + pr[nheads:2 * nheads] + pr[2 * nheads:3 * nheads]
    dec_row = jnp.exp2(acum_col[q - 1:q, :])
    w_row = jnp.exp2(acum_row[:, q - 1:q] - acum_row) * dt_row
    src_row = acum_row - jnp.log2(dt_row)

    nsub = q // LANES
    tri = (lax.broadcasted_iota(jnp.int32, (LANES, LANES), 0)
           >= lax.broadcasted_iota(jnp.int32, (LANES, LANES), 1))
    lane_head = lax.broadcasted_iota(jnp.int32, (1, gw), 1) // hp_
    low_half = lax.broadcasted_iota(jnp.int32, (1, LANES), 1) < hp_

    for g in range(ngroups):
        gs = slice(g * gw, (g + 1) * gw)
        ns = slice(g * n, (g + 1) * n)
        cg = c_ref[0, :, ns]
        btg = bt_ref[0, ns, :]
        btg_f = btg.astype(F32)
        xg_b = xs_ref[0, :, gs]
        heads = range(g * heads_per_group, (g + 1) * heads_per_group)
        cb = jnp.dot(cg, btg, preferred_element_type=F32).astype(BF16)
        state = state_ref[g]
        y_inter = jnp.dot(cg, state.astype(BF16), preferred_element_type=F32)
        e_cols, y_intra, st_new = [], None, None
        for r, h in enumerate(heads):
            acol_b = jnp.broadcast_to(acum_col[:, h:h + 1], (q, LANES))
            e_cols.append(jnp.exp2(acol_b))
            rows = []
            for ti in range(nsub):
                tsl = slice(ti * LANES, (ti + 1) * LANES)
                tiles = []
                for si in range(nsub):
                    ssl = slice(si * LANES, (si + 1) * LANES)
                    if si > ti:
                        tiles.append(jnp.zeros((LANES, LANES), BF16))
                        continue
                    seg = acol_b[tsl] - src_row[h:h + 1, ssl]
                    if si == ti:
                        seg = jnp.where(tri, seg, -jnp.inf)
                    tiles.append(cb[tsl, ssl] * jnp.exp2(seg).astype(BF16))
                rows.append(jnp.concatenate(tiles, axis=1))
            mh = jnp.concatenate(rows, axis=0)
            xh = jnp.where(lane_head == r, xg_b, jnp.zeros_like(xg_b))
            part = jnp.dot(mh, xh, preferred_element_type=F32)
            y_intra = part if y_intra is None else y_intra + part
            btw = (btg_f * w_row[h:h + 1, :]).astype(BF16)
            part = jnp.dot(btw, xh, preferred_element_type=F32)
            st_new = part if st_new is None else st_new + part
        e_x = jnp.concatenate([jnp.where(low_half, e_cols[2 * i], e_cols[2 * i + 1])
                               for i in range(gw // LANES)], axis=1)
        y_g = y_inter * e_x + y_intra
        d_x = _expand_heads([dec_row[:, h:h + 1] for h in heads], lane_head)
        state_ref[g] = state * d_x + st_new
        xg = xg_b.astype(F32)
        zg = z_ref[0, :, gs].astype(F32)
        yv = (y_g + dskip_ref[:, gs] * xg) * (zg * _sigmoid(zg))
        ms = jnp.mean(yv * yv, axis=-1, keepdims=True)
        y_ref[0, :, gs] = (yv * lax.rsqrt(ms + RMS_EPS) * nw_ref[:, gs]).astype(BF16)


def _ssd_scan(xs, bmt, cm, z, dt, dtt, arow, acol, dskip, nw, ltri, utri, q):
    b, s, d_inner = xs.shape
    gn = cm.shape[2]
    nheads = dtt.shape[1]
    ngroups = gn // SSM_STATE
    heads_per_group = nheads // ngroups
    gw = heads_per_group * SSM_HEAD_DIM
    blk = lambda w: pl.BlockSpec((1, q, w), lambda bi, ci: (bi, ci, 0))
    return pl.pallas_call(
        functools.partial(_ssd_scan_kernel, heads_per_group=heads_per_group),
        grid=(b, s // q),
        in_specs=[blk(d_inner), pl.BlockSpec((1, gn, q), lambda bi, ci: (bi, 0, ci)),
                  blk(gn), blk(d_inner), blk(LANES),
                  pl.BlockSpec((1, nheads, q), lambda bi, ci: (bi, 0, ci)),
                  _resident(arow.shape), _resident(acol.shape), _resident(dskip.shape),
                  _resident(nw.shape), _resident(ltri.shape), _resident(utri.shape)],
        out_specs=blk(d_inner),
        out_shape=jax.ShapeDtypeStruct((b, s, d_inner), BF16),
        scratch_shapes=[pltpu.VMEM((ngroups, SSM_STATE, gw), F32)],
        compiler_params=_params("arbitrary", "arbitrary"),
        name="ssd_scan",
    )(xs, bmt, cm, z, dt, dtt, arow, acol, dskip, nw, ltri, utri)


def _pad_cols(a, width):
    return jnp.pad(a, ((0, 0), (0, width - a.shape[1])))


def _fox_mixer(xf, bsz, seq, w_in, b_f):
    m, d = xf.shape
    nh = d // ATTN_HEAD_DIM
    scale = LOG2E / math.sqrt(ATTN_HEAD_DIM)
    wqt = (w_in[:, :d] * scale).T.astype(BF16)
    wk = w_in[:, d:2 * d].astype(BF16)
    wvt = w_in[:, 2 * d:3 * d].T.astype(BF16)
    wft = w_in[:, 3 * d:].T.astype(BF16)
    utri = jnp.triu(jnp.ones((CUMSUM_CHUNK, CUMSUM_CHUNK), F32)).astype(BF16)
    blk = min(ATTN_BLOCK, seq)
    k, qt, vt, qta, ka = _fox_proj(xf, wk, wqt, wvt, wft, b_f.reshape(nh, 1), utri, bsz, seq, blk)
    o = _fox_attn(qt, qta, k.reshape(bsz, seq, d), ka.reshape(bsz, seq, -1), vt, blk)
    return o


def _ssd_mixer(xf, bsz, seq, w_in, conv_w, conv_b, dt_bias, a_log, d_skip, norm_w):
    m, d = xf.shape
    nheads = dt_bias.shape[0]
    d_inner = nheads * SSM_HEAD_DIM
    gn = SSM_GROUPS * SSM_STATE
    wz = w_in[:, :d_inner].astype(BF16)
    wxbc = w_in[:, d_inner:2 * d_inner + 2 * gn].astype(BF16)
    wdt = _pad_cols(w_in[:, 2 * d_inner + 2 * gn:], LANES).astype(BF16)
    dtb = _pad_cols(dt_bias.reshape(1, nheads), LANES)
    z, xs, bmt, cm, dt, dtt = _ssd_proj(xf, wz, wxbc, wdt, conv_w, conv_b.reshape(1, -1), dtb,
                                        bsz, seq, gn, nheads)
    a = -jnp.exp(a_log.astype(F32))
    arow = _pad_cols(a.reshape(1, nheads), LANES)
    acol = a.reshape(nheads, 1)
    dskip = jnp.repeat(d_skip, SSM_HEAD_DIM).reshape(1, d_inner)
    q = min(SSD_CHUNK, seq)
    ltri = jnp.tril(jnp.ones((q, q), F32)).astype(BF16)
    utri = jnp.triu(jnp.ones((q, q), F32)).astype(BF16)
    y = _ssd_scan(xs.reshape(bsz, seq, d_inner), bmt, cm.reshape(bsz, seq, gn),
                  z.reshape(bsz, seq, d_inner), dt.reshape(bsz, seq, LANES), dtt, arow, acol,
                  dskip, norm_w.reshape(1, d_inner), ltri, utri, q)
    return y.reshape(m, d_inner)


def kernel(x, p, attn_w_in, attn_b_f, attn_w_out, ssm_w_in, ssm_conv_w, ssm_conv_b, ssm_dt_bias, ssm_A_log, ssm_D, ssm_norm_w, ssm_w_out, ln_mix_g, ln_mix_b, ffn_w_up, ffn_conv_w, ffn_conv_b, ffn_w_down, ln_ffn_g, ln_ffn_b, ple_w_proj, ple_w_gate, ple_b_gate):
    bsz, seq, d = x.shape
    depth = p.shape[0]
    n_mixers = 2
    alpha = (2 * depth) ** 0.25
    xf = x.reshape(bsz * seq, d)
    for i in range(depth):
        j = i // n_mixers
        if i % n_mixers == 0:
            y, w_out = _fox_mixer(xf, bsz, seq, attn_w_in[j], attn_b_f[j]), attn_w_out[j]
        else:
            y = _ssd_mixer(xf, bsz, seq, ssm_w_in[j], ssm_conv_w[j], ssm_conv_b[j],
                           ssm_dt_bias[j], ssm_A_log[j], ssm_D[j], ssm_norm_w[j])
            w_out = ssm_w_out[j]
        f = ffn_conv_w.shape[-1]
        xf = _ffn(y, w_out.astype(BF16), xf, ln_mix_g[i].reshape(1, d), ln_mix_b[i].reshape(1, d),
                  ffn_w_up[i][:, :f].astype(BF16), ffn_w_up[i][:, f:].astype(BF16),
                  ffn_conv_w[i], ffn_conv_b[i].reshape(1, f), ffn_w_down[i].astype(BF16),
                  ln_ffn_g[i].reshape(1, d), ln_ffn_b[i].reshape(1, d),
                  p[i].reshape(bsz * seq, -1), ple_w_proj[i].astype(BF16),
                  ple_w_gate[i].astype(BF16), ple_b_gate[i].reshape(1, d), alpha, seq)
    return xf.reshape(bsz, seq, d)
```

```python
import functools
import math

import jax
import jax.numpy as jnp
import numpy as np
from jax import lax
from jax.experimental import pallas as pl
from jax.experimental.pallas import tpu as pltpu

F32 = jnp.float32
BF16 = jnp.bfloat16

ATTN_HEAD_DIM = 64
SSM_HEAD_DIM = 64
SSM_GROUPS = 8
SSM_STATE = 128
LN_EPS = 1e-5
RMS_EPS = 1e-5

LANES = 128
SUBLANES = 8
VMEM_LIMIT_BYTES = 56 * 1024 * 1024

ROW_TILE = 512
ATTN_BLOCK = 512
ATTN_UNROLL = 16
SSD_COL_TILE = 1024
FFN_CHUNK_TILES = 4
SSD_CHUNK = 256
CUMSUM_CHUNK = 256
MASK_VALUE = -1e30
LOG2E = 1.4426950408889634
SPLIT_PIECES = 3
AUG_ROWS = 16
ONES_ROWS = 16


def _params(*sem):
    return pltpu.CompilerParams(dimension_semantics=sem, vmem_limit_bytes=VMEM_LIMIT_BYTES)


def _resident(shape):
    nd = len(shape)
    return pl.BlockSpec(shape, lambda *_: (0,) * nd, pipeline_mode=pl.Buffered(1))


def _split3(v):
    hi = v.astype(BF16)
    r1 = v - hi.astype(F32)
    mid = r1.astype(BF16)
    lo = (r1 - mid.astype(F32)).astype(BF16)
    return hi, mid, lo


def _softplus(v):
    return jnp.maximum(v, 0.0) + jnp.log1p(jnp.exp(-jnp.abs(v)))


def _sigmoid(v):
    return 1.0 / (1.0 + jnp.exp2(v * (-LOG2E)))


def _layer_norm(h, g, b):
    mu = jnp.mean(h, axis=-1, keepdims=True)
    d = h - mu
    var = jnp.mean(d * d, axis=-1, keepdims=True)
    return d * lax.rsqrt(var + LN_EPS) * g + b


_NT = (((1,), (1,)), ((), ()))


def _bias_placement(nh):
    npieces = SPLIT_PIECES
    place_q = np.zeros((nh * AUG_ROWS, LANES), np.float32)
    ones_q = np.zeros((nh * AUG_ROWS, 1), np.float32)
    place_k = np.zeros((LANES, (nh // 2) * LANES), np.float32)
    ones_k = np.zeros((1, (nh // 2) * LANES), np.float32)
    for h in range(nh):
        off = 2 * npieces * (h % 2)
        for p in range(npieces):
            ones_q[h * AUG_ROWS + off + p, 0] = 1.0
            place_q[h * AUG_ROWS + off + npieces + p, p * nh + h] = 1.0
            place_k[p * nh + h, (h // 2) * LANES + off + p] = -1.0
            ones_k[0, (h // 2) * LANES + off + npieces + p] = 1.0
    return (jnp.asarray(place_q, BF16), jnp.asarray(ones_q), jnp.asarray(place_k, BF16),
            jnp.asarray(ones_k))


def _fox_proj_kernel(x_ref, wk_ref, wqt_ref, wvt_ref, wft_ref, bf_ref, ut_ref, pq_ref, oq_ref,
                     pk_ref, ok_ref, k_ref, qt_ref, vt_ref, qta_ref, ka_ref, carry_ref, *,
                     steps_per_batch):
    tm, d = x_ref.shape

    @pl.when(pl.program_id(0) % steps_per_batch == 0)
    def _():
        carry_ref[...] = jnp.zeros_like(carry_ref)

    xb = x_ref[...].astype(BF16)
    fl = lax.dot_general(wft_ref[...], xb, _NT, preferred_element_type=F32) + bf_ref[...]
    k_ref[...] = jnp.dot(xb, wk_ref[...], preferred_element_type=F32).astype(BF16)
    logf = -_softplus(-fl) * LOG2E
    nh = logf.shape[0]
    parts = jnp.concatenate(_split3(logf), axis=0)
    sums = [jnp.dot(parts[:, j * CUMSUM_CHUNK:(j + 1) * CUMSUM_CHUNK], ut_ref[...],
                    preferred_element_type=F32) for j in range(tm // CUMSUM_CHUNK)]
    qt_ref[0] = lax.dot_general(wqt_ref[...], xb, _NT, preferred_element_type=F32).astype(BF16)
    carry = carry_ref[...]
    zrows = jnp.zeros((LANES - 3 * nh, CUMSUM_CHUNK), F32)
    stacks = []
    for pj in sums:
        cj = (pj[0:nh] + pj[nh:2 * nh] + pj[2 * nh:3 * nh]) + carry
        stacks.append(jnp.concatenate([cp.astype(F32) for cp in _split3(cj)] + [zrows], axis=0))
        carry = cj[:, CUMSUM_CHUNK - 1:CUMSUM_CHUNK]
    carry_ref[...] = carry
    stack = jnp.concatenate(stacks, axis=1)
    qta_ref[0] = (jnp.dot(pq_ref[...], stack.astype(BF16), preferred_element_type=F32)
                  + oq_ref[...]).astype(BF16)
    ka_ref[...] = (jnp.dot(stack.T.astype(BF16), pk_ref[...], preferred_element_type=F32)
                   + ok_ref[...]).astype(BF16)
    vt_ref[0, 0] = lax.dot_general(wvt_ref[...], xb, _NT, preferred_element_type=F32).astype(BF16)


def _fox_proj(xf, wk, wqt, wvt, wft, bf_col, utri, bsz, seq, tm):
    m, d = xf.shape
    nh = wft.shape[0]
    nblk = seq // tm
    row = lambda i: (i, 0)
    place = _bias_placement(nh)
    out_shape = [jax.ShapeDtypeStruct((m, d), BF16), jax.ShapeDtypeStruct((bsz, d, seq), BF16),
                 jax.ShapeDtypeStruct((bsz, nblk, d, tm), BF16),
                 jax.ShapeDtypeStruct((bsz, nh * AUG_ROWS, seq), BF16),
                 jax.ShapeDtypeStruct((m, (nh // 2) * LANES), BF16)]
    return pl.pallas_call(
        functools.partial(_fox_proj_kernel, steps_per_batch=nblk),
        grid=(m // tm,),
        in_specs=[pl.BlockSpec((tm, d), row), _resident(wk.shape), _resident(wqt.shape),
                  _resident(wvt.shape), _resident(wft.shape), _resident(bf_col.shape),
                  _resident(utri.shape)] + [_resident(a.shape) for a in place],
        out_specs=[pl.BlockSpec((tm, d), row),
                   pl.BlockSpec((1, d, tm), lambda i: (i // nblk, 0, i % nblk)),
                   pl.BlockSpec((1, 1, d, tm), lambda i: (i // nblk, i % nblk, 0, 0)),
                   pl.BlockSpec((1, nh * AUG_ROWS, tm), lambda i: (i // nblk, 0, i % nblk)),
                   pl.BlockSpec((tm, (nh // 2) * LANES), row)],
        out_shape=out_shape,
        scratch_shapes=[pltpu.VMEM((nh, 1), F32)],
        compiler_params=_params("arbitrary"),
        name="fox_proj",
    )(xf, wk, wqt, wvt, wft, bf_col, utri, *place)


def _fox_attn_kernel(qt_ref, qta_ref, k_ref, ka_ref, vt_ref, o_ref, m_scr, acc_scr,
                     sa_scr, sb_scr, ma_scr, mb_scr, *, blk):
    qi = pl.program_id(2)
    hd = ATTN_HEAD_DIM
    qt2 = qt_ref[0]
    zhead = jnp.zeros((hd, blk), BF16)
    zpad = jnp.zeros((LANES - AUG_ROWS, blk), BF16)
    qts = [jnp.concatenate([qt2[0:hd], zhead, qta_ref[0, 0:AUG_ROWS], zpad], axis=0),
           jnp.concatenate([zhead, qt2[hd:2 * hd], qta_ref[0, AUG_ROWS:2 * AUG_ROWS], zpad], axis=0)]
    ones_rows = jnp.ones((ONES_ROWS, blk), BF16)
    key_id = lax.broadcasted_iota(jnp.int32, (blk, blk), 0)
    qry_id = lax.broadcasted_iota(jnp.int32, (blk, blk), 1)
    m_scr[...] = jnp.full_like(m_scr, MASK_VALUE)
    acc_scr[...] = jnp.zeros_like(acc_scr)

    def keys(j):
        start = pl.multiple_of(j * blk, blk)
        return jnp.concatenate([k_ref[0, pl.ds(start, blk), :], ka_ref[0, pl.ds(start, blk), :]],
                               axis=1)

    def values(j):
        vt2 = vt_ref[0, j]
        return [jnp.concatenate([vt2[h * hd:(h + 1) * hd], ones_rows], axis=0) for h in range(2)]

    half = blk // 2

    def produce(kk, h, s_buf, m_buf, diagonal=False):
        if diagonal:
            s_buf[h, 0:half, :] = jnp.dot(kk[0:half], qts[h], preferred_element_type=F32)
            s_buf[h, half:blk, half:blk] = jnp.dot(kk[half:blk], qts[h][:, half:blk],
                                                  preferred_element_type=F32)
            return
        s = jnp.dot(kk, qts[h], preferred_element_type=F32)
        s_buf[h] = s
        m_buf[h] = jnp.max(s, axis=0, keepdims=True)

    def consume(vt, h, s_buf, m_buf):
        s = s_buf[h]
        m_old = m_scr[h]
        m_new = jnp.maximum(m_old, m_buf[h])
        p = jnp.exp2(s - m_new).astype(BF16)
        alpha = jnp.exp2(m_old - m_new)
        acc_scr[h] = alpha * acc_scr[h] + jnp.dot(vt[h], p, preferred_element_type=F32)
        m_scr[h] = m_new

    def consume_diagonal(vt, h, s_buf):
        tri = (lax.broadcasted_iota(jnp.int32, (half, half), 0)
               <= lax.broadcasted_iota(jnp.int32, (half, half), 1))
        early, late = slice(0, half), slice(half, blk)
        for qs in (early, late):
            s_tri = jnp.where(tri, s_buf[h, qs, qs], MASK_VALUE)
            m_blk = jnp.max(s_tri, axis=0, keepdims=True)
            if qs is late:
                s_full = s_buf[h, early, late]
                m_blk = jnp.maximum(m_blk, jnp.max(s_full, axis=0, keepdims=True))
            m_old = m_scr[h, :, qs]
            m_new = jnp.maximum(m_old, m_blk)
            p = jnp.exp2(s_tri - m_new).astype(BF16)
            v_t = vt[h][:, qs]
            if qs is late:
                p = jnp.concatenate([jnp.exp2(s_full - m_new).astype(BF16), p], axis=0)
                v_t = vt[h]
            acc_scr[h, :, qs] = (jnp.exp2(m_old - m_new) * acc_scr[h, :, qs]
                                 + jnp.dot(v_t, p, preferred_element_type=F32))
            m_scr[h, :, qs] = m_new

    kk0 = keys(0)
    for h in range(2):
        produce(kk0, h, sa_scr, ma_scr)

    bufs = ((sa_scr, ma_scr), (sb_scr, mb_scr))

    def advance(j0, nsteps, ends_on_diagonal=False):
        for i in range(nsteps):
            kk, vt = keys(j0 + i + 1), values(j0 + i)
            for h in range(2):
                produce(kk, h, *bufs[(i + 1) % 2], diagonal=ends_on_diagonal and i == nsteps - 1)
                consume(vt, h, *bufs[i % 2])

    def body(t, carry):
        advance(ATTN_UNROLL * t, ATTN_UNROLL)
        return carry

    lax.fori_loop(0, qi // ATTN_UNROLL, body, 0)

    for rem in range(ATTN_UNROLL):
        @pl.when(qi % ATTN_UNROLL == rem)
        def _(rem=rem):
            advance(qi - rem, rem, ends_on_diagonal=True)
            vt = values(qi)
            for h in range(2):
                consume_diagonal(vt, h, bufs[rem % 2][0])

    outs = []
    for h in range(2):
        acc = acc_scr[h]
        outs.append(acc[0:hd] * (1.0 / acc[hd:hd + 1]))
    o_ref[0] = jnp.concatenate(outs, axis=0).astype(BF16)


def _fox_attn(qt, qta, k, ka, vt, blk):
    b, d, s = qt.shape
    nhp = d // LANES
    nblk = s // blk
    return pl.pallas_call(
        functools.partial(_fox_attn_kernel, blk=blk),
        grid=(b, nhp, nblk),
        in_specs=[
            pl.BlockSpec((1, LANES, blk), lambda bi, hp, qi: (bi, hp, qi)),
            pl.BlockSpec((1, 2 * AUG_ROWS, blk), lambda bi, hp, qi: (bi, hp, qi)),
            pl.BlockSpec((1, s, LANES), lambda bi, hp, qi: (bi, 0, hp)),
            pl.BlockSpec((1, s, LANES), lambda bi, hp, qi: (bi, 0, hp)),
            pl.BlockSpec((1, nblk, LANES, blk), lambda bi, hp, qi: (bi, 0, hp, 0)),
        ],
        out_specs=pl.BlockSpec((1, LANES, blk), lambda bi, hp, qi: (bi, hp, qi)),
        out_shape=jax.ShapeDtypeStruct((b, d, s), BF16),
        scratch_shapes=[pltpu.VMEM((2, 1, blk), F32),
                        pltpu.VMEM((2, ATTN_HEAD_DIM + ONES_ROWS, blk), F32),
                        pltpu.VMEM((2, blk, blk), F32), pltpu.VMEM((2, blk, blk), F32),
                        pltpu.VMEM((2, 1, blk), F32), pltpu.VMEM((2, 1, blk), F32)],
        compiler_params=_params("arbitrary", "arbitrary", "arbitrary"),
        name="fox_attn",
    )(qt, qta, k, ka, vt)


def _ffn_kernel(y_ref, wo_ref, x_ref, g1_ref, b1_ref, wu_ref, wg_ref, cw_ref, cb_ref, wd_ref,
                g_ref, b_ref, p_ref, wproj_ref, wgate_ref, bgate_ref, o_ref, gbuf0_ref, gbuf1_ref,
                carry_ref, *, chunks, steps_per_batch, alpha, y_feature_major):
    tm = x_ref.shape[0]
    halo = SUBLANES
    kconv = cw_ref.shape[0]
    gbufs = (gbuf0_ref, gbuf1_ref)
    halves = (slice(0, tm // 2), slice(tm // 2, tm))
    last = len(chunks) - 1

    @pl.when(pl.program_id(0) % steps_per_batch == 0)
    def _():
        carry_ref[...] = jnp.zeros_like(carry_ref)

    if y_feature_major:
        mixes = [lax.dot_general(y_ref[0, :, rs], wo_ref[...], (((0,), (0,)), ((), ())),
                                 preferred_element_type=F32) for rs in halves]
    else:
        mixes = [jnp.dot(y_ref[rs, :], wo_ref[...], preferred_element_type=F32) for rs in halves]
    xh = [_layer_norm(alpha * x_ref[rs, :] + mix, g1_ref[...], b1_ref[...])
          for rs, mix in zip(halves, mixes)]
    xbh = [v.astype(BF16) for v in xh]

    def stage(i, g):
        c0, cw = chunks[i]
        cs = slice(c0, c0 + cw)
        buf = gbufs[i % 2]
        buf[0:halo, 0:cw] = carry_ref[:, cs]
        buf[halo:halo + tm, 0:cw] = g
        carry_ref[:, cs] = g[tm - halo:tm, :]

    def up(i, lhs):
        c0, cw = chunks[i]
        cs = slice(c0, c0 + cw)
        return (jnp.dot(lhs, wu_ref[:, cs], preferred_element_type=F32),
                jnp.dot(lhs, wg_ref[:, cs], preferred_element_type=F32))

    def gated(i, u):
        c0, cw = chunks[i]
        cs = slice(c0, c0 + cw)
        buf = gbufs[i % 2]
        conv = cb_ref[:, cs] + cw_ref[kconv - 1:kconv, cs] * buf[halo:halo + tm, 0:cw]
        for k in range(kconv - 1):
            off = halo - (kconv - 1) + k
            conv = conv + cw_ref[k:k + 1, cs] * buf[off:off + tm, 0:cw]
        gelu = 0.5 * conv * (1.0 + lax.erf(conv * (1.0 / math.sqrt(2.0))))
        return (gelu * u).astype(BF16)

    first = [up(0, xbh[0]), up(0, xbh[1])]
    u_cur = jnp.concatenate([first[0][0], first[1][0]], axis=0)
    stage(0, jnp.concatenate([first[0][1], first[1][1]], axis=0))
    xb = jnp.concatenate(xbh, axis=0)

    acc = None
    for i in range(last):
        u_next, g_next = up(i + 1, xb)
        stage(i + 1, g_next)
        cs = slice(chunks[i][0], chunks[i][0] + chunks[i][1])
        part = jnp.dot(gated(i, u_cur), wd_ref[cs, :], preferred_element_type=F32)
        acc = part if acc is None else acc + part
        u_cur = u_next

    hid = gated(last, u_cur)
    cs = slice(chunks[last][0], chunks[last][0] + chunks[last][1])
    parts = [jnp.dot(hid[rs, :], wd_ref[cs, :], preferred_element_type=F32) for rs in halves]
    for rs, part, xin in zip(halves, parts, xh):
        total = part if acc is None else acc[rs, :] + part
        x2 = _layer_norm(alpha * xin + total, g_ref[...], b_ref[...])
        gate = _sigmoid(jnp.dot(x2.astype(BF16), wgate_ref[...], preferred_element_type=F32)
                        + bgate_ref[...])
        emb = jnp.dot(p_ref[rs, :].astype(BF16), wproj_ref[...], preferred_element_type=F32)
        o_ref[rs, :] = x2 + gate * emb


def _ffn_chunks(f):
    tile = 2 * LANES
    if f % tile:
        return ((0, f),)
    step = FFN_CHUNK_TILES * tile
    return tuple((c0, min(step, f - c0)) for c0 in range(0, f, step))


def _ffn(y, wo, xf, g1, b1, wu, wg, cw, cb, wd, g, b, pf, wproj, wgate, bgate, alpha, seq):
    m, d = xf.shape
    f = wu.shape[1]
    pdim = pf.shape[1]
    y_feature_major = y.ndim == 3
    kdim = y.shape[1]
    tm = min(ROW_TILE, seq)
    nblk = seq // tm
    chunks = _ffn_chunks(f)
    wmax = max(c[1] for c in chunks)
    row = lambda i: (i, 0)
    y_spec = (pl.BlockSpec((1, kdim, tm), lambda i: (i // nblk, 0, i % nblk)) if y_feature_major
              else pl.BlockSpec((tm, kdim), row))
    return pl.pallas_call(
        functools.partial(_ffn_kernel, chunks=chunks, steps_per_batch=nblk, alpha=alpha,
                          y_feature_major=y_feature_major),
        grid=(m // tm,),
        in_specs=[y_spec, _resident(wo.shape), pl.BlockSpec((tm, d), row),
                  _resident(g1.shape), _resident(b1.shape), _resident(wu.shape), _resident(wg.shape),
                  _resident(cw.shape), _resident(cb.shape), _resident(wd.shape),
                  _resident(g.shape), _resident(b.shape), pl.BlockSpec((tm, pdim), row),
                  _resident(wproj.shape), _resident(wgate.shape), _resident(bgate.shape)],
        out_specs=pl.BlockSpec((tm, d), row),
        out_shape=jax.ShapeDtypeStruct((m, d), F32),
        scratch_shapes=[pltpu.VMEM((tm + SUBLANES, wmax), F32), pltpu.VMEM((tm + SUBLANES, wmax), F32),
                        pltpu.VMEM((SUBLANES, f), F32)],
        compiler_params=_params("arbitrary"),
        name="conv_ffn",
    )(y, wo, xf, g1, b1, wu, wg, cw, cb, wd, g, b, pf, wproj, wgate, bgate)


def _ssd_proj_kernel(x_ref, wz_ref, wxbc_ref, wdt_ref, cw_ref, cb_ref, dtb_ref,
                     z_ref, xs_ref, bt_ref, c_ref, dt_ref, dtt_ref, buf0_ref, buf1_ref, carry_ref,
                     *, steps_per_batch, col_tile):
    tm = x_ref.shape[0]
    halo = SUBLANES
    kconv = cw_ref.shape[0]
    bufs = (buf0_ref, buf1_ref)

    @pl.when(pl.program_id(0) % steps_per_batch == 0)
    def _():
        carry_ref[...] = jnp.zeros_like(carry_ref)

    xb = x_ref[...].astype(BF16)
    d_inner = xs_ref.shape[1]
    gn = c_ref.shape[1]
    nconv = wxbc_ref.shape[1] // col_tile
    z_chunks = list(range(0, wz_ref.shape[1], col_tile))

    def project(i):
        cs = slice(i * col_tile, (i + 1) * col_tile)
        buf = bufs[i % 2]
        r = jnp.dot(xb, wxbc_ref[:, cs], preferred_element_type=F32)
        buf[0:halo, :] = carry_ref[:, cs]
        buf[halo:halo + tm, :] = r
        carry_ref[:, cs] = r[tm - halo:tm, :]

    def project_z():
        if z_chunks:
            zs = slice(z_chunks[0], z_chunks.pop(0) + col_tile)
            z_ref[:, zs] = jnp.dot(xb, wz_ref[:, zs], preferred_element_type=F32).astype(BF16)

    def conv_act(i):
        c0 = i * col_tile
        cs = slice(c0, c0 + col_tile)
        buf = bufs[i % 2]
        conv = cb_ref[:, cs] + cw_ref[kconv - 1:kconv, cs] * buf[halo:halo + tm, :]
        for k in range(kconv - 1):
            off = halo - (kconv - 1) + k
            conv = conv + cw_ref[k:k + 1, cs] * buf[off:off + tm, :]
        act = conv * _sigmoid(conv)
        if c0 < d_inner:
            xs_ref[:, cs] = act.astype(BF16)
        elif c0 < d_inner + gn:
            bt_ref[0, c0 - d_inner:c0 - d_inner + col_tile, :] = act.T.astype(BF16)
        else:
            c_ref[:, c0 - d_inner - gn:c0 - d_inner - gn + col_tile] = act.astype(BF16)

    project(0)
    for i in range(nconv):
        if i + 1 < nconv:
            project(i + 1)
        else:
            project_z()
        conv_act(i)
    while z_chunks:
        project_z()
    dt = _softplus(jnp.dot(xb, wdt_ref[...], preferred_element_type=F32) + dtb_ref[...])
    dt_ref[...] = dt
    dtt_ref[0] = dt.T[0:dtt_ref.shape[1], :]


def _ssd_proj(xf, wz, wxbc, wdt, cw, cb, dtb, bsz, seq, gn, nheads):
    m, d = xf.shape
    d_inner = wz.shape[1]
    tm = min(ROW_TILE, seq)
    nblk = seq // tm
    col_tile = min(SSD_COL_TILE, gn)
    row = lambda i: (i, 0)
    tposed = lambda i: (i // nblk, 0, i % nblk)
    out_shape = [jax.ShapeDtypeStruct((m, d_inner), BF16), jax.ShapeDtypeStruct((m, d_inner), BF16),
                 jax.ShapeDtypeStruct((bsz, gn, seq), BF16), jax.ShapeDtypeStruct((m, gn), BF16),
                 jax.ShapeDtypeStruct((m, LANES), F32),
                 jax.ShapeDtypeStruct((bsz, nheads, seq), F32)]
    return pl.pallas_call(
        functools.partial(_ssd_proj_kernel, steps_per_batch=seq // tm, col_tile=col_tile),
        grid=(m // tm,),
        in_specs=[pl.BlockSpec((tm, d), row), _resident(wz.shape), _resident(wxbc.shape),
                  _resident(wdt.shape), _resident(cw.shape), _resident(cb.shape),
                  _resident(dtb.shape)],
        out_specs=[pl.BlockSpec((tm, d_inner), row), pl.BlockSpec((tm, d_inner), row),
                   pl.BlockSpec((1, gn, tm), tposed), pl.BlockSpec((tm, gn), row),
                   pl.BlockSpec((tm, LANES), row), pl.BlockSpec((1, nheads, tm), tposed)],
        out_shape=out_shape,
        scratch_shapes=[pltpu.VMEM((tm + SUBLANES, col_tile), F32),
                        pltpu.VMEM((tm + SUBLANES, col_tile), F32),
                        pltpu.VMEM((SUBLANES, wxbc.shape[1]), F32)],
        compiler_params=_params("arbitrary"),
        name="ssd_proj",
    )(xf, wz, wxbc, wdt, cw, cb, dtb)


def _expand_heads(cols, lane_head):
    out = cols[0]
    for r in range(1, len(cols)):
        out = jnp.where(lane_head >= r, cols[r], out)
    return out


def _ssd_scan_kernel(xs_ref, bt_ref, c_ref, z_ref, dt_ref, dtt_ref, arow_ref, acol_ref,
                     dskip_ref, nw_ref, ltri_ref, utri_ref, y_ref, state_ref, *, heads_per_group):
    q = xs_ref.shape[1]
    n = SSM_STATE
    hp_ = SSM_HEAD_DIM
    gw = heads_per_group * hp_
    ngroups = xs_ref.shape[2] // gw

    @pl.when(pl.program_id(1) == 0)
    def _():
        state_ref[...] = jnp.zeros_like(state_ref)

    dt_col = dt_ref[0]
    dt_row = dtt_ref[0]
    nheads = dt_row.shape[0]
    a_col = dt_col * (arow_ref[...] * LOG2E)
    pc = jnp.dot(ltri_ref[...], jnp.concatenate(_split3(a_col), axis=1),
                 preferred_element_type=F32)
    acum_col = pc[:, 0:LANES] + pc[:, LANES:2 * LANES] + pc[:, 2 * LANES:3 * LANES]
    a_row = dt_row * (acol_ref[...] * LOG2E)
    pr = jnp.dot(jnp.concatenate(_split3(a_row), axis=0), utri_ref[...],
                 preferred_element_type=F32)
    acum_row = pr[0:nheads] + pr[nheads:2 * nheads] + pr[2 * nheads:3 * nheads]
    dec_row = jnp.exp2(acum_col[q - 1:q, :])
    w_row = jnp.exp2(acum_row[:, q - 1:q] - acum_row) * dt_row
    src_row = acum_row - jnp.log2(dt_row)

    nsub = q // LANES
    tri = (lax.broadcasted_iota(jnp.int32, (LANES, LANES), 0)
           >= lax.broadcasted_iota(jnp.int32, (LANES, LANES), 1))
    lane_head = lax.broadcasted_iota(jnp.int32, (1, gw), 1) // hp_
    low_half = lax.broadcasted_iota(jnp.int32, (1, LANES), 1) < hp_

    for g in range(ngroups):
        gs = slice(g * gw, (g + 1) * gw)
        ns = slice(g * n, (g + 1) * n)
        cg = c_ref[0, :, ns]
        btg = bt_ref[0, ns, :]
        btg_f = btg.astype(F32)
        xg_b = xs_ref[0, :, gs]
        heads = range(g * heads_per_group, (g + 1) * heads_per_group)
        cb = jnp.dot(cg, btg, preferred_element_type=F32).astype(BF16)
        state = state_ref[g]
        y_inter = jnp.dot(cg, state.astype(BF16), preferred_element_type=F32)
        e_cols, y_intra, st_new = [], None, None
        for r, h in enumerate(heads):
            acol_b = jnp.broadcast_to(acum_col[:, h:h + 1], (q, LANES))
            e_cols.append(jnp.exp2(acol_b))
            rows = []
            for ti in range(nsub):
                tsl = slice(ti * LANES, (ti + 1) * LANES)
                tiles = []
                for si in range(nsub):
                    ssl = slice(si * LANES, (si + 1) * LANES)
                    if si > ti:
                        tiles.append(jnp.zeros((LANES, LANES), BF16))
                        continue
                    seg = acol_b[tsl] - src_row[h:h + 1, ssl]
                    if si == ti:
                        seg = jnp.where(tri, seg, -jnp.inf)
                    tiles.append(cb[tsl, ssl] * jnp.exp2(seg).astype(BF16))
                rows.append(jnp.concatenate(tiles, axis=1))
            mh = jnp.concatenate(rows, axis=0)
            xh = jnp.where(lane_head == r, xg_b, jnp.zeros_like(xg_b))
            part = jnp.dot(mh, xh, preferred_element_type=F32)
            y_intra = part if y_intra is None else y_intra + part
            btw = (btg_f * w_row[h:h + 1, :]).astype(BF16)
            part = jnp.dot(btw, xh, preferred_element_type=F32)
            st_new = part if st_new is None else st_new + part
        e_x = jnp.concatenate([jnp.where(low_half, e_cols[2 * i], e_cols[2 * i + 1])
                               for i in range(gw // LANES)], axis=1)
        y_g = y_inter * e_x + y_intra
        d_x = _expand_heads([dec_row[:, h:h + 1] for h in heads], lane_head)
        state_ref[g] = state * d_x + st_new
        xg = xg_b.astype(F32)
        zg = z_ref[0, :, gs].astype(F32)
        yv = (y_g + dskip_ref[:, gs] * xg) * (zg * _sigmoid(zg))
        ms = jnp.mean(yv * yv, axis=-1, keepdims=True)
        y_ref[0, :, gs] = (yv * lax.rsqrt(ms + RMS_EPS) * nw_ref[:, gs]).astype(BF16)


def _ssd_scan(xs, bmt, cm, z, dt, dtt, arow, acol, dskip, nw, ltri, utri, q):
    b, s, d_inner = xs.shape
    gn = cm.shape[2]
    nheads = dtt.shape[1]
    ngroups = gn // SSM_STATE
    heads_per_group = nheads // ngroups
    gw = heads_per_group * SSM_HEAD_DIM
    blk = lambda w: pl.BlockSpec((1, q, w), lambda bi, ci: (bi, ci, 0))
    return pl.pallas_call(
        functools.partial(_ssd_scan_kernel, heads_per_group=heads_per_group),
        grid=(b, s // q),
        in_specs=[blk(d_inner), pl.BlockSpec((1, gn, q), lambda bi, ci: (bi, 0, ci)),
                  blk(gn), blk(d_inner), blk(LANES),
                  pl.BlockSpec((1, nheads, q), lambda bi, ci: (bi, 0, ci)),
                  _resident(arow.shape), _resident(acol.shape), _resident(dskip.shape),
                  _resident(nw.shape), _resident(ltri.shape), _resident(utri.shape)],
        out_specs=blk(d_inner),
        out_shape=jax.ShapeDtypeStruct((b, s, d_inner), BF16),
        scratch_shapes=[pltpu.VMEM((ngroups, SSM_STATE, gw), F32)],
        compiler_params=_params("arbitrary", "arbitrary"),
        name="ssd_scan",
    )(xs, bmt, cm, z, dt, dtt, arow, acol, dskip, nw, ltri, utri)


def _pad_cols(a, width):
    return jnp.pad(a, ((0, 0), (0, width - a.shape[1])))


def _fox_mixer(xf, bsz, seq, w_in, b_f):
    m, d = xf.shape
    nh = d // ATTN_HEAD_DIM
    scale = LOG2E / math.sqrt(ATTN_HEAD_DIM)
    wqt = (w_in[:, :d] * scale).T.astype(BF16)
    wk = w_in[:, d:2 * d].astype(BF16)
    wvt = w_in[:, 2 * d:3 * d].T.astype(BF16)
    wft = w_in[:, 3 * d:].T.astype(BF16)
    utri = jnp.triu(jnp.ones((CUMSUM_CHUNK, CUMSUM_CHUNK), F32)).astype(BF16)
    blk = min(ATTN_BLOCK, seq)
    k, qt, vt, qta, ka = _fox_proj(xf, wk, wqt, wvt, wft, b_f.reshape(nh, 1), utri, bsz, seq, blk)
    o = _fox_attn(qt, qta, k.reshape(bsz, seq, d), ka.reshape(bsz, seq, -1), vt, blk)
    return o


def _ssd_mixer(xf, bsz, seq, w_in, conv_w, conv_b, dt_bias, a_log, d_skip, norm_w):
    m, d = xf.shape
    nheads = dt_bias.shape[0]
    d_inner = nheads * SSM_HEAD_DIM
    gn = SSM_GROUPS * SSM_STATE
    wz = w_in[:, :d_inner].astype(BF16)
    wxbc = w_in[:, d_inner:2 * d_inner + 2 * gn].astype(BF16)
    wdt = _pad_cols(w_in[:, 2 * d_inner + 2 * gn:], LANES).astype(BF16)
    dtb = _pad_cols(dt_bias.reshape(1, nheads), LANES)
    z, xs, bmt, cm, dt, dtt = _ssd_proj(xf, wz, wxbc, wdt, conv_w, conv_b.reshape(1, -1), dtb,
                                        bsz, seq, gn, nheads)
    a = -jnp.exp(a_log.astype(F32))
    arow = _pad_cols(a.reshape(1, nheads), LANES)
    acol = a.reshape(nheads, 1)
    dskip = jnp.repeat(d_skip, SSM_HEAD_DIM).reshape(1, d_inner)
    q = min(SSD_CHUNK, seq)
    ltri = jnp.tril(jnp.ones((q, q), F32)).astype(BF16)
    utri = jnp.triu(jnp.ones((q, q), F32)).astype(BF16)
    y = _ssd_scan(xs.reshape(bsz, seq, d_inner), bmt, cm.reshape(bsz, seq, gn),
                  z.reshape(bsz, seq, d_inner), dt.reshape(bsz, seq, LANES), dtt, arow, acol,
                  dskip, norm_w.reshape(1, d_inner), ltri, utri, q)
    return y.reshape(m, d_inner)


def kernel(x, p, attn_w_in, attn_b_f, attn_w_out, ssm_w_in, ssm_conv_w, ssm_conv_b, ssm_dt_bias, ssm_A_log, ssm_D, ssm_norm_w, ssm_w_out, ln_mix_g, ln_mix_b, ffn_w_up, ffn_conv_w, ffn_conv_b, ffn_w_down, ln_ffn_g, ln_ffn_b, ple_w_proj, ple_w_gate, ple_b_gate):
    bsz, seq, d = x.shape
    depth = p.shape[0]
    n_mixers = 2
    alpha = (2 * depth) ** 0.25
    xf = x.reshape(bsz * seq, d)
    for i in range(depth):
        j = i // n_mixers
        if i % n_mixers == 0:
            y, w_out = _fox_mixer(xf, bsz, seq, attn_w_in[j], attn_b_f[j]), attn_w_out[j]
        else:
            y = _ssd_mixer(xf, bsz, seq, ssm_w_in[j], ssm_conv_w[j], ssm_conv_b[j],
                           ssm_dt_bias[j], ssm_A_log[j], ssm_D[j], ssm_norm_w[j])
            w_out = ssm_w_out[j]
        f = ffn_conv_w.shape[-1]
        xf = _ffn(y, w_out.astype(BF16), xf, ln_mix_g[i].reshape(1, d), ln_mix_b[i].reshape(1, d),
                  ffn_w_up[i][:, :f].astype(BF16), ffn_w_up[i][:, f:].astype(BF16),
                  ffn_conv_w[i], ffn_conv_b[i].reshape(1, f), ffn_w_down[i].astype(BF16),
                  ln_ffn_g[i].reshape(1, d), ln_ffn_b[i].reshape(1, d),
                  p[i].reshape(bsz * seq, -1), ple_w_proj[i].astype(BF16),
                  ple_w_gate[i].astype(BF16), ple_b_gate[i].reshape(1, d), alpha, seq)
    return xf.reshape(bsz, seq, d)
```
